```python
import math
import jax, jax.numpy as jnp
from jax import lax
import numpy as np

D_MODEL = 1024
BATCH = 2
SEQ = 8192
DEPTH = 4

GRID_W = 64
CTX_LEN = 256
N_EVEN = (DEPTH + 1) // 2
N_ODD = DEPTH // 2
D_FF = 4 * D_MODEL
NORM_EPS = 1e-6
NEG_INF = -1e30

FNET_GROUPS = 8
FNET_GROUP_W = 64
FNET_W = FNET_GROUPS * FNET_GROUP_W
SSD_HEAD_DIM = 64
SSD_HEADS = 24
SSD_GROUPS = 4
SSD_HPG = SSD_HEADS // SSD_GROUPS
SSD_STATE = 128
SSD_INNER = SSD_HEADS * SSD_HEAD_DIM
SSD_CONV_DIM = SSD_INNER + 2 * SSD_GROUPS * SSD_STATE
SSD_CONV_W = 5
SSD_CHUNK = 128
EVEN_IN = FNET_W + SSD_INNER + SSD_CONV_DIM + 2 * SSD_HEADS
EVEN_MIX = FNET_W + SSD_INNER
HEAD_DIM = 64
WIN_Q_HEADS = 8
WIN_KV_HEADS = 2
WIN_GQA = WIN_Q_HEADS // WIN_KV_HEADS
WIN_RADIUS = 128
WIN_BLOCK = 128
NA_HEADS = 8
NA_ROWS = 8
NA_COLS = 16
ODD_IN = (WIN_Q_HEADS + 2 * WIN_KV_HEADS) * HEAD_DIM + 3 * NA_HEADS * HEAD_DIM
ODD_MIX = (WIN_Q_HEADS + NA_HEADS) * HEAD_DIM
ROPE_THETA = 10000.0

kernel_name = "hybrid_fourier_ssd_window_na_prefix_trunk"


def rms_norm(x, g):
    xf = x.astype(jnp.float32)
    y = xf * lax.rsqrt(jnp.mean(xf * xf, axis=-1, keepdims=True) + NORM_EPS)
    return (y * g.astype(jnp.float32)).astype(x.dtype)


def modulate(x, shift, scale):
    return x * (1 + scale) + shift


def sq_relu_mlp(x, w1, w2):
    return jnp.square(jax.nn.relu(x @ w1)) @ w2


def axial_rope_tables(n_tokens):
    pos = jnp.arange(n_tokens)
    row = (pos // GRID_W).astype(jnp.float32)
    col = (pos % GRID_W).astype(jnp.float32)
    n_freq = HEAD_DIM // 4
    inv = ROPE_THETA ** (-jnp.arange(n_freq, dtype=jnp.float32) / n_freq)
    ang = jnp.stack([row[:, None] * inv, col[:, None] * inv], axis=1)
    return jnp.cos(ang), jnp.sin(ang)


def apply_axial_rope(x, cos, sin):
    b, l, h, dh = x.shape
    xr = x.astype(jnp.float32).reshape(b, l, h, 2, 2, dh // 4)
    x1, x2 = xr[..., 0, :], xr[..., 1, :]
    cs, sn = cos[:, None], sin[:, None]
    out = jnp.stack([x1 * cs - x2 * sn, x2 * cs + x1 * sn], axis=-2)
    return out.reshape(b, l, h, dh).astype(x.dtype)


def centred_depthwise_conv(x, w, bias):
    k = w.shape[0]
    y = lax.conv_general_dilated(x, w[:, None, :].astype(x.dtype), window_strides=(1,),
                                 padding=[(k // 2, k // 2)], dimension_numbers=("NWC", "WIO", "NWC"),
                                 feature_group_count=x.shape[-1])
    return y + bias.astype(x.dtype)


def fourier_mix(u):
    b, l, _ = u.shape
    ug = u.astype(jnp.float32).reshape(b, l, FNET_GROUPS, FNET_GROUP_W)
    f = jnp.fft.fft2(ug, axes=(1, 3), norm="ortho")
    return jnp.real(f).reshape(b, l, FNET_W).astype(u.dtype)


def ssd_inputs(u_xbc, u_dt, conv_w, conv_b, dt_bias):
    b, l = u_xbc.shape[:2]
    xbc = jax.nn.silu(centred_depthwise_conv(u_xbc, conv_w, conv_b)).astype(jnp.float32)
    gn = SSD_GROUPS * SSD_STATE
    xs = xbc[..., :SSD_INNER].reshape(b, l, SSD_GROUPS, SSD_HPG, SSD_HEAD_DIM)
    bm = xbc[..., SSD_INNER:SSD_INNER + gn].reshape(b, l, SSD_GROUPS, SSD_STATE)
    cm = xbc[..., SSD_INNER + gn:].reshape(b, l, SSD_GROUPS, SSD_STATE)
    dt = jax.nn.softplus(u_dt.astype(jnp.float32).reshape(b, l, 2, SSD_GROUPS, SSD_HPG) + dt_bias)
    return xs, bm, cm, dt


def ssd_scan(xs, dt, a, bm, cm, h0):
    b, l = xs.shape[:2]
    nc, q = l // SSD_CHUNK, SSD_CHUNK
    xdt = (xs * dt[..., None]).reshape(b, nc, q, SSD_GROUPS, SSD_HPG, SSD_HEAD_DIM)
    a_cum = jnp.cumsum((dt * a).reshape(b, nc, q, SSD_GROUPS, SSD_HPG), axis=2)
    bc = bm.reshape(b, nc, q, SSD_GROUPS, SSD_STATE)
    cc = cm.reshape(b, nc, q, SSD_GROUPS, SSD_STATE)
    lower = jnp.tril(jnp.ones((q, q), dtype=bool))
    seg = a_cum[:, :, :, None] - a_cum[:, :, None, :]
    decay = jnp.exp(jnp.where(lower[:, :, None, None], seg, NEG_INF))
    cb = jnp.einsum("bclgn,bcsgn->bclsg", cc, bc)
    y_diag = jnp.einsum("bclsgr,bcsgrp->bclgrp", cb[..., None] * decay, xdt)
    decay_to_end = jnp.exp(a_cum[:, :, -1:] - a_cum)
    states = jnp.einsum("bcsgn,bcsgrp->bcgrpn", bc, xdt * decay_to_end[..., None])
    chunk_decay = jnp.exp(a_cum[:, :, -1])

    def carry_state(h, inp):
        s_c, d_c = inp
        return h * d_c[..., None, None] + s_c, h

    h_last, h_enter = lax.scan(carry_state, h0, (jnp.moveaxis(states, 1, 0), jnp.moveaxis(chunk_decay, 1, 0)))
    h_enter = jnp.moveaxis(h_enter, 0, 1)
    y_off = jnp.einsum("bclgn,bcgrpn->bclgrp", cc, h_enter) * jnp.exp(a_cum)[..., None]
    return (y_diag + y_off).reshape(b, l, SSD_GROUPS, SSD_HPG, SSD_HEAD_DIM), h_last


def bidir_ssd(xs, bm, cm, dt, a, h0_fwd, h0_bwd):
    y_f, h_f = ssd_scan(xs, dt[:, :, 0], a[0], bm, cm, h0_fwd)
    flip = lambda t: jnp.flip(t, axis=1)
    y_b, h_b = ssd_scan(flip(xs), flip(dt[:, :, 1]), a[1], flip(bm), flip(cm), h0_bwd)
    return y_f + flip(y_b), h_f, h_b


def ssd_output(y, xs, z, d_skip, norm_g):
    b, l = y.shape[:2]
    y = (y + d_skip[..., None] * xs).reshape(b, l, SSD_INNER) * jax.nn.silu(z.astype(jnp.float32))
    yg = y.reshape(b, l, SSD_GROUPS, SSD_INNER // SSD_GROUPS)
    yg = yg * lax.rsqrt(jnp.mean(yg * yg, axis=-1, keepdims=True) + NORM_EPS)
    return (yg.reshape(b, l, SSD_INNER) * norm_g.astype(jnp.float32)).astype(z.dtype)


def even_mixer(uc, ul, w_in, conv_w, conv_b, dt_bias, a_log, d_skip, ssd_norm_g, w_out, need_ctx):
    s1 = FNET_W
    s2 = s1 + SSD_INNER
    s3 = s2 + SSD_CONV_DIM
    fc, zc, xbcc, dtc = jnp.split(uc @ w_in, [s1, s2, s3], axis=-1)
    fl, zl, xbcl, dtl = jnp.split(ul @ w_in, [s1, s2, s3], axis=-1)
    a = -jnp.exp(a_log.astype(jnp.float32)).reshape(2, SSD_GROUPS, SSD_HPG)
    dtb = dt_bias.astype(jnp.float32).reshape(2, SSD_GROUPS, SSD_HPG)
    dsk = d_skip.astype(jnp.float32).reshape(SSD_GROUPS, SSD_HPG)
    xs_c, b_c, c_c, dt_c = ssd_inputs(xbcc, dtc, conv_w, conv_b, dtb)
    xs_l, b_l, c_l, dt_l = ssd_inputs(xbcl, dtl, conv_w, conv_b, dtb)
    h0 = jnp.zeros((uc.shape[0], SSD_GROUPS, SSD_HPG, SSD_HEAD_DIM, SSD_STATE), jnp.float32)
    y_c, h_f, h_b = bidir_ssd(xs_c, b_c, c_c, dt_c, a, h0, h0)
    y_l, _, _ = bidir_ssd(xs_l, b_l, c_l, dt_l, a, h_f, h_b)
    ol = jnp.concatenate([fourier_mix(fl), ssd_output(y_l, xs_l, zl, dsk, ssd_norm_g)], axis=-1) @ w_out
    oc = None
    if need_ctx:
        oc = jnp.concatenate([fourier_mix(fc), ssd_output(y_c, xs_c, zc, dsk, ssd_norm_g)], axis=-1) @ w_out
    return oc, ol


def context_attention(qc, kc, vc, sink):
    b, n = qc.shape[:2]
    s = jnp.einsum("bqhgd,bkhd->bhgqk", qc, kc).astype(jnp.float32) * (HEAD_DIM ** -0.5)
    if sink is not None:
        s_sink = jnp.broadcast_to(sink.astype(jnp.float32)[None, :, :, None, None], s.shape[:-1] + (1,))
        s = jnp.concatenate([s, s_sink], axis=-1)
    p = jax.nn.softmax(s, axis=-1)[..., :kc.shape[1]]
    o = jnp.einsum("bhgqk,bkhd->bqhgd", p.astype(vc.dtype), vc)
    return o.reshape(b, n, -1)


def window_attention(q, k, v, kc, vc, sink):
    b, l = q.shape[:2]
    nb = l // WIN_BLOCK
    n_ctx = kc.shape[1]
    qb = q.reshape(b, nb, WIN_BLOCK, WIN_KV_HEADS, WIN_GQA, HEAD_DIM)

    def band(t):
        tb = jnp.pad(t.reshape(b, nb, WIN_BLOCK, WIN_KV_HEADS, HEAD_DIM), ((0, 0), (1, 1), (0, 0), (0, 0), (0, 0)))
        return jnp.concatenate([tb[:, :-2], tb[:, 1:-1], tb[:, 2:]], axis=2)

    kb, vb = band(k), band(v)
    scale = HEAD_DIM ** -0.5
    s_loc = jnp.einsum("bnqhgd,bnshd->bnhgqs", qb, kb).astype(jnp.float32) * scale
    qpos = jnp.arange(nb)[:, None] * WIN_BLOCK + jnp.arange(WIN_BLOCK)[None]
    kpos = (jnp.arange(nb)[:, None] - 1) * WIN_BLOCK + jnp.arange(3 * WIN_BLOCK)[None]
    valid = ((jnp.abs(qpos[:, :, None] - kpos[:, None, :]) <= WIN_RADIUS)
             & (kpos[:, None, :] >= 0) & (kpos[:, None, :] < l))
    s_loc = jnp.where(valid[None, :, None, None], s_loc, NEG_INF)
    s_ctx = jnp.einsum("bnqhgd,bchd->bnhgqc", qb, kc).astype(jnp.float32) * scale
    s_sink = jnp.broadcast_to(sink.astype(jnp.float32).reshape(WIN_KV_HEADS, WIN_GQA)[None, None, :, :, None, None],
                              s_loc.shape[:-1] + (1,))
    p = jax.nn.softmax(jnp.concatenate([s_loc, s_ctx, s_sink], axis=-1), axis=-1)
    p_loc = p[..., :3 * WIN_BLOCK].astype(v.dtype)
    p_ctx = p[..., 3 * WIN_BLOCK:3 * WIN_BLOCK + n_ctx].astype(v.dtype)
    o = jnp.einsum("bnhgqs,bnshd->bnqhgd", p_loc, vb) + jnp.einsum("bnhgqc,bchd->bnqhgd", p_ctx, vc)
    return o.reshape(b, l, WIN_Q_HEADS * HEAD_DIM)


def neighbourhood_attention(q, k, v, kc, vc, rpb):
    b, l, h, dh = q.shape
    rows = l // GRID_W
    kr = min(NA_ROWS, rows)
    r = jnp.arange(rows)
    row_idx = jnp.clip(r - kr // 2, 0, rows - kr)[:, None] + jnp.arange(kr)[None]
    cols = jnp.arange(GRID_W)
    col_start = jnp.clip(cols - NA_COLS // 2, 0, GRID_W - NA_COLS)
    col_ok = (cols[None] >= col_start[:, None]) & (cols[None] < col_start[:, None] + NA_COLS)
    qg = q.reshape(b, rows, GRID_W, h, dh)
    kg = k.reshape(b, rows, GRID_W, h, dh)[:, row_idx]
    vg = v.reshape(b, rows, GRID_W, h, dh)[:, row_idx]
    scale = HEAD_DIM ** -0.5
    s_loc = jnp.einsum("brwhd,brkuhd->bhrwku", qg, kg).astype(jnp.float32) * scale
    row_off = row_idx - r[:, None] + (NA_ROWS - 1)
    col_off = jnp.clip(cols[None] - cols[:, None] + (NA_COLS - 1), 0, 2 * NA_COLS - 2)
    bias = rpb.astype(jnp.float32)[:, row_off[:, None, :, None], col_off[None, :, None, :]]
    s_loc = jnp.where(col_ok[:, None, :], s_loc + bias[None], NEG_INF)
    s_ctx = jnp.einsum("brwhd,bchd->bhrwc", qg, kc).astype(jnp.float32) * scale
    n_loc = kr * GRID_W
    p = jax.nn.softmax(jnp.concatenate([s_loc.reshape(b, h, rows, GRID_W, n_loc), s_ctx], axis=-1), axis=-1)
    p_loc = p[..., :n_loc].reshape(b, h, rows, GRID_W, kr, GRID_W).astype(v.dtype)
    p_ctx = p[..., n_loc:].astype(v.dtype)
    o = jnp.einsum("bhrwku,brkuhd->brwhd", p_loc, vg) + jnp.einsum("bhrwc,bchd->brwhd", p_ctx, vc)
    return o.reshape(b, l, h * dh)


def odd_mixer(uc, ul, w_in, q_norm_win, k_norm_win, sink_win, q_norm_na, k_norm_na, rpb_na, w_out, cos, sin, need_ctx):
    wq = WIN_Q_HEADS * HEAD_DIM
    wk = WIN_KV_HEADS * HEAD_DIM
    nh = NA_HEADS * HEAD_DIM
    splits = [wq, wq + wk, wq + 2 * wk, wq + 2 * wk + nh, wq + 2 * wk + 2 * nh]

    def heads(p):
        b, l = p.shape[:2]
        qw, kw, vw, qn, kn, vn = jnp.split(p, splits, axis=-1)
        qw = rms_norm(qw.reshape(b, l, WIN_Q_HEADS, HEAD_DIM), q_norm_win)
        kw = rms_norm(kw.reshape(b, l, WIN_KV_HEADS, HEAD_DIM), k_norm_win)
        vw = vw.reshape(b, l, WIN_KV_HEADS, HEAD_DIM)
        qn = rms_norm(qn.reshape(b, l, NA_HEADS, HEAD_DIM), q_norm_na)
        kn = rms_norm(kn.reshape(b, l, NA_HEADS, HEAD_DIM), k_norm_na)
        vn = vn.reshape(b, l, NA_HEADS, HEAD_DIM)
        return qw, kw, vw, qn, kn, vn

    qwc, kwc, vwc, qnc, knc, vnc = heads(uc @ w_in)
    qwl, kwl, vwl, qnl, knl, vnl = heads(ul @ w_in)
    qwl = apply_axial_rope(qwl, cos, sin)
    kwl = apply_axial_rope(kwl, cos, sin)
    ol = jnp.concatenate([window_attention(qwl, kwl, vwl, kwc, vwc, sink_win),
                          neighbourhood_attention(qnl, knl, vnl, knc, vnc, rpb_na)], axis=-1) @ w_out
    oc = None
    if need_ctx:
        b, n = uc.shape[:2]
        oc = jnp.concatenate([
            context_attention(qwc.reshape(b, n, WIN_KV_HEADS, WIN_GQA, HEAD_DIM), kwc, vwc,
                              sink_win.reshape(WIN_KV_HEADS, WIN_GQA)),
            context_attention(qnc.reshape(b, n, NA_HEADS, 1, HEAD_DIM), knc, vnc, None)], axis=-1) @ w_out
    return oc, ol


def setup_inputs(seed: int = 0) -> dict:
    key = jax.random.key(seed)
    ks = jax.random.split(key, 32)
    f32 = jnp.float32
    D = D_MODEL

    def nrm(k, shape, s):
        return jax.random.normal(k, shape, f32) * s

    dt0 = jnp.exp(jax.random.uniform(ks[14], (N_EVEN, 2, SSD_HEADS), f32, math.log(1e-3), math.log(1e-1)))
    return {
        "x": nrm(ks[0], (BATCH, SEQ, D), 1.0),
        "c": nrm(ks[1], (BATCH, D), 1.0),
        "ctx": nrm(ks[2], (BATCH, CTX_LEN, D), 1.0),
        "c_ctx": nrm(ks[3], (D,), 1.0),
        "w_mod": nrm(ks[4], (DEPTH, D, 6 * D), 0.5 * D ** -0.5),
        "b_mod": nrm(ks[5], (DEPTH, 6 * D), 0.01),
        "norm_mix_g": 1.0 + nrm(ks[6], (DEPTH, D), 0.02),
        "norm_ff_g": 1.0 + nrm(ks[7], (DEPTH, D), 0.02),
        "w_ff1": nrm(ks[8], (DEPTH, D, D_FF), D ** -0.5),
        "w_ff2": nrm(ks[9], (DEPTH, D_FF, D), D_FF ** -0.5),
        "w_in_even": nrm(ks[10], (N_EVEN, D, EVEN_IN), D ** -0.5),
        "conv_w": nrm(ks[11], (N_EVEN, SSD_CONV_W, SSD_CONV_DIM), SSD_CONV_W ** -0.5),
        "conv_b": nrm(ks[12], (N_EVEN, SSD_CONV_DIM), 0.01),
        "dt_bias": dt0 + jnp.log(-jnp.expm1(-dt0)),
        "a_log": jnp.log(jax.random.uniform(ks[15], (N_EVEN, 2, SSD_HEADS), f32, 1.0, 16.0)),
        "d_skip": 1.0 + nrm(ks[16], (N_EVEN, SSD_HEADS), 0.02),
        "ssd_norm_g": 1.0 + nrm(ks[17], (N_EVEN, SSD_INNER), 0.02),
        "w_out_even": nrm(ks[18], (N_EVEN, EVEN_MIX, D), EVEN_MIX ** -0.5),
        "w_in_odd": nrm(ks[19], (N_ODD, D, ODD_IN), D ** -0.5),
        "q_norm_win": 1.0 + nrm(ks[20], (N_ODD, HEAD_DIM), 0.02),
        "k_norm_win": 1.0 + nrm(ks[21], (N_ODD, HEAD_DIM), 0.02),
        "sink_win": nrm(ks[22], (N_ODD, WIN_Q_HEADS), 0.5),
        "q_norm_na": 1.0 + nrm(ks[23], (N_ODD, HEAD_DIM), 0.02),
        "k_norm_na": 1.0 + nrm(ks[24], (N_ODD, HEAD_DIM), 0.02),
        "rpb_na": nrm(ks[25], (N_ODD, NA_HEADS, 2 * NA_ROWS - 1, 2 * NA_COLS - 1), 0.02),
        "w_out_odd": nrm(ks[26], (N_ODD, ODD_MIX, D), ODD_MIX ** -0.5),
    }


def reference(x, c, ctx, c_ctx, w_mod, b_mod, norm_mix_g, norm_ff_g, w_ff1, w_ff2,
              w_in_even, conv_w, conv_b, dt_bias, a_log, d_skip, ssd_norm_g, w_out_even,
              w_in_odd, q_norm_win, k_norm_win, sink_win, q_norm_na, k_norm_na, rpb_na, w_out_odd):
    hl, hc = x, ctx
    cos, sin = axial_rope_tables(x.shape[1])
    silu_c = jax.nn.silu(c)
    silu_cc = jax.nn.silu(c_ctx)
    for i in range(DEPTH):
        need_ctx = i < DEPTH - 1
        mod_l = (silu_c @ w_mod[i] + b_mod[i])[:, None, :]
        mod_c = (silu_cc @ w_mod[i] + b_mod[i])[None, None, :]
        sh1_l, sc1_l, g1_l, sh2_l, sc2_l, g2_l = jnp.split(mod_l, 6, axis=-1)
        sh1_c, sc1_c, g1_c, sh2_c, sc2_c, g2_c = jnp.split(mod_c, 6, axis=-1)
        ul = modulate(rms_norm(hl, norm_mix_g[i]), sh1_l, sc1_l)
        uc = modulate(rms_norm(hc, norm_mix_g[i]), sh1_c, sc1_c)
        if i % 2 == 0:
            j = i // 2
            oc, ol = even_mixer(uc, ul, w_in_even[j], conv_w[j], conv_b[j], dt_bias[j], a_log[j],
                                d_skip[j], ssd_norm_g[j], w_out_even[j], need_ctx)
        else:
            j = i // 2
            oc, ol = odd_mixer(uc, ul, w_in_odd[j], q_norm_win[j], k_norm_win[j], sink_win[j],
                               q_norm_na[j], k_norm_na[j], rpb_na[j], w_out_odd[j], cos, sin, need_ctx)
        hl = hl + g1_l * ol
        hl = hl + g2_l * sq_relu_mlp(modulate(rms_norm(hl, norm_ff_g[i]), sh2_l, sc2_l), w_ff1[i], w_ff2[i])
        if need_ctx:
            hc = hc + g1_c * oc
            hc = hc + g2_c * sq_relu_mlp(modulate(rms_norm(hc, norm_ff_g[i]), sh2_c, sc2_c), w_ff1[i], w_ff2[i])
    return hl
```

```python
import functools
import math

import numpy as np
import jax
import jax.numpy as jnp
from jax import lax
from jax.experimental import pallas as pl
from jax.experimental.pallas import tpu as pltpu

F32 = jnp.float32
BF16 = jnp.bfloat16
HIGHEST = lax.Precision.HIGHEST

D_MODEL = 1024
DEPTH = 4
GRID_W = 64
D_FF = 4 * D_MODEL
NORM_EPS = 1e-6
NEG_INF = -1e30

FNET_GROUPS = 8
FNET_GROUP_W = 64
FNET_W = FNET_GROUPS * FNET_GROUP_W
SSD_HEAD_DIM = 64
SSD_HEADS = 24
SSD_GROUPS = 4
SSD_HPG = SSD_HEADS // SSD_GROUPS
SSD_STATE = 128
SSD_INNER = SSD_HEADS * SSD_HEAD_DIM
SSD_GROUP_W = SSD_HPG * SSD_HEAD_DIM
SSD_XBC_W = SSD_GROUP_W + 2 * SSD_STATE
SSD_CONV_DIM = SSD_INNER + 2 * SSD_GROUPS * SSD_STATE
SSD_CONV_W = 5
SSD_CHUNK = 128
HEAD_DIM = 64
WIN_Q_HEADS = 8
WIN_KV_HEADS = 2
WIN_GQA = WIN_Q_HEADS // WIN_KV_HEADS
WIN_BLOCK = 128
NA_HEADS = 8
NA_ROWS = 8
NA_COLS = 16
ROPE_THETA = 10000.0

LANES = 128
HALO = 8
ROW_TILE = 512
FF_TILE = 512
DFT_N1 = 128
VMEM_LIMIT = 56 * 1024 * 1024


def _cparams(*sem):
    return pltpu.CompilerParams(dimension_semantics=sem, vmem_limit_bytes=VMEM_LIMIT)


def _silu(x):
    return x / (1.0 + jnp.exp(-x))


def _softplus(x):
    return jnp.maximum(x, 0.0) + jnp.log(1.0 + jnp.exp(-jnp.abs(x)))


def _norm_mod(h, g, shift, scale):
    ms = jnp.mean(h * h, axis=-1, keepdims=True)
    y = h * lax.rsqrt(ms + NORM_EPS) * g
    return y * (1.0 + scale) + shift


def _dot(a, b):
    return jnp.dot(a, b, preferred_element_type=F32)


def _dot_nt(a, b):
    return lax.dot_general(a, b, (((1,), (1,)), ((), ())), preferred_element_type=F32)


def _split_dot(x, w_bf16):
    hi = x.astype(BF16)
    r1 = x - hi.astype(F32)
    mid = r1.astype(BF16)
    lo = (r1 - mid.astype(F32)).astype(BF16)
    return _dot(hi, w_bf16) + _dot(mid, w_bf16) + _dot(lo, w_bf16)


def _mod_body(c_ref, w_ref, b_ref, o_ref):
    s = _silu(c_ref[...])
    o_ref[...] = jnp.dot(s, w_ref[...], precision=HIGHEST, preferred_element_type=F32) + b_ref[...]


def _modulation(cvec, w_mod, b_mod):
    depth, d, n = w_mod.shape
    tn = 1536
    return pl.pallas_call(
        _mod_body,
        grid=(depth, n // tn),
        in_specs=[pl.BlockSpec((8, d), lambda i, j: (0, 0)),
                  pl.BlockSpec((None, d, tn), lambda i, j: (i, 0, j)),
                  pl.BlockSpec((None, 1, tn), lambda i, j: (i, 0, j))],
        out_specs=pl.BlockSpec((None, 8, tn), lambda i, j: (i, 0, j)),
        out_shape=jax.ShapeDtypeStruct((depth, 8, n), F32),
        compiler_params=_cparams("parallel", "parallel"),
        name="modulation",
    )(cvec, w_mod, b_mod.reshape(depth, 1, n))


def _wprod_body(a_ref, b_ref, o_ref):
    o_ref[...] = jnp.dot(a_ref[...], b_ref[...], precision=HIGHEST, preferred_element_type=F32)


def _weight_product(a, b):
    n, m, k = a.shape
    p = b.shape[1]
    return pl.pallas_call(
        _wprod_body,
        grid=(n,),
        in_specs=[pl.BlockSpec((None, m, k), lambda i: (i, 0, 0)),
                  pl.BlockSpec((k, p), lambda i: (0, 0))],
        out_specs=pl.BlockSpec((None, m, p), lambda i: (i, 0, 0)),
        out_shape=jax.ShapeDtypeStruct((n, m, p), F32),
        compiler_params=_cparams("parallel"),
        name="fold_channel_dft",
    )(a, b)


def _mod_spec(which, tiles_per_batch, n_batch, tile_off):
    def imap(m):
        return (which, jnp.minimum((m + tile_off) // tiles_per_batch, n_batch), 0, 0)
    return pl.BlockSpec((None, None, 1, D_MODEL), imap)


def _inproj_even_body(widths, h_ref, g_ref, sh_ref, sc_ref, w_ref, *out_refs):
    u = _norm_mod(h_ref[...], g_ref[...], sh_ref[...], sc_ref[...]).astype(BF16)
    off = 0
    for ref, wd in zip(out_refs, widths):
        ref[...] = _dot(u, w_ref[:, off:off + wd]).astype(ref.dtype)
        off += wd


def _inproj_even(h, gain, modv, w, n_batch, tiles_per_batch):
    rows = h.shape[0]
    widths = (FNET_W, FNET_W, SSD_INNER, SSD_CONV_DIM, LANES)
    dtypes = (F32, F32, BF16, F32, F32)
    tm = ROW_TILE
    return pl.pallas_call(
        functools.partial(_inproj_even_body, widths),
        grid=(rows // tm,),
        in_specs=[pl.BlockSpec((tm, D_MODEL), lambda m: (m, 0)),
                  pl.BlockSpec((1, D_MODEL), lambda m: (0, 0)),
                  _mod_spec(0, tiles_per_batch, n_batch, 0),
                  _mod_spec(1, tiles_per_batch, n_batch, 0),
                  pl.BlockSpec(w.shape, lambda m: (0, 0))],
        out_specs=[pl.BlockSpec((tm, wd), lambda m: (m, 0)) for wd in widths],
        out_shape=[jax.ShapeDtypeStruct((rows, wd), dt) for wd, dt in zip(widths, dtypes)],
        compiler_params=_cparams("parallel"),
        name="inproj_even",
    )(h, gain, modv, modv, w)


def _group_rms(x, gmat_ref, gain):
    ms = _split_dot(x * x, gmat_ref[...])
    return x * lax.rsqrt(ms + NORM_EPS) * gain


def _rope(y, cos, sin_signed):
    w = y.shape[-1]
    reps = w // LANES
    lane = lax.broadcasted_iota(jnp.int32, y.shape, 1)
    first = (lane % 32) < 16
    partner = jnp.where(first, pltpu.roll(y, w - 16, 1), pltpu.roll(y, 16, 1))
    if reps > 1:
        cos = jnp.concatenate([cos] * reps, axis=1)
        sin_signed = jnp.concatenate([sin_signed] * reps, axis=1)
    return y * cos + partner * sin_signed


def _inproj_odd_body(h_ref, g_ref, sh_ref, sc_ref, w_ref, cos_ref, sin_ref, g512_ref, g128_ref,
                     gqw_ref, gkw_ref, gqn_ref, gkn_ref,
                     qw_ref, kw_ref, vw_ref, qn_ref, kn_ref, vn_ref):
    u = _norm_mod(h_ref[...], g_ref[...], sh_ref[...], sc_ref[...]).astype(BF16)
    cos = cos_ref[...]
    sin = sin_ref[...]
    qw = _group_rms(_dot(u, w_ref[:, 0:512]), g512_ref, gqw_ref[...])
    qw_ref[...] = _rope(qw, cos, sin).astype(BF16)
    kw = _group_rms(_dot(u, w_ref[:, 512:640]), g128_ref, gkw_ref[...])
    kw_ref[...] = _rope(kw, cos, sin).astype(BF16)
    vw_ref[...] = _dot(u, w_ref[:, 640:768]).astype(BF16)
    qn_ref[...] = _group_rms(_dot(u, w_ref[:, 768:1280]), g512_ref, gqn_ref[...]).astype(BF16)
    kn_ref[...] = _group_rms(_dot(u, w_ref[:, 1280:1792]), g512_ref, gkn_ref[...]).astype(BF16)
    vn_ref[...] = _dot(u, w_ref[:, 1792:2304]).astype(BF16)


def _inproj_odd(h, gain, modv, w, cos_t, sin_t, gmat512, gmat128, gains, n_batch, tiles_per_batch):
    rows = h.shape[0]
    tm = ROW_TILE
    widths = (512, 128, 128, 512, 512, 512)
    n_lat_tiles = n_batch * tiles_per_batch

    def rope_map(m):
        return (jnp.where(m < n_lat_tiles, m % tiles_per_batch, tiles_per_batch), 0)

    const = lambda a: pl.BlockSpec(a.shape, lambda m: (0,) * a.ndim)
    return pl.pallas_call(
        _inproj_odd_body,
        grid=(rows // tm,),
        in_specs=[pl.BlockSpec((tm, D_MODEL), lambda m: (m, 0)),
                  pl.BlockSpec((1, D_MODEL), lambda m: (0, 0)),
                  _mod_spec(0, tiles_per_batch, n_batch, 0),
                  _mod_spec(1, tiles_per_batch, n_batch, 0),
                  const(w),
                  pl.BlockSpec((tm, LANES), rope_map),
                  pl.BlockSpec((tm, LANES), rope_map),
                  const(gmat512), const(gmat128)] + [const(g) for g in gains],
        out_specs=[pl.BlockSpec((tm, wd), lambda m: (m, 0)) for wd in widths],
        out_shape=[jax.ShapeDtypeStruct((rows, wd), BF16) for wd in widths],
        compiler_params=_cparams("parallel"),
        name="inproj_odd",
    )(h, gain, modv, modv, w, cos_t, sin_t, gmat512, gmat128, *gains)


def _outproj_body(k1, h_ref, gate_ref, a1_ref, a2_ref, w_ref, o_ref):
    acc = _dot(a1_ref[...].astype(BF16), w_ref[0:k1, :])
    acc += _dot(a2_ref[...].astype(BF16), w_ref[k1:, :])
    o_ref[...] = h_ref[...] + gate_ref[...] * acc


def _outproj(h, modv, a1, a2, w, n_batch, tiles_per_batch, tile_off, a1_off, a2_off, n_tiles):
    tm = ROW_TILE
    k1, k2 = a1.shape[1], a2.shape[1]
    return pl.pallas_call(
        functools.partial(_outproj_body, k1),
        grid=(n_tiles,),
        in_specs=[pl.BlockSpec((tm, D_MODEL), lambda m: (m + tile_off, 0)),
                  _mod_spec(2, tiles_per_batch, n_batch, tile_off),
                  pl.BlockSpec((tm, k1), lambda m: (m + a1_off, 0)),
                  pl.BlockSpec((tm, k2), lambda m: (m + a2_off, 0)),
                  pl.BlockSpec(w.shape, lambda m: (0, 0))],
        out_specs=pl.BlockSpec((tm, D_MODEL), lambda m: (m + tile_off, 0)),
        out_shape=jax.ShapeDtypeStruct(h.shape, F32),
        input_output_aliases={0: 0},
        compiler_params=_cparams("parallel"),
        name="outproj",
    )(h, modv, a1, a2, w)


def _ffn_body(h_ref, g_ref, sh_ref, sc_ref, gate_ref, w1_ref, w2_ref, o_ref, u_ref, acc_ref):
    k = pl.program_id(1)

    @pl.when(k == 0)
    def _():
        u_ref[...] = _norm_mod(h_ref[...], g_ref[...], sh_ref[...], sc_ref[...]).astype(BF16)
        acc_ref[...] = jnp.zeros_like(acc_ref)

    a = jnp.maximum(_dot(u_ref[...], w1_ref[...]), 0.0)
    acc_ref[...] += _dot((a * a).astype(BF16), w2_ref[...])

    @pl.when(k == pl.num_programs(1) - 1)
    def _():
        o_ref[...] = h_ref[...] + gate_ref[...] * acc_ref[...]


def _ffn(h, gain, modv, w1, w2, n_batch, tiles_per_batch, n_tiles):
    tm, tf = ROW_TILE, FF_TILE
    return pl.pallas_call(
        _ffn_body,
        grid=(n_tiles, D_FF // tf),
        in_specs=[pl.BlockSpec((tm, D_MODEL), lambda m, k: (m, 0)),
                  pl.BlockSpec((1, D_MODEL), lambda m, k: (0, 0)),
                  pl.BlockSpec((None, None, 1, D_MODEL),
                               lambda m, k: (3, jnp.minimum(m // tiles_per_batch, n_batch), 0, 0)),
                  pl.BlockSpec((None, None, 1, D_MODEL),
                               lambda m, k: (4, jnp.minimum(m // tiles_per_batch, n_batch), 0, 0)),
                  pl.BlockSpec((None, None, 1, D_MODEL),
                               lambda m, k: (5, jnp.minimum(m // tiles_per_batch, n_batch), 0, 0)),
                  pl.BlockSpec((D_MODEL, tf), lambda m, k: (0, k)),
                  pl.BlockSpec((tf, D_MODEL), lambda m, k: (k, 0))],
        out_specs=pl.BlockSpec((tm, D_MODEL), lambda m, k: (m, 0)),
        out_shape=jax.ShapeDtypeStruct((n_tiles * tm, D_MODEL), F32),
        scratch_shapes=[pltpu.VMEM((tm, D_MODEL), BF16), pltpu.VMEM((tm, D_MODEL), F32)],
        compiler_params=_cparams("parallel", "arbitrary"),
        name="ffn",
    )(h, gain, modv, modv, modv, w1, w2)


def _dft_a_body(reps, xr_ref, xi_ref, m_ref, c_ref, s_ref, br_ref, bi_ref):
    n1 = DFT_N1
    x = jnp.concatenate([xr_ref[...], xi_ref[...]], axis=0).astype(BF16)
    a = _dot(m_ref[...], x)
    ar, ai = a[:n1], a[n1:]
    c = jnp.concatenate([c_ref[...]] * reps, axis=1)
    s = jnp.concatenate([s_ref[...]] * reps, axis=1)
    br_ref[...] = ar * c + ai * s
    bi_ref[...] = ai * c - ar * s


def _dft_stage_a(xr, xi, mmat, twc, tws, n_batch, n2):
    n1 = DFT_N1
    tn = FNET_W
    reps = tn // LANES
    return pl.pallas_call(
        functools.partial(_dft_a_body, reps),
        grid=(n_batch, n2),
        in_specs=[pl.BlockSpec((n1, tn), lambda b, j: (b, j)),
                  pl.BlockSpec((n1, tn), lambda b, j: (b, j)),
                  pl.BlockSpec((2 * n1, 2 * n1), lambda b, j: (0, 0)),
                  pl.BlockSpec((n1, LANES), lambda b, j: (0, j)),
                  pl.BlockSpec((n1, LANES), lambda b, j: (0, j))],
        out_specs=[pl.BlockSpec((None, n1, tn), lambda b, j: (b, 0, j))] * 2,
        out_shape=[jax.ShapeDtypeStruct((n_batch, n1, n2 * FNET_W), F32)] * 2,
        compiler_params=_cparams("parallel", "parallel"),
        name="seq_dft_stage_a",
    )(xr, xi, mmat, twc, tws)


def _dft_b_body(br_ref, bi_ref, m_ref, o_ref):
    x = jnp.concatenate([br_ref[...], bi_ref[...]], axis=0).astype(BF16)
    res = _dot(m_ref[...], x)
    o_ref[...] = res.reshape(o_ref.shape)


def _dft_stage_b(br, bi, mmat, n_batch, n2):
    n1 = DFT_N1
    kb = 8
    return pl.pallas_call(
        _dft_b_body,
        grid=(n_batch, n1 // kb),
        in_specs=[pl.BlockSpec((None, kb * n2, FNET_W), lambda b, j: (b, j, 0)),
                  pl.BlockSpec((None, kb * n2, FNET_W), lambda b, j: (b, j, 0)),
                  pl.BlockSpec(mmat.shape, lambda b, j: (0, 0))],
        out_specs=pl.BlockSpec((None, n2, kb, FNET_W), lambda b, j: (b, 0, j, 0)),
        out_shape=jax.ShapeDtypeStruct((n_batch, n2, n1, FNET_W), F32),
        compiler_params=_cparams("parallel", "parallel"),
        name="seq_dft_stage_b",
    )(br, bi, mmat)


def _ctx_dft_body(xr_ref, xi_ref, m_ref, o_ref):
    x = jnp.concatenate([xr_ref[...], xi_ref[...]], axis=0).astype(BF16)
    o_ref[...] = _dot(m_ref[...], x)


def _ctx_dft(fr, fi, mmat, n_batch, n_ctx, row_block_off):
    return pl.pallas_call(
        _ctx_dft_body,
        grid=(n_batch,),
        in_specs=[pl.BlockSpec((n_ctx, FNET_W), lambda b: (row_block_off + b, 0)),
                  pl.BlockSpec((n_ctx, FNET_W), lambda b: (row_block_off + b, 0)),
                  pl.BlockSpec(mmat.shape, lambda b: (0, 0))],
        out_specs=pl.BlockSpec((n_ctx, FNET_W), lambda b: (b, 0)),
        out_shape=jax.ShapeDtypeStruct((n_batch * n_ctx, FNET_W), F32),
        compiler_params=_cparams("parallel"),
        name="ctx_dft",
    )(fr, fi, mmat)


def _dft_tables(seq, n_ctx):
    n1 = DFT_N1
    n2 = seq // n1
    k1 = np.arange(n1)
    ang1 = 2.0 * np.pi * np.outer(k1, k1) / n1
    c1, s1 = np.cos(ang1), np.sin(ang1)
    m_a = np.block([[c1, s1], [-s1, c1]])
    ang_t = 2.0 * np.pi * np.outer(k1, np.arange(n2)) / seq
    twc = np.repeat(np.cos(ang_t), LANES, axis=1)
    tws = np.repeat(np.sin(ang_t), LANES, axis=1)
    k2 = np.arange(n2)
    ang2 = 2.0 * np.pi * np.outer(k2, k2) / n2
    scale = 1.0 / math.sqrt(seq)
    c2, s2 = np.cos(ang2) * scale, np.sin(ang2) * scale
    eye = np.eye(8)
    m_b = np.concatenate([np.einsum("kl,ab->kabl", c2, eye).reshape(n2 * 8, 8 * n2),
                          np.einsum("kl,ab->kabl", s2, eye).reshape(n2 * 8, 8 * n2)], axis=1)
    kc = np.arange(n_ctx)
    angc = 2.0 * np.pi * np.outer(kc, kc) / n_ctx
    m_c = np.concatenate([np.cos(angc), np.sin(angc)], axis=1) / math.sqrt(n_ctx)
    ch = np.arange(FNET_GROUP_W)
    angg = 2.0 * np.pi * np.outer(ch, ch) / FNET_GROUP_W
    eg = np.eye(FNET_GROUPS)
    chan = np.concatenate([np.kron(eg, np.cos(angg)), -np.kron(eg, np.sin(angg))], axis=1)
    chan = chan / math.sqrt(FNET_GROUP_W)
    return (jnp.asarray(m_a, BF16), jnp.asarray(twc, F32), jnp.asarray(tws, F32),
            jnp.asarray(m_b, BF16), jnp.asarray(m_c, BF16), jnp.asarray(chan, F32))


def _ssd_body(n_ctx_chunks, n_steps, xm_ref, xp_ref, xn_ref, dt_ref, z_ref, cw_ref, cb_ref, dtb_ref,
              alog_ref, dsk_ref, ng_ref, tri_ref, e_ref, o_ref, yf_ref, h_ref):
    g = pl.program_id(1)
    p = pl.program_id(2)
    s = pl.program_id(3)
    q = SSD_CHUNK
    pos = _ssd_pos(p, s, n_ctx_chunks, n_steps)
    first = (pos == 0) | (pos == n_ctx_chunks)
    last = (pos == n_ctx_chunks - 1) | (pos == n_steps - 1)

    @pl.when(s == 0)
    def _():
        h_ref[...] = jnp.zeros_like(h_ref)

    prev = jnp.where(first, 0.0, xp_ref[...])
    nxt = jnp.where(last, 0.0, xn_ref[...])
    xfull = jnp.concatenate([prev, xm_ref[...], nxt], axis=0)
    conv = cb_ref[...]
    for k in range(SSD_CONV_W):
        lo = HALO - SSD_CONV_W // 2 + k
        conv = conv + xfull[lo:lo + q, :] * cw_ref[k:k + 1, :]
    act = _silu(conv)
    xs = act[:, :SSD_GROUP_W]
    bm = act[:, SSD_GROUP_W:SSD_GROUP_W + SSD_STATE]
    cm = act[:, SSD_GROUP_W + SSD_STATE:]

    j = g * 2 + p
    dsel = pltpu.roll(dt_ref[...], (LANES - 8 * j) % LANES, 1)
    dtv = _softplus(dsel + dtb_ref[pl.ds(j, 1), :])
    lane = lax.broadcasted_iota(jnp.int32, (1, LANES), 1)
    a_row = jnp.where(lane < SSD_HPG, -jnp.exp(alog_ref[pl.ds(j, 1), :]), 0.0)
    adt = dtv * a_row
    tri = tri_ref[...]
    acum = _split_dot_rhs(tri.astype(BF16), adt)
    acum_t = acum.T
    end_row = jnp.where(p == 0, acum[q - 1:q, :], acum[0:1, :])
    dte = jnp.exp(end_row - acum)
    ea = jnp.exp(acum)
    emat = e_ref[...]
    dt_x = _split_dot(dtv, emat)
    dte_x = _split_dot(dte, emat)
    ea_x = _split_dot(ea, emat)
    cd_x = _split_dot(jnp.broadcast_to(jnp.exp(end_row), (16, LANES)), emat)[0:1, :]

    xdt = xs * dt_x
    cb16 = cm.astype(BF16)
    cbm = _dot_nt(cb16, bm.astype(BF16))
    allowed = tri > 0.0
    lane_q = lax.broadcasted_iota(jnp.int32, (q, LANES), 1)
    lo_half = lane_q < SSD_HEAD_DIM
    ys = []
    for m in range(SSD_HPG // 2):
        xpair = xdt[:, LANES * m:LANES * (m + 1)]
        acc = None
        for half in range(2):
            r = 2 * m + half
            seg = acum[:, r:r + 1] - acum_t[r:r + 1, :]
            dec = jnp.exp(jnp.where(allowed, seg, NEG_INF))
            mm = (cbm * dec).astype(BF16)
            xh = jnp.where(lo_half if half == 0 else ~lo_half, xpair, 0.0).astype(BF16)
            t = _dot(mm, xh)
            acc = t if acc is None else acc + t
        ys.append(acc)
    y_diag = jnp.concatenate(ys, axis=1)

    h = h_ref[...]
    y_off = _dot(cb16, h.astype(BF16)) * ea_x
    states = _dot(bm.T.astype(BF16), (xdt * dte_x).astype(BF16))
    h_ref[...] = h * cd_x + states
    y = y_diag + y_off

    row0 = pl.multiple_of(pos * q, q)

    @pl.when(p == 0)
    def _():
        yf_ref[pl.ds(row0, q), :] = y

    @pl.when(p == 1)
    def _():
        yt = yf_ref[pl.ds(row0, q), :] + y + dsk_ref[...] * xs
        yt = yt * _silu(z_ref[...].astype(F32))
        yt = yt * lax.rsqrt(jnp.mean(yt * yt, axis=-1, keepdims=True) + NORM_EPS) * ng_ref[...]
        o_ref[...] = yt.astype(o_ref.dtype)


def _split_dot_rhs(w_bf16, x):
    hi = x.astype(BF16)
    r1 = x - hi.astype(F32)
    mid = r1.astype(BF16)
    lo = (r1 - mid.astype(F32)).astype(BF16)
    return _dot(w_bf16, hi) + _dot(w_bf16, mid) + _dot(w_bf16, lo)


def _ssd_pos(p, s, n_ctx_chunks, n_steps):
    bwd = jnp.where(s < n_ctx_chunks, n_ctx_chunks - 1 - s, n_steps - 1 + n_ctx_chunks - s)
    return jnp.where(p == 0, s, bwd)


def _ssd(xbc, dt, z, conv_w, conv_b, dtb_tab, alog_tab, dsk_x, ng, tri, emat, n_batch, seq, n_ctx):
    rows = xbc.shape[0]
    q = SSD_CHUNK
    ncc, ncl = n_ctx // q, seq // q
    n_steps = ncc + ncl
    lat_blocks = n_batch * ncl
    per = q // HALO

    def rb(b, p, s):
        pos = _ssd_pos(p, s, ncc, n_steps)
        return jnp.where(pos < ncc, lat_blocks + b * ncc + pos, b * ncl + pos - ncc)

    def rb_out(b, p, s):
        return rb(b, 1, jnp.where(p == 0, 0, s))

    last_halo = rows // HALO - 1
    return pl.pallas_call(
        functools.partial(_ssd_body, ncc, n_steps),
        grid=(n_batch, SSD_GROUPS, 2, n_steps),
        in_specs=[pl.BlockSpec((q, SSD_XBC_W), lambda b, g, p, s: (rb(b, p, s), g)),
                  pl.BlockSpec((HALO, SSD_XBC_W),
                               lambda b, g, p, s: (jnp.maximum(rb(b, p, s) * per - 1, 0), g)),
                  pl.BlockSpec((HALO, SSD_XBC_W),
                               lambda b, g, p, s: (jnp.minimum(rb(b, p, s) * per + per, last_halo), g)),
                  pl.BlockSpec((q, LANES), lambda b, g, p, s: (rb(b, p, s), 0)),
                  pl.BlockSpec((q, SSD_GROUP_W), lambda b, g, p, s: (rb_out(b, p, s), g)),
                  pl.BlockSpec((SSD_CONV_W, SSD_XBC_W), lambda b, g, p, s: (0, g)),
                  pl.BlockSpec((1, SSD_XBC_W), lambda b, g, p, s: (0, g)),
                  pl.BlockSpec((8, LANES), lambda b, g, p, s: (0, 0)),
                  pl.BlockSpec((8, LANES), lambda b, g, p, s: (0, 0)),
                  pl.BlockSpec((1, SSD_GROUP_W), lambda b, g, p, s: (0, g)),
                  pl.BlockSpec((1, SSD_GROUP_W), lambda b, g, p, s: (0, g)),
                  pl.BlockSpec((None, q, q), lambda b, g, p, s: (p, 0, 0)),
                  pl.BlockSpec((LANES, SSD_GROUP_W), lambda b, g, p, s: (0, 0))],
        out_specs=pl.BlockSpec((q, SSD_GROUP_W), lambda b, g, p, s: (rb_out(b, p, s), g)),
        out_shape=jax.ShapeDtypeStruct((rows, SSD_INNER), BF16),
        scratch_shapes=[pltpu.VMEM((n_steps * q, SSD_GROUP_W), F32),
                        pltpu.VMEM((SSD_STATE, SSD_GROUP_W), F32)],
        compiler_params=_cparams("arbitrary", "arbitrary", "arbitrary", "arbitrary"),
        name="ssd_bidir",
    )(xbc, xbc, xbc, dt, z, conv_w, conv_b, dtb_tab, alog_tab, dsk_x, ng, tri, emat)


def _stack_heads(qc):
    lane = lax.broadcasted_iota(jnp.int32, qc.shape, 1)
    lo = lane < HEAD_DIM
    zero = jnp.zeros_like(qc)
    return jnp.concatenate([jnp.where(lo, qc, zero), jnp.where(lo, zero, qc)], axis=0)


def _unstack_heads(o, n):
    lane = lax.broadcasted_iota(jnp.int32, (n, LANES), 1)
    return jnp.where(lane < HEAD_DIM, o[:n], o[n:])


def _softmax_pv(scores, values, sink=None):
    m = None
    for sc in scores:
        mx = jnp.max(sc, axis=-1, keepdims=True)
        m = mx if m is None else jnp.maximum(m, mx)
    if sink is not None:
        m = jnp.maximum(m, sink)
    den = None if sink is None else jnp.exp(sink - m)
    acc = None
    for sc, v in zip(scores, values):
        pr = jnp.exp(sc - m)
        sm = jnp.sum(pr, axis=-1, keepdims=True)
        den = sm if den is None else den + sm
        t = _dot(pr.astype(BF16), v)
        acc = t if acc is None else acc + t
    return acc / den


def _win_body(q_ref, kp_ref, ko_ref, kn_ref, vp_ref, vo_ref, vn_ref, kc_ref, vc_ref, sink_ref, o_ref):
    n = pl.program_id(1)
    nb = pl.num_programs(1)
    wb = WIN_BLOCK
    q = q_ref[...]
    lane = lax.broadcasted_iota(jnp.int32, (wb, LANES), 1)
    lo = lane < HEAD_DIM
    parts = []
    for kvh in range(WIN_KV_HEADS):
        for m in range(WIN_GQA):
            qc = q[:, LANES * m:LANES * (m + 1)]
            parts.append(jnp.where(lo if kvh == 0 else ~lo, qc, jnp.zeros_like(qc)))
    qs = jnp.concatenate(parts, axis=0)
    kb = jnp.concatenate([kp_ref[...], ko_ref[...], kn_ref[...]], axis=0)
    vb = jnp.concatenate([vp_ref[...], vo_ref[...], vn_ref[...]], axis=0)
    s_loc = _dot_nt(qs, kb)
    r = lax.broadcasted_iota(jnp.int32, (wb, 3 * wb), 0)
    c = lax.broadcasted_iota(jnp.int32, (wb, 3 * wb), 1)
    pen_prev = jnp.where(n > 0, 0.0, NEG_INF)
    pen_next = jnp.where(n < nb - 1, 0.0, NEG_INF)
    pen = jnp.where(c < wb, jnp.where(c >= r, pen_prev, NEG_INF),
                    jnp.where(c < 2 * wb, 0.0, jnp.where(c - 2 * wb <= r, pen_next, NEG_INF)))
    s_loc = (s_loc.reshape(WIN_Q_HEADS, wb, 3 * wb) + pen[None]).reshape(WIN_Q_HEADS * wb, 3 * wb)
    s_ctx = _dot_nt(qs, kc_ref[...])
    o = _softmax_pv([s_loc, s_ctx], [vb, vc_ref[...]], sink_ref[:, 0:1])
    half = WIN_GQA * wb
    for m in range(WIN_GQA):
        top = o[m * wb:(m + 1) * wb]
        bot = o[half + m * wb:half + (m + 1) * wb]
        o_ref[:, LANES * m:LANES * (m + 1)] = jnp.where(lo, top, bot).astype(o_ref.dtype)


def _window_attention(qw, kw, vw, sink_x, n_batch, seq, n_ctx):
    wb = WIN_BLOCK
    nb = seq // wb
    ctx_blk0 = n_batch * seq // n_ctx
    q_spec = pl.BlockSpec((wb, 512), lambda b, n: (b * nb + n, 0))
    prev = pl.BlockSpec((wb, LANES), lambda b, n: (b * nb + jnp.maximum(n - 1, 0), 0))
    own = pl.BlockSpec((wb, LANES), lambda b, n: (b * nb + n, 0))
    nxt = pl.BlockSpec((wb, LANES), lambda b, n: (b * nb + jnp.minimum(n + 1, nb - 1), 0))
    cspec = pl.BlockSpec((n_ctx, LANES), lambda b, n: (ctx_blk0 + b, 0))
    return pl.pallas_call(
        _win_body,
        grid=(n_batch, nb),
        in_specs=[q_spec, prev, own, nxt, prev, own, nxt, cspec, cspec,
                  pl.BlockSpec(sink_x.shape, lambda b, n: (0, 0))],
        out_specs=pl.BlockSpec((wb, 512), lambda b, n: (b * nb + n, 0)),
        out_shape=jax.ShapeDtypeStruct((n_batch * seq, 512), BF16),
        compiler_params=_cparams("parallel", "parallel"),
        name="window_attention",
    )(qw, kw, kw, kw, vw, vw, vw, kw, vw, sink_x)


def _na_body(n_rows, q_ref, k_ref, v_ref, kc_ref, vc_ref, bias_ref, o_ref):
    r = pl.program_id(1)
    kr = NA_ROWS
    start = pl.multiple_of(jnp.clip(r - kr // 2, 0, n_rows - kr) * GRID_W, GRID_W)
    q = q_ref[...]
    for m in range(NA_HEADS // 2):
        sl = slice(LANES * m, LANES * (m + 1))
        qs = _stack_heads(q[:, sl])
        kw = k_ref[pl.ds(start, kr * GRID_W), sl]
        vw = v_ref[pl.ds(start, kr * GRID_W), sl]
        s_loc = _dot_nt(qs, kw) + bias_ref[2 * GRID_W * m:2 * GRID_W * (m + 1), :]
        s_ctx = _dot_nt(qs, kc_ref[:, sl])
        o = _softmax_pv([s_loc, s_ctx], [vw, vc_ref[:, sl]])
        o_ref[:, sl] = _unstack_heads(o, GRID_W).astype(o_ref.dtype)


def _na_attention(qn, kn, vn, bias_tab, n_batch, seq, n_ctx):
    n_rows = seq // GRID_W
    kr = min(NA_ROWS, n_rows)
    ctx_blk0 = n_batch * seq // n_ctx

    def bias_map(b, r):
        return (jnp.clip(r - kr // 2, 0, n_rows - kr) - r + NA_ROWS - 1, 0, 0)

    return pl.pallas_call(
        functools.partial(_na_body, n_rows),
        grid=(n_batch, n_rows),
        in_specs=[pl.BlockSpec((GRID_W, 512), lambda b, r: (b * n_rows + r, 0)),
                  pl.BlockSpec((seq, 512), lambda b, r: (b, 0)),
                  pl.BlockSpec((seq, 512), lambda b, r: (b, 0)),
                  pl.BlockSpec((n_ctx, 512), lambda b, r: (ctx_blk0 + b, 0)),
                  pl.BlockSpec((n_ctx, 512), lambda b, r: (ctx_blk0 + b, 0)),
                  pl.BlockSpec((None, NA_HEADS * GRID_W, kr * GRID_W), bias_map)],
        out_specs=pl.BlockSpec((GRID_W, 512), lambda b, r: (b * n_rows + r, 0)),
        out_shape=jax.ShapeDtypeStruct((n_batch * seq, 512), BF16),
        compiler_params=_cparams("parallel", "arbitrary"),
        name="neighbourhood_attention",
    )(qn, kn, vn, kn, vn, bias_tab)


def _ctx_attn_body(qw_ref, kw_ref, vw_ref, qn_ref, kn_ref, vn_ref, sink_ref, ow_ref, on_ref):
    n = qw_ref.shape[0]
    qw = qw_ref[...]
    lane = lax.broadcasted_iota(jnp.int32, (n, LANES), 1)
    lo = lane < HEAD_DIM
    for m in range(WIN_GQA):
        qs = _stack_heads(qw[:, LANES * m:LANES * (m + 1)])
        r0, r1 = m * WIN_BLOCK, (WIN_GQA + m) * WIN_BLOCK
        sink = jnp.concatenate([jnp.broadcast_to(sink_ref[r0:r0 + 1, 0:1], (n, 1)),
                                jnp.broadcast_to(sink_ref[r1:r1 + 1, 0:1], (n, 1))], axis=0)
        o = _softmax_pv([_dot_nt(qs, kw_ref[...])], [vw_ref[...]], sink)
        ow_ref[:, LANES * m:LANES * (m + 1)] = jnp.where(lo, o[:n], o[n:]).astype(ow_ref.dtype)
    qn = qn_ref[...]
    for m in range(NA_HEADS // 2):
        sl = slice(LANES * m, LANES * (m + 1))
        qs = _stack_heads(qn[:, sl])
        o = _softmax_pv([_dot_nt(qs, kn_ref[:, sl])], [vn_ref[:, sl]])
        on_ref[:, sl] = _unstack_heads(o, n).astype(on_ref.dtype)


def _ctx_attention(qw, kw, vw, qn, kn, vn, sink_x, n_batch, seq, n_ctx):
    blk0 = n_batch * seq // n_ctx
    wide = pl.BlockSpec((n_ctx, 512), lambda b: (blk0 + b, 0))
    narrow = pl.BlockSpec((n_ctx, LANES), lambda b: (blk0 + b, 0))
    out = pl.BlockSpec((n_ctx, 512), lambda b: (b, 0))
    return pl.pallas_call(
        _ctx_attn_body,
        grid=(n_batch,),
        in_specs=[wide, narrow, narrow, wide, wide, wide,
                  pl.BlockSpec(sink_x.shape, lambda b: (0, 0))],
        out_specs=[out, out],
        out_shape=[jax.ShapeDtypeStruct((n_batch * n_ctx, 512), BF16)] * 2,
        compiler_params=_cparams("parallel"),
        name="context_attention",
    )(qw, kw, vw, qn, kn, vn, sink_x)


def _even_layout():
    gn = SSD_GROUPS * SSD_STATE
    xbc = []
    for g in range(SSD_GROUPS):
        xbc += list(range(g * SSD_GROUP_W, (g + 1) * SSD_GROUP_W))
        xbc += list(range(SSD_INNER + g * SSD_STATE, SSD_INNER + (g + 1) * SSD_STATE))
        xbc += list(range(SSD_INNER + gn + g * SSD_STATE, SSD_INNER + gn + (g + 1) * SSD_STATE))
    dt = np.full((LANES,), 2 * SSD_HEADS, np.int32)
    for g in range(SSD_GROUPS):
        for d in range(2):
            for r in range(SSD_HPG):
                dt[(g * 2 + d) * 8 + r] = d * SSD_HEADS + g * SSD_HPG + r
    return np.asarray(xbc, np.int32), dt


def _win_head_perm():
    cols = []
    for m in range(WIN_GQA):
        for kvh in range(WIN_KV_HEADS):
            h = kvh * WIN_GQA + m
            cols += list(range(h * HEAD_DIM, (h + 1) * HEAD_DIM))
    return np.asarray(cols, np.int32)


def _rope_tables(seq):
    pos = np.arange(seq)
    row = (pos // GRID_W).astype(np.float32)
    col = (pos % GRID_W).astype(np.float32)
    n_freq = HEAD_DIM // 4
    inv = (np.float32(ROPE_THETA) ** (-np.arange(n_freq, dtype=np.float32) / n_freq)).astype(np.float32)
    ar = (row[:, None] * inv).astype(np.float32)
    ac = (col[:, None] * inv).astype(np.float32)
    cos_h = np.concatenate([np.cos(ar), np.cos(ar), np.cos(ac), np.cos(ac)], axis=1)
    sin_h = np.concatenate([-np.sin(ar), np.sin(ar), -np.sin(ac), np.sin(ac)], axis=1)
    ident_c = np.ones((ROW_TILE, HEAD_DIM), np.float32)
    ident_s = np.zeros((ROW_TILE, HEAD_DIM), np.float32)
    cos_t = np.concatenate([cos_h, ident_c], axis=0)
    sin_t = np.concatenate([sin_h, ident_s], axis=0)
    return (jnp.asarray(np.tile(cos_t, (1, 2)), F32), jnp.asarray(np.tile(sin_t, (1, 2)), F32))


def _na_bias_table(rpb, n_rows):
    kr = min(NA_ROWS, n_rows)
    cols = np.arange(GRID_W)
    col_start = np.clip(cols - NA_COLS // 2, 0, GRID_W - NA_COLS)
    col_ok = (cols[None] >= col_start[:, None]) & (cols[None] < col_start[:, None] + NA_COLS)
    col_off = np.clip(cols[None] - cols[:, None] + (NA_COLS - 1), 0, 2 * NA_COLS - 2)
    gathered = rpb.astype(F32)[:, :, col_off]
    gathered = jnp.where(jnp.asarray(col_ok)[None, None], gathered, NEG_INF)
    tabs = []
    for ro0 in range(NA_ROWS):
        rows = np.clip(ro0 + np.arange(kr), 0, 2 * NA_ROWS - 2)
        t = gathered[:, rows]
        tabs.append(jnp.transpose(t, (0, 2, 1, 3)).reshape(NA_HEADS * GRID_W, kr * GRID_W))
    return jnp.stack(tabs, axis=0)


def kernel(x, c, ctx, c_ctx, w_mod, b_mod, norm_mix_g, norm_ff_g, w_ff1, w_ff2, w_in_even, conv_w, conv_b,
           dt_bias, a_log, d_skip, ssd_norm_g, w_out_even, w_in_odd, q_norm_win, k_norm_win, sink_win,
           q_norm_na, k_norm_na, rpb_na, w_out_odd):
    n_batch, seq, d = x.shape
    n_ctx = ctx.shape[1]
    n_lat = n_batch * seq
    tm = ROW_TILE
    tiles_per_batch = seq // tm
    n_lat_tiles = n_lat // tm
    n_ctx_tiles = (n_batch * n_ctx) // tm
    n2 = seq // DFT_N1

    h = jnp.concatenate([x.reshape(n_lat, d), ctx.reshape(n_batch * n_ctx, d)], axis=0)

    cvec = jnp.concatenate([c, c_ctx[None], jnp.zeros((8 - n_batch - 1, d), F32)], axis=0)
    mod_all = _modulation(cvec, w_mod, b_mod)
    mod_all = mod_all.reshape(-1, 8, 6, 1, d)[:, :n_batch + 1].transpose(0, 2, 1, 3, 4)

    m_a, twc, tws, m_b, m_c, chan = _dft_tables(seq, n_ctx)
    xbc_perm, dt_perm = _even_layout()
    s1, s2, s3 = FNET_W, FNET_W + SSD_INNER, FNET_W + SSD_INNER + SSD_CONV_DIM
    w_four = _weight_product(w_in_even[:, :, :s1], chan)
    tri = jnp.asarray(np.stack([np.tril(np.ones((SSD_CHUNK, SSD_CHUNK))),
                                np.triu(np.ones((SSD_CHUNK, SSD_CHUNK)))]), F32)
    emat = np.zeros((LANES, SSD_GROUP_W), np.float32)
    for r in range(SSD_HPG):
        emat[r, r * SSD_HEAD_DIM:(r + 1) * SSD_HEAD_DIM] = 1.0
    emat = jnp.asarray(emat, BF16)

    cos_t, sin_t = _rope_tables(seq)
    gmat512 = jnp.asarray(np.kron(np.eye(8), np.full((HEAD_DIM, HEAD_DIM), 1.0 / HEAD_DIM)), BF16)
    gmat128 = gmat512[:LANES, :LANES]
    win_perm = _win_head_perm()
    wq = WIN_Q_HEADS * HEAD_DIM
    wk = WIN_KV_HEADS * HEAD_DIM

    for i in range(DEPTH):
        need_ctx = i < DEPTH - 1
        modv = mod_all[i]
        j = i // 2
        gain_mix = norm_mix_g[i].reshape(1, d)
        if i % 2 == 0:
            wi = w_in_even[j]
            w_dt = jnp.concatenate([wi[:, s3:], jnp.zeros((d, 1), F32)], axis=1)[:, dt_perm]
            w_all = jnp.concatenate([w_four[j], wi[:, s1:s2], wi[:, s2:s3][:, xbc_perm], w_dt],
                                    axis=1).astype(BF16)
            fr, fi, z, xbc, dtr = _inproj_even(h, gain_mix, modv, w_all, n_batch, tiles_per_batch)

            br, bi = _dft_stage_a(fr.reshape(-1, n2 * FNET_W), fi.reshape(-1, n2 * FNET_W),
                                  m_a, twc, tws, n_batch, n2)
            f_lat = _dft_stage_b(br.reshape(n_batch, seq, FNET_W), bi.reshape(n_batch, seq, FNET_W),
                                 m_b, n_batch, n2).reshape(n_lat, FNET_W)
            f_ctx = _ctx_dft(fr, fi, m_c, n_batch, n_ctx, n_lat // n_ctx)

            pad8 = lambda t: jnp.zeros((8, LANES), F32).at[:, :SSD_HPG].set(
                t.astype(F32).reshape(2, SSD_GROUPS, SSD_HPG).transpose(1, 0, 2).reshape(8, SSD_HPG))
            y_ssd = _ssd(xbc, dtr, z, conv_w[j][:, xbc_perm], conv_b[j][xbc_perm].reshape(1, -1),
                         pad8(dt_bias[j]), pad8(a_log[j]),
                         jnp.repeat(d_skip[j].astype(F32), SSD_HEAD_DIM).reshape(1, -1),
                         ssd_norm_g[j].astype(F32).reshape(1, -1), tri, emat, n_batch, seq, n_ctx)

            w_out = w_out_even[j].astype(BF16)
            h = _outproj(h, modv, f_lat, y_ssd, w_out, n_batch, tiles_per_batch, 0, 0, 0, n_lat_tiles)
            h = _outproj(h, modv, f_ctx, y_ssd, w_out, n_batch, tiles_per_batch, n_lat_tiles, 0,
                         n_lat_tiles, n_ctx_tiles)
        else:
            wi = w_in_odd[j]
            w_all = jnp.concatenate([wi[:, :wq][:, win_perm], wi[:, wq:]], axis=1).astype(BF16)
            tile8 = lambda g, reps, scale=1.0: (jnp.tile(g.astype(F32), reps) * scale).reshape(1, -1)
            gains = (tile8(q_norm_win[j], 8, HEAD_DIM ** -0.5), tile8(k_norm_win[j], 2),
                     tile8(q_norm_na[j], 8, HEAD_DIM ** -0.5), tile8(k_norm_na[j], 8))
            qw, kw, vw, qn, kn, vn = _inproj_odd(h, gain_mix, modv, w_all, cos_t, sin_t, gmat512, gmat128,
                                                 gains, n_batch, tiles_per_batch)
            sink = sink_win[j].astype(F32)
            sink_x = jnp.broadcast_to(jnp.repeat(sink, WIN_BLOCK)[:, None], (WIN_Q_HEADS * WIN_BLOCK, LANES))
            o_win = _window_attention(qw, kw, vw, sink_x, n_batch, seq, n_ctx)
            bias_tab = _na_bias_table(rpb_na[j], seq // GRID_W)
            o_na = _na_attention(qn, kn, vn, bias_tab, n_batch, seq, n_ctx)
            wo = w_out_odd[j]
            w_out = jnp.concatenate([wo[:wq][win_perm], wo[wq:]], axis=0).astype(BF16)
            if need_ctx:
                oc_win, oc_na = _ctx_attention(qw, kw, vw, qn, kn, vn, sink_x, n_batch, seq, n_ctx)
            h = _outproj(h, modv, o_win, o_na, w_out, n_batch, tiles_per_batch, 0, 0, 0, n_lat_tiles)
            if need_ctx:
                h = _outproj(h, modv, oc_win, oc_na, w_out, n_batch, tiles_per_batch, n_lat_tiles, 0, 0,
                             n_ctx_tiles)
        n_tiles = n_lat_tiles + n_ctx_tiles if need_ctx else n_lat_tiles
        h = _ffn(h, norm_ff_g[i].reshape(1, d), modv, w_ff1[i].astype(BF16), w_ff2[i].astype(BF16),
                 n_batch, tiles_per_batch, n_tiles)
    return h[:n_lat].reshape(n_batch, seq, d)
```

```python
import functools
import math

import numpy as np
import jax
import jax.numpy as jnp
from jax import lax
from jax.experimental import pallas as pl
from jax.experimental.pallas import tpu as pltpu

F32 = jnp.float32
BF16 = jnp.bfloat16
HIGHEST = lax.Precision.HIGHEST

D_MODEL = 1024
DEPTH = 4
GRID_W = 64
D_FF = 4 * D_MODEL
NORM_EPS = 1e-6
NEG_INF = -1e30

FNET_GROUPS = 8
FNET_GROUP_W = 64
FNET_W = FNET_GROUPS * FNET_GROUP_W
SSD_HEAD_DIM = 64
SSD_HEADS = 24
SSD_GROUPS = 4
SSD_HPG = SSD_HEADS // SSD_GROUPS
SSD_STATE = 128
SSD_INNER = SSD_HEADS * SSD_HEAD_DIM
SSD_GROUP_W = SSD_HPG * SSD_HEAD_DIM
SSD_XBC_W = SSD_GROUP_W + 2 * SSD_STATE
SSD_CONV_DIM = SSD_INNER + 2 * SSD_GROUPS * SSD_STATE
SSD_CONV_W = 5
SSD_CHUNK = 128
HEAD_DIM = 64
WIN_Q_HEADS = 8
WIN_KV_HEADS = 2
WIN_GQA = WIN_Q_HEADS // WIN_KV_HEADS
WIN_BLOCK = 128
NA_HEADS = 8
NA_ROWS = 8
NA_COLS = 16
ROPE_THETA = 10000.0

LANES = 128
HALO = 8
ROW_TILE = 512
FF_TILE = 4096
DFT_N1 = 128
VMEM_LIMIT = 56 * 1024 * 1024


def _cparams(*sem):
    return pltpu.CompilerParams(dimension_semantics=sem, vmem_limit_bytes=VMEM_LIMIT)


def _silu(x):
    return x / (1.0 + jnp.exp(-x))


def _softplus(x):
    return jnp.maximum(x, 0.0) + jnp.log(1.0 + jnp.exp(-jnp.abs(x)))


def _norm_mod(h, g, shift, scale):
    ms = jnp.mean(h * h, axis=-1, keepdims=True)
    y = h * lax.rsqrt(ms + NORM_EPS) * g
    return y * (1.0 + scale) + shift


def _dot(a, b):
    return jnp.dot(a, b, preferred_element_type=F32)


def _dot_nt(a, b):
    return lax.dot_general(a, b, (((1,), (1,)), ((), ())), preferred_element_type=F32)


def _mod_body(c_ref, w_ref, b_ref, o_ref):
    s = _silu(c_ref[...])
    o_ref[...] = jnp.dot(s, w_ref[...], precision=HIGHEST, preferred_element_type=F32) + b_ref[...]


def _modulation(cvec, w_mod, b_mod):
    depth, d, n = w_mod.shape
    tn = 1536
    return pl.pallas_call(
        _mod_body,
        grid=(depth, n // tn),
        in_specs=[pl.BlockSpec((8, d), lambda i, j: (0, 0)),
                  pl.BlockSpec((None, d, tn), lambda i, j: (i, 0, j)),
                  pl.BlockSpec((None, 1, tn), lambda i, j: (i, 0, j))],
        out_specs=pl.BlockSpec((None, 8, tn), lambda i, j: (i, 0, j)),
        out_shape=jax.ShapeDtypeStruct((depth, 8, n), F32),
        compiler_params=_cparams("parallel", "parallel"),
        name="modulation",
    )(cvec, w_mod, b_mod.reshape(depth, 1, n))


def _wprod_body(a_ref, b_ref, o_ref):
    o_ref[...] = jnp.dot(a_ref[...], b_ref[...], precision=HIGHEST, preferred_element_type=F32)


def _weight_product(a, b):
    n, m, k = a.shape
    p = b.shape[1]
    return pl.pallas_call(
        _wprod_body,
        grid=(n,),
        in_specs=[pl.BlockSpec((None, m, k), lambda i: (i, 0, 0)),
                  pl.BlockSpec((k, p), lambda i: (0, 0))],
        out_specs=pl.BlockSpec((None, m, p), lambda i: (i, 0, 0)),
        out_shape=jax.ShapeDtypeStruct((n, m, p), F32),
        compiler_params=_cparams("parallel"),
        name="fold_channel_dft",
    )(a, b)


def _mod_spec(which, tiles_per_batch, n_batch, tile_off):
    def imap(m):
        return (which, jnp.minimum((m + tile_off) // tiles_per_batch, n_batch), 0, 0)
    return pl.BlockSpec((None, None, 1, D_MODEL), imap)


def _inproj_even_body(widths, h_ref, g_ref, sh_ref, sc_ref, w_ref, *out_refs):
    u = _norm_mod(h_ref[...], g_ref[...], sh_ref[...], sc_ref[...]).astype(BF16)
    off = 0
    for ref, wd in zip(out_refs, widths):
        ref[...] = _dot(u, w_ref[:, off:off + wd]).astype(ref.dtype)
        off += wd


def _inproj_even(h, gain, modv, w, n_batch, tiles_per_batch):
    rows = h.shape[0]
    widths = (FNET_W, FNET_W, SSD_INNER, SSD_CONV_DIM, LANES)
    dtypes = (F32, F32, BF16, F32, F32)
    tm = ROW_TILE
    return pl.pallas_call(
        functools.partial(_inproj_even_body, widths),
        grid=(rows // tm,),
        in_specs=[pl.BlockSpec((tm, D_MODEL), lambda m: (m, 0)),
                  pl.BlockSpec((1, D_MODEL), lambda m: (0, 0)),
                  _mod_spec(0, tiles_per_batch, n_batch, 0),
                  _mod_spec(1, tiles_per_batch, n_batch, 0),
                  _resident(w.shape)],
        out_specs=[pl.BlockSpec((tm, wd), lambda m: (m, 0)) for wd in widths],
        out_shape=[jax.ShapeDtypeStruct((rows, wd), dt) for wd, dt in zip(widths, dtypes)],
        compiler_params=_cparams("parallel"),
        name="inproj_even",
    )(h, gain, modv, modv, w)


def _group_rms(x, gmat_ref, gain):
    sq = x * x
    hi = sq.astype(BF16)
    lo = (sq - hi.astype(F32)).astype(BF16)
    ms = _dot(hi, gmat_ref[...]) + _dot(lo, gmat_ref[...])
    return x * lax.rsqrt(ms + NORM_EPS) * gain


def _rope(y, cos, sin_signed):
    w = y.shape[-1]
    reps = w // LANES
    lane = lax.broadcasted_iota(jnp.int32, y.shape, 1)
    first = (lane % 32) < 16
    partner = jnp.where(first, pltpu.roll(y, w - 16, 1), pltpu.roll(y, 16, 1))
    if reps > 1:
        cos = jnp.concatenate([cos] * reps, axis=1)
        sin_signed = jnp.concatenate([sin_signed] * reps, axis=1)
    return y * cos + partner * sin_signed


def _inproj_odd_body(h_ref, g_ref, sh_ref, sc_ref, w_ref, cos_ref, sin_ref, g512_ref, g128_ref,
                     gqw_ref, gkw_ref, gqn_ref, gkn_ref,
                     qw_ref, kw_ref, vw_ref, qn_ref, kn_ref, vn_ref):
    u = _norm_mod(h_ref[...], g_ref[...], sh_ref[...], sc_ref[...]).astype(BF16)
    cos = cos_ref[...]
    sin = sin_ref[...]
    qw = _group_rms(_dot(u, w_ref[:, 0:512]), g512_ref, gqw_ref[...])
    qw_ref[...] = _rope(qw, cos, sin).astype(BF16)
    kw = _group_rms(_dot(u, w_ref[:, 512:640]), g128_ref, gkw_ref[...])
    kw_ref[...] = _rope(kw, cos, sin).astype(BF16)
    vw_ref[...] = _dot(u, w_ref[:, 640:768]).astype(BF16)
    qn_ref[...] = _group_rms(_dot(u, w_ref[:, 768:1280]), g512_ref, gqn_ref[...]).astype(BF16)
    kn_ref[...] = _group_rms(_dot(u, w_ref[:, 1280:1792]), g512_ref, gkn_ref[...]).astype(BF16)
    vn_ref[...] = _dot(u, w_ref[:, 1792:2304]).astype(BF16)


def _inproj_odd(h, gain, modv, w, cos_t, sin_t, gmat512, gmat128, gains, n_batch, tiles_per_batch):
    rows = h.shape[0]
    tm = ROW_TILE
    widths = (512, 128, 128, 512, 512, 512)
    n_lat_tiles = n_batch * tiles_per_batch

    def rope_map(m):
        return (jnp.where(m < n_lat_tiles, m % tiles_per_batch, tiles_per_batch), 0)

    const = lambda a: pl.BlockSpec(a.shape, lambda m: (0,) * a.ndim)
    return pl.pallas_call(
        _inproj_odd_body,
        grid=(rows // tm,),
        in_specs=[pl.BlockSpec((tm, D_MODEL), lambda m: (m, 0)),
                  pl.BlockSpec((1, D_MODEL), lambda m: (0, 0)),
                  _mod_spec(0, tiles_per_batch, n_batch, 0),
                  _mod_spec(1, tiles_per_batch, n_batch, 0),
                  _resident(w.shape),
                  pl.BlockSpec((tm, LANES), rope_map),
                  pl.BlockSpec((tm, LANES), rope_map),
                  const(gmat512), const(gmat128)] + [const(g) for g in gains],
        out_specs=[pl.BlockSpec((tm, wd), lambda m: (m, 0)) for wd in widths],
        out_shape=[jax.ShapeDtypeStruct((rows, wd), BF16) for wd in widths],
        compiler_params=_cparams("parallel"),
        name="inproj_odd",
    )(h, gain, modv, modv, w, cos_t, sin_t, gmat512, gmat128, *gains)


def _outproj_body(k1, h_ref, gate_ref, a1_ref, a2_ref, w_ref, o_ref):
    acc = _dot(a1_ref[...].astype(BF16), w_ref[0:k1, :])
    acc += _dot(a2_ref[...].astype(BF16), w_ref[k1:, :])
    o_ref[...] = h_ref[...] + gate_ref[...] * acc


def _outproj(h, modv, a1, a2, w, n_batch, tiles_per_batch, tile_off, a1_off, a2_off, n_tiles):
    tm = ROW_TILE
    k1, k2 = a1.shape[1], a2.shape[1]
    return pl.pallas_call(
        functools.partial(_outproj_body, k1),
        grid=(n_tiles,),
        in_specs=[pl.BlockSpec((tm, D_MODEL), lambda m: (m + tile_off, 0)),
                  _mod_spec(2, tiles_per_batch, n_batch, tile_off),
                  pl.BlockSpec((tm, k1), lambda m: (m + a1_off, 0)),
                  pl.BlockSpec((tm, k2), lambda m: (m + a2_off, 0)),
                  _resident(w.shape)],
        out_specs=pl.BlockSpec((tm, D_MODEL), lambda m: (m + tile_off, 0)),
        out_shape=jax.ShapeDtypeStruct(h.shape, F32),
        input_output_aliases={0: 0},
        compiler_params=_cparams("parallel"),
        name="outproj",
    )(h, modv, a1, a2, w)


def _ffn_body(h_ref, g_ref, sh_ref, sc_ref, gate_ref, w1_ref, w2_ref, o_ref):
    h = h_ref[...]
    u = _norm_mod(h, g_ref[...], sh_ref[...], sc_ref[...]).astype(BF16)
    acc = None
    for k in range(D_FF // FF_TILE):
        sl = slice(k * FF_TILE, (k + 1) * FF_TILE)
        a = jnp.maximum(_dot(u, w1_ref[:, sl]), 0.0)
        t = _dot((a * a).astype(BF16), w2_ref[sl, :])
        acc = t if acc is None else acc + t
    o_ref[...] = h + gate_ref[...] * acc


def _resident(shape):
    return pl.BlockSpec(shape, lambda *_: (0,) * len(shape), pipeline_mode=pl.Buffered(1))


def _ffn(h, gain, modv, w1, w2, n_batch, tiles_per_batch, n_tiles):
    tm = ROW_TILE
    return pl.pallas_call(
        _ffn_body,
        grid=(n_tiles,),
        in_specs=[pl.BlockSpec((tm, D_MODEL), lambda m: (m, 0)),
                  pl.BlockSpec((1, D_MODEL), lambda m: (0, 0)),
                  _mod_spec(3, tiles_per_batch, n_batch, 0),
                  _mod_spec(4, tiles_per_batch, n_batch, 0),
                  _mod_spec(5, tiles_per_batch, n_batch, 0),
                  _resident(w1.shape), _resident(w2.shape)],
        out_specs=pl.BlockSpec((tm, D_MODEL), lambda m: (m, 0)),
        out_shape=jax.ShapeDtypeStruct((n_tiles * tm, D_MODEL), F32),
        compiler_params=_cparams("parallel"),
        name="ffn",
    )(h, gain, modv, modv, modv, w1, w2)


def _dft_a_body(xr_ref, xi_ref, m_ref, c_ref, s_ref, br_ref, bi_ref):
    n1 = DFT_N1
    reps = FNET_W // LANES
    for j in range(xr_ref.shape[1]):
        x = jnp.concatenate([xr_ref[:, j, :], xi_ref[:, j, :]], axis=0).astype(BF16)
        a = _dot(m_ref[...], x)
        ar, ai = a[:n1], a[n1:]
        c = jnp.concatenate([c_ref[:, LANES * j:LANES * (j + 1)]] * reps, axis=1)
        s = jnp.concatenate([s_ref[:, LANES * j:LANES * (j + 1)]] * reps, axis=1)
        br_ref[:, j, :] = ar * c + ai * s
        bi_ref[:, j, :] = ai * c - ar * s


def _dft_stage_a(xr, xi, mmat, twc, tws, n_batch, n2):
    n1 = DFT_N1
    lb = 8
    blk = pl.BlockSpec((n1, lb, FNET_W), lambda b, j: (b, j, 0))
    return pl.pallas_call(
        _dft_a_body,
        grid=(n_batch, n2 // lb),
        in_specs=[blk, blk,
                  pl.BlockSpec((2 * n1, 2 * n1), lambda b, j: (0, 0)),
                  pl.BlockSpec((n1, lb * LANES), lambda b, j: (0, j)),
                  pl.BlockSpec((n1, lb * LANES), lambda b, j: (0, j))],
        out_specs=[blk, blk],
        out_shape=[jax.ShapeDtypeStruct((n_batch * n1, n2, FNET_W), F32)] * 2,
        compiler_params=_cparams("parallel", "parallel"),
        name="seq_dft_stage_a",
    )(xr, xi, mmat, twc, tws)


def _dft_b_body(br_ref, bi_ref, m_ref, o_ref):
    x = jnp.concatenate([br_ref[...], bi_ref[...]], axis=0).astype(BF16)
    res = _dot(m_ref[...], x)
    o_ref[...] = res.reshape(o_ref.shape)


def _dft_stage_b(br, bi, mmat, n_batch, n2):
    n1 = DFT_N1
    kb = 8
    return pl.pallas_call(
        _dft_b_body,
        grid=(n_batch, n1 // kb),
        in_specs=[pl.BlockSpec((None, kb * n2, FNET_W), lambda b, j: (b, j, 0)),
                  pl.BlockSpec((None, kb * n2, FNET_W), lambda b, j: (b, j, 0)),
                  pl.BlockSpec(mmat.shape, lambda b, j: (0, 0))],
        out_specs=pl.BlockSpec((None, n2, kb, FNET_W), lambda b, j: (b, 0, j, 0)),
        out_shape=jax.ShapeDtypeStruct((n_batch, n2, n1, FNET_W), F32),
        compiler_params=_cparams("parallel", "parallel"),
        name="seq_dft_stage_b",
    )(br, bi, mmat)


def _ctx_dft_body(xr_ref, xi_ref, m_ref, o_ref):
    x = jnp.concatenate([xr_ref[...], xi_ref[...]], axis=0).astype(BF16)
    o_ref[...] = _dot(m_ref[...], x)


def _ctx_dft(fr, fi, mmat, n_batch, n_ctx, row_block_off):
    return pl.pallas_call(
        _ctx_dft_body,
        grid=(n_batch,),
        in_specs=[pl.BlockSpec((n_ctx, FNET_W), lambda b: (row_block_off + b, 0)),
                  pl.BlockSpec((n_ctx, FNET_W), lambda b: (row_block_off + b, 0)),
                  pl.BlockSpec(mmat.shape, lambda b: (0, 0))],
        out_specs=pl.BlockSpec((n_ctx, FNET_W), lambda b: (b, 0)),
        out_shape=jax.ShapeDtypeStruct((n_batch * n_ctx, FNET_W), F32),
        compiler_params=_cparams("parallel"),
        name="ctx_dft",
    )(fr, fi, mmat)


def _dft_tables(seq, n_ctx):
    n1 = DFT_N1
    n2 = seq // n1
    k1 = np.arange(n1)
    ang1 = 2.0 * np.pi * np.outer(k1, k1) / n1
    c1, s1 = np.cos(ang1), np.sin(ang1)
    m_a = np.block([[c1, s1], [-s1, c1]])
    ang_t = 2.0 * np.pi * np.outer(k1, np.arange(n2)) / seq
    twc = np.repeat(np.cos(ang_t), LANES, axis=1)
    tws = np.repeat(np.sin(ang_t), LANES, axis=1)
    k2 = np.arange(n2)
    ang2 = 2.0 * np.pi * np.outer(k2, k2) / n2
    scale = 1.0 / math.sqrt(seq)
    c2, s2 = np.cos(ang2) * scale, np.sin(ang2) * scale
    eye = np.eye(8)
    m_b = np.concatenate([np.einsum("kl,ab->kabl", c2, eye).reshape(n2 * 8, 8 * n2),
                          np.einsum("kl,ab->kabl", s2, eye).reshape(n2 * 8, 8 * n2)], axis=1)
    kc = np.arange(n_ctx)
    angc = 2.0 * np.pi * np.outer(kc, kc) / n_ctx
    m_c = np.concatenate([np.cos(angc), np.sin(angc)], axis=1) / math.sqrt(n_ctx)
    ch = np.arange(FNET_GROUP_W)
    angg = 2.0 * np.pi * np.outer(ch, ch) / FNET_GROUP_W
    eg = np.eye(FNET_GROUPS)
    chan = np.concatenate([np.kron(eg, np.cos(angg)), -np.kron(eg, np.sin(angg))], axis=1)
    chan = chan / math.sqrt(FNET_GROUP_W)
    return (jnp.asarray(m_a, BF16), jnp.asarray(twc, F32), jnp.asarray(tws, F32),
            jnp.asarray(m_b, BF16), jnp.asarray(m_c, BF16), jnp.asarray(chan, F32))


def _split_dot_rhs(w_bf16, x):
    hi = x.astype(BF16)
    r1 = x - hi.astype(F32)
    mid = r1.astype(BF16)
    lo = (r1 - mid.astype(F32)).astype(BF16)
    return _dot(w_bf16, hi) + _dot(w_bf16, mid) + _dot(w_bf16, lo)


def _split2_dot(x, w_bf16):
    hi = x.astype(BF16)
    lo = (x - hi.astype(F32)).astype(BF16)
    return _dot(hi, w_bf16) + _dot(lo, w_bf16)


SSD_CPS = 2


def _ssd2_body(nb, ncc, xm_ref, xp_ref, xn_ref, dt_ref, z_ref, cw_ref, cb_ref, dtb_ref, alog_ref, dsk_ref,
               ng_ref, tri_ref, e_ref, o_ref, yloc_ref, s_ref, c_ref, ea_ref, cd_ref, hbe_ref, hf_ref, hb_ref):
    g = pl.program_id(1)
    t = pl.program_id(2)
    q = SSD_CHUNK
    gw = SSD_GROUP_W
    ns = nb * SSD_CPS

    @pl.when(t < nb)
    def _local():
        first = t <= 1
        last = (t == 0) | (t == nb - 1)
        prev = jnp.where(first, 0.0, xp_ref[...])
        nxt = jnp.where(last, 0.0, xn_ref[...])
        xfull = jnp.concatenate([prev, xm_ref[...], nxt], axis=0)
        rows_blk = SSD_CPS * q
        conv = cb_ref[...]
        for k in range(SSD_CONV_W):
            lo = HALO - SSD_CONV_W // 2 + k
            conv = conv + xfull[lo:lo + rows_blk, :] * cw_ref[k:k + 1, :]
        act = _silu(conv)
        xs = act[:, :gw]
        bm = act[:, gw:gw + SSD_STATE]
        cm = act[:, gw + SSD_STATE:]

        lane = lax.broadcasted_iota(jnp.int32, (1, LANES), 1)
        is_fwd = lane < 8
        dsel = pltpu.roll(dt_ref[...], (LANES - 16 * g) % LANES, 1)
        dtv = _softplus(dsel + dtb_ref[pl.ds(g, 1), :])
        head_lane = ((lane % 8) < SSD_HPG) & (lane < 16)
        a_row = jnp.where(head_lane, -jnp.exp(alog_ref[pl.ds(g, 1), :]), 0.0)
        adt = dtv * a_row
        tri = tri_ref[...]
        tri16 = tri.astype(BF16)
        allowed = (tri[:q] > 0.0, tri[q:] > 0.0)
        emat = e_ref[...]
        lane_q = lax.broadcasted_iota(jnp.int32, (q, LANES), 1)
        lo_half = lane_q < SSD_HEAD_DIM
        for cc in range(SSD_CPS):
            rows = slice(cc * q, (cc + 1) * q)
            cums = _split_dot_rhs(tri16, adt[rows])
            acum = jnp.where(is_fwd, cums[:q], cums[q:])
            acum_t = acum.T
            end = jnp.where(is_fwd, acum[q - 1:q, :], acum[0:1, :])
            dte = jnp.exp(end - acum)
            stack = jnp.concatenate([dtv[rows], dte, jnp.broadcast_to(jnp.exp(end), (16, LANES))], axis=0)
            ex = _split2_dot(stack, emat)
            xs_c = xs[rows]
            xdt2 = jnp.concatenate([xs_c, xs_c], axis=1) * ex[:q]
            c16 = cm[rows].astype(BF16)
            b_c = bm[rows]
            cbm = _dot_nt(c16, b_c.astype(BF16))
            ys = [None] * (SSD_HPG // 2)
            for d in range(2):
                for r in range(SSD_HPG):
                    ln = 8 * d + r
                    seg = acum[:, ln:ln + 1] - acum_t[ln:ln + 1, :]
                    dec = jnp.exp(jnp.where(allowed[d], seg, NEG_INF))
                    mm = (cbm * dec).astype(BF16)
                    m, half = r // 2, r % 2
                    xpair = xdt2[:, gw * d + LANES * m:gw * d + LANES * (m + 1)]
                    xh = jnp.where(lo_half if half == 0 else ~lo_half, xpair, 0.0).astype(BF16)
                    tt = _dot(mm, xh)
                    ys[m] = tt if ys[m] is None else ys[m] + tt
            y_loc = jnp.concatenate(ys, axis=1) + dsk_ref[...] * xs_c
            sts = _dot(b_c.T.astype(BF16), (xdt2 * ex[q:2 * q]).astype(BF16))
            pos = SSD_CPS * t + cc
            r0 = pl.multiple_of(pos * q, q)
            yloc_ref[pl.ds(r0, q), :] = y_loc
            s_ref[pl.ds(r0, q), :] = sts.astype(BF16)
            c_ref[pl.ds(r0, q), :] = c16
            ea_ref[pl.ds(r0, q), :] = jnp.exp(acum)
            cd_ref[pl.ds(pl.multiple_of(pos * 8, 8), 8), :] = ex[2 * q:2 * q + 8]

    @pl.when(t == nb)
    def _backward_states():
        hb_ref[...] = jnp.zeros_like(hb_ref)
        hf_ref[...] = jnp.zeros_like(hf_ref)

        def body(i, carry):
            pos = jnp.where(i < ncc, ncc - 1 - i, ns - 1 + ncc - i)
            r0 = pl.multiple_of(pos * q, q)
            hb = hb_ref[...]
            hbe_ref[pl.ds(r0, q), :] = hb.astype(BF16)
            cd = cd_ref[pl.ds(pl.multiple_of(pos * 8, 8), 8), :][0:1, gw:]
            hb_ref[...] = hb * cd + s_ref[pl.ds(r0, q), gw:].astype(F32)
            return carry

        lax.fori_loop(0, ns, body, 0)

    @pl.when(t >= nb)
    def _emit():
        emat = e_ref[...]
        for cc in range(SSD_CPS):
            rows = slice(cc * q, (cc + 1) * q)
            pos = SSD_CPS * (t - nb) + cc
            r0 = pl.multiple_of(pos * q, q)
            hf = hf_ref[...]
            hcat = jnp.concatenate([hf.astype(BF16), hbe_ref[pl.ds(r0, q), :]], axis=1)
            yo = _dot(c_ref[pl.ds(r0, q), :], hcat) * _split2_dot(ea_ref[pl.ds(r0, q), :], emat)
            y = yloc_ref[pl.ds(r0, q), :] + yo[:, :gw] + yo[:, gw:]
            cd = cd_ref[pl.ds(pl.multiple_of(pos * 8, 8), 8), :][0:1, :gw]
            hf_ref[...] = hf * cd + s_ref[pl.ds(r0, q), :gw].astype(F32)
            y = y * _silu(z_ref[rows, :].astype(F32))
            y = y * lax.rsqrt(jnp.mean(y * y, axis=-1, keepdims=True) + NORM_EPS) * ng_ref[...]
            o_ref[rows, :] = y.astype(o_ref.dtype)


def _ssd2(xbc, dt, z, conv_w, conv_b, dtb_tab, alog_tab, dsk_x, ng, tri, emat, n_batch, seq, n_ctx):
    rows = xbc.shape[0]
    q = SSD_CHUNK
    blk = SSD_CPS * q
    assert n_ctx == blk and seq % blk == 0
    ncc = n_ctx // q
    nlb = seq // blk
    nb = 1 + nlb
    ns = nb * SSD_CPS
    per = blk // HALO

    def rb(b, pb):
        return jnp.where(pb == 0, n_batch * nlb + b, b * nlb + pb - 1)

    def rb_in(b, t):
        return rb(b, jnp.minimum(t, nb - 1))

    def rb_out(b, t):
        return rb(b, jnp.maximum(t - nb, 0))

    last_halo = rows // HALO - 1
    gw = SSD_GROUP_W
    return pl.pallas_call(
        functools.partial(_ssd2_body, nb, ncc),
        grid=(n_batch, SSD_GROUPS, 2 * nb),
        in_specs=[pl.BlockSpec((blk, SSD_XBC_W), lambda b, g, t: (rb_in(b, t), g)),
                  pl.BlockSpec((HALO, SSD_XBC_W),
                               lambda b, g, t: (jnp.maximum(rb_in(b, t) * per - 1, 0), g)),
                  pl.BlockSpec((HALO, SSD_XBC_W),
                               lambda b, g, t: (jnp.minimum(rb_in(b, t) * per + per, last_halo), g)),
                  pl.BlockSpec((blk, LANES), lambda b, g, t: (rb_in(b, t), 0)),
                  pl.BlockSpec((blk, gw), lambda b, g, t: (rb_out(b, t), g)),
                  pl.BlockSpec((SSD_CONV_W, SSD_XBC_W), lambda b, g, t: (0, g)),
                  pl.BlockSpec((1, SSD_XBC_W), lambda b, g, t: (0, g)),
                  pl.BlockSpec((8, LANES), lambda b, g, t: (0, 0)),
                  pl.BlockSpec((8, LANES), lambda b, g, t: (0, 0)),
                  pl.BlockSpec((1, gw), lambda b, g, t: (0, g)),
                  pl.BlockSpec((1, gw), lambda b, g, t: (0, g)),
                  pl.BlockSpec((2 * q, q), lambda b, g, t: (0, 0)),
                  pl.BlockSpec((LANES, 2 * gw), lambda b, g, t: (0, 0))],
        out_specs=pl.BlockSpec((blk, gw), lambda b, g, t: (rb_out(b, t), g)),
        out_shape=jax.ShapeDtypeStruct((rows, SSD_INNER), BF16),
        scratch_shapes=[pltpu.VMEM((ns * q, gw), F32),
                        pltpu.VMEM((ns * q, 2 * gw), BF16),
                        pltpu.VMEM((ns * q, SSD_STATE), BF16),
                        pltpu.VMEM((ns * q, LANES), F32),
                        pltpu.VMEM((ns * 8, 2 * gw), F32),
                        pltpu.VMEM((ns * q, gw), BF16),
                        pltpu.VMEM((SSD_STATE, gw), F32),
                        pltpu.VMEM((SSD_STATE, gw), F32)],
        compiler_params=_cparams("arbitrary", "arbitrary", "arbitrary"),
        name="ssd_bidir",
    )(xbc, xbc, xbc, dt, z, conv_w, conv_b, dtb_tab, alog_tab, dsk_x, ng, tri, emat)


def _stack_heads(qc):
    lane = lax.broadcasted_iota(jnp.int32, qc.shape, 1)
    lo = lane < HEAD_DIM
    zero = jnp.zeros_like(qc)
    return jnp.concatenate([jnp.where(lo, qc, zero), jnp.where(lo, zero, qc)], axis=0)


def _unstack_heads(o, n):
    lane = lax.broadcasted_iota(jnp.int32, (n, LANES), 1)
    return jnp.where(lane < HEAD_DIM, o[:n], o[n:])


def _softmax_pv(scores, values, sink=None):
    m = None
    for sc in scores:
        mx = jnp.max(sc, axis=-1, keepdims=True)
        m = mx if m is None else jnp.maximum(m, mx)
    if sink is not None:
        m = jnp.maximum(m, sink)
    den = None if sink is None else jnp.exp(sink - m)
    acc = None
    for sc, v in zip(scores, values):
        pr = jnp.exp(sc - m)
        sm = jnp.sum(pr, axis=-1, keepdims=True)
        den = sm if den is None else den + sm
        t = _dot(pr.astype(BF16), v)
        acc = t if acc is None else acc + t
    return acc / den


def _win_body(q_ref, kp_ref, ko_ref, kn_ref, vp_ref, vo_ref, vn_ref, kc_ref, vc_ref, sink_ref, o_ref):
    n = pl.program_id(1)
    nb = pl.num_programs(1)
    wb = WIN_BLOCK
    q = q_ref[...]
    lane = lax.broadcasted_iota(jnp.int32, (wb, LANES), 1)
    lo = lane < HEAD_DIM
    parts = []
    for kvh in range(WIN_KV_HEADS):
        for m in range(WIN_GQA):
            qc = q[:, LANES * m:LANES * (m + 1)]
            parts.append(jnp.where(lo if kvh == 0 else ~lo, qc, jnp.zeros_like(qc)))
    qs = jnp.concatenate(parts, axis=0)
    kb = jnp.concatenate([kp_ref[...], ko_ref[...], kn_ref[...]], axis=0)
    vb = jnp.concatenate([vp_ref[...], vo_ref[...], vn_ref[...]], axis=0)
    s_loc = _dot_nt(qs, kb)
    r = lax.broadcasted_iota(jnp.int32, (wb, 3 * wb), 0)
    c = lax.broadcasted_iota(jnp.int32, (wb, 3 * wb), 1)
    pen_prev = jnp.where(n > 0, 0.0, NEG_INF)
    pen_next = jnp.where(n < nb - 1, 0.0, NEG_INF)
    pen = jnp.where(c < wb, jnp.where(c >= r, pen_prev, NEG_INF),
                    jnp.where(c < 2 * wb, 0.0, jnp.where(c - 2 * wb <= r, pen_next, NEG_INF)))
    s_loc = (s_loc.reshape(WIN_Q_HEADS, wb, 3 * wb) + pen[None]).reshape(WIN_Q_HEADS * wb, 3 * wb)
    s_ctx = _dot_nt(qs, kc_ref[...])
    o = _softmax_pv([s_loc, s_ctx], [vb, vc_ref[...]], sink_ref[:, 0:1])
    half = WIN_GQA * wb
    for m in range(WIN_GQA):
        top = o[m * wb:(m + 1) * wb]
        bot = o[half + m * wb:half + (m + 1) * wb]
        o_ref[:, LANES * m:LANES * (m + 1)] = jnp.where(lo, top, bot).astype(o_ref.dtype)


def _window_attention(qw, kw, vw, sink_x, n_batch, seq, n_ctx):
    wb = WIN_BLOCK
    nb = seq // wb
    ctx_blk0 = n_batch * seq // n_ctx
    q_spec = pl.BlockSpec((wb, 512), lambda b, n: (b * nb + n, 0))
    prev = pl.BlockSpec((wb, LANES), lambda b, n: (b * nb + jnp.maximum(n - 1, 0), 0))
    own = pl.BlockSpec((wb, LANES), lambda b, n: (b * nb + n, 0))
    nxt = pl.BlockSpec((wb, LANES), lambda b, n: (b * nb + jnp.minimum(n + 1, nb - 1), 0))
    cspec = pl.BlockSpec((n_ctx, LANES), lambda b, n: (ctx_blk0 + b, 0))
    return pl.pallas_call(
        _win_body,
        grid=(n_batch, nb),
        in_specs=[q_spec, prev, own, nxt, prev, own, nxt, cspec, cspec,
                  pl.BlockSpec(sink_x.shape, lambda b, n: (0, 0))],
        out_specs=pl.BlockSpec((wb, 512), lambda b, n: (b * nb + n, 0)),
        out_shape=jax.ShapeDtypeStruct((n_batch * seq, 512), BF16),
        compiler_params=_cparams("parallel", "parallel"),
        name="window_attention",
    )(qw, kw, kw, kw, vw, vw, vw, kw, vw, sink_x)


def _na_body(n_rows, q_ref, k_ref, v_ref, kc_ref, vc_ref, bias_ref, o_ref):
    r = pl.program_id(1)
    kr = NA_ROWS
    start = pl.multiple_of(jnp.clip(r - kr // 2, 0, n_rows - kr) * GRID_W, GRID_W)
    q = q_ref[...]
    for m in range(NA_HEADS // 2):
        sl = slice(LANES * m, LANES * (m + 1))
        qs = _stack_heads(q[:, sl])
        kw = k_ref[pl.ds(start, kr * GRID_W), sl]
        vw = v_ref[pl.ds(start, kr * GRID_W), sl]
        s_loc = _dot_nt(qs, kw) + bias_ref[2 * GRID_W * m:2 * GRID_W * (m + 1), :]
        s_ctx = _dot_nt(qs, kc_ref[:, sl])
        o = _softmax_pv([s_loc, s_ctx], [vw, vc_ref[:, sl]])
        o_ref[:, sl] = _unstack_heads(o, GRID_W).astype(o_ref.dtype)


def _na_attention(qn, kn, vn, bias_tab, n_batch, seq, n_ctx):
    n_rows = seq // GRID_W
    kr = min(NA_ROWS, n_rows)
    ctx_blk0 = n_batch * seq // n_ctx

    def bias_map(b, r):
        return (jnp.clip(r - kr // 2, 0, n_rows - kr) - r + NA_ROWS - 1, 0, 0)

    return pl.pallas_call(
        functools.partial(_na_body, n_rows),
        grid=(n_batch, n_rows),
        in_specs=[pl.BlockSpec((GRID_W, 512), lambda b, r: (b * n_rows + r, 0)),
                  pl.BlockSpec((seq, 512), lambda b, r: (b, 0)),
                  pl.BlockSpec((seq, 512), lambda b, r: (b, 0)),
                  pl.BlockSpec((n_ctx, 512), lambda b, r: (ctx_blk0 + b, 0)),
                  pl.BlockSpec((n_ctx, 512), lambda b, r: (ctx_blk0 + b, 0)),
                  pl.BlockSpec((None, NA_HEADS * GRID_W, kr * GRID_W), bias_map)],
        out_specs=pl.BlockSpec((GRID_W, 512), lambda b, r: (b * n_rows + r, 0)),
        out_shape=jax.ShapeDtypeStruct((n_batch * seq, 512), BF16),
        compiler_params=_cparams("parallel", "arbitrary"),
        name="neighbourhood_attention",
    )(qn, kn, vn, kn, vn, bias_tab)


def _ctx_attn_body(qw_ref, kw_ref, vw_ref, qn_ref, kn_ref, vn_ref, sink_ref, ow_ref, on_ref):
    n = qw_ref.shape[0]
    qw = qw_ref[...]
    lane = lax.broadcasted_iota(jnp.int32, (n, LANES), 1)
    lo = lane < HEAD_DIM
    for m in range(WIN_GQA):
        qs = _stack_heads(qw[:, LANES * m:LANES * (m + 1)])
        r0, r1 = m * WIN_BLOCK, (WIN_GQA + m) * WIN_BLOCK
        sink = jnp.concatenate([jnp.broadcast_to(sink_ref[r0:r0 + 1, 0:1], (n, 1)),
                                jnp.broadcast_to(sink_ref[r1:r1 + 1, 0:1], (n, 1))], axis=0)
        o = _softmax_pv([_dot_nt(qs, kw_ref[...])], [vw_ref[...]], sink)
        ow_ref[:, LANES * m:LANES * (m + 1)] = jnp.where(lo, o[:n], o[n:]).astype(ow_ref.dtype)
    qn = qn_ref[...]
    for m in range(NA_HEADS // 2):
        sl = slice(LANES * m, LANES * (m + 1))
        qs = _stack_heads(qn[:, sl])
        o = _softmax_pv([_dot_nt(qs, kn_ref[:, sl])], [vn_ref[:, sl]])
        on_ref[:, sl] = _unstack_heads(o, n).astype(on_ref.dtype)


def _ctx_attention(qw, kw, vw, qn, kn, vn, sink_x, n_batch, seq, n_ctx):
    blk0 = n_batch * seq // n_ctx
    wide = pl.BlockSpec((n_ctx, 512), lambda b: (blk0 + b, 0))
    narrow = pl.BlockSpec((n_ctx, LANES), lambda b: (blk0 + b, 0))
    out = pl.BlockSpec((n_ctx, 512), lambda b: (b, 0))
    return pl.pallas_call(
        _ctx_attn_body,
        grid=(n_batch,),
        in_specs=[wide, narrow, narrow, wide, wide, wide,
                  pl.BlockSpec(sink_x.shape, lambda b: (0, 0))],
        out_specs=[out, out],
        out_shape=[jax.ShapeDtypeStruct((n_batch * n_ctx, 512), BF16)] * 2,
        compiler_params=_cparams("parallel"),
        name="context_attention",
    )(qw, kw, vw, qn, kn, vn, sink_x)


def _take(w, runs, axis):
    parts = []
    for run in runs:
        if run[0] is None:
            shape = list(w.shape)
            shape[axis] = run[1]
            parts.append(jnp.zeros(shape, w.dtype))
        else:
            parts.append(lax.slice_in_dim(w, run[0], run[1], axis=axis))
    return jnp.concatenate(parts, axis=axis)


def _xbc_runs():
    gn = SSD_GROUPS * SSD_STATE
    runs = []
    for g in range(SSD_GROUPS):
        runs.append((g * SSD_GROUP_W, (g + 1) * SSD_GROUP_W))
        runs.append((SSD_INNER + g * SSD_STATE, SSD_INNER + (g + 1) * SSD_STATE))
        runs.append((SSD_INNER + gn + g * SSD_STATE, SSD_INNER + gn + (g + 1) * SSD_STATE))
    return runs


def _dt_runs():
    runs = []
    for g in range(SSD_GROUPS):
        for d in range(2):
            runs.append((d * SSD_HEADS + g * SSD_HPG, d * SSD_HEADS + (g + 1) * SSD_HPG))
            runs.append((None, 8 - SSD_HPG))
    runs.append((None, LANES - 8 * 2 * SSD_GROUPS))
    return runs


def _win_head_runs():
    runs = []
    for m in range(WIN_GQA):
        for kvh in range(WIN_KV_HEADS):
            h = kvh * WIN_GQA + m
            runs.append((h * HEAD_DIM, (h + 1) * HEAD_DIM))
    return runs


def _rope_tables(seq):
    pos = np.arange(seq)
    row = (pos // GRID_W).astype(np.float32)
    col = (pos % GRID_W).astype(np.float32)
    n_freq = HEAD_DIM // 4
    inv = (np.float32(ROPE_THETA) ** (-np.arange(n_freq, dtype=np.float32) / n_freq)).astype(np.float32)
    ar = (row[:, None] * inv).astype(np.float32)
    ac = (col[:, None] * inv).astype(np.float32)
    cos_h = np.concatenate([np.cos(ar), np.cos(ar), np.cos(ac), np.cos(ac)], axis=1)
    sin_h = np.concatenate([-np.sin(ar), np.sin(ar), -np.sin(ac), np.sin(ac)], axis=1)
    ident_c = np.ones((ROW_TILE, HEAD_DIM), np.float32)
    ident_s = np.zeros((ROW_TILE, HEAD_DIM), np.float32)
    cos_t = np.concatenate([cos_h, ident_c], axis=0)
    sin_t = np.concatenate([sin_h, ident_s], axis=0)
    return (jnp.asarray(np.tile(cos_t, (1, 2)), F32), jnp.asarray(np.tile(sin_t, (1, 2)), F32))


def _na_bias_table(rpb, n_rows):
    kr = min(NA_ROWS, n_rows)
    cols = np.arange(GRID_W)
    col_start = np.clip(cols - NA_COLS // 2, 0, GRID_W - NA_COLS)
    col_ok = (cols[None] >= col_start[:, None]) & (cols[None] < col_start[:, None] + NA_COLS)
    col_off = np.clip(cols[None] - cols[:, None] + (NA_COLS - 1), 0, 2 * NA_COLS - 2)
    gathered = rpb.astype(F32)[:, :, col_off]
    gathered = jnp.where(jnp.asarray(col_ok)[None, None], gathered, NEG_INF)
    tabs = []
    for ro0 in range(NA_ROWS):
        rows = np.clip(ro0 + np.arange(kr), 0, 2 * NA_ROWS - 2)
        t = gathered[:, rows]
        tabs.append(jnp.transpose(t, (0, 2, 1, 3)).reshape(NA_HEADS * GRID_W, kr * GRID_W))
    return jnp.stack(tabs, axis=0)


def kernel(x, c, ctx, c_ctx, w_mod, b_mod, norm_mix_g, norm_ff_g, w_ff1, w_ff2, w_in_even, conv_w, conv_b,
           dt_bias, a_log, d_skip, ssd_norm_g, w_out_even, w_in_odd, q_norm_win, k_norm_win, sink_win,
           q_norm_na, k_norm_na, rpb_na, w_out_odd):
    n_batch, seq, d = x.shape
    n_ctx = ctx.shape[1]
    n_lat = n_batch * seq
    tm = ROW_TILE
    tiles_per_batch = seq // tm
    n_lat_tiles = n_lat // tm
    n_ctx_tiles = (n_batch * n_ctx) // tm
    n2 = seq // DFT_N1

    h = jnp.concatenate([x.reshape(n_lat, d), ctx.reshape(n_batch * n_ctx, d)], axis=0)

    cvec = jnp.concatenate([c, c_ctx[None], jnp.zeros((8 - n_batch - 1, d), F32)], axis=0)
    mod_all = _modulation(cvec, w_mod, b_mod)
    mod_all = mod_all.reshape(-1, 8, 6, 1, d)[:, :n_batch + 1].transpose(0, 2, 1, 3, 4)

    m_a, twc, tws, m_b, m_c, chan = _dft_tables(seq, n_ctx)
    xbc_runs, dt_runs = _xbc_runs(), _dt_runs()
    s1, s2, s3 = FNET_W, FNET_W + SSD_INNER, FNET_W + SSD_INNER + SSD_CONV_DIM
    w_four = _weight_product(w_in_even[:, :, :s1], chan)
    tri = jnp.asarray(np.concatenate([np.tril(np.ones((SSD_CHUNK, SSD_CHUNK))),
                                      np.triu(np.ones((SSD_CHUNK, SSD_CHUNK)))], axis=0), F32)
    emat = np.zeros((LANES, 2 * SSD_GROUP_W), np.float32)
    for dr in range(2):
        for r in range(SSD_HPG):
            c0 = dr * SSD_GROUP_W + r * SSD_HEAD_DIM
            emat[8 * dr + r, c0:c0 + SSD_HEAD_DIM] = 1.0
    emat = jnp.asarray(emat, BF16)

    cos_t, sin_t = _rope_tables(seq)
    gmat512 = jnp.asarray(np.kron(np.eye(8), np.full((HEAD_DIM, HEAD_DIM), 1.0 / HEAD_DIM)), BF16)
    gmat128 = gmat512[:LANES, :LANES]
    win_runs = _win_head_runs()
    wq = WIN_Q_HEADS * HEAD_DIM
    wk = WIN_KV_HEADS * HEAD_DIM

    for i in range(DEPTH):
        need_ctx = i < DEPTH - 1
        modv = mod_all[i]
        j = i // 2
        gain_mix = norm_mix_g[i].reshape(1, d)
        if i % 2 == 0:
            wi = w_in_even[j]
            w_all = jnp.concatenate([w_four[j].astype(BF16), wi[:, s1:s2].astype(BF16),
                                     _take(wi[:, s2:s3].astype(BF16), xbc_runs, 1),
                                     _take(wi[:, s3:].astype(BF16), dt_runs, 1)], axis=1)
            fr, fi, z, xbc, dtr = _inproj_even(h, gain_mix, modv, w_all, n_batch, tiles_per_batch)

            br, bi = _dft_stage_a(fr.reshape(-1, n2, FNET_W), fi.reshape(-1, n2, FNET_W),
                                  m_a, twc, tws, n_batch, n2)
            f_lat = _dft_stage_b(br.reshape(n_batch, seq, FNET_W), bi.reshape(n_batch, seq, FNET_W),
                                 m_b, n_batch, n2).reshape(n_lat, FNET_W)
            f_ctx = _ctx_dft(fr, fi, m_c, n_batch, n_ctx, n_lat // n_ctx)

            def pad8(t):
                t = t.astype(F32).reshape(2, SSD_GROUPS, SSD_HPG).transpose(1, 0, 2)
                t = jnp.pad(t, ((0, 8 - SSD_GROUPS), (0, 0), (0, 8 - SSD_HPG))).reshape(8, 16)
                return jnp.pad(t, ((0, 0), (0, LANES - 16)))
            y_ssd = _ssd2(xbc, dtr, z, _take(conv_w[j], xbc_runs, 1),
                         _take(conv_b[j].reshape(1, -1), xbc_runs, 1),
                         pad8(dt_bias[j]), pad8(a_log[j]),
                         jnp.repeat(d_skip[j].astype(F32), SSD_HEAD_DIM).reshape(1, -1),
                         ssd_norm_g[j].astype(F32).reshape(1, -1), tri, emat, n_batch, seq, n_ctx)

            w_out = w_out_even[j].astype(BF16)
            h = _outproj(h, modv, f_lat, y_ssd, w_out, n_batch, tiles_per_batch, 0, 0, 0, n_lat_tiles)
            h = _outproj(h, modv, f_ctx, y_ssd, w_out, n_batch, tiles_per_batch, n_lat_tiles, 0,
                         n_lat_tiles, n_ctx_tiles)
        else:
            wi = w_in_odd[j]
            w_all = jnp.concatenate([_take(wi[:, :wq].astype(BF16), win_runs, 1), wi[:, wq:].astype(BF16)],
                                    axis=1)
            tile8 = lambda g, reps, scale=1.0: (jnp.tile(g.astype(F32), reps) * scale).reshape(1, -1)
            gains = (tile8(q_norm_win[j], 8, HEAD_DIM ** -0.5), tile8(k_norm_win[j], 2),
                     tile8(q_norm_na[j], 8, HEAD_DIM ** -0.5), tile8(k_norm_na[j], 8))
            qw, kw, vw, qn, kn, vn = _inproj_odd(h, gain_mix, modv, w_all, cos_t, sin_t, gmat512, gmat128,
                                                 gains, n_batch, tiles_per_batch)
            sink = sink_win[j].astype(F32)
            sink_x = jnp.broadcast_to(jnp.repeat(sink, WIN_BLOCK)[:, None], (WIN_Q_HEADS * WIN_BLOCK, LANES))
            o_win = _window_attention(qw, kw, vw, sink_x, n_batch, seq, n_ctx)
            bias_tab = _na_bias_table(rpb_na[j], seq // GRID_W)
            o_na = _na_attention(qn, kn, vn, bias_tab, n_batch, seq, n_ctx)
            wo = w_out_odd[j]
            w_out = jnp.concatenate([_take(wo[:wq].astype(BF16), win_runs, 0), wo[wq:].astype(BF16)], axis=0)
            if need_ctx:
                oc_win, oc_na = _ctx_attention(qw, kw, vw, qn, kn, vn, sink_x, n_batch, seq, n_ctx)
            h = _outproj(h, modv, o_win, o_na, w_out, n_batch, tiles_per_batch, 0, 0, 0, n_lat_tiles)
            if need_ctx:
                h = _outproj(h, modv, oc_win, oc_na, w_out, n_batch, tiles_per_batch, n_lat_tiles, 0, 0,
                             n_ctx_tiles)
        n_tiles = n_lat_tiles + n_ctx_tiles if need_ctx else n_lat_tiles
        h = _ffn(h, norm_ff_g[i].reshape(1, d), modv, w_ff1[i].astype(BF16), w_ff2[i].astype(BF16),
                 n_batch, tiles_per_batch, n_tiles)
    return h[:n_lat].reshape(n_batch, seq, d)
```

```python
import functools
import math

import numpy as np
import jax
import jax.numpy as jnp
from jax import lax
from jax.experimental import pallas as pl
from jax.experimental.pallas import tpu as pltpu

F32 = jnp.float32
BF16 = jnp.bfloat16
HIGHEST = lax.Precision.HIGHEST

D_MODEL = 1024
DEPTH = 4
GRID_W = 64
D_FF = 4 * D_MODEL
NORM_EPS = 1e-6
NEG_INF = -1e30

FNET_GROUPS = 8
FNET_GROUP_W = 64
FNET_W = FNET_GROUPS * FNET_GROUP_W
SSD_HEAD_DIM = 64
SSD_HEADS = 24
SSD_GROUPS = 4
SSD_HPG = SSD_HEADS // SSD_GROUPS
SSD_STATE = 128
SSD_INNER = SSD_HEADS * SSD_HEAD_DIM
SSD_GROUP_W = SSD_HPG * SSD_HEAD_DIM
SSD_XBC_W = SSD_GROUP_W + 2 * SSD_STATE
SSD_CONV_DIM = SSD_INNER + 2 * SSD_GROUPS * SSD_STATE
SSD_CONV_W = 5
SSD_CHUNK = 128
HEAD_DIM = 64
WIN_Q_HEADS = 8
WIN_KV_HEADS = 2
WIN_GQA = WIN_Q_HEADS // WIN_KV_HEADS
WIN_BLOCK = 128
NA_HEADS = 8
NA_ROWS = 8
NA_COLS = 16
ROPE_THETA = 10000.0
LOG2E = 1.4426950408889634

LANES = 128
MXU_DIM = 256
HALO = 8
ROW_TILE = 512
FF_TILE = 4096
DFT_N1 = 128
VMEM_LIMIT = 56 * 1024 * 1024


def _cparams(*sem):
    return pltpu.CompilerParams(dimension_semantics=sem, vmem_limit_bytes=VMEM_LIMIT)


def _silu(x):
    return x / (1.0 + jnp.exp2(x * -LOG2E))


def _softplus(x):
    return jnp.maximum(x, 0.0) + jnp.log(1.0 + jnp.exp(-jnp.abs(x)))


def _norm_mod(h, g, shift, scale):
    ms = jnp.mean(h * h, axis=-1, keepdims=True)
    y = h * lax.rsqrt(ms + NORM_EPS) * g
    return y * (1.0 + scale) + shift


def _dot(a, b):
    return jnp.dot(a, b, preferred_element_type=F32)


def _dot_nt(a, b):
    return lax.dot_general(a, b, (((1,), (1,)), ((), ())), preferred_element_type=F32)


def _mod_body(c_ref, w_ref, b_ref, o_ref):
    s = _silu(c_ref[...])
    o_ref[...] = jnp.dot(s, w_ref[...], precision=HIGHEST, preferred_element_type=F32) + b_ref[...]


def _modulation(cvec, w_mod, b_mod):
    depth, d, n = w_mod.shape
    tn = 1536
    return pl.pallas_call(
        _mod_body,
        grid=(depth, n // tn),
        in_specs=[pl.BlockSpec((8, d), lambda i, j: (0, 0)),
                  pl.BlockSpec((None, d, tn), lambda i, j: (i, 0, j)),
                  pl.BlockSpec((None, 1, tn), lambda i, j: (i, 0, j))],
        out_specs=pl.BlockSpec((None, 8, tn), lambda i, j: (i, 0, j)),
        out_shape=jax.ShapeDtypeStruct((depth, 8, n), F32),
        compiler_params=_cparams("parallel", "parallel"),
        name="modulation",
    )(cvec, w_mod, b_mod.reshape(depth, 1, n))


def _wprod_body(a_ref, b_ref, o_ref):
    o_ref[...] = jnp.dot(a_ref[...], b_ref[...], precision=HIGHEST, preferred_element_type=F32)


def _weight_product(a, b):
    n, m, k = a.shape
    p = b.shape[1]
    return pl.pallas_call(
        _wprod_body,
        grid=(n,),
        in_specs=[pl.BlockSpec((None, m, k), lambda i: (i, 0, 0)),
                  pl.BlockSpec((k, p), lambda i: (0, 0))],
        out_specs=pl.BlockSpec((None, m, p), lambda i: (i, 0, 0)),
        out_shape=jax.ShapeDtypeStruct((n, m, p), F32),
        compiler_params=_cparams("parallel"),
        name="fold_channel_dft",
    )(a, b)


def _mod_spec(which, tiles_per_batch, n_batch, tile_off):
    def imap(m):
        return (which, jnp.minimum((m + tile_off) // tiles_per_batch, n_batch), 0, 0)
    return pl.BlockSpec((None, None, 1, D_MODEL), imap)


def _inproj_even_body(widths, h_ref, g_ref, sh_ref, sc_ref, w_ref, *out_refs):
    u = _norm_mod(h_ref[...], g_ref[...], sh_ref[...], sc_ref[...]).astype(BF16)
    off = 0
    for ref, wd in zip(out_refs, widths):
        ref[...] = _dot(u, w_ref[:, off:off + wd]).astype(ref.dtype)
        off += wd


def _inproj_even(h, gain, modv, w, n_batch, tiles_per_batch):
    rows = h.shape[0]
    widths = (FNET_W, FNET_W, SSD_INNER, SSD_CONV_DIM, LANES)
    dtypes = (F32, F32, BF16, F32, F32)
    tm = ROW_TILE
    return pl.pallas_call(
        functools.partial(_inproj_even_body, widths),
        grid=(rows // tm,),
        in_specs=[pl.BlockSpec((tm, D_MODEL), lambda m: (m, 0)),
                  pl.BlockSpec((1, D_MODEL), lambda m: (0, 0)),
                  _mod_spec(0, tiles_per_batch, n_batch, 0),
                  _mod_spec(1, tiles_per_batch, n_batch, 0),
                  _resident(w.shape)],
        out_specs=[pl.BlockSpec((tm, wd), lambda m: (m, 0)) for wd in widths],
        out_shape=[jax.ShapeDtypeStruct((rows, wd), dt) for wd, dt in zip(widths, dtypes)],
        compiler_params=_cparams("parallel"),
        name="inproj_even",
    )(h, gain, modv, modv, w)


def _group_rms(x, gmat_ref, gain):
    sq = x * x
    hi = sq.astype(BF16)
    lo = (sq - hi.astype(F32)).astype(BF16)
    gw = gmat_ref.shape[0]
    parts = []
    for c0 in range(0, x.shape[1], gw):
        parts.append(_dot(hi[:, c0:c0 + gw], gmat_ref[...]) + _dot(lo[:, c0:c0 + gw], gmat_ref[...]))
    ms = parts[0] if len(parts) == 1 else jnp.concatenate(parts, axis=1)
    return x * lax.rsqrt(ms + NORM_EPS) * gain


def _rope(y, cos, sin_signed):
    w = y.shape[-1]
    reps = w // LANES
    lane = lax.broadcasted_iota(jnp.int32, y.shape, 1)
    first = (lane % 32) < 16
    partner = jnp.where(first, pltpu.roll(y, w - 16, 1), pltpu.roll(y, 16, 1))
    if reps > 1:
        cos = jnp.concatenate([cos] * reps, axis=1)
        sin_signed = jnp.concatenate([sin_signed] * reps, axis=1)
    return y * cos + partner * sin_signed


def _inproj_odd_body(h_ref, g_ref, sh_ref, sc_ref, w_ref, cos_ref, sin_ref, gslab_ref, g128_ref,
                     gqw_ref, gkw_ref, gqn_ref, gkn_ref,
                     qw_ref, kw_ref, vw_ref, qn_ref, kn_ref, vn_ref):
    u = _norm_mod(h_ref[...], g_ref[...], sh_ref[...], sc_ref[...]).astype(BF16)
    cos = cos_ref[...]
    sin = sin_ref[...]
    qw = _group_rms(_dot(u, w_ref[:, 0:512]), gslab_ref, gqw_ref[...])
    qw_ref[...] = _rope(qw, cos, sin).astype(BF16)
    kw = _group_rms(_dot(u, w_ref[:, 512:640]), g128_ref, gkw_ref[...])
    kw_ref[...] = _rope(kw, cos, sin).astype(BF16)
    vw_ref[...] = _dot(u, w_ref[:, 640:768]).astype(BF16)
    qn_ref[...] = _group_rms(_dot(u, w_ref[:, 768:1280]), gslab_ref, gqn_ref[...]).astype(BF16)
    kn_ref[...] = _group_rms(_dot(u, w_ref[:, 1280:1792]), gslab_ref, gkn_ref[...]).astype(BF16)
    vn_ref[...] = _dot(u, w_ref[:, 1792:2304]).astype(BF16)


def _inproj_odd(h, gain, modv, w, cos_t, sin_t, gmat_slab, gmat128, gains, n_batch, tiles_per_batch):
    rows = h.shape[0]
    tm = ROW_TILE
    widths = (512, 128, 128, 512, 512, 512)
    n_lat_tiles = n_batch * tiles_per_batch

    def rope_map(m):
        return (jnp.where(m < n_lat_tiles, m % tiles_per_batch, tiles_per_batch), 0)

    const = lambda a: pl.BlockSpec(a.shape, lambda m: (0,) * a.ndim)
    return pl.pallas_call(
        _inproj_odd_body,
        grid=(rows // tm,),
        in_specs=[pl.BlockSpec((tm, D_MODEL), lambda m: (m, 0)),
                  pl.BlockSpec((1, D_MODEL), lambda m: (0, 0)),
                  _mod_spec(0, tiles_per_batch, n_batch, 0),
                  _mod_spec(1, tiles_per_batch, n_batch, 0),
                  _resident(w.shape),
                  pl.BlockSpec((tm, LANES), rope_map),
                  pl.BlockSpec((tm, LANES), rope_map),
                  const(gmat_slab), const(gmat128)] + [const(g) for g in gains],
        out_specs=[pl.BlockSpec((tm, wd), lambda m: (m, 0)) for wd in widths],
        out_shape=[jax.ShapeDtypeStruct((rows, wd), BF16) for wd in widths],
        compiler_params=_cparams("parallel"),
        name="inproj_odd",
    )(h, gain, modv, modv, w, cos_t, sin_t, gmat_slab, gmat128, *gains)


def _outproj_body(k1, h_ref, gate_ref, a1_ref, a2_ref, w_ref, o_ref):
    acc = _dot(a1_ref[...].astype(BF16), w_ref[0:k1, :])
    acc += _dot(a2_ref[...].astype(BF16), w_ref[k1:, :])
    o_ref[...] = h_ref[...] + gate_ref[...] * acc


def _outproj(h, modv, a1, a2, w, n_batch, tiles_per_batch, tile_off, a1_off, a2_off, n_tiles):
    tm = ROW_TILE
    k1, k2 = a1.shape[1], a2.shape[1]
    return pl.pallas_call(
        functools.partial(_outproj_body, k1),
        grid=(n_tiles,),
        in_specs=[pl.BlockSpec((tm, D_MODEL), lambda m: (m + tile_off, 0)),
                  _mod_spec(2, tiles_per_batch, n_batch, tile_off),
                  pl.BlockSpec((tm, k1), lambda m: (m + a1_off, 0)),
                  pl.BlockSpec((tm, k2), lambda m: (m + a2_off, 0)),
                  _resident(w.shape)],
        out_specs=pl.BlockSpec((tm, D_MODEL), lambda m: (m + tile_off, 0)),
        out_shape=jax.ShapeDtypeStruct(h.shape, F32),
        input_output_aliases={0: 0},
        compiler_params=_cparams("parallel"),
        name="outproj",
    )(h, modv, a1, a2, w)


def _ffn_body(h_ref, g_ref, sh_ref, sc_ref, gate_ref, w1_ref, w2_ref, o_ref):
    h = h_ref[...]
    u = _norm_mod(h, g_ref[...], sh_ref[...], sc_ref[...]).astype(BF16)
    acc = None
    for k in range(D_FF // FF_TILE):
        sl = slice(k * FF_TILE, (k + 1) * FF_TILE)
        a = jnp.maximum(_dot(u, w1_ref[:, sl]), 0.0)
        t = _dot((a * a).astype(BF16), w2_ref[sl, :])
        acc = t if acc is None else acc + t
    o_ref[...] = h + gate_ref[...] * acc


def _resident(shape):
    return pl.BlockSpec(shape, lambda *_: (0,) * len(shape), pipeline_mode=pl.Buffered(1))


def _ffn(h, gain, modv, w1, w2, n_batch, tiles_per_batch, n_tiles):
    tm = ROW_TILE
    return pl.pallas_call(
        _ffn_body,
        grid=(n_tiles,),
        in_specs=[pl.BlockSpec((tm, D_MODEL), lambda m: (m, 0)),
                  pl.BlockSpec((1, D_MODEL), lambda m: (0, 0)),
                  _mod_spec(3, tiles_per_batch, n_batch, 0),
                  _mod_spec(4, tiles_per_batch, n_batch, 0),
                  _mod_spec(5, tiles_per_batch, n_batch, 0),
                  _resident(w1.shape), _resident(w2.shape)],
        out_specs=pl.BlockSpec((tm, D_MODEL), lambda m: (m, 0)),
        out_shape=jax.ShapeDtypeStruct((n_tiles * tm, D_MODEL), F32),
        compiler_params=_cparams("parallel"),
        name="ffn",
    )(h, gain, modv, modv, modv, w1, w2)


def _dft_a_body(xr_ref, xi_ref, m_ref, c_ref, s_ref, br_ref, bi_ref):
    n1 = DFT_N1
    reps = FNET_W // LANES
    for j in range(xr_ref.shape[1]):
        x = jnp.concatenate([xr_ref[:, j, :], xi_ref[:, j, :]], axis=0).astype(BF16)
        a = _dot(m_ref[...], x)
        ar, ai = a[:n1], a[n1:]
        c = jnp.concatenate([c_ref[:, LANES * j:LANES * (j + 1)]] * reps, axis=1)
        s = jnp.concatenate([s_ref[:, LANES * j:LANES * (j + 1)]] * reps, axis=1)
        br_ref[:, j, :] = ar * c + ai * s
        bi_ref[:, j, :] = ai * c - ar * s


def _dft_stage_a(xr, xi, mmat, twc, tws, n_batch, n2):
    n1 = DFT_N1
    lb = 8
    blk = pl.BlockSpec((n1, lb, FNET_W), lambda b, j: (b, j, 0))
    return pl.pallas_call(
        _dft_a_body,
        grid=(n_batch, n2 // lb),
        in_specs=[blk, blk,
                  pl.BlockSpec((2 * n1, 2 * n1), lambda b, j: (0, 0)),
                  pl.BlockSpec((n1, lb * LANES), lambda b, j: (0, j)),
                  pl.BlockSpec((n1, lb * LANES), lambda b, j: (0, j))],
        out_specs=[blk, blk],
        out_shape=[jax.ShapeDtypeStruct((n_batch * n1, n2, FNET_W), F32)] * 2,
        compiler_params=_cparams("parallel", "parallel"),
        name="seq_dft_stage_a",
    )(xr, xi, mmat, twc, tws)


def _dft_b_body(br_ref, bi_ref, m_ref, o_ref):
    x = jnp.concatenate([br_ref[...], bi_ref[...]], axis=0).astype(BF16)
    res = _dot(m_ref[...], x)
    o_ref[...] = res.reshape(o_ref.shape)


def _dft_stage_b(br, bi, mmat, n_batch, n2):
    n1 = DFT_N1
    kb = 8
    return pl.pallas_call(
        _dft_b_body,
        grid=(n_batch, n1 // kb),
        in_specs=[pl.BlockSpec((None, kb * n2, FNET_W), lambda b, j: (b, j, 0)),
                  pl.BlockSpec((None, kb * n2, FNET_W), lambda b, j: (b, j, 0)),
                  pl.BlockSpec(mmat.shape, lambda b, j: (0, 0))],
        out_specs=pl.BlockSpec((None, n2, kb, FNET_W), lambda b, j: (b, 0, j, 0)),
        out_shape=jax.ShapeDtypeStruct((n_batch, n2, n1, FNET_W), F32),
        compiler_params=_cparams("parallel", "parallel"),
        name="seq_dft_stage_b",
    )(br, bi, mmat)


def _ctx_dft_body(xr_ref, xi_ref, m_ref, o_ref):
    x = jnp.concatenate([xr_ref[...], xi_ref[...]], axis=0).astype(BF16)
    o_ref[...] = _dot(m_ref[...], x)


def _ctx_dft(fr, fi, mmat, n_batch, n_ctx, row_block_off):
    return pl.pallas_call(
        _ctx_dft_body,
        grid=(n_batch,),
        in_specs=[pl.BlockSpec((n_ctx, FNET_W), lambda b: (row_block_off + b, 0)),
                  pl.BlockSpec((n_ctx, FNET_W), lambda b: (row_block_off + b, 0)),
                  pl.BlockSpec(mmat.shape, lambda b: (0, 0))],
        out_specs=pl.BlockSpec((n_ctx, FNET_W), lambda b: (b, 0)),
        out_shape=jax.ShapeDtypeStruct((n_batch * n_ctx, FNET_W), F32),
        compiler_params=_cparams("parallel"),
        name="ctx_dft",
    )(fr, fi, mmat)


def _dft_tables(seq, n_ctx):
    n1 = DFT_N1
    n2 = seq // n1
    k1 = np.arange(n1)
    ang1 = 2.0 * np.pi * np.outer(k1, k1) / n1
    c1, s1 = np.cos(ang1), np.sin(ang1)
    m_a = np.block([[c1, s1], [-s1, c1]])
    ang_t = 2.0 * np.pi * np.outer(k1, np.arange(n2)) / seq
    twc = np.repeat(np.cos(ang_t), LANES, axis=1)
    tws = np.repeat(np.sin(ang_t), LANES, axis=1)
    k2 = np.arange(n2)
    ang2 = 2.0 * np.pi * np.outer(k2, k2) / n2
    scale = 1.0 / math.sqrt(seq)
    c2, s2 = np.cos(ang2) * scale, np.sin(ang2) * scale
    eye = np.eye(8)
    m_b = np.concatenate([np.einsum("kl,ab->kabl", c2, eye).reshape(n2 * 8, 8 * n2),
                          np.einsum("kl,ab->kabl", s2, eye).reshape(n2 * 8, 8 * n2)], axis=1)
    kc = np.arange(n_ctx)
    angc = 2.0 * np.pi * np.outer(kc, kc) / n_ctx
    m_c = np.concatenate([np.cos(angc), np.sin(angc)], axis=1) / math.sqrt(n_ctx)
    ch = np.arange(FNET_GROUP_W)
    angg = 2.0 * np.pi * np.outer(ch, ch) / FNET_GROUP_W
    eg = np.eye(FNET_GROUPS)
    chan = np.concatenate([np.kron(eg, np.cos(angg)), -np.kron(eg, np.sin(angg))], axis=1)
    chan = chan / math.sqrt(FNET_GROUP_W)
    return (jnp.asarray(m_a, BF16), jnp.asarray(twc, F32), jnp.asarray(tws, F32),
            jnp.asarray(m_b, BF16), jnp.asarray(m_c, BF16), jnp.asarray(chan, F32))


def _split_dot_rhs(w_bf16, x):
    hi = x.astype(BF16)
    r1 = x - hi.astype(F32)
    mid = r1.astype(BF16)
    lo = (r1 - mid.astype(F32)).astype(BF16)
    return _dot(w_bf16, hi) + _dot(w_bf16, mid) + _dot(w_bf16, lo)


def _split2_dot(x, w_bf16):
    hi = x.astype(BF16)
    lo = (x - hi.astype(F32)).astype(BF16)
    return _dot(hi, w_bf16) + _dot(lo, w_bf16)


SSD_CPS = 2


def _ssd2_body(nb, ncc, xm_ref, xp_ref, xn_ref, dt_ref, z_ref, cw_ref, cb_ref, dtb_ref, alog_ref, dsk_ref,
               ng_ref, tri_ref, e_ref, o_ref, yloc_ref, s_ref, c_ref, ea_ref, cd_ref, hbe_ref, hf_ref, hb_ref):
    g = pl.program_id(1)
    t = pl.program_id(2)
    q = SSD_CHUNK
    gw = SSD_GROUP_W
    ns = nb * SSD_CPS

    @pl.when(t < nb)
    def _local():
        first = t <= 1
        last = (t == 0) | (t == nb - 1)
        prev = jnp.where(first, 0.0, xp_ref[...])
        nxt = jnp.where(last, 0.0, xn_ref[...])
        xfull = jnp.concatenate([prev, xm_ref[...], nxt], axis=0)
        rows_blk = SSD_CPS * q
        conv = cb_ref[...]
        for k in range(SSD_CONV_W):
            shift = (SSD_CONV_W // 2 - k) % xfull.shape[0]
            xk = xfull if shift == 0 else pltpu.roll(xfull, shift, 0)
            conv = conv + xk[HALO:HALO + rows_blk, :] * cw_ref[k:k + 1, :]
        act = _silu(conv)
        xs = act[:, :gw]
        bm = act[:, gw:gw + SSD_STATE]
        cm = act[:, gw + SSD_STATE:]

        lane = lax.broadcasted_iota(jnp.int32, (1, LANES), 1)
        is_fwd = lane < 8
        dsel = pltpu.roll(dt_ref[...], (LANES - 16 * g) % LANES, 1)
        dtv = _softplus(dsel + dtb_ref[pl.ds(g, 1), :])
        head_lane = ((lane % 8) < SSD_HPG) & (lane < 16)
        a_row = jnp.where(head_lane, -jnp.exp(alog_ref[pl.ds(g, 1), :]) * LOG2E, 0.0)
        adt = dtv * a_row
        tri = tri_ref[...]
        tri16 = tri.astype(BF16)
        allowed = (tri[:q] > 0.0, tri[q:] > 0.0)
        emat = e_ref[...]
        lane_q = lax.broadcasted_iota(jnp.int32, (q, LANES), 1)
        lo_half = lane_q < SSD_HEAD_DIM
        for cc in range(SSD_CPS):
            rows = slice(cc * q, (cc + 1) * q)
            cums = _split_dot_rhs(tri16, adt[rows])
            acum = jnp.where(is_fwd, cums[:q], cums[q:])
            acum_t = acum.T
            end = jnp.where(is_fwd, acum[q - 1:q, :], acum[0:1, :])
            dte = jnp.exp2(end - acum)
            stack = jnp.concatenate([dtv[rows], dte, jnp.broadcast_to(jnp.exp2(end), (16, LANES))], axis=0)
            ex = _split2_dot(stack, emat)
            xs_c = xs[rows]
            xdt2 = jnp.concatenate([xs_c, xs_c], axis=1) * ex[:q]
            c16 = cm[rows].astype(BF16)
            b_c = bm[rows]
            cbm = _dot_nt(c16, b_c.astype(BF16))
            ys = []
            for m in range(SSD_HPG // 2):
                mms, xhs = [], []
                for d in range(2):
                    xpair = xdt2[:, gw * d + LANES * m:gw * d + LANES * (m + 1)]
                    for half in range(2):
                        ln = 8 * d + 2 * m + half
                        seg = acum[:, ln:ln + 1] - acum_t[ln:ln + 1, :]
                        dec = jnp.exp2(jnp.where(allowed[d], seg, NEG_INF))
                        mms.append((cbm * dec).astype(BF16))
                        xhs.append(jnp.where(lo_half if half == 0 else ~lo_half, xpair, 0.0).astype(BF16))
                ys.append(_dot(jnp.concatenate(mms, axis=1), jnp.concatenate(xhs, axis=0)))
            y_loc = jnp.concatenate(ys, axis=1) + dsk_ref[...] * xs_c
            sts = _dot(b_c.T.astype(BF16), (xdt2 * ex[q:2 * q]).astype(BF16))
            pos = SSD_CPS * t + cc
            r0 = pl.multiple_of(pos * q, q)
            yloc_ref[pl.ds(r0, q), :] = y_loc
            s_ref[pl.ds(r0, q), :] = sts.astype(BF16)
            c_ref[pl.ds(r0, q), :] = c16
            ea_ref[pl.ds(r0, q), :] = jnp.exp2(acum)
            cd_ref[pl.ds(pl.multiple_of(pos * 8, 8), 8), :] = ex[2 * q:2 * q + 8]

    @pl.when(t == nb)
    def _backward_states():
        hb_ref[...] = jnp.zeros_like(hb_ref)
        hf_ref[...] = jnp.zeros_like(hf_ref)

        def body(i, carry):
            pos = jnp.where(i < ncc, ncc - 1 - i, ns - 1 + ncc - i)
            r0 = pl.multiple_of(pos * q, q)
            hb = hb_ref[...]
            hbe_ref[pl.ds(r0, q), :] = hb.astype(BF16)
            cd = cd_ref[pl.ds(pl.multiple_of(pos * 8, 8), 8), :][0:1, gw:]
            hb_ref[...] = hb * cd + s_ref[pl.ds(r0, q), gw:].astype(F32)
            return carry

        lax.fori_loop(0, ns, body, 0)

    @pl.when(t >= nb)
    def _emit():
        emat = e_ref[...]
        for cc in range(SSD_CPS):
            rows = slice(cc * q, (cc + 1) * q)
            pos = SSD_CPS * (t - nb) + cc
            r0 = pl.multiple_of(pos * q, q)
            hf = hf_ref[...]
            hcat = jnp.concatenate([hf.astype(BF16), hbe_ref[pl.ds(r0, q), :]], axis=1)
            yo = _dot(c_ref[pl.ds(r0, q), :], hcat) * _split2_dot(ea_ref[pl.ds(r0, q), :], emat)
            y = yloc_ref[pl.ds(r0, q), :] + yo[:, :gw] + yo[:, gw:]
            cd = cd_ref[pl.ds(pl.multiple_of(pos * 8, 8), 8), :][0:1, :gw]
            hf_ref[...] = hf * cd + s_ref[pl.ds(r0, q), :gw].astype(F32)
            y = y * _silu(z_ref[rows, :].astype(F32))
            y = y * lax.rsqrt(jnp.mean(y * y, axis=-1, keepdims=True) + NORM_EPS) * ng_ref[...]
            o_ref[rows, :] = y.astype(o_ref.dtype)


def _ssd2(xbc, dt, z, conv_w, conv_b, dtb_tab, alog_tab, dsk_x, ng, tri, emat, n_batch, seq, n_ctx):
    rows = xbc.shape[0]
    q = SSD_CHUNK
    blk = SSD_CPS * q
    assert n_ctx == blk and seq % blk == 0
    ncc = n_ctx // q
    nlb = seq // blk
    nb = 1 + nlb
    ns = nb * SSD_CPS
    per = blk // HALO

    def rb(b, pb):
        return jnp.where(pb == 0, n_batch * nlb + b, b * nlb + pb - 1)

    def rb_in(b, t):
        return rb(b, jnp.minimum(t, nb - 1))

    def rb_out(b, t):
        return rb(b, jnp.maximum(t - nb, 0))

    last_halo = rows // HALO - 1
    gw = SSD_GROUP_W
    return pl.pallas_call(
        functools.partial(_ssd2_body, nb, ncc),
        grid=(n_batch, SSD_GROUPS, 2 * nb),
        in_specs=[pl.BlockSpec((blk, SSD_XBC_W), lambda b, g, t: (rb_in(b, t), g)),
                  pl.BlockSpec((HALO, SSD_XBC_W),
                               lambda b, g, t: (jnp.maximum(rb_in(b, t) * per - 1, 0), g)),
                  pl.BlockSpec((HALO, SSD_XBC_W),
                               lambda b, g, t: (jnp.minimum(rb_in(b, t) * per + per, last_halo), g)),
                  pl.BlockSpec((blk, LANES), lambda b, g, t: (rb_in(b, t), 0)),
                  pl.BlockSpec((blk, gw), lambda b, g, t: (rb_out(b, t), g)),
                  pl.BlockSpec((SSD_CONV_W, SSD_XBC_W), lambda b, g, t: (0, g)),
                  pl.BlockSpec((1, SSD_XBC_W), lambda b, g, t: (0, g)),
                  pl.BlockSpec((8, LANES), lambda b, g, t: (0, 0)),
                  pl.BlockSpec((8, LANES), lambda b, g, t: (0, 0)),
                  pl.BlockSpec((1, gw), lambda b, g, t: (0, g)),
                  pl.BlockSpec((1, gw), lambda b, g, t: (0, g)),
                  pl.BlockSpec((2 * q, q), lambda b, g, t: (0, 0)),
                  pl.BlockSpec((LANES, 2 * gw), lambda b, g, t: (0, 0))],
        out_specs=pl.BlockSpec((blk, gw), lambda b, g, t: (rb_out(b, t), g)),
        out_shape=jax.ShapeDtypeStruct((rows, SSD_INNER), BF16),
        scratch_shapes=[pltpu.VMEM((ns * q, gw), F32),
                        pltpu.VMEM((ns * q, 2 * gw), BF16),
                        pltpu.VMEM((ns * q, SSD_STATE), BF16),
                        pltpu.VMEM((ns * q, LANES), F32),
                        pltpu.VMEM((ns * 8, 2 * gw), F32),
                        pltpu.VMEM((ns * q, gw), BF16),
                        pltpu.VMEM((SSD_STATE, gw), F32),
                        pltpu.VMEM((SSD_STATE, gw), F32)],
        compiler_params=_cparams("arbitrary", "arbitrary", "arbitrary"),
        name="ssd_bidir",
    )(xbc, xbc, xbc, dt, z, conv_w, conv_b, dtb_tab, alog_tab, dsk_x, ng, tri, emat)


def _stack_heads(qc):
    lane = lax.broadcasted_iota(jnp.int32, qc.shape, 1)
    lo = lane < HEAD_DIM
    zero = jnp.zeros_like(qc)
    return jnp.concatenate([jnp.where(lo, qc, zero), jnp.where(lo, zero, qc)], axis=0)


def _unstack_heads(o, n):
    lane = lax.broadcasted_iota(jnp.int32, (n, LANES), 1)
    return jnp.where(lane < HEAD_DIM, o[:n], o[n:])


def _softmax_pv(scores, values, sink=None):
    m = None
    for sc in scores:
        mx = jnp.max(sc, axis=-1, keepdims=True)
        m = mx if m is None else jnp.maximum(m, mx)
    if sink is not None:
        m = jnp.maximum(m, sink)
    den = None if sink is None else jnp.exp2(sink - m)
    acc = None
    for sc, v in zip(scores, values):
        pr = jnp.exp2(sc - m)
        sm = jnp.sum(pr, axis=-1, keepdims=True)
        den = sm if den is None else den + sm
        t = _dot(pr.astype(BF16), v)
        acc = t if acc is None else acc + t
    return acc * (1.0 / den)


def _win_body(q_ref, kp_ref, ko_ref, kn_ref, vp_ref, vo_ref, vn_ref, kc_ref, vc_ref, sink_ref, o_ref):
    n = pl.program_id(1)
    nb = pl.num_programs(1)
    wb = WIN_BLOCK
    kb = jnp.concatenate([kp_ref[...], ko_ref[...], kn_ref[...]], axis=0)
    vb = jnp.concatenate([vp_ref[...], vo_ref[...], vn_ref[...]], axis=0)
    r = lax.broadcasted_iota(jnp.int32, (wb, 3 * wb), 0)
    c = lax.broadcasted_iota(jnp.int32, (wb, 3 * wb), 1)
    pen_prev = jnp.where(n > 0, 0.0, NEG_INF)
    pen_next = jnp.where(n < nb - 1, 0.0, NEG_INF)
    pen = jnp.where(c < wb, jnp.where(c >= r, pen_prev, NEG_INF),
                    jnp.where(c < 2 * wb, 0.0, jnp.where(c - 2 * wb <= r, pen_next, NEG_INF)))
    pen = jnp.concatenate([pen, pen], axis=0)
    half = WIN_GQA * wb
    for m in range(WIN_GQA):
        sl = slice(LANES * m, LANES * (m + 1))
        qs = _stack_heads(q_ref[:, sl])
        s_loc = _dot_nt(qs, kb) + pen
        s_ctx = _dot_nt(qs, kc_ref[...])
        sink = jnp.concatenate([sink_ref[m * wb:(m + 1) * wb, 0:1],
                                sink_ref[half + m * wb:half + (m + 1) * wb, 0:1]], axis=0)
        o = _softmax_pv([s_loc, s_ctx], [vb, vc_ref[...]], sink)
        o_ref[:, sl] = _unstack_heads(o, wb).astype(o_ref.dtype)


def _window_attention(qw, kw, vw, sink_x, n_batch, seq, n_ctx):
    wb = WIN_BLOCK
    nb = seq // wb
    ctx_blk0 = n_batch * seq // n_ctx
    q_spec = pl.BlockSpec((wb, 512), lambda b, n: (b * nb + n, 0))
    prev = pl.BlockSpec((wb, LANES), lambda b, n: (b * nb + jnp.maximum(n - 1, 0), 0))
    own = pl.BlockSpec((wb, LANES), lambda b, n: (b * nb + n, 0))
    nxt = pl.BlockSpec((wb, LANES), lambda b, n: (b * nb + jnp.minimum(n + 1, nb - 1), 0))
    cspec = pl.BlockSpec((n_ctx, LANES), lambda b, n: (ctx_blk0 + b, 0))
    return pl.pallas_call(
        _win_body,
        grid=(n_batch, nb),
        in_specs=[q_spec, prev, own, nxt, prev, own, nxt, cspec, cspec,
                  pl.BlockSpec(sink_x.shape, lambda b, n: (0, 0))],
        out_specs=pl.BlockSpec((wb, 512), lambda b, n: (b * nb + n, 0)),
        out_shape=jax.ShapeDtypeStruct((n_batch * seq, 512), BF16),
        compiler_params=_cparams("parallel", "parallel"),
        name="window_attention",
    )(qw, kw, kw, kw, vw, vw, vw, kw, vw, sink_x)


NA_ROWS_PER_STEP = 4


def _na_body(n_rows, q_ref, k_ref, v_ref, kc_ref, vc_ref, bias_ref, o_ref):
    kr = NA_ROWS
    for j in range(NA_ROWS_PER_STEP):
        r = pl.program_id(1) * NA_ROWS_PER_STEP + j
        first_row = jnp.clip(r - kr // 2, 0, n_rows - kr)
        start = pl.multiple_of(first_row * GRID_W, GRID_W)
        ro0 = first_row - r + NA_ROWS - 1
        rows = slice(j * GRID_W, (j + 1) * GRID_W)
        for m in range(NA_HEADS // 2):
            sl = slice(LANES * m, LANES * (m + 1))
            qs = _stack_heads(q_ref[rows, sl])
            kw = k_ref[pl.ds(start, kr * GRID_W), sl]
            vw = v_ref[pl.ds(start, kr * GRID_W), sl]
            s_loc = _dot_nt(qs, kw) + bias_ref[ro0, 2 * GRID_W * m:2 * GRID_W * (m + 1), :]
            s_ctx = _dot_nt(qs, kc_ref[:, sl])
            o = _softmax_pv([s_loc, s_ctx], [vw, vc_ref[:, sl]])
            o_ref[rows, sl] = _unstack_heads(o, GRID_W).astype(o_ref.dtype)


def _na_attention(qn, kn, vn, bias_tab, n_batch, seq, n_ctx):
    n_rows = seq // GRID_W
    assert n_rows >= NA_ROWS and n_rows % NA_ROWS_PER_STEP == 0
    n_steps = n_rows // NA_ROWS_PER_STEP
    qrows = NA_ROWS_PER_STEP * GRID_W
    ctx_blk0 = n_batch * seq // n_ctx
    return pl.pallas_call(
        functools.partial(_na_body, n_rows),
        grid=(n_batch, n_steps),
        in_specs=[pl.BlockSpec((qrows, 512), lambda b, r: (b * n_steps + r, 0)),
                  pl.BlockSpec((seq, 512), lambda b, r: (b, 0)),
                  pl.BlockSpec((seq, 512), lambda b, r: (b, 0)),
                  pl.BlockSpec((n_ctx, 512), lambda b, r: (ctx_blk0 + b, 0)),
                  pl.BlockSpec((n_ctx, 512), lambda b, r: (ctx_blk0 + b, 0)),
                  _resident(bias_tab.shape)],
        out_specs=pl.BlockSpec((qrows, 512), lambda b, r: (b * n_steps + r, 0)),
        out_shape=jax.ShapeDtypeStruct((n_batch * seq, 512), BF16),
        compiler_params=_cparams("parallel", "arbitrary"),
        name="neighbourhood_attention",
    )(qn, kn, vn, kn, vn, bias_tab)


def _ctx_attn_body(qw_ref, kw_ref, vw_ref, qn_ref, kn_ref, vn_ref, sink_ref, ow_ref, on_ref):
    n = qw_ref.shape[0]
    qw = qw_ref[...]
    lane = lax.broadcasted_iota(jnp.int32, (n, LANES), 1)
    lo = lane < HEAD_DIM
    for m in range(WIN_GQA):
        qs = _stack_heads(qw[:, LANES * m:LANES * (m + 1)])
        r0, r1 = m * WIN_BLOCK, (WIN_GQA + m) * WIN_BLOCK
        sink = jnp.concatenate([jnp.broadcast_to(sink_ref[r0:r0 + 1, 0:1], (n, 1)),
                                jnp.broadcast_to(sink_ref[r1:r1 + 1, 0:1], (n, 1))], axis=0)
        o = _softmax_pv([_dot_nt(qs, kw_ref[...])], [vw_ref[...]], sink)
        ow_ref[:, LANES * m:LANES * (m + 1)] = jnp.where(lo, o[:n], o[n:]).astype(ow_ref.dtype)
    qn = qn_ref[...]
    for m in range(NA_HEADS // 2):
        sl = slice(LANES * m, LANES * (m + 1))
        qs = _stack_heads(qn[:, sl])
        o = _softmax_pv([_dot_nt(qs, kn_ref[:, sl])], [vn_ref[:, sl]])
        on_ref[:, sl] = _unstack_heads(o, n).astype(on_ref.dtype)


def _ctx_attention(qw, kw, vw, qn, kn, vn, sink_x, n_batch, seq, n_ctx):
    blk0 = n_batch * seq // n_ctx
    wide = pl.BlockSpec((n_ctx, 512), lambda b: (blk0 + b, 0))
    narrow = pl.BlockSpec((n_ctx, LANES), lambda b: (blk0 + b, 0))
    out = pl.BlockSpec((n_ctx, 512), lambda b: (b, 0))
    return pl.pallas_call(
        _ctx_attn_body,
        grid=(n_batch,),
        in_specs=[wide, narrow, narrow, wide, wide, wide,
                  pl.BlockSpec(sink_x.shape, lambda b: (0, 0))],
        out_specs=[out, out],
        out_shape=[jax.ShapeDtypeStruct((n_batch * n_ctx, 512), BF16)] * 2,
        compiler_params=_cparams("parallel"),
        name="context_attention",
    )(qw, kw, vw, qn, kn, vn, sink_x)


def _take(w, runs, axis):
    parts = []
    for run in runs:
        if run[0] is None:
            shape = list(w.shape)
            shape[axis] = run[1]
            parts.append(jnp.zeros(shape, w.dtype))
        else:
            parts.append(lax.slice_in_dim(w, run[0], run[1], axis=axis))
    return jnp.concatenate(parts, axis=axis)


def _xbc_runs():
    gn = SSD_GROUPS * SSD_STATE
    runs = []
    for g in range(SSD_GROUPS):
        runs.append((g * SSD_GROUP_W, (g + 1) * SSD_GROUP_W))
        runs.append((SSD_INNER + g * SSD_STATE, SSD_INNER + (g + 1) * SSD_STATE))
        runs.append((SSD_INNER + gn + g * SSD_STATE, SSD_INNER + gn + (g + 1) * SSD_STATE))
    return runs


def _dt_runs():
    runs = []
    for g in range(SSD_GROUPS):
        for d in range(2):
            runs.append((d * SSD_HEADS + g * SSD_HPG, d * SSD_HEADS + (g + 1) * SSD_HPG))
            runs.append((None, 8 - SSD_HPG))
    runs.append((None, LANES - 8 * 2 * SSD_GROUPS))
    return runs


def _win_head_runs():
    runs = []
    for m in range(WIN_GQA):
        for kvh in range(WIN_KV_HEADS):
            h = kvh * WIN_GQA + m
            runs.append((h * HEAD_DIM, (h + 1) * HEAD_DIM))
    return runs


def _rope_tables(seq):
    pos = np.arange(seq)
    row = (pos // GRID_W).astype(np.float32)
    col = (pos % GRID_W).astype(np.float32)
    n_freq = HEAD_DIM // 4
    inv = (np.float32(ROPE_THETA) ** (-np.arange(n_freq, dtype=np.float32) / n_freq)).astype(np.float32)
    ar = (row[:, None] * inv).astype(np.float32)
    ac = (col[:, None] * inv).astype(np.float32)
    cos_h = np.concatenate([np.cos(ar), np.cos(ar), np.cos(ac), np.cos(ac)], axis=1)
    sin_h = np.concatenate([-np.sin(ar), np.sin(ar), -np.sin(ac), np.sin(ac)], axis=1)
    ident_c = np.ones((ROW_TILE, HEAD_DIM), np.float32)
    ident_s = np.zeros((ROW_TILE, HEAD_DIM), np.float32)
    cos_t = np.concatenate([cos_h, ident_c], axis=0)
    sin_t = np.concatenate([sin_h, ident_s], axis=0)
    return (jnp.asarray(np.tile(cos_t, (1, 2)), F32), jnp.asarray(np.tile(sin_t, (1, 2)), F32))


def _na_bias_table(rpb, n_rows):
    kr = min(NA_ROWS, n_rows)
    cols = np.arange(GRID_W)
    col_start = np.clip(cols - NA_COLS // 2, 0, GRID_W - NA_COLS)
    col_ok = (cols[None] >= col_start[:, None]) & (cols[None] < col_start[:, None] + NA_COLS)
    col_off = np.clip(cols[None] - cols[:, None] + (NA_COLS - 1), 0, 2 * NA_COLS - 2)
    gathered = rpb.astype(F32)[:, :, col_off] * LOG2E
    gathered = jnp.where(jnp.asarray(col_ok)[None, None], gathered, NEG_INF)
    tabs = []
    for ro0 in range(NA_ROWS):
        rows = np.clip(ro0 + np.arange(kr), 0, 2 * NA_ROWS - 2)
        t = gathered[:, rows]
        tabs.append(jnp.transpose(t, (0, 2, 1, 3)).reshape(NA_HEADS * GRID_W, kr * GRID_W))
    return jnp.stack(tabs, axis=0)


def kernel(x, c, ctx, c_ctx, w_mod, b_mod, norm_mix_g, norm_ff_g, w_ff1, w_ff2, w_in_even, conv_w, conv_b,
           dt_bias, a_log, d_skip, ssd_norm_g, w_out_even, w_in_odd, q_norm_win, k_norm_win, sink_win,
           q_norm_na, k_norm_na, rpb_na, w_out_odd):
    n_batch, seq, d = x.shape
    n_ctx = ctx.shape[1]
    n_lat = n_batch * seq
    tm = ROW_TILE
    tiles_per_batch = seq // tm
    n_lat_tiles = n_lat // tm
    n_ctx_tiles = (n_batch * n_ctx) // tm
    n2 = seq // DFT_N1

    h = jnp.concatenate([x.reshape(n_lat, d), ctx.reshape(n_batch * n_ctx, d)], axis=0)

    cvec = jnp.concatenate([c, c_ctx[None], jnp.zeros((8 - n_batch - 1, d), F32)], axis=0)
    mod_all = _modulation(cvec, w_mod, b_mod)
    mod_all = mod_all.reshape(-1, 8, 6, 1, d)[:, :n_batch + 1].transpose(0, 2, 1, 3, 4)

    m_a, twc, tws, m_b, m_c, chan = _dft_tables(seq, n_ctx)
    xbc_runs, dt_runs = _xbc_runs(), _dt_runs()
    s1, s2, s3 = FNET_W, FNET_W + SSD_INNER, FNET_W + SSD_INNER + SSD_CONV_DIM
    w_four = _weight_product(w_in_even[:, :, :s1], chan)
    tri = jnp.asarray(np.concatenate([np.tril(np.ones((SSD_CHUNK, SSD_CHUNK))),
                                      np.triu(np.ones((SSD_CHUNK, SSD_CHUNK)))], axis=0), F32)
    emat = np.zeros((LANES, 2 * SSD_GROUP_W), np.float32)
    for dr in range(2):
        for r in range(SSD_HPG):
            c0 = dr * SSD_GROUP_W + r * SSD_HEAD_DIM
            emat[8 * dr + r, c0:c0 + SSD_HEAD_DIM] = 1.0
    emat = jnp.asarray(emat, BF16)

    cos_t, sin_t = _rope_tables(seq)
    gmat_slab = jnp.asarray(np.kron(np.eye(MXU_DIM // HEAD_DIM), np.full((HEAD_DIM, HEAD_DIM), 1.0 / HEAD_DIM)),
                          BF16)
    gmat128 = gmat_slab[:LANES, :LANES]
    win_runs = _win_head_runs()
    wq = WIN_Q_HEADS * HEAD_DIM
    wk = WIN_KV_HEADS * HEAD_DIM

    for i in range(DEPTH):
        need_ctx = i < DEPTH - 1
        modv = mod_all[i]
        j = i // 2
        gain_mix = norm_mix_g[i].reshape(1, d)
        if i % 2 == 0:
            wi = w_in_even[j]
            w_all = jnp.concatenate([w_four[j].astype(BF16), wi[:, s1:s2].astype(BF16),
                                     _take(wi[:, s2:s3].astype(BF16), xbc_runs, 1),
                                     _take(wi[:, s3:].astype(BF16), dt_runs, 1)], axis=1)
            fr, fi, z, xbc, dtr = _inproj_even(h, gain_mix, modv, w_all, n_batch, tiles_per_batch)

            br, bi = _dft_stage_a(fr.reshape(-1, n2, FNET_W), fi.reshape(-1, n2, FNET_W),
                                  m_a, twc, tws, n_batch, n2)
            f_lat = _dft_stage_b(br.reshape(n_batch, seq, FNET_W), bi.reshape(n_batch, seq, FNET_W),
                                 m_b, n_batch, n2).reshape(n_lat, FNET_W)
            f_ctx = _ctx_dft(fr, fi, m_c, n_batch, n_ctx, n_lat // n_ctx)

            def pad8(t):
                t = t.astype(F32).reshape(2, SSD_GROUPS, SSD_HPG).transpose(1, 0, 2)
                t = jnp.pad(t, ((0, 8 - SSD_GROUPS), (0, 0), (0, 8 - SSD_HPG))).reshape(8, 16)
                return jnp.pad(t, ((0, 0), (0, LANES - 16)))
            y_ssd = _ssd2(xbc, dtr, z, _take(conv_w[j], xbc_runs, 1),
                         _take(conv_b[j].reshape(1, -1), xbc_runs, 1),
                         pad8(dt_bias[j]), pad8(a_log[j]),
                         jnp.repeat(d_skip[j].astype(F32), SSD_HEAD_DIM).reshape(1, -1),
                         ssd_norm_g[j].astype(F32).reshape(1, -1), tri, emat, n_batch, seq, n_ctx)

            w_out = w_out_even[j].astype(BF16)
            h = _outproj(h, modv, f_lat, y_ssd, w_out, n_batch, tiles_per_batch, 0, 0, 0, n_lat_tiles)
            h = _outproj(h, modv, f_ctx, y_ssd, w_out, n_batch, tiles_per_batch, n_lat_tiles, 0,
                         n_lat_tiles, n_ctx_tiles)
        else:
            wi = w_in_odd[j]
            w_all = jnp.concatenate([_take(wi[:, :wq].astype(BF16), win_runs, 1), wi[:, wq:].astype(BF16)],
                                    axis=1)
            tile8 = lambda g, reps, scale=1.0: (jnp.tile(g.astype(F32), reps) * scale).reshape(1, -1)
            qscale = HEAD_DIM ** -0.5 * LOG2E
            gains = (tile8(q_norm_win[j], 8, qscale), tile8(k_norm_win[j], 2),
                     tile8(q_norm_na[j], 8, qscale), tile8(k_norm_na[j], 8))
            qw, kw, vw, qn, kn, vn = _inproj_odd(h, gain_mix, modv, w_all, cos_t, sin_t, gmat_slab, gmat128,
                                                 gains, n_batch, tiles_per_batch)
            sink = sink_win[j].astype(F32) * LOG2E
            sink_x = jnp.broadcast_to(jnp.repeat(sink, WIN_BLOCK)[:, None], (WIN_Q_HEADS * WIN_BLOCK, LANES))
            o_win = _window_attention(qw, kw, vw, sink_x, n_batch, seq, n_ctx)
            bias_tab = _na_bias_table(rpb_na[j], seq // GRID_W)
            o_na = _na_attention(qn, kn, vn, bias_tab, n_batch, seq, n_ctx)
            wo = w_out_odd[j]
            w_out = jnp.concatenate([_take(wo[:wq].astype(BF16), win_runs, 0), wo[wq:].astype(BF16)], axis=0)
            if need_ctx:
                oc_win, oc_na = _ctx_attention(qw, kw, vw, qn, kn, vn, sink_x, n_batch, seq, n_ctx)
            h = _outproj(h, modv, o_win, o_na, w_out, n_batch, tiles_per_batch, 0, 0, 0, n_lat_tiles)
            if need_ctx:
                h = _outproj(h, modv, oc_win, oc_na, w_out, n_batch, tiles_per_batch, n_lat_tiles, 0, 0,
                             n_ctx_tiles)
        n_tiles = n_lat_tiles + n_ctx_tiles if need_ctx else n_lat_tiles
        h = _ffn(h, norm_ff_g[i].reshape(1, d), modv, w_ff1[i].astype(BF16), w_ff2[i].astype(BF16),
                 n_batch, tiles_per_batch, n_tiles)
    return h[:n_lat].reshape(n_batch, seq, d)
```

```python
import functools
import math

import numpy as np
import jax
import jax.numpy as jnp
from jax import lax
from jax.experimental import pallas as pl
from jax.experimental.pallas import tpu as pltpu

F32 = jnp.float32
BF16 = jnp.bfloat16
HIGHEST = lax.Precision.HIGHEST

D_MODEL = 1024
DEPTH = 4
GRID_W = 64
D_FF = 4 * D_MODEL
NORM_EPS = 1e-6
NEG_INF = -1e30

FNET_GROUPS = 8
FNET_GROUP_W = 64
FNET_W = FNET_GROUPS * FNET_GROUP_W
SSD_HEAD_DIM = 64
SSD_HEADS = 24
SSD_GROUPS = 4
SSD_HPG = SSD_HEADS // SSD_GROUPS
SSD_STATE = 128
SSD_INNER = SSD_HEADS * SSD_HEAD_DIM
SSD_GROUP_W = SSD_HPG * SSD_HEAD_DIM
SSD_XBC_W = SSD_GROUP_W + 2 * SSD_STATE
SSD_CONV_DIM = SSD_INNER + 2 * SSD_GROUPS * SSD_STATE
SSD_CONV_W = 5
SSD_CHUNK = 128
HEAD_DIM = 64
WIN_Q_HEADS = 8
WIN_KV_HEADS = 2
WIN_GQA = WIN_Q_HEADS // WIN_KV_HEADS
WIN_BLOCK = 128
NA_HEADS = 8
NA_ROWS = 8
NA_COLS = 16
ROPE_THETA = 10000.0
LOG2E = 1.4426950408889634

LANES = 128
MXU_DIM = 256
HALO = 8
ROW_TILE = 512
FF_TILE = 4096
DFT_N1 = 128
VMEM_LIMIT = 56 * 1024 * 1024


def _cparams(*sem):
    return pltpu.CompilerParams(dimension_semantics=sem, vmem_limit_bytes=VMEM_LIMIT)


def _silu(x):
    return x / (1.0 + jnp.exp2(x * -LOG2E))


def _softplus(x):
    return jnp.maximum(x, 0.0) + jnp.log(1.0 + jnp.exp(-jnp.abs(x)))


def _norm_mod(h, g, shift, scale):
    ms = jnp.mean(h * h, axis=-1, keepdims=True)
    y = h * lax.rsqrt(ms + NORM_EPS) * g
    return y * (1.0 + scale) + shift


def _dot(a, b):
    return jnp.dot(a, b, preferred_element_type=F32)


def _dot_nt(a, b):
    return lax.dot_general(a, b, (((1,), (1,)), ((), ())), preferred_element_type=F32)


def _mod_body(c_ref, w_ref, b_ref, o_ref):
    s = _silu(c_ref[...])
    o_ref[...] = jnp.dot(s, w_ref[...], precision=HIGHEST, preferred_element_type=F32) + b_ref[...]


def _modulation(cvec, w_mod, b_mod):
    depth, d, n = w_mod.shape
    tn = 1536
    return pl.pallas_call(
        _mod_body,
        grid=(depth, n // tn),
        in_specs=[pl.BlockSpec((8, d), lambda i, j: (0, 0)),
                  pl.BlockSpec((None, d, tn), lambda i, j: (i, 0, j)),
                  pl.BlockSpec((None, 1, tn), lambda i, j: (i, 0, j))],
        out_specs=pl.BlockSpec((None, 8, tn), lambda i, j: (i, 0, j)),
        out_shape=jax.ShapeDtypeStruct((depth, 8, n), F32),
        compiler_params=_cparams("parallel", "parallel"),
        name="modulation",
    )(cvec, w_mod, b_mod.reshape(depth, 1, n))


def _wprod_body(a_ref, b_ref, o_ref):
    o_ref[...] = jnp.dot(a_ref[...], b_ref[...], precision=HIGHEST,
                         preferred_element_type=F32).astype(o_ref.dtype)


def _weight_product(a, k, b):
    n, m, _ = a.shape
    p = b.shape[1]
    return pl.pallas_call(
        _wprod_body,
        grid=(n,),
        in_specs=[pl.BlockSpec((None, m, k), lambda i: (i, 0, 0)),
                  pl.BlockSpec((k, p), lambda i: (0, 0))],
        out_specs=pl.BlockSpec((None, m, p), lambda i: (i, 0, 0)),
        out_shape=jax.ShapeDtypeStruct((n, m, p), BF16),
        compiler_params=_cparams("parallel"),
        name="fold_channel_dft",
    )(a, b)


def _resident(shape):
    return pl.BlockSpec(shape, lambda *_: (0,) * len(shape), pipeline_mode=pl.Buffered(1))


def _layer(arr, j):
    nd = arr.ndim
    return pl.BlockSpec((None,) + arr.shape[1:], lambda *_: (j,) + (0,) * (nd - 1),
                        pipeline_mode=pl.Buffered(1))


def _mod_spec(layer, which, tiles_per_batch, n_batch, tile_off):
    def imap(m):
        return (layer, which, jnp.minimum((m + tile_off) // tiles_per_batch, n_batch), 0, 0)
    return pl.BlockSpec((None, None, None, 1, D_MODEL), imap)


def _inproj_even_body(h_ref, g_ref, sh_ref, sc_ref, wf_ref, w_ref, wdt_ref,
                      fr_ref, fi_ref, z_ref, xbc_ref, dt_ref):
    u = _norm_mod(h_ref[...], g_ref[...], sh_ref[...], sc_ref[...]).astype(BF16)
    fr_ref[...] = _dot(u, wf_ref[:, :FNET_W])
    fi_ref[...] = _dot(u, wf_ref[:, FNET_W:])
    z_ref[...] = _dot(u, w_ref[:, FNET_W:FNET_W + SSD_INNER]).astype(z_ref.dtype)
    c0 = FNET_W + SSD_INNER
    xbc = _dot(u, w_ref[:, c0:c0 + SSD_CONV_DIM])
    gn = SSD_GROUPS * SSD_STATE
    for g in range(SSD_GROUPS):
        o = g * SSD_XBC_W
        b0 = SSD_INNER + g * SSD_STATE
        xbc_ref[:, o:o + SSD_GROUP_W] = xbc[:, g * SSD_GROUP_W:(g + 1) * SSD_GROUP_W]
        xbc_ref[:, o + SSD_GROUP_W:o + SSD_GROUP_W + SSD_STATE] = xbc[:, b0:b0 + SSD_STATE]
        xbc_ref[:, o + SSD_GROUP_W + SSD_STATE:o + SSD_XBC_W] = xbc[:, b0 + gn:b0 + gn + SSD_STATE]
    dt_ref[...] = _dot(u, wdt_ref[...])


def _inproj_even(h, layer, j, gains, mod_all, w_four, w_main, w_dt, n_batch, tiles_per_batch):
    rows = h.shape[0]
    widths = (FNET_W, FNET_W, SSD_INNER, SSD_CONV_DIM, LANES)
    dtypes = (F32, F32, BF16, F32, F32)
    tm = ROW_TILE
    return pl.pallas_call(
        _inproj_even_body,
        grid=(rows // tm,),
        in_specs=[pl.BlockSpec((tm, D_MODEL), lambda m: (m, 0)),
                  _layer(gains, layer),
                  _mod_spec(layer, 0, tiles_per_batch, n_batch, 0),
                  _mod_spec(layer, 1, tiles_per_batch, n_batch, 0),
                  _layer(w_four, j), _layer(w_main, j), _layer(w_dt, j)],
        out_specs=[pl.BlockSpec((tm, wd), lambda m: (m, 0)) for wd in widths],
        out_shape=[jax.ShapeDtypeStruct((rows, wd), dt) for wd, dt in zip(widths, dtypes)],
        compiler_params=_cparams("parallel"),
        name="inproj_even",
    )(h, gains, mod_all, mod_all, w_four, w_main, w_dt)


def _group_rms(x, gmat_ref, gain):
    sq = x * x
    hi = sq.astype(BF16)
    lo = (sq - hi.astype(F32)).astype(BF16)
    gw = gmat_ref.shape[0]
    parts = []
    for c0 in range(0, x.shape[1], gw):
        parts.append(_dot(hi[:, c0:c0 + gw], gmat_ref[...]) + _dot(lo[:, c0:c0 + gw], gmat_ref[...]))
    ms = parts[0] if len(parts) == 1 else jnp.concatenate(parts, axis=1)
    return x * lax.rsqrt(ms + NORM_EPS) * gain


def _rope(y, cos, sin_signed):
    w = y.shape[-1]
    reps = w // LANES
    lane = lax.broadcasted_iota(jnp.int32, y.shape, 1)
    first = (lane % 32) < 16
    partner = jnp.where(first, pltpu.roll(y, w - 16, 1), pltpu.roll(y, 16, 1))
    if reps > 1:
        cos = jnp.concatenate([cos] * reps, axis=1)
        sin_signed = jnp.concatenate([sin_signed] * reps, axis=1)
    return y * cos + partner * sin_signed


def _inproj_odd_body(h_ref, g_ref, sh_ref, sc_ref, wq_ref, w_ref, cos_ref, sin_ref, gslab_ref, g128_ref,
                     hg_ref, qw_ref, kw_ref, vw_ref, qn_ref, kn_ref, vn_ref):
    u = _norm_mod(h_ref[...], g_ref[...], sh_ref[...], sc_ref[...]).astype(BF16)
    cos = cos_ref[...]
    sin = sin_ref[...]
    wq = WIN_Q_HEADS * HEAD_DIM
    wk = WIN_KV_HEADS * HEAD_DIM
    nh = NA_HEADS * HEAD_DIM
    qw = _group_rms(_dot(u, wq_ref[...]), gslab_ref, hg_ref[0:1, :])
    qw_ref[...] = _rope(qw, cos, sin).astype(BF16)
    kw = _group_rms(_dot(u, w_ref[:, wq:wq + wk]), g128_ref, hg_ref[1:2, :wk])
    kw_ref[...] = _rope(kw, cos, sin).astype(BF16)
    vw_ref[...] = _dot(u, w_ref[:, wq + wk:wq + 2 * wk]).astype(BF16)
    c0 = wq + 2 * wk
    qn_ref[...] = _group_rms(_dot(u, w_ref[:, c0:c0 + nh]), gslab_ref, hg_ref[2:3, :]).astype(BF16)
    kn_ref[...] = _group_rms(_dot(u, w_ref[:, c0 + nh:c0 + 2 * nh]), gslab_ref, hg_ref[3:4, :]).astype(BF16)
    vn_ref[...] = _dot(u, w_ref[:, c0 + 2 * nh:c0 + 3 * nh]).astype(BF16)


def _inproj_odd(h, layer, j, gains, mod_all, wq_perm, w_main, cos_t, sin_t, gmat_slab, gmat128, head_gains,
                n_batch, tiles_per_batch):
    rows = h.shape[0]
    tm = ROW_TILE
    widths = (512, 128, 128, 512, 512, 512)
    n_lat_tiles = n_batch * tiles_per_batch

    def rope_map(m):
        return (jnp.where(m < n_lat_tiles, m % tiles_per_batch, tiles_per_batch), 0)

    return pl.pallas_call(
        _inproj_odd_body,
        grid=(rows // tm,),
        in_specs=[pl.BlockSpec((tm, D_MODEL), lambda m: (m, 0)),
                  _layer(gains, layer),
                  _mod_spec(layer, 0, tiles_per_batch, n_batch, 0),
                  _mod_spec(layer, 1, tiles_per_batch, n_batch, 0),
                  _layer(wq_perm, j), _layer(w_main, j),
                  pl.BlockSpec((tm, LANES), rope_map),
                  pl.BlockSpec((tm, LANES), rope_map),
                  _resident(gmat_slab.shape), _resident(gmat128.shape), _layer(head_gains, j)],
        out_specs=[pl.BlockSpec((tm, wd), lambda m: (m, 0)) for wd in widths],
        out_shape=[jax.ShapeDtypeStruct((rows, wd), BF16) for wd in widths],
        compiler_params=_cparams("parallel"),
        name="inproj_odd",
    )(h, gains, mod_all, mod_all, wq_perm, w_main, cos_t, sin_t, gmat_slab, gmat128, head_gains)


def _outproj_body(k1, k2, b_row0, h_ref, gate_ref, a1_ref, a2_ref, wa_ref, wb_ref, o_ref):
    acc = _dot(a1_ref[...].astype(BF16), wa_ref[0:k1, :])
    acc += _dot(a2_ref[...].astype(BF16), wb_ref[b_row0:b_row0 + k2, :])
    o_ref[...] = h_ref[...] + gate_ref[...] * acc


def _outproj(h, layer, mod_all, a1, a2, wa, wb, j, b_row0, n_batch, tiles_per_batch, tile_off, a1_off,
             a2_off, n_tiles):
    tm = ROW_TILE
    k1, k2 = a1.shape[1], a2.shape[1]
    return pl.pallas_call(
        functools.partial(_outproj_body, k1, k2, b_row0),
        grid=(n_tiles,),
        in_specs=[pl.BlockSpec((tm, D_MODEL), lambda m: (m + tile_off, 0)),
                  _mod_spec(layer, 2, tiles_per_batch, n_batch, tile_off),
                  pl.BlockSpec((tm, k1), lambda m: (m + a1_off, 0)),
                  pl.BlockSpec((tm, k2), lambda m: (m + a2_off, 0)),
                  _layer(wa, j), _layer(wb, j)],
        out_specs=pl.BlockSpec((tm, D_MODEL), lambda m: (m + tile_off, 0)),
        out_shape=jax.ShapeDtypeStruct(h.shape, F32),
        input_output_aliases={0: 0},
        compiler_params=_cparams("parallel"),
        name="outproj",
    )(h, mod_all, a1, a2, wa, wb)


def _ffn_body(h_ref, g_ref, sh_ref, sc_ref, gate_ref, w1_ref, w2_ref, o_ref):
    h = h_ref[...]
    u = _norm_mod(h, g_ref[...], sh_ref[...], sc_ref[...]).astype(BF16)
    acc = None
    for k in range(D_FF // FF_TILE):
        sl = slice(k * FF_TILE, (k + 1) * FF_TILE)
        a = jnp.maximum(_dot(u, w1_ref[:, sl]), 0.0)
        t = _dot((a * a).astype(BF16), w2_ref[sl, :])
        acc = t if acc is None else acc + t
    o_ref[...] = h + gate_ref[...] * acc


def _ffn(h, layer, gains, mod_all, w1, w2, n_batch, tiles_per_batch, n_tiles):
    tm = ROW_TILE
    return pl.pallas_call(
        _ffn_body,
        grid=(n_tiles,),
        in_specs=[pl.BlockSpec((tm, D_MODEL), lambda m: (m, 0)),
                  _layer(gains, layer),
                  _mod_spec(layer, 3, tiles_per_batch, n_batch, 0),
                  _mod_spec(layer, 4, tiles_per_batch, n_batch, 0),
                  _mod_spec(layer, 5, tiles_per_batch, n_batch, 0),
                  _layer(w1, layer), _layer(w2, layer)],
        out_specs=pl.BlockSpec((tm, D_MODEL), lambda m: (m, 0)),
        out_shape=jax.ShapeDtypeStruct((n_tiles * tm, D_MODEL), F32),
        compiler_params=_cparams("parallel"),
        name="ffn",
    )(h, gains, mod_all, mod_all, mod_all, w1, w2)


def _dft_a_body(xr_ref, xi_ref, m_ref, c_ref, s_ref, br_ref, bi_ref):
    n1 = DFT_N1
    reps = FNET_W // LANES
    for j in range(xr_ref.shape[1]):
        x = jnp.concatenate([xr_ref[:, j, :], xi_ref[:, j, :]], axis=0).astype(BF16)
        a = _dot(m_ref[...], x)
        ar, ai = a[:n1], a[n1:]
        c = jnp.concatenate([c_ref[:, LANES * j:LANES * (j + 1)]] * reps, axis=1)
        s = jnp.concatenate([s_ref[:, LANES * j:LANES * (j + 1)]] * reps, axis=1)
        br_ref[:, j, :] = ar * c + ai * s
        bi_ref[:, j, :] = ai * c - ar * s


def _dft_stage_a(xr, xi, mmat, twc, tws, n_batch, n2):
    n1 = DFT_N1
    lb = 8
    blk = pl.BlockSpec((n1, lb, FNET_W), lambda b, j: (b, j, 0))
    return pl.pallas_call(
        _dft_a_body,
        grid=(n_batch, n2 // lb),
        in_specs=[blk, blk,
                  pl.BlockSpec((2 * n1, 2 * n1), lambda b, j: (0, 0)),
                  pl.BlockSpec((n1, lb * LANES), lambda b, j: (0, j)),
                  pl.BlockSpec((n1, lb * LANES), lambda b, j: (0, j))],
        out_specs=[blk, blk],
        out_shape=[jax.ShapeDtypeStruct((n_batch * n1, n2, FNET_W), F32)] * 2,
        compiler_params=_cparams("parallel", "parallel"),
        name="seq_dft_stage_a",
    )(xr, xi, mmat, twc, tws)


def _dft_b_body(br_ref, bi_ref, m_ref, o_ref):
    x = jnp.concatenate([br_ref[...], bi_ref[...]], axis=0).astype(BF16)
    res = _dot(m_ref[...], x)
    o_ref[...] = res.reshape(o_ref.shape)


def _dft_stage_b(br, bi, mmat, n_batch, n2):
    n1 = DFT_N1
    kb = 8
    return pl.pallas_call(
        _dft_b_body,
        grid=(n_batch, n1 // kb),
        in_specs=[pl.BlockSpec((None, kb * n2, FNET_W), lambda b, j: (b, j, 0)),
                  pl.BlockSpec((None, kb * n2, FNET_W), lambda b, j: (b, j, 0)),
                  pl.BlockSpec(mmat.shape, lambda b, j: (0, 0))],
        out_specs=pl.BlockSpec((None, n2, kb, FNET_W), lambda b, j: (b, 0, j, 0)),
        out_shape=jax.ShapeDtypeStruct((n_batch, n2, n1, FNET_W), F32),
        compiler_params=_cparams("parallel", "parallel"),
        name="seq_dft_stage_b",
    )(br, bi, mmat)


def _ctx_dft_body(xr_ref, xi_ref, m_ref, o_ref):
    x = jnp.concatenate([xr_ref[...], xi_ref[...]], axis=0).astype(BF16)
    o_ref[...] = _dot(m_ref[...], x)


def _ctx_dft(fr, fi, mmat, n_batch, n_ctx, row_block_off):
    return pl.pallas_call(
        _ctx_dft_body,
        grid=(n_batch,),
        in_specs=[pl.BlockSpec((n_ctx, FNET_W), lambda b: (row_block_off + b, 0)),
                  pl.BlockSpec((n_ctx, FNET_W), lambda b: (row_block_off + b, 0)),
                  pl.BlockSpec(mmat.shape, lambda b: (0, 0))],
        out_specs=pl.BlockSpec((n_ctx, FNET_W), lambda b: (b, 0)),
        out_shape=jax.ShapeDtypeStruct((n_batch * n_ctx, FNET_W), F32),
        compiler_params=_cparams("parallel"),
        name="ctx_dft",
    )(fr, fi, mmat)


def _dft_tables(seq, n_ctx):
    n1 = DFT_N1
    n2 = seq // n1
    k1 = np.arange(n1)
    ang1 = 2.0 * np.pi * np.outer(k1, k1) / n1
    c1, s1 = np.cos(ang1), np.sin(ang1)
    m_a = np.block([[c1, s1], [-s1, c1]])
    ang_t = 2.0 * np.pi * np.outer(k1, np.arange(n2)) / seq
    twc = np.repeat(np.cos(ang_t), LANES, axis=1)
    tws = np.repeat(np.sin(ang_t), LANES, axis=1)
    k2 = np.arange(n2)
    ang2 = 2.0 * np.pi * np.outer(k2, k2) / n2
    scale = 1.0 / math.sqrt(seq)
    c2, s2 = np.cos(ang2) * scale, np.sin(ang2) * scale
    eye = np.eye(8)
    m_b = np.concatenate([np.einsum("kl,ab->kabl", c2, eye).reshape(n2 * 8, 8 * n2),
                          np.einsum("kl,ab->kabl", s2, eye).reshape(n2 * 8, 8 * n2)], axis=1)
    kc = np.arange(n_ctx)
    angc = 2.0 * np.pi * np.outer(kc, kc) / n_ctx
    m_c = np.concatenate([np.cos(angc), np.sin(angc)], axis=1) / math.sqrt(n_ctx)
    ch = np.arange(FNET_GROUP_W)
    angg = 2.0 * np.pi * np.outer(ch, ch) / FNET_GROUP_W
    eg = np.eye(FNET_GROUPS)
    chan = np.concatenate([np.kron(eg, np.cos(angg)), -np.kron(eg, np.sin(angg))], axis=1)
    chan = chan / math.sqrt(FNET_GROUP_W)
    return (jnp.asarray(m_a, BF16), jnp.asarray(twc, F32), jnp.asarray(tws, F32),
            jnp.asarray(m_b, BF16), jnp.asarray(m_c, BF16), jnp.asarray(chan, F32))


def _split_dot_rhs(w_bf16, x):
    hi = x.astype(BF16)
    r1 = x - hi.astype(F32)
    mid = r1.astype(BF16)
    lo = (r1 - mid.astype(F32)).astype(BF16)
    return _dot(w_bf16, hi) + _dot(w_bf16, mid) + _dot(w_bf16, lo)


def _split2_dot(x, w_bf16):
    hi = x.astype(BF16)
    lo = (x - hi.astype(F32)).astype(BF16)
    return _dot(hi, w_bf16) + _dot(lo, w_bf16)


SSD_CPS = 2


def _ssd2_body(nb, ncc, xm_ref, xp_ref, xn_ref, dt_ref, z_ref, cw_ref, cb_ref, dtb_ref, alog_ref, dsk_ref,
               ng_ref, tri_ref, e_ref, o_ref, yloc_ref, s_ref, c_ref, ea_ref, cd_ref, hbe_ref, hf_ref, hb_ref):
    g = pl.program_id(1)
    t = pl.program_id(2)
    q = SSD_CHUNK
    gw = SSD_GROUP_W
    ns = nb * SSD_CPS

    @pl.when(t < nb)
    def _local():
        first = t <= 1
        last = (t == 0) | (t == nb - 1)
        prev = jnp.where(first, 0.0, xp_ref[...])
        nxt = jnp.where(last, 0.0, xn_ref[...])
        xfull = jnp.concatenate([prev, xm_ref[...], nxt], axis=0)
        rows_blk = SSD_CPS * q
        conv = cb_ref[...]
        for k in range(SSD_CONV_W):
            shift = (SSD_CONV_W // 2 - k) % xfull.shape[0]
            xk = xfull if shift == 0 else pltpu.roll(xfull, shift, 0)
            conv = conv + xk[HALO:HALO + rows_blk, :] * cw_ref[k:k + 1, :]
        act = _silu(conv)
        xs = act[:, :gw]
        bm = act[:, gw:gw + SSD_STATE]
        cm = act[:, gw + SSD_STATE:]

        lane = lax.broadcasted_iota(jnp.int32, (1, LANES), 1)
        is_fwd = lane < SSD_HPG
        dsel = pltpu.roll(dt_ref[...], (LANES - SSD_HPG * g) % LANES, 1)
        dtv = _softplus(dsel + dtb_ref[pl.ds(g, 1), :])
        head_lane = is_fwd | ((lane >= SSD_HEADS) & (lane < SSD_HEADS + SSD_HPG))
        a_row = jnp.where(head_lane, -jnp.exp(alog_ref[pl.ds(g, 1), :]) * LOG2E, 0.0)
        adt = dtv * a_row
        tri = tri_ref[...]
        tri16 = tri.astype(BF16)
        allowed = (tri[:q] > 0.0, tri[q:] > 0.0)
        emat = e_ref[...]
        lane_q = lax.broadcasted_iota(jnp.int32, (q, LANES), 1)
        lo_half = lane_q < SSD_HEAD_DIM
        for cc in range(SSD_CPS):
            rows = slice(cc * q, (cc + 1) * q)
            cums = _split_dot_rhs(tri16, adt[rows])
            acum = jnp.where(is_fwd, cums[:q], cums[q:])
            acum_t = acum.T
            end = jnp.where(is_fwd, acum[q - 1:q, :], acum[0:1, :])
            dte = jnp.exp2(end - acum)
            stack = jnp.concatenate([dtv[rows], dte, jnp.broadcast_to(jnp.exp2(end), (16, LANES))], axis=0)
            ex = _split2_dot(stack, emat)
            xs_c = xs[rows]
            xdt2 = jnp.concatenate([xs_c, xs_c], axis=1) * ex[:q]
            c16 = cm[rows].astype(BF16)
            b_c = bm[rows]
            cbm = _dot_nt(c16, b_c.astype(BF16))
            ys = []
            for m in range(SSD_HPG // 2):
                mms, xhs = [], []
                for d in range(2):
                    xpair = xdt2[:, gw * d + LANES * m:gw * d + LANES * (m + 1)]
                    for half in range(2):
                        ln = SSD_HEADS * d + 2 * m + half
                        seg = acum[:, ln:ln + 1] - acum_t[ln:ln + 1, :]
                        dec = jnp.exp2(jnp.where(allowed[d], seg, NEG_INF))
                        mms.append((cbm * dec).astype(BF16))
                        xhs.append(jnp.where(lo_half if half == 0 else ~lo_half, xpair, 0.0).astype(BF16))
                ys.append(_dot(jnp.concatenate(mms, axis=1), jnp.concatenate(xhs, axis=0)))
            y_loc = jnp.concatenate(ys, axis=1) + dsk_ref[...] * xs_c
            sts = _dot(b_c.T.astype(BF16), (xdt2 * ex[q:2 * q]).astype(BF16))
            pos = SSD_CPS * t + cc
            r0 = pl.multiple_of(pos * q, q)
            yloc_ref[pl.ds(r0, q), :] = y_loc
            s_ref[pl.ds(r0, q), :] = sts.astype(BF16)
            c_ref[pl.ds(r0, q), :] = c16
            ea_ref[pl.ds(r0, q), :] = jnp.exp2(acum)
            cd_ref[pl.ds(pl.multiple_of(pos * 8, 8), 8), :] = ex[2 * q:2 * q + 8]

    @pl.when(t == nb)
    def _backward_states():
        hb_ref[...] = jnp.zeros_like(hb_ref)
        hf_ref[...] = jnp.zeros_like(hf_ref)

        def body(i, carry):
            pos = jnp.where(i < ncc, ncc - 1 - i, ns - 1 + ncc - i)
            r0 = pl.multiple_of(pos * q, q)
            hb = hb_ref[...]
            hbe_ref[pl.ds(r0, q), :] = hb.astype(BF16)
            cd = cd_ref[pl.ds(pl.multiple_of(pos * 8, 8), 8), :][0:1, gw:]
            hb_ref[...] = hb * cd + s_ref[pl.ds(r0, q), gw:].astype(F32)
            return carry

        lax.fori_loop(0, ns, body, 0)

    @pl.when(t >= nb)
    def _emit():
        emat = e_ref[...]
        for cc in range(SSD_CPS):
            rows = slice(cc * q, (cc + 1) * q)
            pos = SSD_CPS * (t - nb) + cc
            r0 = pl.multiple_of(pos * q, q)
            hf = hf_ref[...]
            hcat = jnp.concatenate([hf.astype(BF16), hbe_ref[pl.ds(r0, q), :]], axis=1)
            yo = _dot(c_ref[pl.ds(r0, q), :], hcat) * _split2_dot(ea_ref[pl.ds(r0, q), :], emat)
            y = yloc_ref[pl.ds(r0, q), :] + yo[:, :gw] + yo[:, gw:]
            cd = cd_ref[pl.ds(pl.multiple_of(pos * 8, 8), 8), :][0:1, :gw]
            hf_ref[...] = hf * cd + s_ref[pl.ds(r0, q), :gw].astype(F32)
            y = y * _silu(z_ref[rows, :].astype(F32))
            y = y * lax.rsqrt(jnp.mean(y * y, axis=-1, keepdims=True) + NORM_EPS) * ng_ref[...]
            o_ref[rows, :] = y.astype(o_ref.dtype)


def _ssd2(xbc, dt, z, j, conv_w, conv_b, dtb_tab, alog_tab, dsk_x, ng, tri, emat, n_batch, seq, n_ctx):
    rows = xbc.shape[0]
    q = SSD_CHUNK
    blk = SSD_CPS * q
    assert n_ctx == blk and seq % blk == 0
    ncc = n_ctx // q
    nlb = seq // blk
    nb = 1 + nlb
    ns = nb * SSD_CPS
    per = blk // HALO

    def rb(b, pb):
        return jnp.where(pb == 0, n_batch * nlb + b, b * nlb + pb - 1)

    def rb_in(b, t):
        return rb(b, jnp.minimum(t, nb - 1))

    def rb_out(b, t):
        return rb(b, jnp.maximum(t - nb, 0))

    last_halo = rows // HALO - 1
    gw = SSD_GROUP_W
    return pl.pallas_call(
        functools.partial(_ssd2_body, nb, ncc),
        grid=(n_batch, SSD_GROUPS, 2 * nb),
        in_specs=[pl.BlockSpec((blk, SSD_XBC_W), lambda b, g, t: (rb_in(b, t), g)),
                  pl.BlockSpec((HALO, SSD_XBC_W),
                               lambda b, g, t: (jnp.maximum(rb_in(b, t) * per - 1, 0), g)),
                  pl.BlockSpec((HALO, SSD_XBC_W),
                               lambda b, g, t: (jnp.minimum(rb_in(b, t) * per + per, last_halo), g)),
                  pl.BlockSpec((blk, LANES), lambda b, g, t: (rb_in(b, t), 0)),
                  pl.BlockSpec((blk, gw), lambda b, g, t: (rb_out(b, t), g)),
                  pl.BlockSpec((None, SSD_CONV_W, SSD_XBC_W), lambda b, g, t: (j, 0, g)),
                  pl.BlockSpec((None, 1, SSD_XBC_W), lambda b, g, t: (j, 0, g)),
                  pl.BlockSpec((None, 8, LANES), lambda b, g, t: (j, 0, 0)),
                  pl.BlockSpec((None, 8, LANES), lambda b, g, t: (j, 0, 0)),
                  pl.BlockSpec((None, 1, gw), lambda b, g, t: (j, 0, g)),
                  pl.BlockSpec((None, 1, gw), lambda b, g, t: (j, 0, g)),
                  pl.BlockSpec((2 * q, q), lambda b, g, t: (0, 0)),
                  pl.BlockSpec((LANES, 2 * gw), lambda b, g, t: (0, 0))],
        out_specs=pl.BlockSpec((blk, gw), lambda b, g, t: (rb_out(b, t), g)),
        out_shape=jax.ShapeDtypeStruct((rows, SSD_INNER), BF16),
        scratch_shapes=[pltpu.VMEM((ns * q, gw), F32),
                        pltpu.VMEM((ns * q, 2 * gw), BF16),
                        pltpu.VMEM((ns * q, SSD_STATE), BF16),
                        pltpu.VMEM((ns * q, LANES), F32),
                        pltpu.VMEM((ns * 8, 2 * gw), F32),
                        pltpu.VMEM((ns * q, gw), BF16),
                        pltpu.VMEM((SSD_STATE, gw), F32),
                        pltpu.VMEM((SSD_STATE, gw), F32)],
        compiler_params=_cparams("arbitrary", "arbitrary", "arbitrary"),
        name="ssd_bidir",
    )(xbc, xbc, xbc, dt, z, conv_w, conv_b, dtb_tab, alog_tab, dsk_x, ng, tri, emat)


def _stack_heads(qc):
    lane = lax.broadcasted_iota(jnp.int32, qc.shape, 1)
    lo = lane < HEAD_DIM
    zero = jnp.zeros_like(qc)
    return jnp.concatenate([jnp.where(lo, qc, zero), jnp.where(lo, zero, qc)], axis=0)


def _unstack_heads(o, n):
    lane = lax.broadcasted_iota(jnp.int32, (n, LANES), 1)
    return jnp.where(lane < HEAD_DIM, o[:n], o[n:])


def _softmax_pv(scores, values, sink=None):
    m = None
    for sc in scores:
        mx = jnp.max(sc, axis=-1, keepdims=True)
        m = mx if m is None else jnp.maximum(m, mx)
    if sink is not None:
        m = jnp.maximum(m, sink)
    den = None if sink is None else jnp.exp2(sink - m)
    acc = None
    for sc, v in zip(scores, values):
        pr = jnp.exp2(sc - m)
        sm = jnp.sum(pr, axis=-1, keepdims=True)
        den = sm if den is None else den + sm
        t = _dot(pr.astype(BF16), v)
        acc = t if acc is None else acc + t
    return acc * (1.0 / den)


def _win_body(q_ref, kp_ref, ko_ref, kn_ref, vp_ref, vo_ref, vn_ref, kc_ref, vc_ref, sink_ref, o_ref):
    n = pl.program_id(1)
    nb = pl.num_programs(1)
    wb = WIN_BLOCK
    kb = jnp.concatenate([kp_ref[...], ko_ref[...], kn_ref[...]], axis=0)
    vb = jnp.concatenate([vp_ref[...], vo_ref[...], vn_ref[...]], axis=0)
    r = lax.broadcasted_iota(jnp.int32, (wb, 3 * wb), 0)
    c = lax.broadcasted_iota(jnp.int32, (wb, 3 * wb), 1)
    pen_prev = jnp.where(n > 0, 0.0, NEG_INF)
    pen_next = jnp.where(n < nb - 1, 0.0, NEG_INF)
    pen = jnp.where(c < wb, jnp.where(c >= r, pen_prev, NEG_INF),
                    jnp.where(c < 2 * wb, 0.0, jnp.where(c - 2 * wb <= r, pen_next, NEG_INF)))
    pen = jnp.concatenate([pen, pen], axis=0)
    half = WIN_GQA * wb
    for m in range(WIN_GQA):
        sl = slice(LANES * m, LANES * (m + 1))
        qs = _stack_heads(q_ref[:, sl])
        s_loc = _dot_nt(qs, kb) + pen
        s_ctx = _dot_nt(qs, kc_ref[...])
        sink = jnp.concatenate([sink_ref[m * wb:(m + 1) * wb, 0:1],
                                sink_ref[half + m * wb:half + (m + 1) * wb, 0:1]], axis=0)
        o = _softmax_pv([s_loc, s_ctx], [vb, vc_ref[...]], sink)
        o_ref[:, sl] = _unstack_heads(o, wb).astype(o_ref.dtype)


def _window_attention(qw, kw, vw, sink_x, j, n_batch, seq, n_ctx):
    wb = WIN_BLOCK
    nb = seq // wb
    ctx_blk0 = n_batch * seq // n_ctx
    q_spec = pl.BlockSpec((wb, 512), lambda b, n: (b * nb + n, 0))
    prev = pl.BlockSpec((wb, LANES), lambda b, n: (b * nb + jnp.maximum(n - 1, 0), 0))
    own = pl.BlockSpec((wb, LANES), lambda b, n: (b * nb + n, 0))
    nxt = pl.BlockSpec((wb, LANES), lambda b, n: (b * nb + jnp.minimum(n + 1, nb - 1), 0))
    cspec = pl.BlockSpec((n_ctx, LANES), lambda b, n: (ctx_blk0 + b, 0))
    return pl.pallas_call(
        _win_body,
        grid=(n_batch, nb),
        in_specs=[q_spec, prev, own, nxt, prev, own, nxt, cspec, cspec,
                  _layer(sink_x, j)],
        out_specs=pl.BlockSpec((wb, 512), lambda b, n: (b * nb + n, 0)),
        out_shape=jax.ShapeDtypeStruct((n_batch * seq, 512), BF16),
        compiler_params=_cparams("parallel", "parallel"),
        name="window_attention",
    )(qw, kw, kw, kw, vw, vw, vw, kw, vw, sink_x)


NA_ROWS_PER_STEP = 4


def _na_body(n_rows, q_ref, k_ref, v_ref, kc_ref, vc_ref, bias_ref, o_ref):
    kr = NA_ROWS
    for j in range(NA_ROWS_PER_STEP):
        r = pl.program_id(1) * NA_ROWS_PER_STEP + j
        first_row = jnp.clip(r - kr // 2, 0, n_rows - kr)
        start = pl.multiple_of(first_row * GRID_W, GRID_W)
        ro0 = first_row - r + NA_ROWS - 1
        rows = slice(j * GRID_W, (j + 1) * GRID_W)
        for m in range(NA_HEADS // 2):
            sl = slice(LANES * m, LANES * (m + 1))
            qs = _stack_heads(q_ref[rows, sl])
            kw = k_ref[pl.ds(start, kr * GRID_W), sl]
            vw = v_ref[pl.ds(start, kr * GRID_W), sl]
            s_loc = _dot_nt(qs, kw) + bias_ref[ro0, 2 * GRID_W * m:2 * GRID_W * (m + 1), :]
            s_ctx = _dot_nt(qs, kc_ref[:, sl])
            o = _softmax_pv([s_loc, s_ctx], [vw, vc_ref[:, sl]])
            o_ref[rows, sl] = _unstack_heads(o, GRID_W).astype(o_ref.dtype)


def _na_attention(qn, kn, vn, bias_tab, j, n_batch, seq, n_ctx):
    n_rows = seq // GRID_W
    assert n_rows >= NA_ROWS and n_rows % NA_ROWS_PER_STEP == 0
    n_steps = n_rows // NA_ROWS_PER_STEP
    qrows = NA_ROWS_PER_STEP * GRID_W
    ctx_blk0 = n_batch * seq // n_ctx
    return pl.pallas_call(
        functools.partial(_na_body, n_rows),
        grid=(n_batch, n_steps),
        in_specs=[pl.BlockSpec((qrows, 512), lambda b, r: (b * n_steps + r, 0)),
                  pl.BlockSpec((seq, 512), lambda b, r: (b, 0)),
                  pl.BlockSpec((seq, 512), lambda b, r: (b, 0)),
                  pl.BlockSpec((n_ctx, 512), lambda b, r: (ctx_blk0 + b, 0)),
                  pl.BlockSpec((n_ctx, 512), lambda b, r: (ctx_blk0 + b, 0)),
                  _layer(bias_tab, j)],
        out_specs=pl.BlockSpec((qrows, 512), lambda b, r: (b * n_steps + r, 0)),
        out_shape=jax.ShapeDtypeStruct((n_batch * seq, 512), BF16),
        compiler_params=_cparams("parallel", "arbitrary"),
        name="neighbourhood_attention",
    )(qn, kn, vn, kn, vn, bias_tab)


def _ctx_attn_body(qw_ref, kw_ref, vw_ref, qn_ref, kn_ref, vn_ref, sink_ref, ow_ref, on_ref):
    n = qw_ref.shape[0]
    qw = qw_ref[...]
    lane = lax.broadcasted_iota(jnp.int32, (n, LANES), 1)
    lo = lane < HEAD_DIM
    for m in range(WIN_GQA):
        qs = _stack_heads(qw[:, LANES * m:LANES * (m + 1)])
        r0, r1 = m * WIN_BLOCK, (WIN_GQA + m) * WIN_BLOCK
        sink = jnp.concatenate([jnp.broadcast_to(sink_ref[r0:r0 + 1, 0:1], (n, 1)),
                                jnp.broadcast_to(sink_ref[r1:r1 + 1, 0:1], (n, 1))], axis=0)
        o = _softmax_pv([_dot_nt(qs, kw_ref[...])], [vw_ref[...]], sink)
        ow_ref[:, LANES * m:LANES * (m + 1)] = jnp.where(lo, o[:n], o[n:]).astype(ow_ref.dtype)
    qn = qn_ref[...]
    for m in range(NA_HEADS // 2):
        sl = slice(LANES * m, LANES * (m + 1))
        qs = _stack_heads(qn[:, sl])
        o = _softmax_pv([_dot_nt(qs, kn_ref[:, sl])], [vn_ref[:, sl]])
        on_ref[:, sl] = _unstack_heads(o, n).astype(on_ref.dtype)


def _ctx_attention(qw, kw, vw, qn, kn, vn, sink_x, j, n_batch, seq, n_ctx):
    blk0 = n_batch * seq // n_ctx
    wide = pl.BlockSpec((n_ctx, 512), lambda b: (blk0 + b, 0))
    narrow = pl.BlockSpec((n_ctx, LANES), lambda b: (blk0 + b, 0))
    out = pl.BlockSpec((n_ctx, 512), lambda b: (b, 0))
    return pl.pallas_call(
        _ctx_attn_body,
        grid=(n_batch,),
        in_specs=[wide, narrow, narrow, wide, wide, wide,
                  _layer(sink_x, j)],
        out_specs=[out, out],
        out_shape=[jax.ShapeDtypeStruct((n_batch * n_ctx, 512), BF16)] * 2,
        compiler_params=_cparams("parallel"),
        name="context_attention",
    )(qw, kw, vw, qn, kn, vn, sink_x)


def _take(w, runs, axis):
    parts = []
    for run in runs:
        if run[0] is None:
            shape = list(w.shape)
            shape[axis] = run[1]
            parts.append(jnp.zeros(shape, w.dtype))
        else:
            parts.append(lax.slice_in_dim(w, run[0], run[1], axis=axis))
    return jnp.concatenate(parts, axis=axis)


def _xbc_runs():
    gn = SSD_GROUPS * SSD_STATE
    runs = []
    for g in range(SSD_GROUPS):
        runs.append((g * SSD_GROUP_W, (g + 1) * SSD_GROUP_W))
        runs.append((SSD_INNER + g * SSD_STATE, SSD_INNER + (g + 1) * SSD_STATE))
        runs.append((SSD_INNER + gn + g * SSD_STATE, SSD_INNER + gn + (g + 1) * SSD_STATE))
    return runs


def _head_rows(t):
    n = t.shape[0]
    flat = jnp.pad(t.astype(F32).reshape(n, 2 * SSD_HEADS), ((0, 0), (0, LANES - 2 * SSD_HEADS)))
    rows = [jnp.roll(flat, -SSD_HPG * g, axis=1) for g in range(SSD_GROUPS)]
    rows += [jnp.zeros_like(flat)] * (8 - SSD_GROUPS)
    return jnp.stack(rows, axis=1)


def _win_head_runs():
    runs = []
    for m in range(WIN_GQA):
        for kvh in range(WIN_KV_HEADS):
            h = kvh * WIN_GQA + m
            runs.append((h * HEAD_DIM, (h + 1) * HEAD_DIM))
    return runs


def _rope_tables(seq):
    pos = np.arange(seq)
    row = (pos // GRID_W).astype(np.float32)
    col = (pos % GRID_W).astype(np.float32)
    n_freq = HEAD_DIM // 4
    inv = (np.float32(ROPE_THETA) ** (-np.arange(n_freq, dtype=np.float32) / n_freq)).astype(np.float32)
    ar = (row[:, None] * inv).astype(np.float32)
    ac = (col[:, None] * inv).astype(np.float32)
    cos_h = np.concatenate([np.cos(ar), np.cos(ar), np.cos(ac), np.cos(ac)], axis=1)
    sin_h = np.concatenate([-np.sin(ar), np.sin(ar), -np.sin(ac), np.sin(ac)], axis=1)
    ident_c = np.ones((ROW_TILE, HEAD_DIM), np.float32)
    ident_s = np.zeros((ROW_TILE, HEAD_DIM), np.float32)
    cos_t = np.concatenate([cos_h, ident_c], axis=0)
    sin_t = np.concatenate([sin_h, ident_s], axis=0)
    return (jnp.asarray(np.tile(cos_t, (1, 2)), F32), jnp.asarray(np.tile(sin_t, (1, 2)), F32))


def _na_bias_table(rpb):
    n, h, nr, _ = rpb.shape
    kr = NA_ROWS
    cols = np.arange(GRID_W)
    col_start = np.clip(cols - NA_COLS // 2, 0, GRID_W - NA_COLS)
    col_ok = (cols[None] >= col_start[:, None]) & (cols[None] < col_start[:, None] + NA_COLS)
    r32 = rpb.astype(F32) * LOG2E
    v = jnp.concatenate([r32[..., NA_COLS - 1:], jnp.zeros((n, h, nr, LANES - (2 * NA_COLS - 1)), F32),
                         r32[..., :NA_COLS - 1]], axis=-1)
    flat = jnp.tile(v, (1, 1, 1, GRID_W))[..., :GRID_W * (LANES - 1)]
    toep = flat.reshape(n, h, nr, GRID_W, LANES - 1)[..., :GRID_W]
    toep = jnp.where(jnp.asarray(col_ok), toep, NEG_INF)
    tabs = [jnp.transpose(toep[:, :, ro0:ro0 + kr], (0, 1, 3, 2, 4)).reshape(n, h * GRID_W, kr * GRID_W)
            for ro0 in range(NA_ROWS)]
    return jnp.stack(tabs, axis=1)


def kernel(x, c, ctx, c_ctx, w_mod, b_mod, norm_mix_g, norm_ff_g, w_ff1, w_ff2, w_in_even, conv_w, conv_b,
           dt_bias, a_log, d_skip, ssd_norm_g, w_out_even, w_in_odd, q_norm_win, k_norm_win, sink_win,
           q_norm_na, k_norm_na, rpb_na, w_out_odd):
    n_batch, seq, d = x.shape
    n_ctx = ctx.shape[1]
    n_lat = n_batch * seq
    tm = ROW_TILE
    tiles_per_batch = seq // tm
    n_lat_tiles = n_lat // tm
    n_ctx_tiles = (n_batch * n_ctx) // tm
    n2 = seq // DFT_N1

    h = jnp.concatenate([x.reshape(n_lat, d), ctx.reshape(n_batch * n_ctx, d)], axis=0)

    cvec = jnp.concatenate([c, c_ctx[None], jnp.zeros((8 - n_batch - 1, d), F32)], axis=0)
    mod_all = _modulation(cvec, w_mod, b_mod)
    mod_all = mod_all.reshape(-1, 8, 6, 1, d)[:, :n_batch + 1].transpose(0, 2, 1, 3, 4)

    m_a, twc, tws, m_b, m_c, chan = _dft_tables(seq, n_ctx)
    xbc_runs = _xbc_runs()
    s3 = FNET_W + SSD_INNER + SSD_CONV_DIM
    w_four = _weight_product(w_in_even, FNET_W, chan)
    tri = jnp.asarray(np.concatenate([np.tril(np.ones((SSD_CHUNK, SSD_CHUNK))),
                                      np.triu(np.ones((SSD_CHUNK, SSD_CHUNK)))], axis=0), F32)
    emat = np.zeros((LANES, 2 * SSD_GROUP_W), np.float32)
    for dr in range(2):
        for r in range(SSD_HPG):
            c0 = dr * SSD_GROUP_W + r * SSD_HEAD_DIM
            emat[SSD_HEADS * dr + r, c0:c0 + SSD_HEAD_DIM] = 1.0
    emat = jnp.asarray(emat, BF16)

    cos_t, sin_t = _rope_tables(seq)
    gmat_slab = jnp.asarray(np.kron(np.eye(MXU_DIM // HEAD_DIM), np.full((HEAD_DIM, HEAD_DIM), 1.0 / HEAD_DIM)),
                          BF16)
    gmat128 = gmat_slab[:LANES, :LANES]
    win_runs = _win_head_runs()
    wq = WIN_Q_HEADS * HEAD_DIM

    gains_mix = norm_mix_g.astype(F32)[:, None, :]
    gains_ff = norm_ff_g.astype(F32)[:, None, :]
    w_ff1_b, w_ff2_b = w_ff1.astype(BF16), w_ff2.astype(BF16)
    w_even_b = w_in_even.astype(BF16)
    w_dt_b = jnp.pad(w_even_b[:, :, s3:], ((0, 0), (0, 0), (0, LANES - 2 * SSD_HEADS)))
    w_out_even_b = w_out_even.astype(BF16)
    conv_w_p = _take(conv_w.astype(F32), xbc_runs, 2)
    conv_b_p = _take(conv_b.astype(F32)[:, None, :], xbc_runs, 2)
    dtb_rows, alog_rows = _head_rows(dt_bias), _head_rows(a_log)
    dsk_x = jnp.repeat(d_skip.astype(F32), SSD_HEAD_DIM, axis=1)[:, None, :]
    ssd_ng = ssd_norm_g.astype(F32)[:, None, :]
    w_odd_b = w_in_odd.astype(BF16)
    wq_perm = _take(w_odd_b[:, :, :wq], win_runs, 2)
    w_out_odd_b = w_out_odd.astype(BF16)
    wo_win_perm = _take(w_out_odd_b[:, :wq], win_runs, 1)
    qscale = HEAD_DIM ** -0.5 * LOG2E
    rep = lambda g, k: jnp.tile(g.astype(F32), (1, k))
    head_gains = jnp.stack([rep(q_norm_win, WIN_Q_HEADS) * qscale,
                            jnp.pad(rep(k_norm_win, WIN_KV_HEADS), ((0, 0), (0, wq - WIN_KV_HEADS * HEAD_DIM))),
                            rep(q_norm_na, NA_HEADS) * qscale, rep(k_norm_na, NA_HEADS)], axis=1)
    head_gains = jnp.pad(head_gains, ((0, 0), (0, 4), (0, 0)))
    sink_x = jnp.broadcast_to(jnp.repeat(sink_win.astype(F32) * LOG2E, WIN_BLOCK, axis=1)[:, :, None],
                              (sink_win.shape[0], WIN_Q_HEADS * WIN_BLOCK, LANES))
    bias_tab = _na_bias_table(rpb_na)

    for i in range(DEPTH):
        need_ctx = i < DEPTH - 1
        j = i // 2
        if i % 2 == 0:
            fr, fi, z, xbc, dtr = _inproj_even(h, i, j, gains_mix, mod_all, w_four, w_even_b, w_dt_b,
                                               n_batch, tiles_per_batch)

            br, bi = _dft_stage_a(fr.reshape(-1, n2, FNET_W), fi.reshape(-1, n2, FNET_W),
                                  m_a, twc, tws, n_batch, n2)
            f_lat = _dft_stage_b(br.reshape(n_batch, seq, FNET_W), bi.reshape(n_batch, seq, FNET_W),
                                 m_b, n_batch, n2).reshape(n_lat, FNET_W)
            f_ctx = _ctx_dft(fr, fi, m_c, n_batch, n_ctx, n_lat // n_ctx)

            y_ssd = _ssd2(xbc, dtr, z, j, conv_w_p, conv_b_p, dtb_rows, alog_rows, dsk_x, ssd_ng, tri, emat,
                          n_batch, seq, n_ctx)

            h = _outproj(h, i, mod_all, f_lat, y_ssd, w_out_even_b, w_out_even_b, j, FNET_W,
                         n_batch, tiles_per_batch, 0, 0, 0, n_lat_tiles)
            h = _outproj(h, i, mod_all, f_ctx, y_ssd, w_out_even_b, w_out_even_b, j, FNET_W,
                         n_batch, tiles_per_batch, n_lat_tiles, 0, n_lat_tiles, n_ctx_tiles)
        else:
            qw, kw, vw, qn, kn, vn = _inproj_odd(h, i, j, gains_mix, mod_all, wq_perm, w_odd_b, cos_t, sin_t,
                                                 gmat_slab, gmat128, head_gains, n_batch, tiles_per_batch)
            o_win = _window_attention(qw, kw, vw, sink_x, j, n_batch, seq, n_ctx)
            o_na = _na_attention(qn, kn, vn, bias_tab, j, n_batch, seq, n_ctx)
            if need_ctx:
                oc_win, oc_na = _ctx_attention(qw, kw, vw, qn, kn, vn, sink_x, j, n_batch, seq, n_ctx)
            h = _outproj(h, i, mod_all, o_win, o_na, wo_win_perm, w_out_odd_b, j, wq,
                         n_batch, tiles_per_batch, 0, 0, 0, n_lat_tiles)
            if need_ctx:
                h = _outproj(h, i, mod_all, oc_win, oc_na, wo_win_perm, w_out_odd_b, j, wq,
                             n_batch, tiles_per_batch, n_lat_tiles, 0, 0, n_ctx_tiles)
        n_tiles = n_lat_tiles + n_ctx_tiles if need_ctx else n_lat_tiles
        h = _ffn(h, i, gains_ff, mod_all, w_ff1_b, w_ff2_b, n_batch, tiles_per_batch, n_tiles)
    return h.reshape(n_batch, seq, d)
```

```python
import functools
import math

import numpy as np
import jax
import jax.numpy as jnp
from jax import lax
from jax.experimental import pallas as pl
from jax.experimental.pallas import tpu as pltpu

F32 = jnp.float32
BF16 = jnp.bfloat16
HIGHEST = lax.Precision.HIGHEST

D_MODEL = 1024
DEPTH = 4
GRID_W = 64
D_FF = 4 * D_MODEL
NORM_EPS = 1e-6
NEG_INF = -1e30

FNET_GROUPS = 8
FNET_GROUP_W = 64
FNET_W = FNET_GROUPS * FNET_GROUP_W
SSD_HEAD_DIM = 64
SSD_HEADS = 24
SSD_GROUPS = 4
SSD_HPG = SSD_HEADS // SSD_GROUPS
SSD_STATE = 128
SSD_INNER = SSD_HEADS * SSD_HEAD_DIM
SSD_GROUP_W = SSD_HPG * SSD_HEAD_DIM
SSD_XBC_W = SSD_GROUP_W + 2 * SSD_STATE
SSD_CONV_DIM = SSD_INNER + 2 * SSD_GROUPS * SSD_STATE
SSD_CONV_W = 5
SSD_CHUNK = 128
HEAD_DIM = 64
WIN_Q_HEADS = 8
WIN_KV_HEADS = 2
WIN_GQA = WIN_Q_HEADS // WIN_KV_HEADS
WIN_BLOCK = 128
NA_HEADS = 8
NA_ROWS = 8
NA_COLS = 16
ROPE_THETA = 10000.0
LOG2E = 1.4426950408889634

LANES = 128
MXU_DIM = 256
HALO = 8
ROW_TILE = 512
FF_TILE = 4096
DFT_N1 = 128
VMEM_LIMIT = 56 * 1024 * 1024


def _cparams(*sem):
    return pltpu.CompilerParams(dimension_semantics=sem, vmem_limit_bytes=VMEM_LIMIT)


def _silu(x):
    return x / (1.0 + jnp.exp2(x * -LOG2E))


def _softplus(x):
    return jnp.maximum(x, 0.0) + jnp.log(1.0 + jnp.exp(-jnp.abs(x)))


def _norm_mod(h, g, shift, scale):
    ms = jnp.mean(h * h, axis=-1, keepdims=True)
    y = h * lax.rsqrt(ms + NORM_EPS) * g
    return y * (1.0 + scale) + shift


def _dot(a, b):
    return jnp.dot(a, b, preferred_element_type=F32)


def _dot_nt(a, b):
    return lax.dot_general(a, b, (((1,), (1,)), ((), ())), preferred_element_type=F32)


def _mod_body(c_ref, w_ref, b_ref, o_ref):
    s = _silu(c_ref[...])
    o_ref[...] = jnp.dot(s, w_ref[...], precision=HIGHEST, preferred_element_type=F32) + b_ref[...]


def _modulation(cvec, w_mod, b_mod):
    depth, d, n = w_mod.shape
    tn = 1536
    return pl.pallas_call(
        _mod_body,
        grid=(depth, n // tn),
        in_specs=[pl.BlockSpec((8, d), lambda i, j: (0, 0)),
                  pl.BlockSpec((None, d, tn), lambda i, j: (i, 0, j)),
                  pl.BlockSpec((None, 1, tn), lambda i, j: (i, 0, j))],
        out_specs=pl.BlockSpec((None, 8, tn), lambda i, j: (i, 0, j)),
        out_shape=jax.ShapeDtypeStruct((depth, 8, n), F32),
        compiler_params=_cparams("parallel", "parallel"),
        name="modulation",
    )(cvec, w_mod, b_mod.reshape(depth, 1, n))


def _wprod_body(a_ref, b_ref, o_ref):
    o_ref[...] = jnp.dot(a_ref[...], b_ref[...], precision=HIGHEST,
                         preferred_element_type=F32).astype(o_ref.dtype)


def _weight_product(a, k, b):
    n, m, _ = a.shape
    p = b.shape[1]
    return pl.pallas_call(
        _wprod_body,
        grid=(n,),
        in_specs=[pl.BlockSpec((None, m, k), lambda i: (i, 0, 0)),
                  pl.BlockSpec((k, p), lambda i: (0, 0))],
        out_specs=pl.BlockSpec((None, m, p), lambda i: (i, 0, 0)),
        out_shape=jax.ShapeDtypeStruct((n, m, p), BF16),
        compiler_params=_cparams("parallel"),
        name="fold_channel_dft",
    )(a, b)


def _resident(shape):
    return pl.BlockSpec(shape, lambda *_: (0,) * len(shape), pipeline_mode=pl.Buffered(1))


def _layer(arr, j):
    nd = arr.ndim
    return pl.BlockSpec((None,) + arr.shape[1:], lambda *_: (j,) + (0,) * (nd - 1),
                        pipeline_mode=pl.Buffered(1))


def _mod_spec(layer, which, tiles_per_batch, n_batch, tile_off):
    def imap(m):
        return (layer, which, jnp.minimum((m + tile_off) // tiles_per_batch, n_batch), 0, 0)
    return pl.BlockSpec((None, None, None, 1, D_MODEL), imap)


def _inproj_even_body(h_ref, g_ref, sh_ref, sc_ref, wf_ref, w_ref, wdt_ref,
                      fr_ref, fi_ref, z_ref, xbc_ref, dt_ref):
    u = _norm_mod(h_ref[...], g_ref[...], sh_ref[...], sc_ref[...]).astype(BF16)
    fr_ref[...] = _dot(u, wf_ref[:, :FNET_W])
    fi_ref[...] = _dot(u, wf_ref[:, FNET_W:])
    z_ref[...] = _dot(u, w_ref[:, FNET_W:FNET_W + SSD_INNER]).astype(z_ref.dtype)
    c0 = FNET_W + SSD_INNER
    xbc = _dot(u, w_ref[:, c0:c0 + SSD_CONV_DIM])
    gn = SSD_GROUPS * SSD_STATE
    for g in range(SSD_GROUPS):
        o = g * SSD_XBC_W
        b0 = SSD_INNER + g * SSD_STATE
        xbc_ref[:, o:o + SSD_GROUP_W] = xbc[:, g * SSD_GROUP_W:(g + 1) * SSD_GROUP_W]
        xbc_ref[:, o + SSD_GROUP_W:o + SSD_GROUP_W + SSD_STATE] = xbc[:, b0:b0 + SSD_STATE]
        xbc_ref[:, o + SSD_GROUP_W + SSD_STATE:o + SSD_XBC_W] = xbc[:, b0 + gn:b0 + gn + SSD_STATE]
    dt_ref[...] = _dot(u, wdt_ref[...])


def _inproj_even(h, layer, j, gains, mod_all, w_four, w_main, w_dt, n_batch, tiles_per_batch):
    rows = h.shape[0]
    widths = (FNET_W, FNET_W, SSD_INNER, SSD_CONV_DIM, LANES)
    dtypes = (F32, F32, BF16, F32, F32)
    tm = ROW_TILE
    return pl.pallas_call(
        _inproj_even_body,
        grid=(rows // tm,),
        in_specs=[pl.BlockSpec((tm, D_MODEL), lambda m: (m, 0)),
                  _layer(gains, layer),
                  _mod_spec(layer, 0, tiles_per_batch, n_batch, 0),
                  _mod_spec(layer, 1, tiles_per_batch, n_batch, 0),
                  _layer(w_four, j), _layer(w_main, j), _layer(w_dt, j)],
        out_specs=[pl.BlockSpec((tm, wd), lambda m: (m, 0)) for wd in widths],
        out_shape=[jax.ShapeDtypeStruct((rows, wd), dt) for wd, dt in zip(widths, dtypes)],
        compiler_params=_cparams("parallel"),
        name="inproj_even",
    )(h, gains, mod_all, mod_all, w_four, w_main, w_dt)


def _group_rms(x, gmat_ref, gain):
    sq = x * x
    hi = sq.astype(BF16)
    lo = (sq - hi.astype(F32)).astype(BF16)
    gw = gmat_ref.shape[0]
    parts = []
    for c0 in range(0, x.shape[1], gw):
        parts.append(_dot(hi[:, c0:c0 + gw], gmat_ref[...]) + _dot(lo[:, c0:c0 + gw], gmat_ref[...]))
    ms = parts[0] if len(parts) == 1 else jnp.concatenate(parts, axis=1)
    return x * lax.rsqrt(ms + NORM_EPS) * gain


def _rope(y, cos, sin_signed):
    w = y.shape[-1]
    reps = w // LANES
    lane = lax.broadcasted_iota(jnp.int32, y.shape, 1)
    first = (lane % 32) < 16
    partner = jnp.where(first, pltpu.roll(y, w - 16, 1), pltpu.roll(y, 16, 1))
    if reps > 1:
        cos = jnp.concatenate([cos] * reps, axis=1)
        sin_signed = jnp.concatenate([sin_signed] * reps, axis=1)
    return y * cos + partner * sin_signed


def _inproj_odd_body(h_ref, g_ref, sh_ref, sc_ref, wq_ref, w_ref, cos_ref, sin_ref, gslab_ref, g128_ref,
                     hg_ref, qw_ref, kw_ref, vw_ref, qn_ref, kn_ref, vn_ref):
    u = _norm_mod(h_ref[...], g_ref[...], sh_ref[...], sc_ref[...]).astype(BF16)
    cos = cos_ref[...]
    sin = sin_ref[...]
    wq = WIN_Q_HEADS * HEAD_DIM
    wk = WIN_KV_HEADS * HEAD_DIM
    nh = NA_HEADS * HEAD_DIM
    qw = _group_rms(_dot(u, wq_ref[...]), gslab_ref, hg_ref[0:1, :])
    qw_ref[...] = _rope(qw, cos, sin).astype(BF16)
    kw = _group_rms(_dot(u, w_ref[:, wq:wq + wk]), g128_ref, hg_ref[1:2, :wk])
    kw_ref[...] = _rope(kw, cos, sin).astype(BF16)
    vw_ref[...] = _dot(u, w_ref[:, wq + wk:wq + 2 * wk]).astype(BF16)
    c0 = wq + 2 * wk
    qn_ref[...] = _group_rms(_dot(u, w_ref[:, c0:c0 + nh]), gslab_ref, hg_ref[2:3, :]).astype(BF16)
    kn_ref[...] = _group_rms(_dot(u, w_ref[:, c0 + nh:c0 + 2 * nh]), gslab_ref, hg_ref[3:4, :]).astype(BF16)
    vn_ref[...] = _dot(u, w_ref[:, c0 + 2 * nh:c0 + 3 * nh]).astype(BF16)


def _inproj_odd(h, layer, j, gains, mod_all, wq_perm, w_main, cos_t, sin_t, gmat_slab, gmat128, head_gains,
                n_batch, tiles_per_batch):
    rows = h.shape[0]
    tm = ROW_TILE
    widths = (512, 128, 128, 512, 512, 512)
    n_lat_tiles = n_batch * tiles_per_batch

    def rope_map(m):
        return (jnp.where(m < n_lat_tiles, m % tiles_per_batch, tiles_per_batch), 0)

    return pl.pallas_call(
        _inproj_odd_body,
        grid=(rows // tm,),
        in_specs=[pl.BlockSpec((tm, D_MODEL), lambda m: (m, 0)),
                  _layer(gains, layer),
                  _mod_spec(layer, 0, tiles_per_batch, n_batch, 0),
                  _mod_spec(layer, 1, tiles_per_batch, n_batch, 0),
                  _layer(wq_perm, j), _layer(w_main, j),
                  pl.BlockSpec((tm, LANES), rope_map),
                  pl.BlockSpec((tm, LANES), rope_map),
                  _resident(gmat_slab.shape), _resident(gmat128.shape), _layer(head_gains, j)],
        out_specs=[pl.BlockSpec((tm, wd), lambda m: (m, 0)) for wd in widths],
        out_shape=[jax.ShapeDtypeStruct((rows, wd), BF16) for wd in widths],
        compiler_params=_cparams("parallel"),
        name="inproj_odd",
    )(h, gains, mod_all, mod_all, wq_perm, w_main, cos_t, sin_t, gmat_slab, gmat128, head_gains)


def _outproj_body(k1, k2, b_row0, h_ref, gate_ref, a1_ref, a2_ref, wa_ref, wb_ref, o_ref):
    acc = _dot(a1_ref[...].astype(BF16), wa_ref[0:k1, :])
    acc += _dot(a2_ref[...].astype(BF16), wb_ref[b_row0:b_row0 + k2, :])
    o_ref[...] = h_ref[...] + gate_ref[...] * acc


def _outproj(h, layer, mod_all, a1, a2, wa, wb, j, b_row0, n_batch, tiles_per_batch, tile_off, a1_off,
             a2_off, n_tiles):
    tm = ROW_TILE
    k1, k2 = a1.shape[1], a2.shape[1]
    return pl.pallas_call(
        functools.partial(_outproj_body, k1, k2, b_row0),
        grid=(n_tiles,),
        in_specs=[pl.BlockSpec((tm, D_MODEL), lambda m: (m + tile_off, 0)),
                  _mod_spec(layer, 2, tiles_per_batch, n_batch, tile_off),
                  pl.BlockSpec((tm, k1), lambda m: (m + a1_off, 0)),
                  pl.BlockSpec((tm, k2), lambda m: (m + a2_off, 0)),
                  _layer(wa, j), _layer(wb, j)],
        out_specs=pl.BlockSpec((tm, D_MODEL), lambda m: (m + tile_off, 0)),
        out_shape=jax.ShapeDtypeStruct(h.shape, F32),
        input_output_aliases={0: 0},
        compiler_params=_cparams("parallel"),
        name="outproj",
    )(h, mod_all, a1, a2, wa, wb)


def _ffn_body(h_ref, g_ref, sh_ref, sc_ref, gate_ref, w1_ref, w2_ref, o_ref):
    h = h_ref[...]
    u = _norm_mod(h, g_ref[...], sh_ref[...], sc_ref[...]).astype(BF16)
    acc = None
    for k in range(D_FF // FF_TILE):
        sl = slice(k * FF_TILE, (k + 1) * FF_TILE)
        a = jnp.maximum(_dot(u, w1_ref[:, sl]), 0.0)
        t = _dot((a * a).astype(BF16), w2_ref[sl, :])
        acc = t if acc is None else acc + t
    o_ref[...] = h + gate_ref[...] * acc


def _ffn(h, layer, gains, mod_all, w1, w2, n_batch, tiles_per_batch, n_tiles):
    tm = ROW_TILE
    return pl.pallas_call(
        _ffn_body,
        grid=(n_tiles,),
        in_specs=[pl.BlockSpec((tm, D_MODEL), lambda m: (m, 0)),
                  _layer(gains, layer),
                  _mod_spec(layer, 3, tiles_per_batch, n_batch, 0),
                  _mod_spec(layer, 4, tiles_per_batch, n_batch, 0),
                  _mod_spec(layer, 5, tiles_per_batch, n_batch, 0),
                  _layer(w1, layer), _layer(w2, layer)],
        out_specs=pl.BlockSpec((tm, D_MODEL), lambda m: (m, 0)),
        out_shape=jax.ShapeDtypeStruct((n_tiles * tm, D_MODEL), F32),
        compiler_params=_cparams("parallel"),
        name="ffn",
    )(h, gains, mod_all, mod_all, mod_all, w1, w2)


def _dft_a_body(xr_ref, xi_ref, m_ref, c_ref, s_ref, br_ref, bi_ref):
    n1 = DFT_N1
    reps = FNET_W // LANES
    for j in range(xr_ref.shape[1]):
        x = jnp.concatenate([xr_ref[:, j, :], xi_ref[:, j, :]], axis=0).astype(BF16)
        a = _dot(m_ref[...], x)
        ar, ai = a[:n1], a[n1:]
        c = jnp.concatenate([c_ref[:, LANES * j:LANES * (j + 1)]] * reps, axis=1)
        s = jnp.concatenate([s_ref[:, LANES * j:LANES * (j + 1)]] * reps, axis=1)
        br_ref[:, j, :] = ar * c + ai * s
        bi_ref[:, j, :] = ai * c - ar * s


def _dft_stage_a(xr, xi, mmat, twc, tws, n_batch, n2):
    n1 = DFT_N1
    lb = 8
    blk = pl.BlockSpec((n1, lb, FNET_W), lambda b, j: (b, j, 0))
    return pl.pallas_call(
        _dft_a_body,
        grid=(n_batch, n2 // lb),
        in_specs=[blk, blk,
                  pl.BlockSpec((2 * n1, 2 * n1), lambda b, j: (0, 0)),
                  pl.BlockSpec((n1, lb * LANES), lambda b, j: (0, j)),
                  pl.BlockSpec((n1, lb * LANES), lambda b, j: (0, j))],
        out_specs=[blk, blk],
        out_shape=[jax.ShapeDtypeStruct((n_batch * n1, n2, FNET_W), F32)] * 2,
        compiler_params=_cparams("parallel", "parallel"),
        name="seq_dft_stage_a",
    )(xr, xi, mmat, twc, tws)


def _dft_b_body(br_ref, bi_ref, m_ref, o_ref):
    x = jnp.concatenate([br_ref[...], bi_ref[...]], axis=0).astype(BF16)
    res = _dot(m_ref[...], x)
    o_ref[...] = res.reshape(o_ref.shape)


def _dft_stage_b(br, bi, mmat, n_batch, n2):
    n1 = DFT_N1
    kb = 8
    return pl.pallas_call(
        _dft_b_body,
        grid=(n_batch, n1 // kb),
        in_specs=[pl.BlockSpec((None, kb * n2, FNET_W), lambda b, j: (b, j, 0)),
                  pl.BlockSpec((None, kb * n2, FNET_W), lambda b, j: (b, j, 0)),
                  pl.BlockSpec(mmat.shape, lambda b, j: (0, 0))],
        out_specs=pl.BlockSpec((None, n2, kb, FNET_W), lambda b, j: (b, 0, j, 0)),
        out_shape=jax.ShapeDtypeStruct((n_batch, n2, n1, FNET_W), F32),
        compiler_params=_cparams("parallel", "parallel"),
        name="seq_dft_stage_b",
    )(br, bi, mmat)


def _ctx_dft_body(xr_ref, xi_ref, m_ref, o_ref):
    x = jnp.concatenate([xr_ref[...], xi_ref[...]], axis=0).astype(BF16)
    o_ref[...] = _dot(m_ref[...], x)


def _ctx_dft(fr, fi, mmat, n_batch, n_ctx, row_block_off):
    return pl.pallas_call(
        _ctx_dft_body,
        grid=(n_batch,),
        in_specs=[pl.BlockSpec((n_ctx, FNET_W), lambda b: (row_block_off + b, 0)),
                  pl.BlockSpec((n_ctx, FNET_W), lambda b: (row_block_off + b, 0)),
                  pl.BlockSpec(mmat.shape, lambda b: (0, 0))],
        out_specs=pl.BlockSpec((n_ctx, FNET_W), lambda b: (b, 0)),
        out_shape=jax.ShapeDtypeStruct((n_batch * n_ctx, FNET_W), F32),
        compiler_params=_cparams("parallel"),
        name="ctx_dft",
    )(fr, fi, mmat)


def _dft_tables(seq, n_ctx):
    n1 = DFT_N1
    n2 = seq // n1
    k1 = np.arange(n1)
    ang1 = 2.0 * np.pi * np.outer(k1, k1) / n1
    c1, s1 = np.cos(ang1), np.sin(ang1)
    m_a = np.block([[c1, s1], [-s1, c1]])
    ang_t = 2.0 * np.pi * np.outer(k1, np.arange(n2)) / seq
    twc = np.repeat(np.cos(ang_t), LANES, axis=1)
    tws = np.repeat(np.sin(ang_t), LANES, axis=1)
    k2 = np.arange(n2)
    ang2 = 2.0 * np.pi * np.outer(k2, k2) / n2
    scale = 1.0 / math.sqrt(seq)
    c2, s2 = np.cos(ang2) * scale, np.sin(ang2) * scale
    eye = np.eye(8)
    m_b = np.concatenate([np.einsum("kl,ab->kabl", c2, eye).reshape(n2 * 8, 8 * n2),
                          np.einsum("kl,ab->kabl", s2, eye).reshape(n2 * 8, 8 * n2)], axis=1)
    kc = np.arange(n_ctx)
    angc = 2.0 * np.pi * np.outer(kc, kc) / n_ctx
    m_c = np.concatenate([np.cos(angc), np.sin(angc)], axis=1) / math.sqrt(n_ctx)
    ch = np.arange(FNET_GROUP_W)
    angg = 2.0 * np.pi * np.outer(ch, ch) / FNET_GROUP_W
    eg = np.eye(FNET_GROUPS)
    chan = np.concatenate([np.kron(eg, np.cos(angg)), -np.kron(eg, np.sin(angg))], axis=1)
    chan = chan / math.sqrt(FNET_GROUP_W)
    return (jnp.asarray(m_a, BF16), jnp.asarray(twc, F32), jnp.asarray(tws, F32),
            jnp.asarray(m_b, BF16), jnp.asarray(m_c, BF16), jnp.asarray(chan, F32))


def _split_dot_rhs(w_bf16, x):
    hi = x.astype(BF16)
    r1 = x - hi.astype(F32)
    mid = r1.astype(BF16)
    lo = (r1 - mid.astype(F32)).astype(BF16)
    return _dot(w_bf16, hi) + _dot(w_bf16, mid) + _dot(w_bf16, lo)


def _split2_dot(x, w_bf16):
    hi = x.astype(BF16)
    lo = (x - hi.astype(F32)).astype(BF16)
    return _dot(hi, w_bf16) + _dot(lo, w_bf16)


SSD_CPS = 2


def _ssd2_body(nb, ncc, xm_ref, xp_ref, xn_ref, dt_ref, z_ref, cw_ref, cb_ref, dtb_ref, alog_ref, dsk_ref,
               ng_ref, tri_ref, e_ref, o_ref, yloc_ref, s_ref, c_ref, ea_ref, cd_ref, hbe_ref, hf_ref, hb_ref):
    g = pl.program_id(1)
    t = pl.program_id(2)
    q = SSD_CHUNK
    gw = SSD_GROUP_W
    ns = nb * SSD_CPS

    @pl.when(t < nb)
    def _local():
        first = t <= 1
        last = (t == 0) | (t == nb - 1)
        prev = jnp.where(first, 0.0, xp_ref[...])
        nxt = jnp.where(last, 0.0, xn_ref[...])
        xfull = jnp.concatenate([prev, xm_ref[...], nxt], axis=0)
        rows_blk = SSD_CPS * q
        conv = cb_ref[...]
        for k in range(SSD_CONV_W):
            shift = (SSD_CONV_W // 2 - k) % xfull.shape[0]
            xk = xfull if shift == 0 else pltpu.roll(xfull, shift, 0)
            conv = conv + xk[HALO:HALO + rows_blk, :] * cw_ref[k:k + 1, :]
        act = _silu(conv)
        xs = act[:, :gw]
        bm = act[:, gw:gw + SSD_STATE]
        cm = act[:, gw + SSD_STATE:]

        lane = lax.broadcasted_iota(jnp.int32, (1, LANES), 1)
        is_fwd = lane < SSD_HPG
        dsel = pltpu.roll(dt_ref[...], (LANES - SSD_HPG * g) % LANES, 1)
        dtv = _softplus(dsel + dtb_ref[pl.ds(g, 1), :])
        head_lane = is_fwd | ((lane >= SSD_HEADS) & (lane < SSD_HEADS + SSD_HPG))
        a_row = jnp.where(head_lane, -jnp.exp(alog_ref[pl.ds(g, 1), :]) * LOG2E, 0.0)
        adt = dtv * a_row
        tri = tri_ref[...]
        tri16 = tri.astype(BF16)
        allowed = (tri[:q] > 0.0, tri[q:] > 0.0)
        emat = e_ref[...]
        lane_q = lax.broadcasted_iota(jnp.int32, (q, LANES), 1)
        lo_half = lane_q < SSD_HEAD_DIM
        prep = []
        for cc in range(SSD_CPS):
            rows = slice(cc * q, (cc + 1) * q)
            cums = _split_dot_rhs(tri16, adt[rows])
            acum = jnp.where(is_fwd, cums[:q], cums[q:])
            end = jnp.where(is_fwd, acum[q - 1:q, :], acum[0:1, :])
            dte = jnp.exp2(end - acum)
            stack = jnp.concatenate([dtv[rows], dte, jnp.broadcast_to(jnp.exp2(end), (16, LANES))], axis=0)
            ex = _split2_dot(stack, emat)
            c16 = cm[rows].astype(BF16)
            b_c = bm[rows]
            cbm = _dot_nt(c16, b_c.astype(BF16))
            prep.append((acum, acum.T, ex, c16, b_c, cbm))
        for cc in range(SSD_CPS):
            rows = slice(cc * q, (cc + 1) * q)
            acum, acum_t, ex, c16, b_c, cbm = prep[cc]
            xs_c = xs[rows]
            xdt2 = jnp.concatenate([xs_c, xs_c], axis=1) * ex[:q]
            ys = []
            for m in range(SSD_HPG // 2):
                mms, xhs = [], []
                for d in range(2):
                    xpair = xdt2[:, gw * d + LANES * m:gw * d + LANES * (m + 1)]
                    for half in range(2):
                        ln = SSD_HEADS * d + 2 * m + half
                        seg = acum[:, ln:ln + 1] - acum_t[ln:ln + 1, :]
                        dec = jnp.exp2(jnp.where(allowed[d], seg, NEG_INF))
                        mms.append((cbm * dec).astype(BF16))
                        xhs.append(jnp.where(lo_half if half == 0 else ~lo_half, xpair, 0.0).astype(BF16))
                ys.append(_dot(jnp.concatenate(mms, axis=1), jnp.concatenate(xhs, axis=0)))
            y_loc = jnp.concatenate(ys, axis=1) + dsk_ref[...] * xs_c
            sts = _dot(b_c.T.astype(BF16), (xdt2 * ex[q:2 * q]).astype(BF16))
            pos = SSD_CPS * t + cc
            r0 = pl.multiple_of(pos * q, q)
            yloc_ref[pl.ds(r0, q), :] = y_loc
            s_ref[pl.ds(r0, q), :] = sts.astype(BF16)
            c_ref[pl.ds(r0, q), :] = c16
            ea_ref[pl.ds(r0, q), :] = jnp.exp2(acum)
            cd_ref[pl.ds(pl.multiple_of(pos * 8, 8), 8), :] = ex[2 * q:2 * q + 8]

    @pl.when(t == nb)
    def _backward_states():
        hb_ref[...] = jnp.zeros_like(hb_ref)
        hf_ref[...] = jnp.zeros_like(hf_ref)

        def body(i, carry):
            pos = jnp.where(i < ncc, ncc - 1 - i, ns - 1 + ncc - i)
            r0 = pl.multiple_of(pos * q, q)
            hb = hb_ref[...]
            hbe_ref[pl.ds(r0, q), :] = hb.astype(BF16)
            cd = cd_ref[pl.ds(pl.multiple_of(pos * 8, 8), 8), :][0:1, gw:]
            hb_ref[...] = hb * cd + s_ref[pl.ds(r0, q), gw:].astype(F32)
            return carry

        lax.fori_loop(0, ns, body, 0)

    @pl.when(t >= nb)
    def _emit():
        emat = e_ref[...]
        starts = [pl.multiple_of((SSD_CPS * (t - nb) + cc) * q, q) for cc in range(SSD_CPS)]
        ea_x = [_split2_dot(ea_ref[pl.ds(r0, q), :], emat) for r0 in starts]
        gate = [_silu(z_ref[cc * q:(cc + 1) * q, :].astype(F32)) for cc in range(SSD_CPS)]
        hf = hf_ref[...]
        for cc in range(SSD_CPS):
            rows = slice(cc * q, (cc + 1) * q)
            pos = SSD_CPS * (t - nb) + cc
            r0 = starts[cc]
            hcat = jnp.concatenate([hf.astype(BF16), hbe_ref[pl.ds(r0, q), :]], axis=1)
            yo = _dot(c_ref[pl.ds(r0, q), :], hcat) * ea_x[cc]
            y = yloc_ref[pl.ds(r0, q), :] + yo[:, :gw] + yo[:, gw:]
            cd = cd_ref[pl.ds(pl.multiple_of(pos * 8, 8), 8), :][0:1, :gw]
            hf = hf * cd + s_ref[pl.ds(r0, q), :gw].astype(F32)
            y = y * gate[cc]
            y = y * lax.rsqrt(jnp.mean(y * y, axis=-1, keepdims=True) + NORM_EPS) * ng_ref[...]
            o_ref[rows, :] = y.astype(o_ref.dtype)
        hf_ref[...] = hf


def _ssd2(xbc, dt, z, j, conv_w, conv_b, dtb_tab, alog_tab, dsk_x, ng, tri, emat, n_batch, seq, n_ctx):
    rows = xbc.shape[0]
    q = SSD_CHUNK
    blk = SSD_CPS * q
    assert n_ctx == blk and seq % blk == 0
    ncc = n_ctx // q
    nlb = seq // blk
    nb = 1 + nlb
    ns = nb * SSD_CPS
    per = blk // HALO

    def rb(b, pb):
        return jnp.where(pb == 0, n_batch * nlb + b, b * nlb + pb - 1)

    def rb_in(b, t):
        return rb(b, jnp.minimum(t, nb - 1))

    def rb_out(b, t):
        return rb(b, jnp.maximum(t - nb, 0))

    last_halo = rows // HALO - 1
    gw = SSD_GROUP_W
    return pl.pallas_call(
        functools.partial(_ssd2_body, nb, ncc),
        grid=(n_batch, SSD_GROUPS, 2 * nb),
        in_specs=[pl.BlockSpec((blk, SSD_XBC_W), lambda b, g, t: (rb_in(b, t), g)),
                  pl.BlockSpec((HALO, SSD_XBC_W),
                               lambda b, g, t: (jnp.maximum(rb_in(b, t) * per - 1, 0), g)),
                  pl.BlockSpec((HALO, SSD_XBC_W),
                               lambda b, g, t: (jnp.minimum(rb_in(b, t) * per + per, last_halo), g)),
                  pl.BlockSpec((blk, LANES), lambda b, g, t: (rb_in(b, t), 0)),
                  pl.BlockSpec((blk, gw), lambda b, g, t: (rb_out(b, t), g)),
                  pl.BlockSpec((None, SSD_CONV_W, SSD_XBC_W), lambda b, g, t: (j, 0, g)),
                  pl.BlockSpec((None, 1, SSD_XBC_W), lambda b, g, t: (j, 0, g)),
                  pl.BlockSpec((None, 8, LANES), lambda b, g, t: (j, 0, 0)),
                  pl.BlockSpec((None, 8, LANES), lambda b, g, t: (j, 0, 0)),
                  pl.BlockSpec((None, 1, gw), lambda b, g, t: (j, 0, g)),
                  pl.BlockSpec((None, 1, gw), lambda b, g, t: (j, 0, g)),
                  pl.BlockSpec((2 * q, q), lambda b, g, t: (0, 0)),
                  pl.BlockSpec((LANES, 2 * gw), lambda b, g, t: (0, 0))],
        out_specs=pl.BlockSpec((blk, gw), lambda b, g, t: (rb_out(b, t), g)),
        out_shape=jax.ShapeDtypeStruct((rows, SSD_INNER), BF16),
        scratch_shapes=[pltpu.VMEM((ns * q, gw), F32),
                        pltpu.VMEM((ns * q, 2 * gw), BF16),
                        pltpu.VMEM((ns * q, SSD_STATE), BF16),
                        pltpu.VMEM((ns * q, LANES), F32),
                        pltpu.VMEM((ns * 8, 2 * gw), F32),
                        pltpu.VMEM((ns * q, gw), BF16),
                        pltpu.VMEM((SSD_STATE, gw), F32),
                        pltpu.VMEM((SSD_STATE, gw), F32)],
        compiler_params=_cparams("arbitrary", "arbitrary", "arbitrary"),
        name="ssd_bidir",
    )(xbc, xbc, xbc, dt, z, conv_w, conv_b, dtb_tab, alog_tab, dsk_x, ng, tri, emat)


def _stack_heads(qc):
    lane = lax.broadcasted_iota(jnp.int32, qc.shape, 1)
    lo = lane < HEAD_DIM
    zero = jnp.zeros_like(qc)
    return jnp.concatenate([jnp.where(lo, qc, zero), jnp.where(lo, zero, qc)], axis=0)


def _unstack_heads(o, n):
    lane = lax.broadcasted_iota(jnp.int32, (n, LANES), 1)
    return jnp.where(lane < HEAD_DIM, o[:n], o[n:])


def _softmax_pv(scores, values, sink=None):
    m = None
    for sc in scores:
        mx = jnp.max(sc, axis=-1, keepdims=True)
        m = mx if m is None else jnp.maximum(m, mx)
    if sink is not None:
        m = jnp.maximum(m, sink)
    den = None if sink is None else jnp.exp2(sink - m)
    acc = None
    for sc, v in zip(scores, values):
        pr = jnp.exp2(sc - m)
        sm = jnp.sum(pr, axis=-1, keepdims=True)
        den = sm if den is None else den + sm
        t = _dot(pr.astype(BF16), v)
        acc = t if acc is None else acc + t
    return acc * (1.0 / den)


def _win_body(q_ref, kp_ref, ko_ref, kn_ref, vp_ref, vo_ref, vn_ref, kc_ref, vc_ref, sink_ref, o_ref):
    n = pl.program_id(1)
    nb = pl.num_programs(1)
    wb = WIN_BLOCK
    kb = jnp.concatenate([kp_ref[...], ko_ref[...], kn_ref[...]], axis=0)
    vb = jnp.concatenate([vp_ref[...], vo_ref[...], vn_ref[...]], axis=0)
    r = lax.broadcasted_iota(jnp.int32, (wb, 3 * wb), 0)
    c = lax.broadcasted_iota(jnp.int32, (wb, 3 * wb), 1)
    pen_prev = jnp.where(n > 0, 0.0, NEG_INF)
    pen_next = jnp.where(n < nb - 1, 0.0, NEG_INF)
    pen = jnp.where(c < wb, jnp.where(c >= r, pen_prev, NEG_INF),
                    jnp.where(c < 2 * wb, 0.0, jnp.where(c - 2 * wb <= r, pen_next, NEG_INF)))
    pen = jnp.concatenate([pen, pen], axis=0)
    half = WIN_GQA * wb
    for m in range(WIN_GQA):
        sl = slice(LANES * m, LANES * (m + 1))
        qs = _stack_heads(q_ref[:, sl])
        s_loc = _dot_nt(qs, kb) + pen
        s_ctx = _dot_nt(qs, kc_ref[...])
        sink = jnp.concatenate([sink_ref[m * wb:(m + 1) * wb, 0:1],
                                sink_ref[half + m * wb:half + (m + 1) * wb, 0:1]], axis=0)
        o = _softmax_pv([s_loc, s_ctx], [vb, vc_ref[...]], sink)
        o_ref[:, sl] = _unstack_heads(o, wb).astype(o_ref.dtype)


def _window_attention(qw, kw, vw, sink_x, j, n_batch, seq, n_ctx):
    wb = WIN_BLOCK
    nb = seq // wb
    ctx_blk0 = n_batch * seq // n_ctx
    q_spec = pl.BlockSpec((wb, 512), lambda b, n: (b * nb + n, 0))
    prev = pl.BlockSpec((wb, LANES), lambda b, n: (b * nb + jnp.maximum(n - 1, 0), 0))
    own = pl.BlockSpec((wb, LANES), lambda b, n: (b * nb + n, 0))
    nxt = pl.BlockSpec((wb, LANES), lambda b, n: (b * nb + jnp.minimum(n + 1, nb - 1), 0))
    cspec = pl.BlockSpec((n_ctx, LANES), lambda b, n: (ctx_blk0 + b, 0))
    return pl.pallas_call(
        _win_body,
        grid=(n_batch, nb),
        in_specs=[q_spec, prev, own, nxt, prev, own, nxt, cspec, cspec,
                  _layer(sink_x, j)],
        out_specs=pl.BlockSpec((wb, 512), lambda b, n: (b * nb + n, 0)),
        out_shape=jax.ShapeDtypeStruct((n_batch * seq, 512), BF16),
        compiler_params=_cparams("parallel", "parallel"),
        name="window_attention",
    )(qw, kw, kw, kw, vw, vw, vw, kw, vw, sink_x)


NA_ROWS_PER_STEP = 4


def _na_body(n_rows, q_ref, k_ref, v_ref, kc_ref, vc_ref, bias_ref, o_ref):
    kr = NA_ROWS
    for j in range(NA_ROWS_PER_STEP):
        r = pl.program_id(1) * NA_ROWS_PER_STEP + j
        first_row = jnp.clip(r - kr // 2, 0, n_rows - kr)
        start = pl.multiple_of(first_row * GRID_W, GRID_W)
        ro0 = first_row - r + NA_ROWS - 1
        rows = slice(j * GRID_W, (j + 1) * GRID_W)
        scores = []
        for m in range(NA_HEADS // 2):
            sl = slice(LANES * m, LANES * (m + 1))
            qs = _stack_heads(q_ref[rows, sl])
            kw = k_ref[pl.ds(start, kr * GRID_W), sl]
            s_loc = _dot_nt(qs, kw) + bias_ref[ro0, 2 * GRID_W * m:2 * GRID_W * (m + 1), :]
            scores.append((s_loc, _dot_nt(qs, kc_ref[:, sl])))
        for m in range(NA_HEADS // 2):
            sl = slice(LANES * m, LANES * (m + 1))
            vw = v_ref[pl.ds(start, kr * GRID_W), sl]
            o = _softmax_pv(list(scores[m]), [vw, vc_ref[:, sl]])
            o_ref[rows, sl] = _unstack_heads(o, GRID_W).astype(o_ref.dtype)


def _na_attention(qn, kn, vn, bias_tab, j, n_batch, seq, n_ctx):
    n_rows = seq // GRID_W
    assert n_rows >= NA_ROWS and n_rows % NA_ROWS_PER_STEP == 0
    n_steps = n_rows // NA_ROWS_PER_STEP
    qrows = NA_ROWS_PER_STEP * GRID_W
    ctx_blk0 = n_batch * seq // n_ctx
    return pl.pallas_call(
        functools.partial(_na_body, n_rows),
        grid=(n_batch, n_steps),
        in_specs=[pl.BlockSpec((qrows, 512), lambda b, r: (b * n_steps + r, 0)),
                  pl.BlockSpec((seq, 512), lambda b, r: (b, 0)),
                  pl.BlockSpec((seq, 512), lambda b, r: (b, 0)),
                  pl.BlockSpec((n_ctx, 512), lambda b, r: (ctx_blk0 + b, 0)),
                  pl.BlockSpec((n_ctx, 512), lambda b, r: (ctx_blk0 + b, 0)),
                  _layer(bias_tab, j)],
        out_specs=pl.BlockSpec((qrows, 512), lambda b, r: (b * n_steps + r, 0)),
        out_shape=jax.ShapeDtypeStruct((n_batch * seq, 512), BF16),
        compiler_params=_cparams("parallel", "arbitrary"),
        name="neighbourhood_attention",
    )(qn, kn, vn, kn, vn, bias_tab)


def _ctx_attn_body(qw_ref, kw_ref, vw_ref, qn_ref, kn_ref, vn_ref, sink_ref, ow_ref, on_ref):
    n = qw_ref.shape[0]
    qw = qw_ref[...]
    lane = lax.broadcasted_iota(jnp.int32, (n, LANES), 1)
    lo = lane < HEAD_DIM
    for m in range(WIN_GQA):
        qs = _stack_heads(qw[:, LANES * m:LANES * (m + 1)])
        r0, r1 = m * WIN_BLOCK, (WIN_GQA + m) * WIN_BLOCK
        sink = jnp.concatenate([jnp.broadcast_to(sink_ref[r0:r0 + 1, 0:1], (n, 1)),
                                jnp.broadcast_to(sink_ref[r1:r1 + 1, 0:1], (n, 1))], axis=0)
        o = _softmax_pv([_dot_nt(qs, kw_ref[...])], [vw_ref[...]], sink)
        ow_ref[:, LANES * m:LANES * (m + 1)] = jnp.where(lo, o[:n], o[n:]).astype(ow_ref.dtype)
    qn = qn_ref[...]
    for m in range(NA_HEADS // 2):
        sl = slice(LANES * m, LANES * (m + 1))
        qs = _stack_heads(qn[:, sl])
        o = _softmax_pv([_dot_nt(qs, kn_ref[:, sl])], [vn_ref[:, sl]])
        on_ref[:, sl] = _unstack_heads(o, n).astype(on_ref.dtype)


def _ctx_attention(qw, kw, vw, qn, kn, vn, sink_x, j, n_batch, seq, n_ctx):
    blk0 = n_batch * seq // n_ctx
    wide = pl.BlockSpec((n_ctx, 512), lambda b: (blk0 + b, 0))
    narrow = pl.BlockSpec((n_ctx, LANES), lambda b: (blk0 + b, 0))
    out = pl.BlockSpec((n_ctx, 512), lambda b: (b, 0))
    return pl.pallas_call(
        _ctx_attn_body,
        grid=(n_batch,),
        in_specs=[wide, narrow, narrow, wide, wide, wide,
                  _layer(sink_x, j)],
        out_specs=[out, out],
        out_shape=[jax.ShapeDtypeStruct((n_batch * n_ctx, 512), BF16)] * 2,
        compiler_params=_cparams("parallel"),
        name="context_attention",
    )(qw, kw, vw, qn, kn, vn, sink_x)


def _take(w, runs, axis):
    parts = []
    for run in runs:
        if run[0] is None:
            shape = list(w.shape)
            shape[axis] = run[1]
            parts.append(jnp.zeros(shape, w.dtype))
        else:
            parts.append(lax.slice_in_dim(w, run[0], run[1], axis=axis))
    return jnp.concatenate(parts, axis=axis)


def _xbc_runs():
    gn = SSD_GROUPS * SSD_STATE
    runs = []
    for g in range(SSD_GROUPS):
        runs.append((g * SSD_GROUP_W, (g + 1) * SSD_GROUP_W))
        runs.append((SSD_INNER + g * SSD_STATE, SSD_INNER + (g + 1) * SSD_STATE))
        runs.append((SSD_INNER + gn + g * SSD_STATE, SSD_INNER + gn + (g + 1) * SSD_STATE))
    return runs


def _head_rows(t):
    n = t.shape[0]
    flat = jnp.pad(t.astype(F32).reshape(n, 2 * SSD_HEADS), ((0, 0), (0, LANES - 2 * SSD_HEADS)))
    rows = [jnp.roll(flat, -SSD_HPG * g, axis=1) for g in range(SSD_GROUPS)]
    rows += [jnp.zeros_like(flat)] * (8 - SSD_GROUPS)
    return jnp.stack(rows, axis=1)


def _win_head_runs():
    runs = []
    for m in range(WIN_GQA):
        for kvh in range(WIN_KV_HEADS):
            h = kvh * WIN_GQA + m
            runs.append((h * HEAD_DIM, (h + 1) * HEAD_DIM))
    return runs


def _rope_tables(seq):
    pos = np.arange(seq)
    row = (pos // GRID_W).astype(np.float32)
    col = (pos % GRID_W).astype(np.float32)
    n_freq = HEAD_DIM // 4
    inv = (np.float32(ROPE_THETA) ** (-np.arange(n_freq, dtype=np.float32) / n_freq)).astype(np.float32)
    ar = (row[:, None] * inv).astype(np.float32)
    ac = (col[:, None] * inv).astype(np.float32)
    cos_h = np.concatenate([np.cos(ar), np.cos(ar), np.cos(ac), np.cos(ac)], axis=1)
    sin_h = np.concatenate([-np.sin(ar), np.sin(ar), -np.sin(ac), np.sin(ac)], axis=1)
    ident_c = np.ones((ROW_TILE, HEAD_DIM), np.float32)
    ident_s = np.zeros((ROW_TILE, HEAD_DIM), np.float32)
    cos_t = np.concatenate([cos_h, ident_c], axis=0)
    sin_t = np.concatenate([sin_h, ident_s], axis=0)
    return (jnp.asarray(np.tile(cos_t, (1, 2)), F32), jnp.asarray(np.tile(sin_t, (1, 2)), F32))


def _na_bias_table(rpb):
    n, h, nr, _ = rpb.shape
    kr = NA_ROWS
    cols = np.arange(GRID_W)
    col_start = np.clip(cols - NA_COLS // 2, 0, GRID_W - NA_COLS)
    col_ok = (cols[None] >= col_start[:, None]) & (cols[None] < col_start[:, None] + NA_COLS)
    r32 = rpb.astype(F32) * LOG2E
    v = jnp.concatenate([r32[..., NA_COLS - 1:], jnp.zeros((n, h, nr, LANES - (2 * NA_COLS - 1)), F32),
                         r32[..., :NA_COLS - 1]], axis=-1)
    flat = jnp.tile(v, (1, 1, 1, GRID_W))[..., :GRID_W * (LANES - 1)]
    toep = flat.reshape(n, h, nr, GRID_W, LANES - 1)[..., :GRID_W]
    toep = jnp.where(jnp.asarray(col_ok), toep, NEG_INF)
    tabs = [jnp.transpose(toep[:, :, ro0:ro0 + kr], (0, 1, 3, 2, 4)).reshape(n, h * GRID_W, kr * GRID_W)
            for ro0 in range(NA_ROWS)]
    return jnp.stack(tabs, axis=1)


def kernel(x, c, ctx, c_ctx, w_mod, b_mod, norm_mix_g, norm_ff_g, w_ff1, w_ff2, w_in_even, conv_w, conv_b,
           dt_bias, a_log, d_skip, ssd_norm_g, w_out_even, w_in_odd, q_norm_win, k_norm_win, sink_win,
           q_norm_na, k_norm_na, rpb_na, w_out_odd):
    n_batch, seq, d = x.shape
    n_ctx = ctx.shape[1]
    n_lat = n_batch * seq
    tm = ROW_TILE
    tiles_per_batch = seq // tm
    n_lat_tiles = n_lat // tm
    n_ctx_tiles = (n_batch * n_ctx) // tm
    n2 = seq // DFT_N1

    h = jnp.concatenate([x.reshape(n_lat, d), ctx.reshape(n_batch * n_ctx, d)], axis=0)

    cvec = jnp.concatenate([c, c_ctx[None], jnp.zeros((8 - n_batch - 1, d), F32)], axis=0)
    mod_all = _modulation(cvec, w_mod, b_mod)
    mod_all = mod_all.reshape(-1, 8, 6, 1, d)[:, :n_batch + 1].transpose(0, 2, 1, 3, 4)

    m_a, twc, tws, m_b, m_c, chan = _dft_tables(seq, n_ctx)
    xbc_runs = _xbc_runs()
    s3 = FNET_W + SSD_INNER + SSD_CONV_DIM
    w_four = _weight_product(w_in_even, FNET_W, chan)
    tri = jnp.asarray(np.concatenate([np.tril(np.ones((SSD_CHUNK, SSD_CHUNK))),
                                      np.triu(np.ones((SSD_CHUNK, SSD_CHUNK)))], axis=0), F32)
    emat = np.zeros((LANES, 2 * SSD_GROUP_W), np.float32)
    for dr in range(2):
        for r in range(SSD_HPG):
            c0 = dr * SSD_GROUP_W + r * SSD_HEAD_DIM
            emat[SSD_HEADS * dr + r, c0:c0 + SSD_HEAD_DIM] = 1.0
    emat = jnp.asarray(emat, BF16)

    cos_t, sin_t = _rope_tables(seq)
    gmat_slab = jnp.asarray(np.kron(np.eye(MXU_DIM // HEAD_DIM), np.full((HEAD_DIM, HEAD_DIM), 1.0 / HEAD_DIM)),
                          BF16)
    gmat128 = gmat_slab[:LANES, :LANES]
    win_runs = _win_head_runs()
    wq = WIN_Q_HEADS * HEAD_DIM

    gains_mix = norm_mix_g.astype(F32)[:, None, :]
    gains_ff = norm_ff_g.astype(F32)[:, None, :]
    w_ff1_b, w_ff2_b = w_ff1.astype(BF16), w_ff2.astype(BF16)
    w_even_b = w_in_even.astype(BF16)
    w_dt_b = jnp.pad(w_even_b[:, :, s3:], ((0, 0), (0, 0), (0, LANES - 2 * SSD_HEADS)))
    w_out_even_b = w_out_even.astype(BF16)
    conv_w_p = _take(conv_w.astype(F32), xbc_runs, 2)
    conv_b_p = _take(conv_b.astype(F32)[:, None, :], xbc_runs, 2)
    dtb_rows, alog_rows = _head_rows(dt_bias), _head_rows(a_log)
    dsk_x = jnp.repeat(d_skip.astype(F32), SSD_HEAD_DIM, axis=1)[:, None, :]
    ssd_ng = ssd_norm_g.astype(F32)[:, None, :]
    w_odd_b = w_in_odd.astype(BF16)
    wq_perm = _take(w_odd_b[:, :, :wq], win_runs, 2)
    w_out_odd_b = w_out_odd.astype(BF16)
    wo_win_perm = _take(w_out_odd_b[:, :wq], win_runs, 1)
    qscale = HEAD_DIM ** -0.5 * LOG2E
    rep = lambda g, k: jnp.tile(g.astype(F32), (1, k))
    head_gains = jnp.stack([rep(q_norm_win, WIN_Q_HEADS) * qscale,
                            jnp.pad(rep(k_norm_win, WIN_KV_HEADS), ((0, 0), (0, wq - WIN_KV_HEADS * HEAD_DIM))),
                            rep(q_norm_na, NA_HEADS) * qscale, rep(k_norm_na, NA_HEADS)], axis=1)
    head_gains = jnp.pad(head_gains, ((0, 0), (0, 4), (0, 0)))
    sink_x = jnp.broadcast_to(jnp.repeat(sink_win.astype(F32) * LOG2E, WIN_BLOCK, axis=1)[:, :, None],
                              (sink_win.shape[0], WIN_Q_HEADS * WIN_BLOCK, LANES))
    bias_tab = _na_bias_table(rpb_na)

    for i in range(DEPTH):
        need_ctx = i < DEPTH - 1
        j = i // 2
        if i % 2 == 0:
            fr, fi, z, xbc, dtr = _inproj_even(h, i, j, gains_mix, mod_all, w_four, w_even_b, w_dt_b,
                                               n_batch, tiles_per_batch)

            br, bi = _dft_stage_a(fr.reshape(-1, n2, FNET_W), fi.reshape(-1, n2, FNET_W),
                                  m_a, twc, tws, n_batch, n2)
            f_lat = _dft_stage_b(br.reshape(n_batch, seq, FNET_W), bi.reshape(n_batch, seq, FNET_W),
                                 m_b, n_batch, n2).reshape(n_lat, FNET_W)
            f_ctx = _ctx_dft(fr, fi, m_c, n_batch, n_ctx, n_lat // n_ctx)

            y_ssd = _ssd2(xbc, dtr, z, j, conv_w_p, conv_b_p, dtb_rows, alog_rows, dsk_x, ssd_ng, tri, emat,
                          n_batch, seq, n_ctx)

            h = _outproj(h, i, mod_all, f_lat, y_ssd, w_out_even_b, w_out_even_b, j, FNET_W,
                         n_batch, tiles_per_batch, 0, 0, 0, n_lat_tiles)
            h = _outproj(h, i, mod_all, f_ctx, y_ssd, w_out_even_b, w_out_even_b, j, FNET_W,
                         n_batch, tiles_per_batch, n_lat_tiles, 0, n_lat_tiles, n_ctx_tiles)
        else:
            qw, kw, vw, qn, kn, vn = _inproj_odd(h, i, j, gains_mix, mod_all, wq_perm, w_odd_b, cos_t, sin_t,
                                                 gmat_slab, gmat128, head_gains, n_batch, tiles_per_batch)
            o_win = _window_attention(qw, kw, vw, sink_x, j, n_batch, seq, n_ctx)
            o_na = _na_attention(qn, kn, vn, bias_tab, j, n_batch, seq, n_ctx)
            if need_ctx:
                oc_win, oc_na = _ctx_attention(qw, kw, vw, qn, kn, vn, sink_x, j, n_batch, seq, n_ctx)
            h = _outproj(h, i, mod_all, o_win, o_na, wo_win_perm, w_out_odd_b, j, wq,
                         n_batch, tiles_per_batch, 0, 0, 0, n_lat_tiles)
            if need_ctx:
                h = _outproj(h, i, mod_all, oc_win, oc_na, wo_win_perm, w_out_odd_b, j, wq,
                             n_batch, tiles_per_batch, n_lat_tiles, 0, 0, n_ctx_tiles)
        n_tiles = n_lat_tiles + n_ctx_tiles if need_ctx else n_lat_tiles
        h = _ffn(h, i, gains_ff, mod_all, w_ff1_b, w_ff2_b, n_batch, tiles_per_batch, n_tiles)
    return h.reshape(n_batch, seq, d)
```

```python
import functools
import math

import numpy as np
import jax
import jax.numpy as jnp
from jax import lax
from jax.experimental import pallas as pl
from jax.experimental.pallas import tpu as pltpu

F32 = jnp.float32
BF16 = jnp.bfloat16
HIGHEST = lax.Precision.HIGHEST

D_MODEL = 1024
DEPTH = 4
GRID_W = 64
D_FF = 4 * D_MODEL
NORM_EPS = 1e-6
NEG_INF = -1e30

FNET_GROUPS = 8
FNET_GROUP_W = 64
FNET_W = FNET_GROUPS * FNET_GROUP_W
SSD_HEAD_DIM = 64
SSD_HEADS = 24
SSD_GROUPS = 4
SSD_HPG = SSD_HEADS // SSD_GROUPS
SSD_STATE = 128
SSD_INNER = SSD_HEADS * SSD_HEAD_DIM
SSD_GROUP_W = SSD_HPG * SSD_HEAD_DIM
SSD_XBC_W = SSD_GROUP_W + 2 * SSD_STATE
SSD_CONV_DIM = SSD_INNER + 2 * SSD_GROUPS * SSD_STATE
SSD_CONV_W = 5
SSD_CHUNK = 128
HEAD_DIM = 64
WIN_Q_HEADS = 8
WIN_KV_HEADS = 2
WIN_GQA = WIN_Q_HEADS // WIN_KV_HEADS
WIN_BLOCK = 128
WIN_RADIUS = 128
NA_HEADS = 8
NA_ROWS = 8
NA_COLS = 16
ROPE_THETA = 10000.0
LOG2E = 1.4426950408889634

LANES = 128
MXU_DIM = 256
HALO = 8
ROW_TILE = 512
FF_TILE = 4096
DFT_N1 = 128
VMEM_LIMIT = 56 * 1024 * 1024


def _cparams(*sem):
    return pltpu.CompilerParams(dimension_semantics=sem, vmem_limit_bytes=VMEM_LIMIT)


def _silu(x):
    return x / (1.0 + jnp.exp2(x * -LOG2E))


def _softplus(x):
    return jnp.maximum(x, 0.0) + jnp.log(1.0 + jnp.exp(-jnp.abs(x)))


def _norm_mod(h, g, shift, scale):
    ms = jnp.mean(h * h, axis=-1, keepdims=True)
    y = h * lax.rsqrt(ms + NORM_EPS) * g
    return y * (1.0 + scale) + shift


def _dot(a, b):
    return jnp.dot(a, b, preferred_element_type=F32)


def _dot_nt(a, b):
    return lax.dot_general(a, b, (((1,), (1,)), ((), ())), preferred_element_type=F32)


def _mod_body(c_ref, w_ref, b_ref, o_ref):
    s = _silu(c_ref[...])
    o_ref[...] = jnp.dot(s, w_ref[...], precision=HIGHEST, preferred_element_type=F32) + b_ref[...]


def _modulation(cvec, w_mod, b_mod):
    depth, d, n = w_mod.shape
    tn = 1536
    return pl.pallas_call(
        _mod_body,
        grid=(depth, n // tn),
        in_specs=[pl.BlockSpec((8, d), lambda i, j: (0, 0)),
                  pl.BlockSpec((None, d, tn), lambda i, j: (i, 0, j)),
                  pl.BlockSpec((None, 1, tn), lambda i, j: (i, 0, j))],
        out_specs=pl.BlockSpec((None, 8, tn), lambda i, j: (i, 0, j)),
        out_shape=jax.ShapeDtypeStruct((depth, 8, n), F32),
        compiler_params=_cparams("parallel", "parallel"),
        name="modulation",
    )(cvec, w_mod, b_mod.reshape(depth, 1, n))


def _wprod_body(a_ref, b_ref, o_ref):
    o_ref[...] = jnp.dot(a_ref[...], b_ref[...], precision=HIGHEST,
                         preferred_element_type=F32).astype(o_ref.dtype)


def _weight_product(a, k, b):
    n, m, _ = a.shape
    p = b.shape[1]
    return pl.pallas_call(
        _wprod_body,
        grid=(n,),
        in_specs=[pl.BlockSpec((None, m, k), lambda i: (i, 0, 0)),
                  pl.BlockSpec((k, p), lambda i: (0, 0))],
        out_specs=pl.BlockSpec((None, m, p), lambda i: (i, 0, 0)),
        out_shape=jax.ShapeDtypeStruct((n, m, p), BF16),
        compiler_params=_cparams("parallel"),
        name="fold_channel_dft",
    )(a, b)


def _resident(shape):
    return pl.BlockSpec(shape, lambda *_: (0,) * len(shape), pipeline_mode=pl.Buffered(1))


def _layer(arr, j):
    nd = arr.ndim
    return pl.BlockSpec((None,) + arr.shape[1:], lambda *_: (j,) + (0,) * (nd - 1),
                        pipeline_mode=pl.Buffered(1))


def _mod_spec(layer, which, tiles_per_batch, n_batch, tile_off):
    def imap(m):
        return (layer, which, jnp.minimum((m + tile_off) // tiles_per_batch, n_batch), 0, 0)
    return pl.BlockSpec((None, None, None, 1, D_MODEL), imap)


def _inproj_even_body(h_ref, g_ref, sh_ref, sc_ref, wf_ref, w_ref, wdt_ref,
                      fr_ref, fi_ref, z_ref, xbc_ref, dt_ref):
    u = _norm_mod(h_ref[...], g_ref[...], sh_ref[...], sc_ref[...]).astype(BF16)
    fr_ref[...] = _dot(u, wf_ref[:, :FNET_W])
    fi_ref[...] = _dot(u, wf_ref[:, FNET_W:])
    z_ref[...] = _dot(u, w_ref[:, FNET_W:FNET_W + SSD_INNER]).astype(z_ref.dtype)
    c0 = FNET_W + SSD_INNER
    xbc = _dot(u, w_ref[:, c0:c0 + SSD_CONV_DIM])
    gn = SSD_GROUPS * SSD_STATE
    for g in range(SSD_GROUPS):
        o = g * SSD_XBC_W
        b0 = SSD_INNER + g * SSD_STATE
        xbc_ref[:, o:o + SSD_GROUP_W] = xbc[:, g * SSD_GROUP_W:(g + 1) * SSD_GROUP_W]
        xbc_ref[:, o + SSD_GROUP_W:o + SSD_GROUP_W + SSD_STATE] = xbc[:, b0:b0 + SSD_STATE]
        xbc_ref[:, o + SSD_GROUP_W + SSD_STATE:o + SSD_XBC_W] = xbc[:, b0 + gn:b0 + gn + SSD_STATE]
    dt_ref[...] = _dot(u, wdt_ref[...])


def _inproj_even(h, layer, j, gains, mod_all, w_four, w_main, w_dt, n_batch, tiles_per_batch):
    rows = h.shape[0]
    widths = (FNET_W, FNET_W, SSD_INNER, SSD_CONV_DIM, LANES)
    dtypes = (F32, F32, BF16, F32, F32)
    tm = ROW_TILE
    return pl.pallas_call(
        _inproj_even_body,
        grid=(rows // tm,),
        in_specs=[pl.BlockSpec((tm, D_MODEL), lambda m: (m, 0)),
                  _layer(gains, layer),
                  _mod_spec(layer, 0, tiles_per_batch, n_batch, 0),
                  _mod_spec(layer, 1, tiles_per_batch, n_batch, 0),
                  _layer(w_four, j), _layer(w_main, j), _layer(w_dt, j)],
        out_specs=[pl.BlockSpec((tm, wd), lambda m: (m, 0)) for wd in widths],
        out_shape=[jax.ShapeDtypeStruct((rows, wd), dt) for wd, dt in zip(widths, dtypes)],
        compiler_params=_cparams("parallel"),
        name="inproj_even",
    )(h, gains, mod_all, mod_all, w_four, w_main, w_dt)


def _group_rms(x, gmat_ref, gain):
    sq = x * x
    hi = sq.astype(BF16)
    lo = (sq - hi.astype(F32)).astype(BF16)
    gw = gmat_ref.shape[0]
    parts = []
    for c0 in range(0, x.shape[1], gw):
        parts.append(_dot(hi[:, c0:c0 + gw], gmat_ref[...]) + _dot(lo[:, c0:c0 + gw], gmat_ref[...]))
    ms = parts[0] if len(parts) == 1 else jnp.concatenate(parts, axis=1)
    return x * lax.rsqrt(ms + NORM_EPS) * gain


def _rope(y, cos, sin_signed):
    w = y.shape[-1]
    reps = w // LANES
    lane = lax.broadcasted_iota(jnp.int32, y.shape, 1)
    first = (lane % 32) < 16
    partner = jnp.where(first, pltpu.roll(y, w - 16, 1), pltpu.roll(y, 16, 1))
    if reps > 1:
        cos = jnp.concatenate([cos] * reps, axis=1)
        sin_signed = jnp.concatenate([sin_signed] * reps, axis=1)
    return y * cos + partner * sin_signed


def _inproj_odd_body(h_ref, g_ref, sh_ref, sc_ref, wq_ref, w_ref, cos_ref, sin_ref, gslab_ref, g128_ref,
                     hg_ref, qw_ref, kw_ref, vw_ref, qn_ref, kn_ref, vn_ref):
    u = _norm_mod(h_ref[...], g_ref[...], sh_ref[...], sc_ref[...]).astype(BF16)
    cos = cos_ref[...]
    sin = sin_ref[...]
    wq = WIN_Q_HEADS * HEAD_DIM
    wk = WIN_KV_HEADS * HEAD_DIM
    nh = NA_HEADS * HEAD_DIM
    qw = _group_rms(_dot(u, wq_ref[...]), gslab_ref, hg_ref[0:1, :])
    qw_ref[...] = _rope(qw, cos, sin).astype(BF16)
    kw = _group_rms(_dot(u, w_ref[:, wq:wq + wk]), g128_ref, hg_ref[1:2, :wk])
    kw_ref[...] = _rope(kw, cos, sin).astype(BF16)
    vw_ref[...] = _dot(u, w_ref[:, wq + wk:wq + 2 * wk]).astype(BF16)
    c0 = wq + 2 * wk
    qn_ref[...] = _group_rms(_dot(u, w_ref[:, c0:c0 + nh]), gslab_ref, hg_ref[2:3, :]).astype(BF16)
    kn_ref[...] = _group_rms(_dot(u, w_ref[:, c0 + nh:c0 + 2 * nh]), gslab_ref, hg_ref[3:4, :]).astype(BF16)
    vn_ref[...] = _dot(u, w_ref[:, c0 + 2 * nh:c0 + 3 * nh]).astype(BF16)


def _inproj_odd(h, layer, j, gains, mod_all, wq_perm, w_main, cos_t, sin_t, gmat_slab, gmat128, head_gains,
                n_batch, tiles_per_batch):
    rows = h.shape[0]
    tm = ROW_TILE
    widths = (512, 128, 128, 512, 512, 512)
    n_lat_tiles = n_batch * tiles_per_batch

    def rope_map(m):
        return (jnp.where(m < n_lat_tiles, m % tiles_per_batch, tiles_per_batch), 0)

    return pl.pallas_call(
        _inproj_odd_body,
        grid=(rows // tm,),
        in_specs=[pl.BlockSpec((tm, D_MODEL), lambda m: (m, 0)),
                  _layer(gains, layer),
                  _mod_spec(layer, 0, tiles_per_batch, n_batch, 0),
                  _mod_spec(layer, 1, tiles_per_batch, n_batch, 0),
                  _layer(wq_perm, j), _layer(w_main, j),
                  pl.BlockSpec((tm, LANES), rope_map),
                  pl.BlockSpec((tm, LANES), rope_map),
                  _resident(gmat_slab.shape), _resident(gmat128.shape), _layer(head_gains, j)],
        out_specs=[pl.BlockSpec((tm, wd), lambda m: (m, 0)) for wd in widths],
        out_shape=[jax.ShapeDtypeStruct((rows, wd), BF16) for wd in widths],
        compiler_params=_cparams("parallel"),
        name="inproj_odd",
    )(h, gains, mod_all, mod_all, wq_perm, w_main, cos_t, sin_t, gmat_slab, gmat128, head_gains)


def _outproj_body(k1, k2, b_row0, h_ref, gate_ref, a1_ref, a2_ref, wa_ref, wb_ref, o_ref):
    acc = _dot(a1_ref[...].astype(BF16), wa_ref[0:k1, :])
    acc += _dot(a2_ref[...].astype(BF16), wb_ref[b_row0:b_row0 + k2, :])
    o_ref[...] = h_ref[...] + gate_ref[...] * acc


def _outproj(h, layer, mod_all, a1, a2, wa, wb, j, b_row0, n_batch, tiles_per_batch, tile_off, a1_off,
             a2_off, n_tiles):
    tm = ROW_TILE
    k1, k2 = a1.shape[1], a2.shape[1]
    return pl.pallas_call(
        functools.partial(_outproj_body, k1, k2, b_row0),
        grid=(n_tiles,),
        in_specs=[pl.BlockSpec((tm, D_MODEL), lambda m: (m + tile_off, 0)),
                  _mod_spec(layer, 2, tiles_per_batch, n_batch, tile_off),
                  pl.BlockSpec((tm, k1), lambda m: (m + a1_off, 0)),
                  pl.BlockSpec((tm, k2), lambda m: (m + a2_off, 0)),
                  _layer(wa, j), _layer(wb, j)],
        out_specs=pl.BlockSpec((tm, D_MODEL), lambda m: (m + tile_off, 0)),
        out_shape=jax.ShapeDtypeStruct(h.shape, F32),
        input_output_aliases={0: 0},
        compiler_params=_cparams("parallel"),
        name="outproj",
    )(h, mod_all, a1, a2, wa, wb)


def _ffn_body(h_ref, g_ref, sh_ref, sc_ref, gate_ref, w1_ref, w2_ref, o_ref):
    h = h_ref[...]
    u = _norm_mod(h, g_ref[...], sh_ref[...], sc_ref[...]).astype(BF16)
    acc = None
    for k in range(D_FF // FF_TILE):
        sl = slice(k * FF_TILE, (k + 1) * FF_TILE)
        a = jnp.maximum(_dot(u, w1_ref[:, sl]), 0.0)
        t = _dot((a * a).astype(BF16), w2_ref[sl, :])
        acc = t if acc is None else acc + t
    o_ref[...] = h + gate_ref[...] * acc


def _ffn(h, layer, gains, mod_all, w1, w2, n_batch, tiles_per_batch, n_tiles):
    tm = ROW_TILE
    return pl.pallas_call(
        _ffn_body,
        grid=(n_tiles,),
        in_specs=[pl.BlockSpec((tm, D_MODEL), lambda m: (m, 0)),
                  _layer(gains, layer),
                  _mod_spec(layer, 3, tiles_per_batch, n_batch, 0),
                  _mod_spec(layer, 4, tiles_per_batch, n_batch, 0),
                  _mod_spec(layer, 5, tiles_per_batch, n_batch, 0),
                  _layer(w1, layer), _layer(w2, layer)],
        out_specs=pl.BlockSpec((tm, D_MODEL), lambda m: (m, 0)),
        out_shape=jax.ShapeDtypeStruct((n_tiles * tm, D_MODEL), F32),
        compiler_params=_cparams("parallel"),
        name="ffn",
    )(h, gains, mod_all, mod_all, mod_all, w1, w2)


def _dft_a_body(xr_ref, xi_ref, m_ref, c_ref, s_ref, br_ref, bi_ref):
    n1 = DFT_N1
    reps = FNET_W // LANES
    for j in range(xr_ref.shape[1]):
        x = jnp.concatenate([xr_ref[:, j, :], xi_ref[:, j, :]], axis=0).astype(BF16)
        a = _dot(m_ref[...], x)
        ar, ai = a[:n1], a[n1:]
        c = jnp.concatenate([c_ref[:, LANES * j:LANES * (j + 1)]] * reps, axis=1)
        s = jnp.concatenate([s_ref[:, LANES * j:LANES * (j + 1)]] * reps, axis=1)
        br_ref[:, j, :] = ar * c + ai * s
        bi_ref[:, j, :] = ai * c - ar * s


def _dft_stage_a(xr, xi, mmat, twc, tws, n_batch, n2):
    n1 = DFT_N1
    lb = 8
    blk = pl.BlockSpec((n1, lb, FNET_W), lambda b, j: (b, j, 0))
    return pl.pallas_call(
        _dft_a_body,
        grid=(n_batch, n2 // lb),
        in_specs=[blk, blk,
                  pl.BlockSpec((2 * n1, 2 * n1), lambda b, j: (0, 0)),
                  pl.BlockSpec((n1, lb * LANES), lambda b, j: (0, j)),
                  pl.BlockSpec((n1, lb * LANES), lambda b, j: (0, j))],
        out_specs=[blk, blk],
        out_shape=[jax.ShapeDtypeStruct((n_batch * n1, n2, FNET_W), F32)] * 2,
        compiler_params=_cparams("parallel", "parallel"),
        name="seq_dft_stage_a",
    )(xr, xi, mmat, twc, tws)


def _dft_b_body(br_ref, bi_ref, m_ref, o_ref):
    x = jnp.concatenate([br_ref[...], bi_ref[...]], axis=0).astype(BF16)
    res = _dot(m_ref[...], x)
    o_ref[...] = res.reshape(o_ref.shape)


def _dft_stage_b(br, bi, mmat, n_batch, n2):
    n1 = DFT_N1
    kb = 8
    return pl.pallas_call(
        _dft_b_body,
        grid=(n_batch, n1 // kb),
        in_specs=[pl.BlockSpec((None, kb * n2, FNET_W), lambda b, j: (b, j, 0)),
                  pl.BlockSpec((None, kb * n2, FNET_W), lambda b, j: (b, j, 0)),
                  pl.BlockSpec(mmat.shape, lambda b, j: (0, 0))],
        out_specs=pl.BlockSpec((None, n2, kb, FNET_W), lambda b, j: (b, 0, j, 0)),
        out_shape=jax.ShapeDtypeStruct((n_batch, n2, n1, FNET_W), F32),
        compiler_params=_cparams("parallel", "parallel"),
        name="seq_dft_stage_b",
    )(br, bi, mmat)


def _ctx_dft_body(xr_ref, xi_ref, m_ref, o_ref):
    x = jnp.concatenate([xr_ref[...], xi_ref[...]], axis=0).astype(BF16)
    o_ref[...] = _dot(m_ref[...], x)


def _ctx_dft(fr, fi, mmat, n_batch, n_ctx, row_block_off):
    return pl.pallas_call(
        _ctx_dft_body,
        grid=(n_batch,),
        in_specs=[pl.BlockSpec((n_ctx, FNET_W), lambda b: (row_block_off + b, 0)),
                  pl.BlockSpec((n_ctx, FNET_W), lambda b: (row_block_off + b, 0)),
                  pl.BlockSpec(mmat.shape, lambda b: (0, 0))],
        out_specs=pl.BlockSpec((n_ctx, FNET_W), lambda b: (b, 0)),
        out_shape=jax.ShapeDtypeStruct((n_batch * n_ctx, FNET_W), F32),
        compiler_params=_cparams("parallel"),
        name="ctx_dft",
    )(fr, fi, mmat)


def _dft_tables(seq, n_ctx):
    n1 = DFT_N1
    n2 = seq // n1
    k1 = np.arange(n1)
    ang1 = 2.0 * np.pi * np.outer(k1, k1) / n1
    c1, s1 = np.cos(ang1), np.sin(ang1)
    m_a = np.block([[c1, s1], [-s1, c1]])
    ang_t = 2.0 * np.pi * np.outer(k1, np.arange(n2)) / seq
    twc = np.repeat(np.cos(ang_t), LANES, axis=1)
    tws = np.repeat(np.sin(ang_t), LANES, axis=1)
    k2 = np.arange(n2)
    ang2 = 2.0 * np.pi * np.outer(k2, k2) / n2
    scale = 1.0 / math.sqrt(seq)
    c2, s2 = np.cos(ang2) * scale, np.sin(ang2) * scale
    eye = np.eye(8)
    m_b = np.concatenate([np.einsum("kl,ab->kabl", c2, eye).reshape(n2 * 8, 8 * n2),
                          np.einsum("kl,ab->kabl", s2, eye).reshape(n2 * 8, 8 * n2)], axis=1)
    kc = np.arange(n_ctx)
    angc = 2.0 * np.pi * np.outer(kc, kc) / n_ctx
    m_c = np.concatenate([np.cos(angc), np.sin(angc)], axis=1) / math.sqrt(n_ctx)
    ch = np.arange(FNET_GROUP_W)
    angg = 2.0 * np.pi * np.outer(ch, ch) / FNET_GROUP_W
    eg = np.eye(FNET_GROUPS)
    chan = np.concatenate([np.kron(eg, np.cos(angg)), -np.kron(eg, np.sin(angg))], axis=1)
    chan = chan / math.sqrt(FNET_GROUP_W)
    return (jnp.asarray(m_a, BF16), jnp.asarray(twc, F32), jnp.asarray(tws, F32),
            jnp.asarray(m_b, BF16), jnp.asarray(m_c, BF16), jnp.asarray(chan, F32))


def _split_dot_rhs(w_bf16, x):
    hi = x.astype(BF16)
    r1 = x - hi.astype(F32)
    mid = r1.astype(BF16)
    lo = (r1 - mid.astype(F32)).astype(BF16)
    return _dot(w_bf16, hi) + _dot(w_bf16, mid) + _dot(w_bf16, lo)


def _split2_dot(x, w_bf16):
    hi = x.astype(BF16)
    lo = (x - hi.astype(F32)).astype(BF16)
    return _dot(hi, w_bf16) + _dot(lo, w_bf16)


SSD_CPS = 2


def _ssd2_body(nb, ncc, xm_ref, xp_ref, xn_ref, dt_ref, z_ref, cw_ref, cb_ref, dtb_ref, alog_ref, dsk_ref,
               ng_ref, tri_ref, e_ref, o_ref, yloc_ref, s_ref, c_ref, ea_ref, cd_ref, hbe_ref, hf_ref, hb_ref):
    g = pl.program_id(1)
    t = pl.program_id(2)
    q = SSD_CHUNK
    gw = SSD_GROUP_W
    ns = nb * SSD_CPS

    @pl.when(t < nb)
    def _local():
        first = t <= 1
        last = (t == 0) | (t == nb - 1)
        prev = jnp.where(first, 0.0, xp_ref[...])
        nxt = jnp.where(last, 0.0, xn_ref[...])
        xfull = jnp.concatenate([prev, xm_ref[...], nxt], axis=0)
        rows_blk = SSD_CPS * q
        conv = cb_ref[...]
        for k in range(SSD_CONV_W):
            shift = (SSD_CONV_W // 2 - k) % xfull.shape[0]
            xk = xfull if shift == 0 else pltpu.roll(xfull, shift, 0)
            conv = conv + xk[HALO:HALO + rows_blk, :] * cw_ref[k:k + 1, :]
        act = _silu(conv)
        xs = act[:, :gw]
        bm = act[:, gw:gw + SSD_STATE]
        cm = act[:, gw + SSD_STATE:]

        lane = lax.broadcasted_iota(jnp.int32, (1, LANES), 1)
        is_fwd = lane < SSD_HPG
        dsel = pltpu.roll(dt_ref[...], (LANES - SSD_HPG * g) % LANES, 1)
        dtv = _softplus(dsel + dtb_ref[pl.ds(g, 1), :])
        head_lane = is_fwd | ((lane >= SSD_HEADS) & (lane < SSD_HEADS + SSD_HPG))
        a_row = jnp.where(head_lane, -jnp.exp(alog_ref[pl.ds(g, 1), :]) * LOG2E, 0.0)
        adt = dtv * a_row
        tri = tri_ref[...]
        tri16 = tri.astype(BF16)
        allowed = (tri[:q] > 0.0, tri[q:] > 0.0)
        emat = e_ref[...]
        lane_q = lax.broadcasted_iota(jnp.int32, (q, LANES), 1)
        lo_half = lane_q < SSD_HEAD_DIM
        prep = []
        for cc in range(SSD_CPS):
            rows = slice(cc * q, (cc + 1) * q)
            cums = _split_dot_rhs(tri16, adt[rows])
            acum = jnp.where(is_fwd, cums[:q], cums[q:])
            end = jnp.where(is_fwd, acum[q - 1:q, :], acum[0:1, :])
            dte = jnp.exp2(end - acum)
            ex = _dot(jnp.concatenate([dtv[rows], dte], axis=0).astype(BF16), emat)
            cdx = _split2_dot(jnp.broadcast_to(jnp.exp2(end), (16, LANES)), emat)
            c16 = cm[rows].astype(BF16)
            b_c = bm[rows]
            cbm = _dot_nt(c16, b_c.astype(BF16))
            prep.append((acum, acum.T, ex, cdx, c16, b_c, cbm))
        for cc in range(SSD_CPS):
            rows = slice(cc * q, (cc + 1) * q)
            acum, acum_t, ex, cdx, c16, b_c, cbm = prep[cc]
            xs_c = xs[rows]
            xdt2 = jnp.concatenate([xs_c, xs_c], axis=1) * ex[:q]
            ys = []
            for m in range(SSD_HPG // 2):
                mms, xhs = [], []
                for d in range(2):
                    xpair = xdt2[:, gw * d + LANES * m:gw * d + LANES * (m + 1)]
                    for half in range(2):
                        ln = SSD_HEADS * d + 2 * m + half
                        seg = acum[:, ln:ln + 1] - acum_t[ln:ln + 1, :]
                        dec = jnp.exp2(jnp.where(allowed[d], seg, NEG_INF))
                        mms.append((cbm * dec).astype(BF16))
                        xhs.append(jnp.where(lo_half if half == 0 else ~lo_half, xpair, 0.0).astype(BF16))
                ys.append(_dot(jnp.concatenate(mms, axis=1), jnp.concatenate(xhs, axis=0)))
            y_loc = jnp.concatenate(ys, axis=1) + dsk_ref[...] * xs_c
            sts = _dot(b_c.T.astype(BF16), (xdt2 * ex[q:2 * q]).astype(BF16))
            pos = SSD_CPS * t + cc
            r0 = pl.multiple_of(pos * q, q)
            yloc_ref[pl.ds(r0, q), :] = y_loc
            s_ref[pl.ds(r0, q), :] = sts.astype(BF16)
            c_ref[pl.ds(r0, q), :] = c16
            ea_ref[pl.ds(r0, q), :] = jnp.exp2(acum).astype(BF16)
            cd_ref[pl.ds(pl.multiple_of(pos * 8, 8), 8), :] = cdx[0:8]

    @pl.when(t == nb)
    def _backward_states():
        hb_ref[...] = jnp.zeros_like(hb_ref)
        hf_ref[...] = jnp.zeros_like(hf_ref)

        def body(i, carry):
            pos = jnp.where(i < ncc, ncc - 1 - i, ns - 1 + ncc - i)
            r0 = pl.multiple_of(pos * q, q)
            hb = hb_ref[...]
            hbe_ref[pl.ds(r0, q), :] = hb.astype(BF16)
            cd = cd_ref[pl.ds(pl.multiple_of(pos * 8, 8), 8), :][0:1, gw:]
            hb_ref[...] = hb * cd + s_ref[pl.ds(r0, q), gw:].astype(F32)
            return carry

        lax.fori_loop(0, ns, body, 0)

    @pl.when(t >= nb)
    def _emit():
        emat = e_ref[...]
        starts = [pl.multiple_of((SSD_CPS * (t - nb) + cc) * q, q) for cc in range(SSD_CPS)]
        ea_x = [_dot(ea_ref[pl.ds(r0, q), :], emat) for r0 in starts]
        gate = [_silu(z_ref[cc * q:(cc + 1) * q, :].astype(F32)) for cc in range(SSD_CPS)]
        hf = hf_ref[...]
        for cc in range(SSD_CPS):
            rows = slice(cc * q, (cc + 1) * q)
            pos = SSD_CPS * (t - nb) + cc
            r0 = starts[cc]
            hcat = jnp.concatenate([hf.astype(BF16), hbe_ref[pl.ds(r0, q), :]], axis=1)
            yo = _dot(c_ref[pl.ds(r0, q), :], hcat) * ea_x[cc]
            y = yloc_ref[pl.ds(r0, q), :] + yo[:, :gw] + yo[:, gw:]
            cd = cd_ref[pl.ds(pl.multiple_of(pos * 8, 8), 8), :][0:1, :gw]
            hf = hf * cd + s_ref[pl.ds(r0, q), :gw].astype(F32)
            y = y * gate[cc]
            y = y * lax.rsqrt(jnp.mean(y * y, axis=-1, keepdims=True) + NORM_EPS) * ng_ref[...]
            o_ref[rows, :] = y.astype(o_ref.dtype)
        hf_ref[...] = hf


def _ssd2(xbc, dt, z, j, conv_w, conv_b, dtb_tab, alog_tab, dsk_x, ng, tri, emat, n_batch, seq, n_ctx):
    rows = xbc.shape[0]
    q = SSD_CHUNK
    blk = SSD_CPS * q
    assert n_ctx == blk and seq % blk == 0
    ncc = n_ctx // q
    nlb = seq // blk
    nb = 1 + nlb
    ns = nb * SSD_CPS
    per = blk // HALO

    def rb(b, pb):
        return jnp.where(pb == 0, n_batch * nlb + b, b * nlb + pb - 1)

    def rb_in(b, t):
        return rb(b, jnp.minimum(t, nb - 1))

    def rb_out(b, t):
        return rb(b, jnp.maximum(t - nb, 0))

    last_halo = rows // HALO - 1
    gw = SSD_GROUP_W
    return pl.pallas_call(
        functools.partial(_ssd2_body, nb, ncc),
        grid=(n_batch, SSD_GROUPS, 2 * nb),
        in_specs=[pl.BlockSpec((blk, SSD_XBC_W), lambda b, g, t: (rb_in(b, t), g)),
                  pl.BlockSpec((HALO, SSD_XBC_W),
                               lambda b, g, t: (jnp.maximum(rb_in(b, t) * per - 1, 0), g)),
                  pl.BlockSpec((HALO, SSD_XBC_W),
                               lambda b, g, t: (jnp.minimum(rb_in(b, t) * per + per, last_halo), g)),
                  pl.BlockSpec((blk, LANES), lambda b, g, t: (rb_in(b, t), 0)),
                  pl.BlockSpec((blk, gw), lambda b, g, t: (rb_out(b, t), g)),
                  pl.BlockSpec((None, SSD_CONV_W, SSD_XBC_W), lambda b, g, t: (j, 0, g)),
                  pl.BlockSpec((None, 1, SSD_XBC_W), lambda b, g, t: (j, 0, g)),
                  pl.BlockSpec((None, 8, LANES), lambda b, g, t: (j, 0, 0)),
                  pl.BlockSpec((None, 8, LANES), lambda b, g, t: (j, 0, 0)),
                  pl.BlockSpec((None, 1, gw), lambda b, g, t: (j, 0, g)),
                  pl.BlockSpec((None, 1, gw), lambda b, g, t: (j, 0, g)),
                  pl.BlockSpec((2 * q, q), lambda b, g, t: (0, 0)),
                  pl.BlockSpec((LANES, 2 * gw), lambda b, g, t: (0, 0))],
        out_specs=pl.BlockSpec((blk, gw), lambda b, g, t: (rb_out(b, t), g)),
        out_shape=jax.ShapeDtypeStruct((rows, SSD_INNER), BF16),
        scratch_shapes=[pltpu.VMEM((ns * q, gw), F32),
                        pltpu.VMEM((ns * q, 2 * gw), BF16),
                        pltpu.VMEM((ns * q, SSD_STATE), BF16),
                        pltpu.VMEM((ns * q, LANES), BF16),
                        pltpu.VMEM((ns * 8, 2 * gw), F32),
                        pltpu.VMEM((ns * q, gw), BF16),
                        pltpu.VMEM((SSD_STATE, gw), F32),
                        pltpu.VMEM((SSD_STATE, gw), F32)],
        compiler_params=_cparams("arbitrary", "arbitrary", "arbitrary"),
        name="ssd_bidir",
    )(xbc, xbc, xbc, dt, z, conv_w, conv_b, dtb_tab, alog_tab, dsk_x, ng, tri, emat)


def _stack_heads(qc):
    lane = lax.broadcasted_iota(jnp.int32, qc.shape, 1)
    lo = lane < HEAD_DIM
    zero = jnp.zeros_like(qc)
    return jnp.concatenate([jnp.where(lo, qc, zero), jnp.where(lo, zero, qc)], axis=0)


def _unstack_heads(o, n):
    lane = lax.broadcasted_iota(jnp.int32, (n, LANES), 1)
    return jnp.where(lane < HEAD_DIM, o[:n], o[n:])


def _softmax_pv(scores, values, sink=None):
    m = None
    for sc in scores:
        mx = jnp.max(sc, axis=-1, keepdims=True)
        m = mx if m is None else jnp.maximum(m, mx)
    if sink is not None:
        m = jnp.maximum(m, sink)
    den = None if sink is None else jnp.exp2(sink - m)
    acc = None
    for sc, v in zip(scores, values):
        pr = jnp.exp2(sc - m)
        sm = jnp.sum(pr, axis=-1, keepdims=True)
        den = sm if den is None else den + sm
        t = _dot(pr.astype(BF16), v)
        acc = t if acc is None else acc + t
    return acc * (1.0 / den)


def _win_body(seq, q_ref, k_ref, v_ref, kc_ref, vc_ref, sink_ref, o_ref):
    n = pl.program_id(1)
    wb = WIN_BLOCK
    start = pl.multiple_of(jnp.clip((n - 1) * wb, 0, seq - 3 * wb), wb)
    kb = k_ref[pl.ds(start, 3 * wb), :]
    vb = v_ref[pl.ds(start, 3 * wb), :]
    r = lax.broadcasted_iota(jnp.int32, (wb, 3 * wb), 0)
    c = lax.broadcasted_iota(jnp.int32, (wb, 3 * wb), 1)
    dist = (c + start) - (r + n * wb)
    pen = jnp.where(jnp.abs(dist) <= WIN_RADIUS, 0.0, NEG_INF)
    pen = jnp.concatenate([pen, pen], axis=0)
    half = WIN_GQA * wb
    for m in range(WIN_GQA):
        sl = slice(LANES * m, LANES * (m + 1))
        qs = _stack_heads(q_ref[:, sl])
        s_loc = _dot_nt(qs, kb) + pen
        s_ctx = _dot_nt(qs, kc_ref[...])
        sink = jnp.concatenate([sink_ref[m * wb:(m + 1) * wb, 0:1],
                                sink_ref[half + m * wb:half + (m + 1) * wb, 0:1]], axis=0)
        o = _softmax_pv([s_loc, s_ctx], [vb, vc_ref[...]], sink)
        o_ref[:, sl] = _unstack_heads(o, wb).astype(o_ref.dtype)


def _window_attention(qw, kw, vw, sink_x, j, n_batch, seq, n_ctx):
    wb = WIN_BLOCK
    nb = seq // wb
    assert nb >= 3
    ctx_blk0 = n_batch * seq // n_ctx
    q_spec = pl.BlockSpec((wb, 512), lambda b, n: (b * nb + n, 0))
    seq_spec = pl.BlockSpec((seq, LANES), lambda b, n: (b, 0))
    cspec = pl.BlockSpec((n_ctx, LANES), lambda b, n: (ctx_blk0 + b, 0))
    return pl.pallas_call(
        functools.partial(_win_body, seq),
        grid=(n_batch, nb),
        in_specs=[q_spec, seq_spec, seq_spec, cspec, cspec, _layer(sink_x, j)],
        out_specs=pl.BlockSpec((wb, 512), lambda b, n: (b * nb + n, 0)),
        out_shape=jax.ShapeDtypeStruct((n_batch * seq, 512), BF16),
        compiler_params=_cparams("parallel", "arbitrary"),
        name="window_attention",
    )(qw, kw, vw, kw, vw, sink_x)


NA_ROWS_PER_STEP = 4


def _na_body(n_rows, q_ref, k_ref, v_ref, kc_ref, vc_ref, bias_ref, o_ref):
    kr = NA_ROWS
    for j in range(NA_ROWS_PER_STEP):
        r = pl.program_id(1) * NA_ROWS_PER_STEP + j
        first_row = jnp.clip(r - kr // 2, 0, n_rows - kr)
        start = pl.multiple_of(first_row * GRID_W, GRID_W)
        ro0 = first_row - r + NA_ROWS - 1
        rows = slice(j * GRID_W, (j + 1) * GRID_W)
        scores = []
        for m in range(NA_HEADS // 2):
            sl = slice(LANES * m, LANES * (m + 1))
            qs = _stack_heads(q_ref[rows, sl])
            kw = k_ref[pl.ds(start, kr * GRID_W), sl]
            s_loc = _dot_nt(qs, kw) + bias_ref[ro0, 2 * GRID_W * m:2 * GRID_W * (m + 1), :]
            scores.append((s_loc, _dot_nt(qs, kc_ref[:, sl])))
        for m in range(NA_HEADS // 2):
            sl = slice(LANES * m, LANES * (m + 1))
            vw = v_ref[pl.ds(start, kr * GRID_W), sl]
            o = _softmax_pv(list(scores[m]), [vw, vc_ref[:, sl]])
            o_ref[rows, sl] = _unstack_heads(o, GRID_W).astype(o_ref.dtype)


def _na_attention(qn, kn, vn, bias_tab, j, n_batch, seq, n_ctx):
    n_rows = seq // GRID_W
    assert n_rows >= NA_ROWS and n_rows % NA_ROWS_PER_STEP == 0
    n_steps = n_rows // NA_ROWS_PER_STEP
    qrows = NA_ROWS_PER_STEP * GRID_W
    ctx_blk0 = n_batch * seq // n_ctx
    return pl.pallas_call(
        functools.partial(_na_body, n_rows),
        grid=(n_batch, n_steps),
        in_specs=[pl.BlockSpec((qrows, 512), lambda b, r: (b * n_steps + r, 0)),
                  pl.BlockSpec((seq, 512), lambda b, r: (b, 0)),
                  pl.BlockSpec((seq, 512), lambda b, r: (b, 0)),
                  pl.BlockSpec((n_ctx, 512), lambda b, r: (ctx_blk0 + b, 0)),
                  pl.BlockSpec((n_ctx, 512), lambda b, r: (ctx_blk0 + b, 0)),
                  _layer(bias_tab, j)],
        out_specs=pl.BlockSpec((qrows, 512), lambda b, r: (b * n_steps + r, 0)),
        out_shape=jax.ShapeDtypeStruct((n_batch * seq, 512), BF16),
        compiler_params=_cparams("parallel", "arbitrary"),
        name="neighbourhood_attention",
    )(qn, kn, vn, kn, vn, bias_tab)


def _ctx_attn_body(qw_ref, kw_ref, vw_ref, qn_ref, kn_ref, vn_ref, sink_ref, ow_ref, on_ref):
    n = qw_ref.shape[0]
    qw = qw_ref[...]
    lane = lax.broadcasted_iota(jnp.int32, (n, LANES), 1)
    lo = lane < HEAD_DIM
    for m in range(WIN_GQA):
        qs = _stack_heads(qw[:, LANES * m:LANES * (m + 1)])
        r0, r1 = m * WIN_BLOCK, (WIN_GQA + m) * WIN_BLOCK
        sink = jnp.concatenate([jnp.broadcast_to(sink_ref[r0:r0 + 1, 0:1], (n, 1)),
                                jnp.broadcast_to(sink_ref[r1:r1 + 1, 0:1], (n, 1))], axis=0)
        o = _softmax_pv([_dot_nt(qs, kw_ref[...])], [vw_ref[...]], sink)
        ow_ref[:, LANES * m:LANES * (m + 1)] = jnp.where(lo, o[:n], o[n:]).astype(ow_ref.dtype)
    qn = qn_ref[...]
    for m in range(NA_HEADS // 2):
        sl = slice(LANES * m, LANES * (m + 1))
        qs = _stack_heads(qn[:, sl])
        o = _softmax_pv([_dot_nt(qs, kn_ref[:, sl])], [vn_ref[:, sl]])
        on_ref[:, sl] = _unstack_heads(o, n).astype(on_ref.dtype)


def _ctx_attention(qw, kw, vw, qn, kn, vn, sink_x, j, n_batch, seq, n_ctx):
    blk0 = n_batch * seq // n_ctx
    wide = pl.BlockSpec((n_ctx, 512), lambda b: (blk0 + b, 0))
    narrow = pl.BlockSpec((n_ctx, LANES), lambda b: (blk0 + b, 0))
    out = pl.BlockSpec((n_ctx, 512), lambda b: (b, 0))
    return pl.pallas_call(
        _ctx_attn_body,
        grid=(n_batch,),
        in_specs=[wide, narrow, narrow, wide, wide, wide,
                  _layer(sink_x, j)],
        out_specs=[out, out],
        out_shape=[jax.ShapeDtypeStruct((n_batch * n_ctx, 512), BF16)] * 2,
        compiler_params=_cparams("parallel"),
        name="context_attention",
    )(qw, kw, vw, qn, kn, vn, sink_x)


def _take(w, runs, axis):
    parts = []
    for run in runs:
        if run[0] is None:
            shape = list(w.shape)
            shape[axis] = run[1]
            parts.append(jnp.zeros(shape, w.dtype))
        else:
            parts.append(lax.slice_in_dim(w, run[0], run[1], axis=axis))
    return jnp.concatenate(parts, axis=axis)


def _xbc_runs():
    gn = SSD_GROUPS * SSD_STATE
    runs = []
    for g in range(SSD_GROUPS):
        runs.append((g * SSD_GROUP_W, (g + 1) * SSD_GROUP_W))
        runs.append((SSD_INNER + g * SSD_STATE, SSD_INNER + (g + 1) * SSD_STATE))
        runs.append((SSD_INNER + gn + g * SSD_STATE, SSD_INNER + gn + (g + 1) * SSD_STATE))
    return runs


def _head_rows(t):
    n = t.shape[0]
    flat = jnp.pad(t.astype(F32).reshape(n, 2 * SSD_HEADS), ((0, 0), (0, LANES - 2 * SSD_HEADS)))
    rows = [jnp.roll(flat, -SSD_HPG * g, axis=1) for g in range(SSD_GROUPS)]
    rows += [jnp.zeros_like(flat)] * (8 - SSD_GROUPS)
    return jnp.stack(rows, axis=1)


def _win_head_runs():
    runs = []
    for m in range(WIN_GQA):
        for kvh in range(WIN_KV_HEADS):
            h = kvh * WIN_GQA + m
            runs.append((h * HEAD_DIM, (h + 1) * HEAD_DIM))
    return runs


def _rope_tables(seq):
    pos = np.arange(seq)
    row = (pos // GRID_W).astype(np.float32)
    col = (pos % GRID_W).astype(np.float32)
    n_freq = HEAD_DIM // 4
    inv = (np.float32(ROPE_THETA) ** (-np.arange(n_freq, dtype=np.float32) / n_freq)).astype(np.float32)
    ar = (row[:, None] * inv).astype(np.float32)
    ac = (col[:, None] * inv).astype(np.float32)
    cos_h = np.concatenate([np.cos(ar), np.cos(ar), np.cos(ac), np.cos(ac)], axis=1)
    sin_h = np.concatenate([-np.sin(ar), np.sin(ar), -np.sin(ac), np.sin(ac)], axis=1)
    ident_c = np.ones((ROW_TILE, HEAD_DIM), np.float32)
    ident_s = np.zeros((ROW_TILE, HEAD_DIM), np.float32)
    cos_t = np.concatenate([cos_h, ident_c], axis=0)
    sin_t = np.concatenate([sin_h, ident_s], axis=0)
    return (jnp.asarray(np.tile(cos_t, (1, 2)), F32), jnp.asarray(np.tile(sin_t, (1, 2)), F32))


def _na_bias_table(rpb):
    n, h, nr, _ = rpb.shape
    kr = NA_ROWS
    cols = np.arange(GRID_W)
    col_start = np.clip(cols - NA_COLS // 2, 0, GRID_W - NA_COLS)
    col_ok = (cols[None] >= col_start[:, None]) & (cols[None] < col_start[:, None] + NA_COLS)
    r32 = rpb.astype(F32) * LOG2E
    v = jnp.concatenate([r32[..., NA_COLS - 1:], jnp.zeros((n, h, nr, LANES - (2 * NA_COLS - 1)), F32),
                         r32[..., :NA_COLS - 1]], axis=-1)
    flat = jnp.tile(v, (1, 1, 1, GRID_W))[..., :GRID_W * (LANES - 1)]
    toep = flat.reshape(n, h, nr, GRID_W, LANES - 1)[..., :GRID_W]
    toep = jnp.where(jnp.asarray(col_ok), toep, NEG_INF)
    toep = jnp.transpose(toep, (0, 1, 3, 2, 4))
    tabs = [toep[:, :, :, ro0:ro0 + kr].reshape(n, h * GRID_W, kr * GRID_W) for ro0 in range(NA_ROWS)]
    return jnp.stack(tabs, axis=1)


def kernel(x, c, ctx, c_ctx, w_mod, b_mod, norm_mix_g, norm_ff_g, w_ff1, w_ff2, w_in_even, conv_w, conv_b,
           dt_bias, a_log, d_skip, ssd_norm_g, w_out_even, w_in_odd, q_norm_win, k_norm_win, sink_win,
           q_norm_na, k_norm_na, rpb_na, w_out_odd):
    n_batch, seq, d = x.shape
    n_ctx = ctx.shape[1]
    n_lat = n_batch * seq
    tm = ROW_TILE
    tiles_per_batch = seq // tm
    n_lat_tiles = n_lat // tm
    n_ctx_tiles = (n_batch * n_ctx) // tm
    n2 = seq // DFT_N1

    h = jnp.concatenate([x.reshape(n_lat, d), ctx.reshape(n_batch * n_ctx, d)], axis=0)

    cvec = jnp.concatenate([c, c_ctx[None], jnp.zeros((8 - n_batch - 1, d), F32)], axis=0)
    mod_all = _modulation(cvec, w_mod, b_mod)
    mod_all = mod_all.reshape(-1, 8, 6, 1, d)[:, :n_batch + 1].transpose(0, 2, 1, 3, 4)

    m_a, twc, tws, m_b, m_c, chan = _dft_tables(seq, n_ctx)
    xbc_runs = _xbc_runs()
    s3 = FNET_W + SSD_INNER + SSD_CONV_DIM
    w_four = _weight_product(w_in_even, FNET_W, chan)
    tri = jnp.asarray(np.concatenate([np.tril(np.ones((SSD_CHUNK, SSD_CHUNK))),
                                      np.triu(np.ones((SSD_CHUNK, SSD_CHUNK)))], axis=0), F32)
    emat = np.zeros((LANES, 2 * SSD_GROUP_W), np.float32)
    for dr in range(2):
        for r in range(SSD_HPG):
            c0 = dr * SSD_GROUP_W + r * SSD_HEAD_DIM
            emat[SSD_HEADS * dr + r, c0:c0 + SSD_HEAD_DIM] = 1.0
    emat = jnp.asarray(emat, BF16)

    cos_t, sin_t = _rope_tables(seq)
    gmat_slab = jnp.asarray(np.kron(np.eye(MXU_DIM // HEAD_DIM), np.full((HEAD_DIM, HEAD_DIM), 1.0 / HEAD_DIM)),
                          BF16)
    gmat128 = gmat_slab[:LANES, :LANES]
    win_runs = _win_head_runs()
    wq = WIN_Q_HEADS * HEAD_DIM

    gains_mix = norm_mix_g.astype(F32)[:, None, :]
    gains_ff = norm_ff_g.astype(F32)[:, None, :]
    w_ff1_b, w_ff2_b = w_ff1.astype(BF16), w_ff2.astype(BF16)
    w_even_b = w_in_even.astype(BF16)
    w_dt_b = jnp.pad(w_even_b[:, :, s3:], ((0, 0), (0, 0), (0, LANES - 2 * SSD_HEADS)))
    w_out_even_b = w_out_even.astype(BF16)
    conv_w_p = _take(conv_w.astype(F32), xbc_runs, 2)
    conv_b_p = _take(conv_b.astype(F32)[:, None, :], xbc_runs, 2)
    dtb_rows, alog_rows = _head_rows(dt_bias), _head_rows(a_log)
    dsk_x = jnp.repeat(d_skip.astype(F32), SSD_HEAD_DIM, axis=1)[:, None, :]
    ssd_ng = ssd_norm_g.astype(F32)[:, None, :]
    w_odd_b = w_in_odd.astype(BF16)
    wq_perm = _take(w_odd_b[:, :, :wq], win_runs, 2)
    w_out_odd_b = w_out_odd.astype(BF16)
    wo_win_perm = _take(w_out_odd_b[:, :wq], win_runs, 1)
    qscale = HEAD_DIM ** -0.5 * LOG2E
    rep = lambda g, k: jnp.tile(g.astype(F32), (1, k))
    head_gains = jnp.stack([rep(q_norm_win, WIN_Q_HEADS) * qscale,
                            jnp.pad(rep(k_norm_win, WIN_KV_HEADS), ((0, 0), (0, wq - WIN_KV_HEADS * HEAD_DIM))),
                            rep(q_norm_na, NA_HEADS) * qscale, rep(k_norm_na, NA_HEADS)], axis=1)
    head_gains = jnp.pad(head_gains, ((0, 0), (0, 4), (0, 0)))
    sink_x = jnp.broadcast_to(jnp.repeat(sink_win.astype(F32) * LOG2E, WIN_BLOCK, axis=1)[:, :, None],
                              (sink_win.shape[0], WIN_Q_HEADS * WIN_BLOCK, LANES))
    bias_tab = _na_bias_table(rpb_na)

    for i in range(DEPTH):
        need_ctx = i < DEPTH - 1
        j = i // 2
        if i % 2 == 0:
            fr, fi, z, xbc, dtr = _inproj_even(h, i, j, gains_mix, mod_all, w_four, w_even_b, w_dt_b,
                                               n_batch, tiles_per_batch)

            br, bi = _dft_stage_a(fr.reshape(-1, n2, FNET_W), fi.reshape(-1, n2, FNET_W),
                                  m_a, twc, tws, n_batch, n2)
            f_lat = _dft_stage_b(br.reshape(n_batch, seq, FNET_W), bi.reshape(n_batch, seq, FNET_W),
                                 m_b, n_batch, n2).reshape(n_lat, FNET_W)
            f_ctx = _ctx_dft(fr, fi, m_c, n_batch, n_ctx, n_lat // n_ctx)

            y_ssd = _ssd2(xbc, dtr, z, j, conv_w_p, conv_b_p, dtb_rows, alog_rows, dsk_x, ssd_ng, tri, emat,
                          n_batch, seq, n_ctx)

            h = _outproj(h, i, mod_all, f_lat, y_ssd, w_out_even_b, w_out_even_b, j, FNET_W,
                         n_batch, tiles_per_batch, 0, 0, 0, n_lat_tiles)
            h = _outproj(h, i, mod_all, f_ctx, y_ssd, w_out_even_b, w_out_even_b, j, FNET_W,
                         n_batch, tiles_per_batch, n_lat_tiles, 0, n_lat_tiles, n_ctx_tiles)
        else:
            qw, kw, vw, qn, kn, vn = _inproj_odd(h, i, j, gains_mix, mod_all, wq_perm, w_odd_b, cos_t, sin_t,
                                                 gmat_slab, gmat128, head_gains, n_batch, tiles_per_batch)
            o_win = _window_attention(qw, kw, vw, sink_x, j, n_batch, seq, n_ctx)
            o_na = _na_attention(qn, kn, vn, bias_tab, j, n_batch, seq, n_ctx)
            if need_ctx:
                oc_win, oc_na = _ctx_attention(qw, kw, vw, qn, kn, vn, sink_x, j, n_batch, seq, n_ctx)
            h = _outproj(h, i, mod_all, o_win, o_na, wo_win_perm, w_out_odd_b, j, wq,
                         n_batch, tiles_per_batch, 0, 0, 0, n_lat_tiles)
            if need_ctx:
                h = _outproj(h, i, mod_all, oc_win, oc_na, wo_win_perm, w_out_odd_b, j, wq,
                             n_batch, tiles_per_batch, n_lat_tiles, 0, 0, n_ctx_tiles)
        n_tiles = n_lat_tiles + n_ctx_tiles if need_ctx else n_lat_tiles
        h = _ffn(h, i, gains_ff, mod_all, w_ff1_b, w_ff2_b, n_batch, tiles_per_batch, n_tiles)
    return h.reshape(n_batch, seq, d)
```

```python
import functools
import math

import numpy as np
import jax
import jax.numpy as jnp
from jax import lax
from jax.experimental import pallas as pl
from jax.experimental.pallas import tpu as pltpu

F32 = jnp.float32
BF16 = jnp.bfloat16
HIGHEST = lax.Precision.HIGHEST

D_MODEL = 1024
DEPTH = 4
GRID_W = 64
D_FF = 4 * D_MODEL
NORM_EPS = 1e-6
NEG_INF = -1e30

FNET_GROUPS = 8
FNET_GROUP_W = 64
FNET_W = FNET_GROUPS * FNET_GROUP_W
SSD_HEAD_DIM = 64
SSD_HEADS = 24
SSD_GROUPS = 4
SSD_HPG = SSD_HEADS // SSD_GROUPS
SSD_STATE = 128
SSD_INNER = SSD_HEADS * SSD_HEAD_DIM
SSD_GROUP_W = SSD_HPG * SSD_HEAD_DIM
SSD_XBC_W = SSD_GROUP_W + 2 * SSD_STATE
SSD_CONV_DIM = SSD_INNER + 2 * SSD_GROUPS * SSD_STATE
SSD_CONV_W = 5
SSD_CHUNK = 128
HEAD_DIM = 64
WIN_Q_HEADS = 8
WIN_KV_HEADS = 2
WIN_GQA = WIN_Q_HEADS // WIN_KV_HEADS
WIN_BLOCK = 128
WIN_RADIUS = 128
NA_HEADS = 8
NA_ROWS = 8
NA_COLS = 16
ROPE_THETA = 10000.0
LOG2E = 1.4426950408889634

LANES = 128
MXU_DIM = 256
HALO = 8
ROW_TILE = 512
FF_TILE = 4096
DFT_N1 = 128
VMEM_LIMIT = 56 * 1024 * 1024


def _cparams(*sem):
    return pltpu.CompilerParams(dimension_semantics=sem, vmem_limit_bytes=VMEM_LIMIT)


def _silu(x):
    return x / (1.0 + jnp.exp2(x * -LOG2E))


def _softplus(x):
    return jnp.maximum(x, 0.0) + jnp.log(1.0 + jnp.exp(-jnp.abs(x)))


def _norm_mod(h, g, shift, scale):
    ms = jnp.mean(h * h, axis=-1, keepdims=True)
    y = h * lax.rsqrt(ms + NORM_EPS) * g
    return y * (1.0 + scale) + shift


def _dot(a, b):
    return jnp.dot(a, b, preferred_element_type=F32)


def _dot_nt(a, b):
    return lax.dot_general(a, b, (((1,), (1,)), ((), ())), preferred_element_type=F32)


def _mod_body(c_ref, w_ref, b_ref, o_ref):
    s = _silu(c_ref[...])
    o_ref[...] = jnp.dot(s, w_ref[...], precision=HIGHEST, preferred_element_type=F32) + b_ref[...]


def _modulation(cvec, w_mod, b_mod):
    depth, d, n = w_mod.shape
    tn = 1536
    return pl.pallas_call(
        _mod_body,
        grid=(depth, n // tn),
        in_specs=[pl.BlockSpec((8, d), lambda i, j: (0, 0)),
                  pl.BlockSpec((None, d, tn), lambda i, j: (i, 0, j)),
                  pl.BlockSpec((None, 1, tn), lambda i, j: (i, 0, j))],
        out_specs=pl.BlockSpec((None, 8, tn), lambda i, j: (i, 0, j)),
        out_shape=jax.ShapeDtypeStruct((depth, 8, n), F32),
        compiler_params=_cparams("parallel", "parallel"),
        name="modulation",
    )(cvec, w_mod, b_mod.reshape(depth, 1, n))


def _wprod_body(a_ref, b_ref, o_ref):
    o_ref[...] = jnp.dot(a_ref[...], b_ref[...], precision=HIGHEST,
                         preferred_element_type=F32).astype(o_ref.dtype)


def _weight_product(a, k, b):
    n, m, _ = a.shape
    p = b.shape[1]
    return pl.pallas_call(
        _wprod_body,
        grid=(n,),
        in_specs=[pl.BlockSpec((None, m, k), lambda i: (i, 0, 0)),
                  pl.BlockSpec((k, p), lambda i: (0, 0))],
        out_specs=pl.BlockSpec((None, m, p), lambda i: (i, 0, 0)),
        out_shape=jax.ShapeDtypeStruct((n, m, p), BF16),
        compiler_params=_cparams("parallel"),
        name="fold_channel_dft",
    )(a, b)


def _resident(shape):
    return pl.BlockSpec(shape, lambda *_: (0,) * len(shape), pipeline_mode=pl.Buffered(1))


def _layer(arr, j):
    nd = arr.ndim
    return pl.BlockSpec((None,) + arr.shape[1:], lambda *_: (j,) + (0,) * (nd - 1),
                        pipeline_mode=pl.Buffered(1))


def _mod_spec(layer, which, tiles_per_batch, n_batch, tile_off):
    def imap(m):
        return (layer, which, jnp.minimum((m + tile_off) // tiles_per_batch, n_batch), 0, 0)
    return pl.BlockSpec((None, None, None, 1, D_MODEL), imap)


def _inproj_even_body(h_ref, g_ref, sh_ref, sc_ref, wf_ref, w_ref, wdt_ref,
                      fr_ref, fi_ref, z_ref, xbc_ref, dt_ref):
    u = _norm_mod(h_ref[...], g_ref[...], sh_ref[...], sc_ref[...]).astype(BF16)
    fr_ref[...] = _dot(u, wf_ref[:, :FNET_W])
    fi_ref[...] = _dot(u, wf_ref[:, FNET_W:])
    z_ref[...] = _dot(u, w_ref[:, FNET_W:FNET_W + SSD_INNER]).astype(z_ref.dtype)
    c0 = FNET_W + SSD_INNER
    xbc = _dot(u, w_ref[:, c0:c0 + SSD_CONV_DIM])
    gn = SSD_GROUPS * SSD_STATE
    for g in range(SSD_GROUPS):
        o = g * SSD_XBC_W
        b0 = SSD_INNER + g * SSD_STATE
        xbc_ref[:, o:o + SSD_GROUP_W] = xbc[:, g * SSD_GROUP_W:(g + 1) * SSD_GROUP_W]
        xbc_ref[:, o + SSD_GROUP_W:o + SSD_GROUP_W + SSD_STATE] = xbc[:, b0:b0 + SSD_STATE]
        xbc_ref[:, o + SSD_GROUP_W + SSD_STATE:o + SSD_XBC_W] = xbc[:, b0 + gn:b0 + gn + SSD_STATE]
    dt_ref[...] = _dot(u, wdt_ref[...])


def _inproj_even(h, layer, j, gains, mod_all, w_four, w_main, w_dt, n_batch, tiles_per_batch):
    rows = h.shape[0]
    widths = (FNET_W, FNET_W, SSD_INNER, SSD_CONV_DIM, LANES)
    dtypes = (F32, F32, BF16, F32, F32)
    tm = ROW_TILE
    return pl.pallas_call(
        _inproj_even_body,
        grid=(rows // tm,),
        in_specs=[pl.BlockSpec((tm, D_MODEL), lambda m: (m, 0)),
                  _layer(gains, layer),
                  _mod_spec(layer, 0, tiles_per_batch, n_batch, 0),
                  _mod_spec(layer, 1, tiles_per_batch, n_batch, 0),
                  _layer(w_four, j), _layer(w_main, j), _layer(w_dt, j)],
        out_specs=[pl.BlockSpec((tm, wd), lambda m: (m, 0)) for wd in widths],
        out_shape=[jax.ShapeDtypeStruct((rows, wd), dt) for wd, dt in zip(widths, dtypes)],
        compiler_params=_cparams("parallel"),
        name="inproj_even",
    )(h, gains, mod_all, mod_all, w_four, w_main, w_dt)


def _group_rms(x, gmat_ref, gain):
    sq = x * x
    hi = sq.astype(BF16)
    lo = (sq - hi.astype(F32)).astype(BF16)
    gw = gmat_ref.shape[0]
    parts = []
    for c0 in range(0, x.shape[1], gw):
        parts.append(_dot(hi[:, c0:c0 + gw], gmat_ref[...]) + _dot(lo[:, c0:c0 + gw], gmat_ref[...]))
    ms = parts[0] if len(parts) == 1 else jnp.concatenate(parts, axis=1)
    return x * lax.rsqrt(ms + NORM_EPS) * gain


def _rope(y, cos, sin_signed):
    w = y.shape[-1]
    reps = w // LANES
    lane = lax.broadcasted_iota(jnp.int32, y.shape, 1)
    first = (lane % 32) < 16
    partner = jnp.where(first, pltpu.roll(y, w - 16, 1), pltpu.roll(y, 16, 1))
    if reps > 1:
        cos = jnp.concatenate([cos] * reps, axis=1)
        sin_signed = jnp.concatenate([sin_signed] * reps, axis=1)
    return y * cos + partner * sin_signed


def _inproj_odd_body(h_ref, g_ref, sh_ref, sc_ref, wq_ref, w_ref, cos_ref, sin_ref, gslab_ref, g128_ref,
                     hg_ref, qw_ref, kw_ref, vw_ref, qn_ref, kn_ref, vn_ref):
    u = _norm_mod(h_ref[...], g_ref[...], sh_ref[...], sc_ref[...]).astype(BF16)
    cos = cos_ref[...]
    sin = sin_ref[...]
    wq = WIN_Q_HEADS * HEAD_DIM
    wk = WIN_KV_HEADS * HEAD_DIM
    nh = NA_HEADS * HEAD_DIM
    qw = _group_rms(_dot(u, wq_ref[...]), gslab_ref, hg_ref[0:1, :])
    qw_ref[...] = _rope(qw, cos, sin).astype(BF16)
    kw = _group_rms(_dot(u, w_ref[:, wq:wq + wk]), g128_ref, hg_ref[1:2, :wk])
    kw_ref[...] = _rope(kw, cos, sin).astype(BF16)
    vw_ref[...] = _dot(u, w_ref[:, wq + wk:wq + 2 * wk]).astype(BF16)
    c0 = wq + 2 * wk
    qn_ref[...] = _group_rms(_dot(u, w_ref[:, c0:c0 + nh]), gslab_ref, hg_ref[2:3, :]).astype(BF16)
    kn_ref[...] = _group_rms(_dot(u, w_ref[:, c0 + nh:c0 + 2 * nh]), gslab_ref, hg_ref[3:4, :]).astype(BF16)
    vn_ref[...] = _dot(u, w_ref[:, c0 + 2 * nh:c0 + 3 * nh]).astype(BF16)


def _inproj_odd(h, layer, j, gains, mod_all, wq_perm, w_main, cos_t, sin_t, gmat_slab, gmat128, head_gains,
                n_batch, tiles_per_batch):
    rows = h.shape[0]
    tm = ROW_TILE
    widths = (512, 128, 128, 512, 512, 512)
    n_lat_tiles = n_batch * tiles_per_batch

    def rope_map(m):
        return (jnp.where(m < n_lat_tiles, m % tiles_per_batch, tiles_per_batch), 0)

    return pl.pallas_call(
        _inproj_odd_body,
        grid=(rows // tm,),
        in_specs=[pl.BlockSpec((tm, D_MODEL), lambda m: (m, 0)),
                  _layer(gains, layer),
                  _mod_spec(layer, 0, tiles_per_batch, n_batch, 0),
                  _mod_spec(layer, 1, tiles_per_batch, n_batch, 0),
                  _layer(wq_perm, j), _layer(w_main, j),
                  pl.BlockSpec((tm, LANES), rope_map),
                  pl.BlockSpec((tm, LANES), rope_map),
                  _resident(gmat_slab.shape), _resident(gmat128.shape), _layer(head_gains, j)],
        out_specs=[pl.BlockSpec((tm, wd), lambda m: (m, 0)) for wd in widths],
        out_shape=[jax.ShapeDtypeStruct((rows, wd), BF16) for wd in widths],
        compiler_params=_cparams("parallel"),
        name="inproj_odd",
    )(h, gains, mod_all, mod_all, wq_perm, w_main, cos_t, sin_t, gmat_slab, gmat128, head_gains)


def _outproj_body(k1, k2, b_row0, h_ref, gate_ref, a1_ref, a2_ref, wa_ref, wb_ref, o_ref):
    acc = _dot(a1_ref[...].astype(BF16), wa_ref[0:k1, :])
    acc += _dot(a2_ref[...].astype(BF16), wb_ref[b_row0:b_row0 + k2, :])
    o_ref[...] = h_ref[...] + gate_ref[...] * acc


def _outproj(h, layer, mod_all, a1, a2, wa, wb, j, b_row0, n_batch, tiles_per_batch, tile_off, a1_off,
             a2_off, n_tiles):
    tm = ROW_TILE
    k1, k2 = a1.shape[1], a2.shape[1]
    return pl.pallas_call(
        functools.partial(_outproj_body, k1, k2, b_row0),
        grid=(n_tiles,),
        in_specs=[pl.BlockSpec((tm, D_MODEL), lambda m: (m + tile_off, 0)),
                  _mod_spec(layer, 2, tiles_per_batch, n_batch, tile_off),
                  pl.BlockSpec((tm, k1), lambda m: (m + a1_off, 0)),
                  pl.BlockSpec((tm, k2), lambda m: (m + a2_off, 0)),
                  _layer(wa, j), _layer(wb, j)],
        out_specs=pl.BlockSpec((tm, D_MODEL), lambda m: (m + tile_off, 0)),
        out_shape=jax.ShapeDtypeStruct(h.shape, F32),
        input_output_aliases={0: 0},
        compiler_params=_cparams("parallel"),
        name="outproj",
    )(h, mod_all, a1, a2, wa, wb)


def _ffn_body(h_ref, g_ref, sh_ref, sc_ref, gate_ref, w1_ref, w2_ref, o_ref):
    h = h_ref[...]
    u = _norm_mod(h, g_ref[...], sh_ref[...], sc_ref[...]).astype(BF16)
    acc = None
    for k in range(D_FF // FF_TILE):
        sl = slice(k * FF_TILE, (k + 1) * FF_TILE)
        a = jnp.maximum(_dot(u, w1_ref[:, sl]), 0.0)
        t = _dot((a * a).astype(BF16), w2_ref[sl, :])
        acc = t if acc is None else acc + t
    o_ref[...] = h + gate_ref[...] * acc


def _ffn(h, layer, gains, mod_all, w1, w2, n_batch, tiles_per_batch, n_tiles):
    tm = ROW_TILE
    return pl.pallas_call(
        _ffn_body,
        grid=(n_tiles,),
        in_specs=[pl.BlockSpec((tm, D_MODEL), lambda m: (m, 0)),
                  _layer(gains, layer),
                  _mod_spec(layer, 3, tiles_per_batch, n_batch, 0),
                  _mod_spec(layer, 4, tiles_per_batch, n_batch, 0),
                  _mod_spec(layer, 5, tiles_per_batch, n_batch, 0),
                  _layer(w1, layer), _layer(w2, layer)],
        out_specs=pl.BlockSpec((tm, D_MODEL), lambda m: (m, 0)),
        out_shape=jax.ShapeDtypeStruct((n_tiles * tm, D_MODEL), F32),
        compiler_params=_cparams("parallel"),
        name="ffn",
    )(h, gains, mod_all, mod_all, mod_all, w1, w2)


def _dft_a_body(xr_ref, xi_ref, m_ref, c_ref, s_ref, br_ref, bi_ref):
    n1 = DFT_N1
    reps = FNET_W // LANES
    for j in range(xr_ref.shape[1]):
        x = jnp.concatenate([xr_ref[:, j, :], xi_ref[:, j, :]], axis=0).astype(BF16)
        a = _dot(m_ref[...], x)
        ar, ai = a[:n1], a[n1:]
        c = jnp.concatenate([c_ref[:, LANES * j:LANES * (j + 1)]] * reps, axis=1)
        s = jnp.concatenate([s_ref[:, LANES * j:LANES * (j + 1)]] * reps, axis=1)
        br_ref[:, j, :] = ar * c + ai * s
        bi_ref[:, j, :] = ai * c - ar * s


def _dft_stage_a(xr, xi, mmat, twc, tws, n_batch, n2):
    n1 = DFT_N1
    lb = 8
    blk = pl.BlockSpec((n1, lb, FNET_W), lambda b, j: (b, j, 0))
    return pl.pallas_call(
        _dft_a_body,
        grid=(n_batch, n2 // lb),
        in_specs=[blk, blk,
                  pl.BlockSpec((2 * n1, 2 * n1), lambda b, j: (0, 0)),
                  pl.BlockSpec((n1, lb * LANES), lambda b, j: (0, j)),
                  pl.BlockSpec((n1, lb * LANES), lambda b, j: (0, j))],
        out_specs=[blk, blk],
        out_shape=[jax.ShapeDtypeStruct((n_batch * n1, n2, FNET_W), F32)] * 2,
        compiler_params=_cparams("parallel", "parallel"),
        name="seq_dft_stage_a",
    )(xr, xi, mmat, twc, tws)


def _dft_b_body(br_ref, bi_ref, m_ref, o_ref):
    x = jnp.concatenate([br_ref[...], bi_ref[...]], axis=0).astype(BF16)
    res = _dot(m_ref[...], x)
    o_ref[...] = res.reshape(o_ref.shape)


def _dft_stage_b(br, bi, mmat, n_batch, n2):
    n1 = DFT_N1
    kb = 8
    return pl.pallas_call(
        _dft_b_body,
        grid=(n_batch, n1 // kb),
        in_specs=[pl.BlockSpec((None, kb * n2, FNET_W), lambda b, j: (b, j, 0)),
                  pl.BlockSpec((None, kb * n2, FNET_W), lambda b, j: (b, j, 0)),
                  pl.BlockSpec(mmat.shape, lambda b, j: (0, 0))],
        out_specs=pl.BlockSpec((None, n2, kb, FNET_W), lambda b, j: (b, 0, j, 0)),
        out_shape=jax.ShapeDtypeStruct((n_batch, n2, n1, FNET_W), F32),
        compiler_params=_cparams("parallel", "parallel"),
        name="seq_dft_stage_b",
    )(br, bi, mmat)


def _ctx_dft_body(xr_ref, xi_ref, m_ref, o_ref):
    x = jnp.concatenate([xr_ref[...], xi_ref[...]], axis=0).astype(BF16)
    o_ref[...] = _dot(m_ref[...], x)


def _ctx_dft(fr, fi, mmat, n_batch, n_ctx, row_block_off):
    return pl.pallas_call(
        _ctx_dft_body,
        grid=(n_batch,),
        in_specs=[pl.BlockSpec((n_ctx, FNET_W), lambda b: (row_block_off + b, 0)),
                  pl.BlockSpec((n_ctx, FNET_W), lambda b: (row_block_off + b, 0)),
                  pl.BlockSpec(mmat.shape, lambda b: (0, 0))],
        out_specs=pl.BlockSpec((n_ctx, FNET_W), lambda b: (b, 0)),
        out_shape=jax.ShapeDtypeStruct((n_batch * n_ctx, FNET_W), F32),
        compiler_params=_cparams("parallel"),
        name="ctx_dft",
    )(fr, fi, mmat)


def _dft_tables(seq, n_ctx):
    n1 = DFT_N1
    n2 = seq // n1
    k1 = np.arange(n1)
    ang1 = 2.0 * np.pi * np.outer(k1, k1) / n1
    c1, s1 = np.cos(ang1), np.sin(ang1)
    m_a = np.block([[c1, s1], [-s1, c1]])
    ang_t = 2.0 * np.pi * np.outer(k1, np.arange(n2)) / seq
    twc = np.repeat(np.cos(ang_t), LANES, axis=1)
    tws = np.repeat(np.sin(ang_t), LANES, axis=1)
    k2 = np.arange(n2)
    ang2 = 2.0 * np.pi * np.outer(k2, k2) / n2
    scale = 1.0 / math.sqrt(seq)
    c2, s2 = np.cos(ang2) * scale, np.sin(ang2) * scale
    eye = np.eye(8)
    m_b = np.concatenate([np.einsum("kl,ab->kabl", c2, eye).reshape(n2 * 8, 8 * n2),
                          np.einsum("kl,ab->kabl", s2, eye).reshape(n2 * 8, 8 * n2)], axis=1)
    kc = np.arange(n_ctx)
    angc = 2.0 * np.pi * np.outer(kc, kc) / n_ctx
    m_c = np.concatenate([np.cos(angc), np.sin(angc)], axis=1) / math.sqrt(n_ctx)
    ch = np.arange(FNET_GROUP_W)
    angg = 2.0 * np.pi * np.outer(ch, ch) / FNET_GROUP_W
    eg = np.eye(FNET_GROUPS)
    chan = np.concatenate([np.kron(eg, np.cos(angg)), -np.kron(eg, np.sin(angg))], axis=1)
    chan = chan / math.sqrt(FNET_GROUP_W)
    return (jnp.asarray(m_a, BF16), jnp.asarray(twc, F32), jnp.asarray(tws, F32),
            jnp.asarray(m_b, BF16), jnp.asarray(m_c, BF16), jnp.asarray(chan, F32))


def _split_dot_rhs(w_bf16, x):
    hi = x.astype(BF16)
    r1 = x - hi.astype(F32)
    mid = r1.astype(BF16)
    lo = (r1 - mid.astype(F32)).astype(BF16)
    return _dot(w_bf16, hi) + _dot(w_bf16, mid) + _dot(w_bf16, lo)


def _split2_dot(x, w_bf16):
    hi = x.astype(BF16)
    lo = (x - hi.astype(F32)).astype(BF16)
    return _dot(hi, w_bf16) + _dot(lo, w_bf16)


SSD_CPS = 2


def _ssd2_body(nb, ncc, xm_ref, xp_ref, xn_ref, dt_ref, z_ref, cw_ref, cb_ref, dtb_ref, alog_ref, dsk_ref,
               ng_ref, tri_ref, e_ref, o_ref, yloc_ref, s_ref, c_ref, ea_ref, cd_ref, hbe_ref, hf_ref, hb_ref):
    g = pl.program_id(1)
    t = pl.program_id(2)
    q = SSD_CHUNK
    gw = SSD_GROUP_W
    ns = nb * SSD_CPS

    @pl.when(t < nb)
    def _local():
        first = t <= 1
        last = (t == 0) | (t == nb - 1)
        prev = jnp.where(first, 0.0, xp_ref[...])
        nxt = jnp.where(last, 0.0, xn_ref[...])
        xfull = jnp.concatenate([prev, xm_ref[...], nxt], axis=0)
        rows_blk = SSD_CPS * q
        conv = cb_ref[...]
        for k in range(SSD_CONV_W):
            shift = (SSD_CONV_W // 2 - k) % xfull.shape[0]
            xk = xfull if shift == 0 else pltpu.roll(xfull, shift, 0)
            conv = conv + xk[HALO:HALO + rows_blk, :] * cw_ref[k:k + 1, :]
        act = _silu(conv)
        xs = act[:, :gw]
        bm = act[:, gw:gw + SSD_STATE]
        cm = act[:, gw + SSD_STATE:]

        lane = lax.broadcasted_iota(jnp.int32, (1, LANES), 1)
        is_fwd = lane < SSD_HPG
        dsel = pltpu.roll(dt_ref[...], (LANES - SSD_HPG * g) % LANES, 1)
        dtv = _softplus(dsel + dtb_ref[pl.ds(g, 1), :])
        head_lane = is_fwd | ((lane >= SSD_HEADS) & (lane < SSD_HEADS + SSD_HPG))
        a_row = jnp.where(head_lane, -jnp.exp(alog_ref[pl.ds(g, 1), :]) * LOG2E, 0.0)
        adt = dtv * a_row
        tri = tri_ref[...]
        tri16 = tri.astype(BF16)
        allowed = (tri[:q] > 0.0, tri[q:] > 0.0)
        emat = e_ref[...]
        lane_q = lax.broadcasted_iota(jnp.int32, (q, LANES), 1)
        lo_half = lane_q < SSD_HEAD_DIM
        prep = []
        for cc in range(SSD_CPS):
            rows = slice(cc * q, (cc + 1) * q)
            cums = _split_dot_rhs(tri16, adt[rows])
            acum = jnp.where(is_fwd, cums[:q], cums[q:])
            end = jnp.where(is_fwd, acum[q - 1:q, :], acum[0:1, :])
            dte = jnp.exp2(end - acum)
            ex = _dot(jnp.concatenate([dtv[rows], dte], axis=0).astype(BF16), emat)
            cdx = _split2_dot(jnp.broadcast_to(jnp.exp2(end), (16, LANES)), emat)
            c16 = cm[rows].astype(BF16)
            b_c = bm[rows]
            cbm = _dot_nt(c16, b_c.astype(BF16))
            prep.append((acum, acum.T, ex, cdx, c16, b_c, cbm))
        for cc in range(SSD_CPS):
            rows = slice(cc * q, (cc + 1) * q)
            acum, acum_t, ex, cdx, c16, b_c, cbm = prep[cc]
            xs_c = xs[rows]
            xdt2 = jnp.concatenate([xs_c, xs_c], axis=1) * ex[:q]
            ys = []
            for m in range(SSD_HPG // 2):
                mms, xhs = [], []
                for d in range(2):
                    xpair = xdt2[:, gw * d + LANES * m:gw * d + LANES * (m + 1)]
                    for half in range(2):
                        ln = SSD_HEADS * d + 2 * m + half
                        seg = acum[:, ln:ln + 1] - acum_t[ln:ln + 1, :]
                        dec = jnp.exp2(jnp.where(allowed[d], seg, NEG_INF))
                        mms.append((cbm * dec).astype(BF16))
                        xhs.append(jnp.where(lo_half if half == 0 else ~lo_half, xpair, 0.0).astype(BF16))
                ys.append(_dot(jnp.concatenate(mms, axis=1), jnp.concatenate(xhs, axis=0)))
            y_loc = jnp.concatenate(ys, axis=1) + dsk_ref[...] * xs_c
            sts = _dot(b_c.T.astype(BF16), (xdt2 * ex[q:2 * q]).astype(BF16))
            pos = SSD_CPS * t + cc
            r0 = pl.multiple_of(pos * q, q)
            yloc_ref[pl.ds(r0, q), :] = y_loc
            s_ref[pl.ds(r0, q), :] = sts.astype(BF16)
            c_ref[pl.ds(r0, q), :] = c16
            ea_ref[pl.ds(r0, q), :] = jnp.exp2(acum).astype(BF16)
            cd_ref[pl.ds(pl.multiple_of(pos * 8, 8), 8), :] = cdx[0:8]

    @pl.when(t == nb)
    def _backward_states():
        hb_ref[...] = jnp.zeros_like(hb_ref)
        hf_ref[...] = jnp.zeros_like(hf_ref)

        def body(i, carry):
            pos = jnp.where(i < ncc, ncc - 1 - i, ns - 1 + ncc - i)
            r0 = pl.multiple_of(pos * q, q)
            hb = hb_ref[...]
            hbe_ref[pl.ds(r0, q), :] = hb.astype(BF16)
            cd = cd_ref[pl.ds(pl.multiple_of(pos * 8, 8), 8), :][0:1, gw:]
            hb_ref[...] = hb * cd + s_ref[pl.ds(r0, q), gw:].astype(F32)
            return carry

        lax.fori_loop(0, ns, body, 0)

    @pl.when(t >= nb)
    def _emit():
        emat = e_ref[...]
        starts = [pl.multiple_of((SSD_CPS * (t - nb) + cc) * q, q) for cc in range(SSD_CPS)]
        ea_x = [_dot(ea_ref[pl.ds(r0, q), :], emat) for r0 in starts]
        gate = [_silu(z_ref[cc * q:(cc + 1) * q, :].astype(F32)) for cc in range(SSD_CPS)]
        hf = hf_ref[...]
        for cc in range(SSD_CPS):
            rows = slice(cc * q, (cc + 1) * q)
            pos = SSD_CPS * (t - nb) + cc
            r0 = starts[cc]
            hcat = jnp.concatenate([hf.astype(BF16), hbe_ref[pl.ds(r0, q), :]], axis=1)
            yo = _dot(c_ref[pl.ds(r0, q), :], hcat) * ea_x[cc]
            y = yloc_ref[pl.ds(r0, q), :] + yo[:, :gw] + yo[:, gw:]
            cd = cd_ref[pl.ds(pl.multiple_of(pos * 8, 8), 8), :][0:1, :gw]
            hf = hf * cd + s_ref[pl.ds(r0, q), :gw].astype(F32)
            y = y * gate[cc]
            y = y * lax.rsqrt(jnp.mean(y * y, axis=-1, keepdims=True) + NORM_EPS) * ng_ref[...]
            o_ref[rows, :] = y.astype(o_ref.dtype)
        hf_ref[...] = hf


def _ssd2(xbc, dt, z, j, conv_w, conv_b, dtb_tab, alog_tab, dsk_x, ng, tri, emat, n_batch, seq, n_ctx):
    rows = xbc.shape[0]
    q = SSD_CHUNK
    blk = SSD_CPS * q
    assert n_ctx == blk and seq % blk == 0
    ncc = n_ctx // q
    nlb = seq // blk
    nb = 1 + nlb
    ns = nb * SSD_CPS
    per = blk // HALO

    def rb(b, pb):
        return jnp.where(pb == 0, n_batch * nlb + b, b * nlb + pb - 1)

    def rb_in(b, t):
        return rb(b, jnp.minimum(t, nb - 1))

    def rb_out(b, t):
        return rb(b, jnp.maximum(t - nb, 0))

    last_halo = rows // HALO - 1
    gw = SSD_GROUP_W
    return pl.pallas_call(
        functools.partial(_ssd2_body, nb, ncc),
        grid=(n_batch, SSD_GROUPS, 2 * nb),
        in_specs=[pl.BlockSpec((blk, SSD_XBC_W), lambda b, g, t: (rb_in(b, t), g)),
                  pl.BlockSpec((HALO, SSD_XBC_W),
                               lambda b, g, t: (jnp.maximum(rb_in(b, t) * per - 1, 0), g)),
                  pl.BlockSpec((HALO, SSD_XBC_W),
                               lambda b, g, t: (jnp.minimum(rb_in(b, t) * per + per, last_halo), g)),
                  pl.BlockSpec((blk, LANES), lambda b, g, t: (rb_in(b, t), 0)),
                  pl.BlockSpec((blk, gw), lambda b, g, t: (rb_out(b, t), g)),
                  pl.BlockSpec((None, SSD_CONV_W, SSD_XBC_W), lambda b, g, t: (j, 0, g)),
                  pl.BlockSpec((None, 1, SSD_XBC_W), lambda b, g, t: (j, 0, g)),
                  pl.BlockSpec((None, 8, LANES), lambda b, g, t: (j, 0, 0)),
                  pl.BlockSpec((None, 8, LANES), lambda b, g, t: (j, 0, 0)),
                  pl.BlockSpec((None, 1, gw), lambda b, g, t: (j, 0, g)),
                  pl.BlockSpec((None, 1, gw), lambda b, g, t: (j, 0, g)),
                  pl.BlockSpec((2 * q, q), lambda b, g, t: (0, 0)),
                  pl.BlockSpec((LANES, 2 * gw), lambda b, g, t: (0, 0))],
        out_specs=pl.BlockSpec((blk, gw), lambda b, g, t: (rb_out(b, t), g)),
        out_shape=jax.ShapeDtypeStruct((rows, SSD_INNER), BF16),
        scratch_shapes=[pltpu.VMEM((ns * q, gw), F32),
                        pltpu.VMEM((ns * q, 2 * gw), BF16),
                        pltpu.VMEM((ns * q, SSD_STATE), BF16),
                        pltpu.VMEM((ns * q, LANES), BF16),
                        pltpu.VMEM((ns * 8, 2 * gw), F32),
                        pltpu.VMEM((ns * q, gw), BF16),
                        pltpu.VMEM((SSD_STATE, gw), F32),
                        pltpu.VMEM((SSD_STATE, gw), F32)],
        compiler_params=_cparams("arbitrary", "arbitrary", "arbitrary"),
        name="ssd_bidir",
    )(xbc, xbc, xbc, dt, z, conv_w, conv_b, dtb_tab, alog_tab, dsk_x, ng, tri, emat)


SSD_LAT_CPS = 8


def _ssd_local(xfull, dt_raw, g, pos0, n_chunks, cw_ref, cb_ref, dtb_ref, alog_ref, dsk_ref, tri_ref, e_ref,
               yloc_ref, s_ref, c_ref, ea_ref, cd_ref):
    q = SSD_CHUNK
    gw = SSD_GROUP_W
    rows_blk = n_chunks * q
    conv = cb_ref[...]
    for k in range(SSD_CONV_W):
        shift = (SSD_CONV_W // 2 - k) % xfull.shape[0]
        xk = xfull if shift == 0 else pltpu.roll(xfull, shift, 0)
        conv = conv + xk[HALO:HALO + rows_blk, :] * cw_ref[k:k + 1, :]
    act = _silu(conv)
    xs = act[:, :gw]
    bm = act[:, gw:gw + SSD_STATE]
    cm = act[:, gw + SSD_STATE:]

    lane = lax.broadcasted_iota(jnp.int32, (1, LANES), 1)
    is_fwd = lane < SSD_HPG
    dsel = pltpu.roll(dt_raw, (LANES - SSD_HPG * g) % LANES, 1)
    dtv = _softplus(dsel + dtb_ref[pl.ds(g, 1), :])
    head_lane = is_fwd | ((lane >= SSD_HEADS) & (lane < SSD_HEADS + SSD_HPG))
    a_row = jnp.where(head_lane, -jnp.exp(alog_ref[pl.ds(g, 1), :]) * LOG2E, 0.0)
    adt = dtv * a_row
    tri = tri_ref[...]
    tri16 = tri.astype(BF16)
    allowed = (tri[:q] > 0.0, tri[q:] > 0.0)
    emat = e_ref[...]
    lane_q = lax.broadcasted_iota(jnp.int32, (q, LANES), 1)
    lo_half = lane_q < SSD_HEAD_DIM
    for c0 in range(0, n_chunks, 2):
        pair = range(c0, min(c0 + 2, n_chunks))
        prep = {}
        for cc in pair:
            rows = slice(cc * q, (cc + 1) * q)
            cums = _split_dot_rhs(tri16, adt[rows])
            acum = jnp.where(is_fwd, cums[:q], cums[q:])
            end = jnp.where(is_fwd, acum[q - 1:q, :], acum[0:1, :])
            dte = jnp.exp2(end - acum)
            ex = _dot(jnp.concatenate([dtv[rows], dte], axis=0).astype(BF16), emat)
            cdx = _split2_dot(jnp.broadcast_to(jnp.exp2(end), (16, LANES)), emat)
            c16 = cm[rows].astype(BF16)
            b_c = bm[rows]
            cbm = _dot_nt(c16, b_c.astype(BF16))
            prep[cc] = (acum, acum.T, ex, cdx, c16, b_c, cbm)
        for cc in pair:
            rows = slice(cc * q, (cc + 1) * q)
            acum, acum_t, ex, cdx, c16, b_c, cbm = prep[cc]
            xs_c = xs[rows]
            xdt2 = jnp.concatenate([xs_c, xs_c], axis=1) * ex[:q]
            ys = []
            for m in range(SSD_HPG // 2):
                mms, xhs = [], []
                for d in range(2):
                    xpair = xdt2[:, gw * d + LANES * m:gw * d + LANES * (m + 1)]
                    for half in range(2):
                        ln = SSD_HEADS * d + 2 * m + half
                        seg = acum[:, ln:ln + 1] - acum_t[ln:ln + 1, :]
                        dec = jnp.exp2(jnp.where(allowed[d], seg, NEG_INF))
                        mms.append((cbm * dec).astype(BF16))
                        xhs.append(jnp.where(lo_half if half == 0 else ~lo_half, xpair, 0.0).astype(BF16))
                ys.append(_dot(jnp.concatenate(mms, axis=1), jnp.concatenate(xhs, axis=0)))
            y_loc = jnp.concatenate(ys, axis=1) + dsk_ref[...] * xs_c
            sts = _dot(b_c.T.astype(BF16), (xdt2 * ex[q:2 * q]).astype(BF16))
            pos = pos0 + cc
            r0 = pl.multiple_of(pos * q, q)
            yloc_ref[pl.ds(r0, q), :] = y_loc
            s_ref[pl.ds(r0, q), :] = sts.astype(BF16)
            c_ref[pl.ds(r0, q), :] = c16
            ea_ref[pl.ds(r0, q), :] = jnp.exp2(acum).astype(BF16)
            cd_ref[pl.ds(pl.multiple_of(pos * 8, 8), 8), :] = cdx[0:8]


def _ssd_emit(z_ref, o_ref, pos0, n_chunks, hf, ng_ref, e_ref, yloc_ref, s_ref, c_ref, ea_ref, cd_ref, hbe_ref):
    q = SSD_CHUNK
    gw = SSD_GROUP_W
    emat = e_ref[...]
    starts = [pl.multiple_of((pos0 + cc) * q, q) for cc in range(n_chunks)]
    ea_x = [_dot(ea_ref[pl.ds(r0, q), :], emat) for r0 in starts]
    gate = [_silu(z_ref[cc * q:(cc + 1) * q, :].astype(F32)) for cc in range(n_chunks)]
    for cc in range(n_chunks):
        r0 = starts[cc]
        hcat = jnp.concatenate([hf.astype(BF16), hbe_ref[pl.ds(r0, q), :]], axis=1)
        yo = _dot(c_ref[pl.ds(r0, q), :], hcat) * ea_x[cc]
        y = yloc_ref[pl.ds(r0, q), :] + yo[:, :gw] + yo[:, gw:]
        cd = cd_ref[pl.ds(pl.multiple_of((pos0 + cc) * 8, 8), 8), :][0:1, :gw]
        hf = hf * cd + s_ref[pl.ds(r0, q), :gw].astype(F32)
        y = y * gate[cc]
        y = y * lax.rsqrt(jnp.mean(y * y, axis=-1, keepdims=True) + NORM_EPS) * ng_ref[...]
        o_ref[cc * q:(cc + 1) * q, :] = y.astype(o_ref.dtype)
    return hf


def _ssd3_body(nb, ncc, xm_ref, xp_ref, xn_ref, dt_ref, z_ref, xc_ref, dtc_ref, zc_ref, cw_ref, cb_ref, dtb_ref,
               alog_ref, dsk_ref, ng_ref, tri_ref, e_ref, o_ref, oc_ref,
               yloc_ref, s_ref, c_ref, ea_ref, cd_ref, hbe_ref, hf_ref, hb_ref):
    g = pl.program_id(1)
    t = pl.program_id(2)
    q = SSD_CHUNK
    gw = SSD_GROUP_W
    ns = ncc + nb * SSD_LAT_CPS
    local_refs = (cw_ref, cb_ref, dtb_ref, alog_ref, dsk_ref, tri_ref, e_ref, yloc_ref, s_ref, c_ref, ea_ref,
                  cd_ref)
    emit_refs = (ng_ref, e_ref, yloc_ref, s_ref, c_ref, ea_ref, cd_ref, hbe_ref)

    @pl.when(t == 0)
    def _local_context():
        pad = jnp.zeros((HALO, SSD_XBC_W), F32)
        _ssd_local(jnp.concatenate([pad, xc_ref[...], pad], axis=0), dtc_ref[...], g, 0, ncc, *local_refs)

    @pl.when(t < nb)
    def _local_latent():
        prev = jnp.where(t == 0, 0.0, xp_ref[...])
        nxt = jnp.where(t == nb - 1, 0.0, xn_ref[...])
        xfull = jnp.concatenate([prev, xm_ref[...], nxt], axis=0)
        _ssd_local(xfull, dt_ref[...], g, ncc + SSD_LAT_CPS * t, SSD_LAT_CPS, *local_refs)

    @pl.when(t == nb)
    def _backward_states_and_context():
        hb_ref[...] = jnp.zeros_like(hb_ref)

        def body(i, carry):
            pos = jnp.where(i < ncc, ncc - 1 - i, ns - 1 + ncc - i)
            r0 = pl.multiple_of(pos * q, q)
            hb = hb_ref[...]
            hbe_ref[pl.ds(r0, q), :] = hb.astype(BF16)
            cd = cd_ref[pl.ds(pl.multiple_of(pos * 8, 8), 8), :][0:1, gw:]
            hb_ref[...] = hb * cd + s_ref[pl.ds(r0, q), gw:].astype(F32)
            return carry

        lax.fori_loop(0, ns, body, 0)
        hf_ref[...] = _ssd_emit(zc_ref, oc_ref, 0, ncc, jnp.zeros(hf_ref.shape, F32), *emit_refs)

    @pl.when(t >= nb)
    def _emit_latent():
        hf_ref[...] = _ssd_emit(z_ref, o_ref, ncc + SSD_LAT_CPS * (t - nb), SSD_LAT_CPS, hf_ref[...], *emit_refs)


def _ssd3(xbc, dt, z, j, conv_w, conv_b, dtb_tab, alog_tab, dsk_x, ng, tri, emat, n_batch, seq, n_ctx):
    rows = xbc.shape[0]
    q = SSD_CHUNK
    blk = SSD_LAT_CPS * q
    assert n_ctx % q == 0 and seq % blk == 0 and (n_batch * seq) % n_ctx == 0
    ncc = n_ctx // q
    nb = seq // blk
    ns = ncc + nb * SSD_LAT_CPS
    per = blk // HALO
    ctx_blk0 = n_batch * seq // n_ctx
    last_halo = rows // HALO - 1
    gw = SSD_GROUP_W

    def rb_in(b, t):
        return b * nb + jnp.minimum(t, nb - 1)

    def rb_out(b, t):
        return b * nb + jnp.maximum(t - nb, 0)

    return pl.pallas_call(
        functools.partial(_ssd3_body, nb, ncc),
        grid=(n_batch, SSD_GROUPS, 2 * nb),
        in_specs=[pl.BlockSpec((blk, SSD_XBC_W), lambda b, g, t: (rb_in(b, t), g)),
                  pl.BlockSpec((HALO, SSD_XBC_W),
                               lambda b, g, t: (jnp.maximum(rb_in(b, t) * per - 1, 0), g)),
                  pl.BlockSpec((HALO, SSD_XBC_W),
                               lambda b, g, t: (jnp.minimum(rb_in(b, t) * per + per, last_halo), g)),
                  pl.BlockSpec((blk, LANES), lambda b, g, t: (rb_in(b, t), 0)),
                  pl.BlockSpec((blk, gw), lambda b, g, t: (rb_out(b, t), g)),
                  pl.BlockSpec((n_ctx, SSD_XBC_W), lambda b, g, t: (ctx_blk0 + b, g)),
                  pl.BlockSpec((n_ctx, LANES), lambda b, g, t: (ctx_blk0 + b, 0)),
                  pl.BlockSpec((n_ctx, gw), lambda b, g, t: (ctx_blk0 + b, g)),
                  pl.BlockSpec((None, SSD_CONV_W, SSD_XBC_W), lambda b, g, t: (j, 0, g)),
                  pl.BlockSpec((None, 1, SSD_XBC_W), lambda b, g, t: (j, 0, g)),
                  pl.BlockSpec((None, 8, LANES), lambda b, g, t: (j, 0, 0)),
                  pl.BlockSpec((None, 8, LANES), lambda b, g, t: (j, 0, 0)),
                  pl.BlockSpec((None, 1, gw), lambda b, g, t: (j, 0, g)),
                  pl.BlockSpec((None, 1, gw), lambda b, g, t: (j, 0, g)),
                  pl.BlockSpec((2 * q, q), lambda b, g, t: (0, 0)),
                  pl.BlockSpec((LANES, 2 * gw), lambda b, g, t: (0, 0))],
        out_specs=[pl.BlockSpec((blk, gw), lambda b, g, t: (rb_out(b, t), g)),
                   pl.BlockSpec((n_ctx, gw), lambda b, g, t: (b, g))],
        out_shape=[jax.ShapeDtypeStruct((n_batch * seq, SSD_INNER), BF16),
                   jax.ShapeDtypeStruct((n_batch * n_ctx, SSD_INNER), BF16)],
        scratch_shapes=[pltpu.VMEM((ns * q, gw), F32),
                        pltpu.VMEM((ns * q, 2 * gw), BF16),
                        pltpu.VMEM((ns * q, SSD_STATE), BF16),
                        pltpu.VMEM((ns * q, LANES), BF16),
                        pltpu.VMEM((ns * 8, 2 * gw), F32),
                        pltpu.VMEM((ns * q, gw), BF16),
                        pltpu.VMEM((SSD_STATE, gw), F32),
                        pltpu.VMEM((SSD_STATE, gw), F32)],
        compiler_params=_cparams("arbitrary", "arbitrary", "arbitrary"),
        name="ssd_bidir",
    )(xbc, xbc, xbc, dt, z, xbc, dt, z, conv_w, conv_b, dtb_tab, alog_tab, dsk_x, ng, tri, emat)


def _stack_heads(qc):
    lane = lax.broadcasted_iota(jnp.int32, qc.shape, 1)
    lo = lane < HEAD_DIM
    zero = jnp.zeros_like(qc)
    return jnp.concatenate([jnp.where(lo, qc, zero), jnp.where(lo, zero, qc)], axis=0)


def _unstack_heads(o, n):
    lane = lax.broadcasted_iota(jnp.int32, (n, LANES), 1)
    return jnp.where(lane < HEAD_DIM, o[:n], o[n:])


def _softmax_pv(scores, values, sink=None):
    m = None
    for sc in scores:
        mx = jnp.max(sc, axis=-1, keepdims=True)
        m = mx if m is None else jnp.maximum(m, mx)
    if sink is not None:
        m = jnp.maximum(m, sink)
    den = None if sink is None else jnp.exp2(sink - m)
    acc = None
    for sc, v in zip(scores, values):
        pr = jnp.exp2(sc - m)
        sm = jnp.sum(pr, axis=-1, keepdims=True)
        den = sm if den is None else den + sm
        t = _dot(pr.astype(BF16), v)
        acc = t if acc is None else acc + t
    return acc * (1.0 / den)


def _win_body(seq, q_ref, k_ref, v_ref, kc_ref, vc_ref, sink_ref, o_ref):
    n = pl.program_id(1)
    wb = WIN_BLOCK
    start = pl.multiple_of(jnp.clip((n - 1) * wb, 0, seq - 3 * wb), wb)
    kb = k_ref[pl.ds(start, 3 * wb), :]
    vb = v_ref[pl.ds(start, 3 * wb), :]
    r = lax.broadcasted_iota(jnp.int32, (wb, 3 * wb), 0)
    c = lax.broadcasted_iota(jnp.int32, (wb, 3 * wb), 1)
    dist = (c + start) - (r + n * wb)
    pen = jnp.where(jnp.abs(dist) <= WIN_RADIUS, 0.0, NEG_INF)
    pen = jnp.concatenate([pen, pen], axis=0)
    half = WIN_GQA * wb
    for m in range(WIN_GQA):
        sl = slice(LANES * m, LANES * (m + 1))
        qs = _stack_heads(q_ref[:, sl])
        s_loc = _dot_nt(qs, kb) + pen
        s_ctx = _dot_nt(qs, kc_ref[...])
        sink = jnp.concatenate([sink_ref[m * wb:(m + 1) * wb, 0:1],
                                sink_ref[half + m * wb:half + (m + 1) * wb, 0:1]], axis=0)
        o = _softmax_pv([s_loc, s_ctx], [vb, vc_ref[...]], sink)
        o_ref[:, sl] = _unstack_heads(o, wb).astype(o_ref.dtype)


def _window_attention(qw, kw, vw, sink_x, j, n_batch, seq, n_ctx):
    wb = WIN_BLOCK
    nb = seq // wb
    assert nb >= 3
    ctx_blk0 = n_batch * seq // n_ctx
    q_spec = pl.BlockSpec((wb, 512), lambda b, n: (b * nb + n, 0))
    seq_spec = pl.BlockSpec((seq, LANES), lambda b, n: (b, 0))
    cspec = pl.BlockSpec((n_ctx, LANES), lambda b, n: (ctx_blk0 + b, 0))
    return pl.pallas_call(
        functools.partial(_win_body, seq),
        grid=(n_batch, nb),
        in_specs=[q_spec, seq_spec, seq_spec, cspec, cspec, _layer(sink_x, j)],
        out_specs=pl.BlockSpec((wb, 512), lambda b, n: (b * nb + n, 0)),
        out_shape=jax.ShapeDtypeStruct((n_batch * seq, 512), BF16),
        compiler_params=_cparams("parallel", "arbitrary"),
        name="window_attention",
    )(qw, kw, vw, kw, vw, sink_x)


NA_ROWS_PER_STEP = 4


def _na_body(n_rows, q_ref, k_ref, v_ref, kc_ref, vc_ref, bias_ref, o_ref):
    kr = NA_ROWS
    for j in range(NA_ROWS_PER_STEP):
        r = pl.program_id(1) * NA_ROWS_PER_STEP + j
        first_row = jnp.clip(r - kr // 2, 0, n_rows - kr)
        start = pl.multiple_of(first_row * GRID_W, GRID_W)
        ro0 = first_row - r + NA_ROWS - 1
        rows = slice(j * GRID_W, (j + 1) * GRID_W)
        scores = []
        for m in range(NA_HEADS // 2):
            sl = slice(LANES * m, LANES * (m + 1))
            qs = _stack_heads(q_ref[rows, sl])
            kw = k_ref[pl.ds(start, kr * GRID_W), sl]
            s_loc = _dot_nt(qs, kw) + bias_ref[ro0, 2 * GRID_W * m:2 * GRID_W * (m + 1), :]
            scores.append((s_loc, _dot_nt(qs, kc_ref[:, sl])))
        for m in range(NA_HEADS // 2):
            sl = slice(LANES * m, LANES * (m + 1))
            vw = v_ref[pl.ds(start, kr * GRID_W), sl]
            o = _softmax_pv(list(scores[m]), [vw, vc_ref[:, sl]])
            o_ref[rows, sl] = _unstack_heads(o, GRID_W).astype(o_ref.dtype)


def _na_attention(qn, kn, vn, bias_tab, j, n_batch, seq, n_ctx):
    n_rows = seq // GRID_W
    assert n_rows >= NA_ROWS and n_rows % NA_ROWS_PER_STEP == 0
    n_steps = n_rows // NA_ROWS_PER_STEP
    qrows = NA_ROWS_PER_STEP * GRID_W
    ctx_blk0 = n_batch * seq // n_ctx
    return pl.pallas_call(
        functools.partial(_na_body, n_rows),
        grid=(n_batch, n_steps),
        in_specs=[pl.BlockSpec((qrows, 512), lambda b, r: (b * n_steps + r, 0)),
                  pl.BlockSpec((seq, 512), lambda b, r: (b, 0)),
                  pl.BlockSpec((seq, 512), lambda b, r: (b, 0)),
                  pl.BlockSpec((n_ctx, 512), lambda b, r: (ctx_blk0 + b, 0)),
                  pl.BlockSpec((n_ctx, 512), lambda b, r: (ctx_blk0 + b, 0)),
                  _layer(bias_tab, j)],
        out_specs=pl.BlockSpec((qrows, 512), lambda b, r: (b * n_steps + r, 0)),
        out_shape=jax.ShapeDtypeStruct((n_batch * seq, 512), BF16),
        compiler_params=_cparams("parallel", "arbitrary"),
        name="neighbourhood_attention",
    )(qn, kn, vn, kn, vn, bias_tab)


def _ctx_attn_body(qw_ref, kw_ref, vw_ref, qn_ref, kn_ref, vn_ref, sink_ref, ow_ref, on_ref):
    n = qw_ref.shape[0]
    qw = qw_ref[...]
    lane = lax.broadcasted_iota(jnp.int32, (n, LANES), 1)
    lo = lane < HEAD_DIM
    for m in range(WIN_GQA):
        qs = _stack_heads(qw[:, LANES * m:LANES * (m + 1)])
        r0, r1 = m * WIN_BLOCK, (WIN_GQA + m) * WIN_BLOCK
        sink = jnp.concatenate([jnp.broadcast_to(sink_ref[r0:r0 + 1, 0:1], (n, 1)),
                                jnp.broadcast_to(sink_ref[r1:r1 + 1, 0:1], (n, 1))], axis=0)
        o = _softmax_pv([_dot_nt(qs, kw_ref[...])], [vw_ref[...]], sink)
        ow_ref[:, LANES * m:LANES * (m + 1)] = jnp.where(lo, o[:n], o[n:]).astype(ow_ref.dtype)
    qn = qn_ref[...]
    for m in range(NA_HEADS // 2):
        sl = slice(LANES * m, LANES * (m + 1))
        qs = _stack_heads(qn[:, sl])
        o = _softmax_pv([_dot_nt(qs, kn_ref[:, sl])], [vn_ref[:, sl]])
        on_ref[:, sl] = _unstack_heads(o, n).astype(on_ref.dtype)


def _ctx_attention(qw, kw, vw, qn, kn, vn, sink_x, j, n_batch, seq, n_ctx):
    blk0 = n_batch * seq // n_ctx
    wide = pl.BlockSpec((n_ctx, 512), lambda b: (blk0 + b, 0))
    narrow = pl.BlockSpec((n_ctx, LANES), lambda b: (blk0 + b, 0))
    out = pl.BlockSpec((n_ctx, 512), lambda b: (b, 0))
    return pl.pallas_call(
        _ctx_attn_body,
        grid=(n_batch,),
        in_specs=[wide, narrow, narrow, wide, wide, wide,
                  _layer(sink_x, j)],
        out_specs=[out, out],
        out_shape=[jax.ShapeDtypeStruct((n_batch * n_ctx, 512), BF16)] * 2,
        compiler_params=_cparams("parallel"),
        name="context_attention",
    )(qw, kw, vw, qn, kn, vn, sink_x)


def _take(w, runs, axis):
    parts = []
    for run in runs:
        if run[0] is None:
            shape = list(w.shape)
            shape[axis] = run[1]
            parts.append(jnp.zeros(shape, w.dtype))
        else:
            parts.append(lax.slice_in_dim(w, run[0], run[1], axis=axis))
    return jnp.concatenate(parts, axis=axis)


def _xbc_runs():
    gn = SSD_GROUPS * SSD_STATE
    runs = []
    for g in range(SSD_GROUPS):
        runs.append((g * SSD_GROUP_W, (g + 1) * SSD_GROUP_W))
        runs.append((SSD_INNER + g * SSD_STATE, SSD_INNER + (g + 1) * SSD_STATE))
        runs.append((SSD_INNER + gn + g * SSD_STATE, SSD_INNER + gn + (g + 1) * SSD_STATE))
    return runs


def _head_rows(t):
    n = t.shape[0]
    flat = jnp.pad(t.astype(F32).reshape(n, 2 * SSD_HEADS), ((0, 0), (0, LANES - 2 * SSD_HEADS)))
    rows = [jnp.roll(flat, -SSD_HPG * g, axis=1) for g in range(SSD_GROUPS)]
    rows += [jnp.zeros_like(flat)] * (8 - SSD_GROUPS)
    return jnp.stack(rows, axis=1)


def _win_head_runs():
    runs = []
    for m in range(WIN_GQA):
        for kvh in range(WIN_KV_HEADS):
            h = kvh * WIN_GQA + m
            runs.append((h * HEAD_DIM, (h + 1) * HEAD_DIM))
    return runs


def _rope_tables(seq):
    pos = np.arange(seq)
    row = (pos // GRID_W).astype(np.float32)
    col = (pos % GRID_W).astype(np.float32)
    n_freq = HEAD_DIM // 4
    inv = (np.float32(ROPE_THETA) ** (-np.arange(n_freq, dtype=np.float32) / n_freq)).astype(np.float32)
    ar = (row[:, None] * inv).astype(np.float32)
    ac = (col[:, None] * inv).astype(np.float32)
    cos_h = np.concatenate([np.cos(ar), np.cos(ar), np.cos(ac), np.cos(ac)], axis=1)
    sin_h = np.concatenate([-np.sin(ar), np.sin(ar), -np.sin(ac), np.sin(ac)], axis=1)
    ident_c = np.ones((ROW_TILE, HEAD_DIM), np.float32)
    ident_s = np.zeros((ROW_TILE, HEAD_DIM), np.float32)
    cos_t = np.concatenate([cos_h, ident_c], axis=0)
    sin_t = np.concatenate([sin_h, ident_s], axis=0)
    return (jnp.asarray(np.tile(cos_t, (1, 2)), F32), jnp.asarray(np.tile(sin_t, (1, 2)), F32))


def _na_bias_table(rpb):
    n, h, nr, _ = rpb.shape
    kr = NA_ROWS
    cols = np.arange(GRID_W)
    col_start = np.clip(cols - NA_COLS // 2, 0, GRID_W - NA_COLS)
    col_ok = (cols[None] >= col_start[:, None]) & (cols[None] < col_start[:, None] + NA_COLS)
    r32 = rpb.astype(F32) * LOG2E
    v = jnp.concatenate([r32[..., NA_COLS - 1:], jnp.zeros((n, h, nr, LANES - (2 * NA_COLS - 1)), F32),
                         r32[..., :NA_COLS - 1]], axis=-1)
    flat = jnp.tile(v, (1, 1, 1, GRID_W))[..., :GRID_W * (LANES - 1)]
    toep = flat.reshape(n, h, nr, GRID_W, LANES - 1)[..., :GRID_W]
    toep = jnp.where(jnp.asarray(col_ok), toep, NEG_INF)
    toep = jnp.transpose(toep, (0, 1, 3, 2, 4))
    tabs = [toep[:, :, :, ro0:ro0 + kr].reshape(n, h * GRID_W, kr * GRID_W) for ro0 in range(NA_ROWS)]
    return jnp.stack(tabs, axis=1)


def kernel(x, c, ctx, c_ctx, w_mod, b_mod, norm_mix_g, norm_ff_g, w_ff1, w_ff2, w_in_even, conv_w, conv_b,
           dt_bias, a_log, d_skip, ssd_norm_g, w_out_even, w_in_odd, q_norm_win, k_norm_win, sink_win,
           q_norm_na, k_norm_na, rpb_na, w_out_odd):
    n_batch, seq, d = x.shape
    n_ctx = ctx.shape[1]
    n_lat = n_batch * seq
    tm = ROW_TILE
    tiles_per_batch = seq // tm
    n_lat_tiles = n_lat // tm
    n_ctx_tiles = (n_batch * n_ctx) // tm
    n2 = seq // DFT_N1

    h = jnp.concatenate([x.reshape(n_lat, d), ctx.reshape(n_batch * n_ctx, d)], axis=0)

    cvec = jnp.concatenate([c, c_ctx[None], jnp.zeros((8 - n_batch - 1, d), F32)], axis=0)
    mod_all = _modulation(cvec, w_mod, b_mod)
    mod_all = mod_all.reshape(-1, 8, 6, 1, d)[:, :n_batch + 1].transpose(0, 2, 1, 3, 4)

    m_a, twc, tws, m_b, m_c, chan = _dft_tables(seq, n_ctx)
    xbc_runs = _xbc_runs()
    s3 = FNET_W + SSD_INNER + SSD_CONV_DIM
    w_four = _weight_product(w_in_even, FNET_W, chan)
    tri = jnp.asarray(np.concatenate([np.tril(np.ones((SSD_CHUNK, SSD_CHUNK))),
                                      np.triu(np.ones((SSD_CHUNK, SSD_CHUNK)))], axis=0), F32)
    emat = np.zeros((LANES, 2 * SSD_GROUP_W), np.float32)
    for dr in range(2):
        for r in range(SSD_HPG):
            c0 = dr * SSD_GROUP_W + r * SSD_HEAD_DIM
            emat[SSD_HEADS * dr + r, c0:c0 + SSD_HEAD_DIM] = 1.0
    emat = jnp.asarray(emat, BF16)

    cos_t, sin_t = _rope_tables(seq)
    gmat_slab = jnp.asarray(np.kron(np.eye(MXU_DIM // HEAD_DIM), np.full((HEAD_DIM, HEAD_DIM), 1.0 / HEAD_DIM)),
                          BF16)
    gmat128 = gmat_slab[:LANES, :LANES]
    win_runs = _win_head_runs()
    wq = WIN_Q_HEADS * HEAD_DIM

    gains_mix = norm_mix_g.astype(F32)[:, None, :]
    gains_ff = norm_ff_g.astype(F32)[:, None, :]
    w_ff1_b, w_ff2_b = w_ff1.astype(BF16), w_ff2.astype(BF16)
    w_even_b = w_in_even.astype(BF16)
    w_dt_b = jnp.pad(w_even_b[:, :, s3:], ((0, 0), (0, 0), (0, LANES - 2 * SSD_HEADS)))
    w_out_even_b = w_out_even.astype(BF16)
    conv_w_p = _take(conv_w.astype(F32), xbc_runs, 2)
    conv_b_p = _take(conv_b.astype(F32)[:, None, :], xbc_runs, 2)
    dtb_rows, alog_rows = _head_rows(dt_bias), _head_rows(a_log)
    dsk_x = jnp.repeat(d_skip.astype(F32), SSD_HEAD_DIM, axis=1)[:, None, :]
    ssd_ng = ssd_norm_g.astype(F32)[:, None, :]
    w_odd_b = w_in_odd.astype(BF16)
    wq_perm = _take(w_odd_b[:, :, :wq], win_runs, 2)
    w_out_odd_b = w_out_odd.astype(BF16)
    wo_win_perm = _take(w_out_odd_b[:, :wq], win_runs, 1)
    qscale = HEAD_DIM ** -0.5 * LOG2E
    rep = lambda g, k: jnp.tile(g.astype(F32), (1, k))
    head_gains = jnp.stack([rep(q_norm_win, WIN_Q_HEADS) * qscale,
                            jnp.pad(rep(k_norm_win, WIN_KV_HEADS), ((0, 0), (0, wq - WIN_KV_HEADS * HEAD_DIM))),
                            rep(q_norm_na, NA_HEADS) * qscale, rep(k_norm_na, NA_HEADS)], axis=1)
    head_gains = jnp.pad(head_gains, ((0, 0), (0, 4), (0, 0)))
    sink_x = jnp.broadcast_to(jnp.repeat(sink_win.astype(F32) * LOG2E, WIN_BLOCK, axis=1)[:, :, None],
                              (sink_win.shape[0], WIN_Q_HEADS * WIN_BLOCK, LANES))
    bias_tab = _na_bias_table(rpb_na)

    for i in range(DEPTH):
        need_ctx = i < DEPTH - 1
        j = i // 2
        if i % 2 == 0:
            fr, fi, z, xbc, dtr = _inproj_even(h, i, j, gains_mix, mod_all, w_four, w_even_b, w_dt_b,
                                               n_batch, tiles_per_batch)

            br, bi = _dft_stage_a(fr.reshape(-1, n2, FNET_W), fi.reshape(-1, n2, FNET_W),
                                  m_a, twc, tws, n_batch, n2)
            f_lat = _dft_stage_b(br.reshape(n_batch, seq, FNET_W), bi.reshape(n_batch, seq, FNET_W),
                                 m_b, n_batch, n2).reshape(n_lat, FNET_W)
            f_ctx = _ctx_dft(fr, fi, m_c, n_batch, n_ctx, n_lat // n_ctx)

            y_lat, y_ctx = _ssd3(xbc, dtr, z, j, conv_w_p, conv_b_p, dtb_rows, alog_rows, dsk_x, ssd_ng, tri, emat,
                                 n_batch, seq, n_ctx)

            h = _outproj(h, i, mod_all, f_lat, y_lat, w_out_even_b, w_out_even_b, j, FNET_W,
                         n_batch, tiles_per_batch, 0, 0, 0, n_lat_tiles)
            h = _outproj(h, i, mod_all, f_ctx, y_ctx, w_out_even_b, w_out_even_b, j, FNET_W,
                         n_batch, tiles_per_batch, n_lat_tiles, 0, 0, n_ctx_tiles)
        else:
            qw, kw, vw, qn, kn, vn = _inproj_odd(h, i, j, gains_mix, mod_all, wq_perm, w_odd_b, cos_t, sin_t,
                                                 gmat_slab, gmat128, head_gains, n_batch, tiles_per_batch)
            o_win = _window_attention(qw, kw, vw, sink_x, j, n_batch, seq, n_ctx)
            o_na = _na_attention(qn, kn, vn, bias_tab, j, n_batch, seq, n_ctx)
            if need_ctx:
                oc_win, oc_na = _ctx_attention(qw, kw, vw, qn, kn, vn, sink_x, j, n_batch, seq, n_ctx)
            h = _outproj(h, i, mod_all, o_win, o_na, wo_win_perm, w_out_odd_b, j, wq,
                         n_batch, tiles_per_batch, 0, 0, 0, n_lat_tiles)
            if need_ctx:
                h = _outproj(h, i, mod_all, oc_win, oc_na, wo_win_perm, w_out_odd_b, j, wq,
                             n_batch, tiles_per_batch, n_lat_tiles, 0, 0, n_ctx_tiles)
        n_tiles = n_lat_tiles + n_ctx_tiles if need_ctx else n_lat_tiles
        h = _ffn(h, i, gains_ff, mod_all, w_ff1_b, w_ff2_b, n_batch, tiles_per_batch, n_tiles)
    return h.reshape(n_batch, seq, d)
```

```python
import functools
import math

import numpy as np
import jax
import jax.numpy as jnp
from jax import lax
from jax.experimental import pallas as pl
from jax.experimental.pallas import tpu as pltpu

F32 = jnp.float32
BF16 = jnp.bfloat16
HIGHEST = lax.Precision.HIGHEST

D_MODEL = 1024
DEPTH = 4
GRID_W = 64
D_FF = 4 * D_MODEL
NORM_EPS = 1e-6
NEG_INF = -1e30

FNET_GROUPS = 8
FNET_GROUP_W = 64
FNET_W = FNET_GROUPS * FNET_GROUP_W
SSD_HEAD_DIM = 64
SSD_HEADS = 24
SSD_GROUPS = 4
SSD_HPG = SSD_HEADS // SSD_GROUPS
SSD_STATE = 128
SSD_INNER = SSD_HEADS * SSD_HEAD_DIM
SSD_GROUP_W = SSD_HPG * SSD_HEAD_DIM
SSD_XBC_W = SSD_GROUP_W + 2 * SSD_STATE
SSD_CONV_DIM = SSD_INNER + 2 * SSD_GROUPS * SSD_STATE
SSD_CONV_W = 5
SSD_CHUNK = 128
HEAD_DIM = 64
WIN_Q_HEADS = 8
WIN_KV_HEADS = 2
WIN_GQA = WIN_Q_HEADS // WIN_KV_HEADS
WIN_BLOCK = 128
WIN_RADIUS = 128
NA_HEADS = 8
NA_ROWS = 8
NA_COLS = 16
ROPE_THETA = 10000.0
LOG2E = 1.4426950408889634

LANES = 128
MXU_DIM = 256
HALO = 8
ROW_TILE = 512
FF_TILE = 4096
DFT_N1 = 128
VMEM_LIMIT = 56 * 1024 * 1024


def _cparams(*sem):
    return pltpu.CompilerParams(dimension_semantics=sem, vmem_limit_bytes=VMEM_LIMIT)


def _silu(x):
    return x / (1.0 + jnp.exp2(x * -LOG2E))


def _softplus(x):
    return jnp.maximum(x, 0.0) + jnp.log(1.0 + jnp.exp(-jnp.abs(x)))


def _norm_mod(h, g, shift, scale):
    ms = jnp.mean(h * h, axis=-1, keepdims=True)
    y = h * lax.rsqrt(ms + NORM_EPS) * g
    return y * (1.0 + scale) + shift


def _dot(a, b):
    return jnp.dot(a, b, preferred_element_type=F32)


def _dot_nt(a, b):
    return lax.dot_general(a, b, (((1,), (1,)), ((), ())), preferred_element_type=F32)


def _mod_body(c_ref, w_ref, b_ref, o_ref):
    s = _silu(c_ref[...])
    o_ref[...] = jnp.dot(s, w_ref[...], precision=HIGHEST, preferred_element_type=F32) + b_ref[...]


def _modulation(cvec, w_mod, b_mod):
    depth, d, n = w_mod.shape
    tn = 1536
    return pl.pallas_call(
        _mod_body,
        grid=(depth, n // tn),
        in_specs=[pl.BlockSpec((8, d), lambda i, j: (0, 0)),
                  pl.BlockSpec((None, d, tn), lambda i, j: (i, 0, j)),
                  pl.BlockSpec((None, 1, tn), lambda i, j: (i, 0, j))],
        out_specs=pl.BlockSpec((None, 8, tn), lambda i, j: (i, 0, j)),
        out_shape=jax.ShapeDtypeStruct((depth, 8, n), F32),
        compiler_params=_cparams("parallel", "parallel"),
        name="modulation",
    )(cvec, w_mod, b_mod.reshape(depth, 1, n))


def _wprod_body(a_ref, b_ref, o_ref):
    o_ref[...] = jnp.dot(a_ref[...], b_ref[...], precision=HIGHEST,
                         preferred_element_type=F32).astype(o_ref.dtype)


def _weight_product(a, k, b):
    n, m, _ = a.shape
    p = b.shape[1]
    return pl.pallas_call(
        _wprod_body,
        grid=(n,),
        in_specs=[pl.BlockSpec((None, m, k), lambda i: (i, 0, 0)),
                  pl.BlockSpec((k, p), lambda i: (0, 0))],
        out_specs=pl.BlockSpec((None, m, p), lambda i: (i, 0, 0)),
        out_shape=jax.ShapeDtypeStruct((n, m, p), BF16),
        compiler_params=_cparams("parallel"),
        name="fold_channel_dft",
    )(a, b)


def _resident(shape):
    return pl.BlockSpec(shape, lambda *_: (0,) * len(shape), pipeline_mode=pl.Buffered(1))


def _layer(arr, j):
    nd = arr.ndim
    return pl.BlockSpec((None,) + arr.shape[1:], lambda *_: (j,) + (0,) * (nd - 1),
                        pipeline_mode=pl.Buffered(1))


def _mod_spec(layer, which, tiles_per_batch, n_batch, tile_off):
    def imap(m):
        return (layer, which, jnp.minimum((m + tile_off) // tiles_per_batch, n_batch), 0, 0)
    return pl.BlockSpec((None, None, None, 1, D_MODEL), imap)


def _inproj_even_body(h_ref, g_ref, sh_ref, sc_ref, wf_ref, w_ref, wdt_ref,
                      fr_ref, fi_ref, z_ref, xbc_ref, dt_ref):
    u = _norm_mod(h_ref[...], g_ref[...], sh_ref[...], sc_ref[...]).astype(BF16)
    fr_ref[...] = _dot(u, wf_ref[:, :FNET_W])
    fi_ref[...] = _dot(u, wf_ref[:, FNET_W:])
    z_ref[...] = _dot(u, w_ref[:, FNET_W:FNET_W + SSD_INNER]).astype(z_ref.dtype)
    c0 = FNET_W + SSD_INNER
    xbc = _dot(u, w_ref[:, c0:c0 + SSD_CONV_DIM])
    gn = SSD_GROUPS * SSD_STATE
    for g in range(SSD_GROUPS):
        o = g * SSD_XBC_W
        b0 = SSD_INNER + g * SSD_STATE
        xbc_ref[:, o:o + SSD_GROUP_W] = xbc[:, g * SSD_GROUP_W:(g + 1) * SSD_GROUP_W]
        xbc_ref[:, o + SSD_GROUP_W:o + SSD_GROUP_W + SSD_STATE] = xbc[:, b0:b0 + SSD_STATE]
        xbc_ref[:, o + SSD_GROUP_W + SSD_STATE:o + SSD_XBC_W] = xbc[:, b0 + gn:b0 + gn + SSD_STATE]
    dt_ref[...] = _dot(u, wdt_ref[...])


def _inproj_even(h, layer, j, gains, mod_all, w_four, w_main, w_dt, n_batch, tiles_per_batch):
    rows = h.shape[0]
    widths = (FNET_W, FNET_W, SSD_INNER, SSD_CONV_DIM, LANES)
    dtypes = (F32, F32, BF16, F32, F32)
    tm = ROW_TILE
    return pl.pallas_call(
        _inproj_even_body,
        grid=(rows // tm,),
        in_specs=[pl.BlockSpec((tm, D_MODEL), lambda m: (m, 0)),
                  _layer(gains, layer),
                  _mod_spec(layer, 0, tiles_per_batch, n_batch, 0),
                  _mod_spec(layer, 1, tiles_per_batch, n_batch, 0),
                  _layer(w_four, j), _layer(w_main, j), _layer(w_dt, j)],
        out_specs=[pl.BlockSpec((tm, wd), lambda m: (m, 0)) for wd in widths],
        out_shape=[jax.ShapeDtypeStruct((rows, wd), dt) for wd, dt in zip(widths, dtypes)],
        compiler_params=_cparams("parallel"),
        name="inproj_even",
    )(h, gains, mod_all, mod_all, w_four, w_main, w_dt)


def _group_rms(x, gmat_ref, gain):
    sq = x * x
    hi = sq.astype(BF16)
    lo = (sq - hi.astype(F32)).astype(BF16)
    gw = gmat_ref.shape[0]
    parts = []
    for c0 in range(0, x.shape[1], gw):
        parts.append(_dot(hi[:, c0:c0 + gw], gmat_ref[...]) + _dot(lo[:, c0:c0 + gw], gmat_ref[...]))
    ms = parts[0] if len(parts) == 1 else jnp.concatenate(parts, axis=1)
    return x * lax.rsqrt(ms + NORM_EPS) * gain


def _rope(y, cos, sin_signed):
    w = y.shape[-1]
    reps = w // LANES
    lane = lax.broadcasted_iota(jnp.int32, y.shape, 1)
    first = (lane % 32) < 16
    partner = jnp.where(first, pltpu.roll(y, w - 16, 1), pltpu.roll(y, 16, 1))
    if reps > 1:
        cos = jnp.concatenate([cos] * reps, axis=1)
        sin_signed = jnp.concatenate([sin_signed] * reps, axis=1)
    return y * cos + partner * sin_signed


def _inproj_odd_body(h_ref, g_ref, sh_ref, sc_ref, wq_ref, w_ref, cos_ref, sin_ref, gslab_ref, g128_ref,
                     hg_ref, qw_ref, kw_ref, vw_ref, qn_ref, kn_ref, vn_ref):
    u = _norm_mod(h_ref[...], g_ref[...], sh_ref[...], sc_ref[...]).astype(BF16)
    cos = cos_ref[...]
    sin = sin_ref[...]
    wq = WIN_Q_HEADS * HEAD_DIM
    wk = WIN_KV_HEADS * HEAD_DIM
    nh = NA_HEADS * HEAD_DIM
    qw = _group_rms(_dot(u, wq_ref[...]), gslab_ref, hg_ref[0:1, :])
    qw_ref[...] = _rope(qw, cos, sin).astype(BF16)
    kw = _group_rms(_dot(u, w_ref[:, wq:wq + wk]), g128_ref, hg_ref[1:2, :wk])
    kw_ref[...] = _rope(kw, cos, sin).astype(BF16)
    vw_ref[...] = _dot(u, w_ref[:, wq + wk:wq + 2 * wk]).astype(BF16)
    c0 = wq + 2 * wk
    qn_ref[...] = _group_rms(_dot(u, w_ref[:, c0:c0 + nh]), gslab_ref, hg_ref[2:3, :]).astype(BF16)
    kn_ref[...] = _group_rms(_dot(u, w_ref[:, c0 + nh:c0 + 2 * nh]), gslab_ref, hg_ref[3:4, :]).astype(BF16)
    vn_ref[...] = _dot(u, w_ref[:, c0 + 2 * nh:c0 + 3 * nh]).astype(BF16)


def _inproj_odd(h, layer, j, gains, mod_all, wq_perm, w_main, cos_t, sin_t, gmat_slab, gmat128, head_gains,
                n_batch, tiles_per_batch):
    rows = h.shape[0]
    tm = ROW_TILE
    widths = (512, 128, 128, 512, 512, 512)
    n_lat_tiles = n_batch * tiles_per_batch

    def rope_map(m):
        return (jnp.where(m < n_lat_tiles, m % tiles_per_batch, tiles_per_batch), 0)

    return pl.pallas_call(
        _inproj_odd_body,
        grid=(rows // tm,),
        in_specs=[pl.BlockSpec((tm, D_MODEL), lambda m: (m, 0)),
                  _layer(gains, layer),
                  _mod_spec(layer, 0, tiles_per_batch, n_batch, 0),
                  _mod_spec(layer, 1, tiles_per_batch, n_batch, 0),
                  _layer(wq_perm, j), _layer(w_main, j),
                  pl.BlockSpec((tm, LANES), rope_map),
                  pl.BlockSpec((tm, LANES), rope_map),
                  _resident(gmat_slab.shape), _resident(gmat128.shape), _layer(head_gains, j)],
        out_specs=[pl.BlockSpec((tm, wd), lambda m: (m, 0)) for wd in widths],
        out_shape=[jax.ShapeDtypeStruct((rows, wd), BF16) for wd in widths],
        compiler_params=_cparams("parallel"),
        name="inproj_odd",
    )(h, gains, mod_all, mod_all, wq_perm, w_main, cos_t, sin_t, gmat_slab, gmat128, head_gains)


def _outproj_body(k1, k2, b_row0, h_ref, gate_ref, a1_ref, a2_ref, wa_ref, wb_ref, o_ref):
    acc = _dot(a1_ref[...].astype(BF16), wa_ref[0:k1, :])
    acc += _dot(a2_ref[...].astype(BF16), wb_ref[b_row0:b_row0 + k2, :])
    o_ref[...] = h_ref[...] + gate_ref[...] * acc


def _outproj(h, layer, mod_all, a1, a2, wa, wb, j, b_row0, n_batch, tiles_per_batch, tile_off, a1_off,
             a2_off, n_tiles):
    tm = ROW_TILE
    k1, k2 = a1.shape[1], a2.shape[1]
    return pl.pallas_call(
        functools.partial(_outproj_body, k1, k2, b_row0),
        grid=(n_tiles,),
        in_specs=[pl.BlockSpec((tm, D_MODEL), lambda m: (m + tile_off, 0)),
                  _mod_spec(layer, 2, tiles_per_batch, n_batch, tile_off),
                  pl.BlockSpec((tm, k1), lambda m: (m + a1_off, 0)),
                  pl.BlockSpec((tm, k2), lambda m: (m + a2_off, 0)),
                  _layer(wa, j), _layer(wb, j)],
        out_specs=pl.BlockSpec((tm, D_MODEL), lambda m: (m + tile_off, 0)),
        out_shape=jax.ShapeDtypeStruct(h.shape, F32),
        input_output_aliases={0: 0},
        compiler_params=_cparams("parallel"),
        name="outproj",
    )(h, mod_all, a1, a2, wa, wb)


def _ffn_body(h_ref, g_ref, sh_ref, sc_ref, gate_ref, w1_ref, w2_ref, o_ref):
    h = h_ref[...]
    u = _norm_mod(h, g_ref[...], sh_ref[...], sc_ref[...]).astype(BF16)
    acc = None
    for k in range(D_FF // FF_TILE):
        sl = slice(k * FF_TILE, (k + 1) * FF_TILE)
        a = jnp.maximum(_dot(u, w1_ref[:, sl]), 0.0)
        t = _dot((a * a).astype(BF16), w2_ref[sl, :])
        acc = t if acc is None else acc + t
    o_ref[...] = h + gate_ref[...] * acc


def _ffn(h, layer, gains, mod_all, w1, w2, n_batch, tiles_per_batch, n_tiles):
    tm = ROW_TILE
    return pl.pallas_call(
        _ffn_body,
        grid=(n_tiles,),
        in_specs=[pl.BlockSpec((tm, D_MODEL), lambda m: (m, 0)),
                  _layer(gains, layer),
                  _mod_spec(layer, 3, tiles_per_batch, n_batch, 0),
                  _mod_spec(layer, 4, tiles_per_batch, n_batch, 0),
                  _mod_spec(layer, 5, tiles_per_batch, n_batch, 0),
                  _layer(w1, layer), _layer(w2, layer)],
        out_specs=pl.BlockSpec((tm, D_MODEL), lambda m: (m, 0)),
        out_shape=jax.ShapeDtypeStruct((n_tiles * tm, D_MODEL), F32),
        compiler_params=_cparams("parallel"),
        name="ffn",
    )(h, gains, mod_all, mod_all, mod_all, w1, w2)


def _dft_a_body(xr_ref, xi_ref, m_ref, c_ref, s_ref, br_ref, bi_ref):
    n1 = DFT_N1
    reps = FNET_W // LANES
    for j in range(xr_ref.shape[1]):
        x = jnp.concatenate([xr_ref[:, j, :], xi_ref[:, j, :]], axis=0).astype(BF16)
        a = _dot(m_ref[...], x)
        ar, ai = a[:n1], a[n1:]
        c = jnp.concatenate([c_ref[:, LANES * j:LANES * (j + 1)]] * reps, axis=1)
        s = jnp.concatenate([s_ref[:, LANES * j:LANES * (j + 1)]] * reps, axis=1)
        br_ref[:, j, :] = ar * c + ai * s
        bi_ref[:, j, :] = ai * c - ar * s


def _dft_stage_a(xr, xi, mmat, twc, tws, n_batch, n2):
    n1 = DFT_N1
    lb = 8
    blk = pl.BlockSpec((n1, lb, FNET_W), lambda b, j: (b, j, 0))
    return pl.pallas_call(
        _dft_a_body,
        grid=(n_batch, n2 // lb),
        in_specs=[blk, blk,
                  pl.BlockSpec((2 * n1, 2 * n1), lambda b, j: (0, 0)),
                  pl.BlockSpec((n1, lb * LANES), lambda b, j: (0, j)),
                  pl.BlockSpec((n1, lb * LANES), lambda b, j: (0, j))],
        out_specs=[blk, blk],
        out_shape=[jax.ShapeDtypeStruct((n_batch * n1, n2, FNET_W), F32)] * 2,
        compiler_params=_cparams("parallel", "parallel"),
        name="seq_dft_stage_a",
    )(xr, xi, mmat, twc, tws)


def _dft_b_body(br_ref, bi_ref, m_ref, o_ref):
    x = jnp.concatenate([br_ref[...], bi_ref[...]], axis=0).astype(BF16)
    res = _dot(m_ref[...], x)
    o_ref[...] = res.reshape(o_ref.shape)


def _dft_stage_b(br, bi, mmat, n_batch, n2):
    n1 = DFT_N1
    kb = 8
    return pl.pallas_call(
        _dft_b_body,
        grid=(n_batch, n1 // kb),
        in_specs=[pl.BlockSpec((None, kb * n2, FNET_W), lambda b, j: (b, j, 0)),
                  pl.BlockSpec((None, kb * n2, FNET_W), lambda b, j: (b, j, 0)),
                  pl.BlockSpec(mmat.shape, lambda b, j: (0, 0))],
        out_specs=pl.BlockSpec((None, n2, kb, FNET_W), lambda b, j: (b, 0, j, 0)),
        out_shape=jax.ShapeDtypeStruct((n_batch, n2, n1, FNET_W), F32),
        compiler_params=_cparams("parallel", "parallel"),
        name="seq_dft_stage_b",
    )(br, bi, mmat)


def _ctx_dft_body(xr_ref, xi_ref, m_ref, o_ref):
    x = jnp.concatenate([xr_ref[...], xi_ref[...]], axis=0).astype(BF16)
    o_ref[...] = _dot(m_ref[...], x)


def _ctx_dft(fr, fi, mmat, n_batch, n_ctx, row_block_off):
    return pl.pallas_call(
        _ctx_dft_body,
        grid=(n_batch,),
        in_specs=[pl.BlockSpec((n_ctx, FNET_W), lambda b: (row_block_off + b, 0)),
                  pl.BlockSpec((n_ctx, FNET_W), lambda b: (row_block_off + b, 0)),
                  pl.BlockSpec(mmat.shape, lambda b: (0, 0))],
        out_specs=pl.BlockSpec((n_ctx, FNET_W), lambda b: (b, 0)),
        out_shape=jax.ShapeDtypeStruct((n_batch * n_ctx, FNET_W), F32),
        compiler_params=_cparams("parallel"),
        name="ctx_dft",
    )(fr, fi, mmat)


def _dft_tables(seq, n_ctx):
    n1 = DFT_N1
    n2 = seq // n1
    k1 = np.arange(n1)
    ang1 = 2.0 * np.pi * np.outer(k1, k1) / n1
    c1, s1 = np.cos(ang1), np.sin(ang1)
    m_a = np.block([[c1, s1], [-s1, c1]])
    ang_t = 2.0 * np.pi * np.outer(k1, np.arange(n2)) / seq
    twc = np.repeat(np.cos(ang_t), LANES, axis=1)
    tws = np.repeat(np.sin(ang_t), LANES, axis=1)
    k2 = np.arange(n2)
    ang2 = 2.0 * np.pi * np.outer(k2, k2) / n2
    scale = 1.0 / math.sqrt(seq)
    c2, s2 = np.cos(ang2) * scale, np.sin(ang2) * scale
    eye = np.eye(8)
    m_b = np.concatenate([np.einsum("kl,ab->kabl", c2, eye).reshape(n2 * 8, 8 * n2),
                          np.einsum("kl,ab->kabl", s2, eye).reshape(n2 * 8, 8 * n2)], axis=1)
    kc = np.arange(n_ctx)
    angc = 2.0 * np.pi * np.outer(kc, kc) / n_ctx
    m_c = np.concatenate([np.cos(angc), np.sin(angc)], axis=1) / math.sqrt(n_ctx)
    ch = np.arange(FNET_GROUP_W)
    angg = 2.0 * np.pi * np.outer(ch, ch) / FNET_GROUP_W
    eg = np.eye(FNET_GROUPS)
    chan = np.concatenate([np.kron(eg, np.cos(angg)), -np.kron(eg, np.sin(angg))], axis=1)
    chan = chan / math.sqrt(FNET_GROUP_W)
    return (jnp.asarray(m_a, BF16), jnp.asarray(twc, F32), jnp.asarray(tws, F32),
            jnp.asarray(m_b, BF16), jnp.asarray(m_c, BF16), jnp.asarray(chan, F32))


def _split_dot_rhs(w_bf16, x):
    hi = x.astype(BF16)
    r1 = x - hi.astype(F32)
    mid = r1.astype(BF16)
    lo = (r1 - mid.astype(F32)).astype(BF16)
    return _dot(w_bf16, hi) + _dot(w_bf16, mid) + _dot(w_bf16, lo)


def _split2_dot(x, w_bf16):
    hi = x.astype(BF16)
    lo = (x - hi.astype(F32)).astype(BF16)
    return _dot(hi, w_bf16) + _dot(lo, w_bf16)


SSD_LAT_CPS = 8


def _ssd_local(xfull, dt_raw, g, pos0, n_chunks, cw_ref, cb_ref, dtb_ref, alog_ref, dsk_ref, tri_ref, e_ref,
               yloc_ref, s_ref, c_ref, ea_ref, cd_ref):
    q = SSD_CHUNK
    gw = SSD_GROUP_W
    rows_blk = n_chunks * q
    conv = cb_ref[...]
    for k in range(SSD_CONV_W):
        shift = (SSD_CONV_W // 2 - k) % xfull.shape[0]
        xk = xfull if shift == 0 else pltpu.roll(xfull, shift, 0)
        conv = conv + xk[HALO:HALO + rows_blk, :] * cw_ref[k:k + 1, :]
    act = _silu(conv)
    xs = act[:, :gw]
    bm = act[:, gw:gw + SSD_STATE]
    cm = act[:, gw + SSD_STATE:]

    lane = lax.broadcasted_iota(jnp.int32, (1, LANES), 1)
    is_fwd = lane < SSD_HPG
    dsel = pltpu.roll(dt_raw, (LANES - SSD_HPG * g) % LANES, 1)
    dtv = _softplus(dsel + dtb_ref[pl.ds(g, 1), :])
    head_lane = is_fwd | ((lane >= SSD_HEADS) & (lane < SSD_HEADS + SSD_HPG))
    a_row = jnp.where(head_lane, -jnp.exp(alog_ref[pl.ds(g, 1), :]) * LOG2E, 0.0)
    adt = dtv * a_row
    tri = tri_ref[...]
    tri16 = tri.astype(BF16)
    allowed = (tri[:q] > 0.0, tri[q:] > 0.0)
    emat = e_ref[...]
    lane_q = lax.broadcasted_iota(jnp.int32, (q, LANES), 1)
    lo_half = lane_q < SSD_HEAD_DIM
    for c0 in range(0, n_chunks, 2):
        pair = range(c0, min(c0 + 2, n_chunks))
        prep = {}
        for cc in pair:
            rows = slice(cc * q, (cc + 1) * q)
            cums = _split_dot_rhs(tri16, adt[rows])
            acum = jnp.where(is_fwd, cums[:q], cums[q:])
            end = jnp.where(is_fwd, acum[q - 1:q, :], acum[0:1, :])
            dte = jnp.exp2(end - acum)
            ex = _dot(jnp.concatenate([dtv[rows], dte], axis=0).astype(BF16), emat)
            cdx = _split2_dot(jnp.broadcast_to(jnp.exp2(end), (16, LANES)), emat)
            c16 = cm[rows].astype(BF16)
            b_c = bm[rows]
            cbm = _dot_nt(c16, b_c.astype(BF16))
            prep[cc] = (acum, acum.T, ex, cdx, c16, b_c, cbm)
        for cc in pair:
            rows = slice(cc * q, (cc + 1) * q)
            acum, acum_t, ex, cdx, c16, b_c, cbm = prep[cc]
            xs_c = xs[rows]
            xdt2 = jnp.concatenate([xs_c, xs_c], axis=1) * ex[:q]
            ys = []
            for m in range(SSD_HPG // 2):
                mms, xhs = [], []
                for d in range(2):
                    xpair = xdt2[:, gw * d + LANES * m:gw * d + LANES * (m + 1)]
                    for half in range(2):
                        ln = SSD_HEADS * d + 2 * m + half
                        seg = acum[:, ln:ln + 1] - acum_t[ln:ln + 1, :]
                        dec = jnp.exp2(jnp.where(allowed[d], seg, NEG_INF))
                        mms.append((cbm * dec).astype(BF16))
                        xhs.append(jnp.where(lo_half if half == 0 else ~lo_half, xpair, 0.0).astype(BF16))
                ys.append(_dot(jnp.concatenate(mms, axis=1), jnp.concatenate(xhs, axis=0)))
            y_loc = jnp.concatenate(ys, axis=1) + dsk_ref[...] * xs_c
            sts = _dot(b_c.T.astype(BF16), (xdt2 * ex[q:2 * q]).astype(BF16))
            pos = pos0 + cc
            r0 = pl.multiple_of(pos * q, q)
            yloc_ref[pl.ds(r0, q), :] = y_loc
            s_ref[pl.ds(r0, q), :] = sts.astype(BF16)
            c_ref[pl.ds(r0, q), :] = c16
            ea_ref[pl.ds(r0, q), :] = jnp.exp2(acum).astype(BF16)
            cd_ref[pl.ds(pl.multiple_of(pos * 8, 8), 8), :] = cdx[0:8]


def _ssd_emit(z_ref, o_ref, pos0, n_chunks, hf, ng_ref, e_ref, yloc_ref, s_ref, c_ref, ea_ref, cd_ref, hbe_ref):
    q = SSD_CHUNK
    gw = SSD_GROUP_W
    emat = e_ref[...]
    starts = [pl.multiple_of((pos0 + cc) * q, q) for cc in range(n_chunks)]
    ea_x = [_dot(ea_ref[pl.ds(r0, q), :], emat) for r0 in starts]
    gate = [_silu(z_ref[cc * q:(cc + 1) * q, :].astype(F32)) for cc in range(n_chunks)]
    for cc in range(n_chunks):
        r0 = starts[cc]
        hcat = jnp.concatenate([hf.astype(BF16), hbe_ref[pl.ds(r0, q), :]], axis=1)
        yo = _dot(c_ref[pl.ds(r0, q), :], hcat) * ea_x[cc]
        y = yloc_ref[pl.ds(r0, q), :] + yo[:, :gw] + yo[:, gw:]
        cd = cd_ref[pl.ds(pl.multiple_of((pos0 + cc) * 8, 8), 8), :][0:1, :gw]
        hf = hf * cd + s_ref[pl.ds(r0, q), :gw].astype(F32)
        y = y * gate[cc]
        y = y * lax.rsqrt(jnp.mean(y * y, axis=-1, keepdims=True) + NORM_EPS) * ng_ref[...]
        o_ref[cc * q:(cc + 1) * q, :] = y.astype(o_ref.dtype)
    return hf


def _ssd3_body(nb, ncc, xm_ref, xp_ref, xn_ref, dt_ref, z_ref, xc_ref, dtc_ref, zc_ref, cw_ref, cb_ref, dtb_ref,
               alog_ref, dsk_ref, ng_ref, tri_ref, e_ref, o_ref, oc_ref,
               yloc_ref, s_ref, c_ref, ea_ref, cd_ref, hbe_ref, hf_ref, hb_ref):
    g = pl.program_id(1)
    t = pl.program_id(2)
    q = SSD_CHUNK
    gw = SSD_GROUP_W
    ns = ncc + nb * SSD_LAT_CPS
    local_refs = (cw_ref, cb_ref, dtb_ref, alog_ref, dsk_ref, tri_ref, e_ref, yloc_ref, s_ref, c_ref, ea_ref,
                  cd_ref)
    emit_refs = (ng_ref, e_ref, yloc_ref, s_ref, c_ref, ea_ref, cd_ref, hbe_ref)

    @pl.when(t == 0)
    def _local_context():
        pad = jnp.zeros((HALO, SSD_XBC_W), F32)
        _ssd_local(jnp.concatenate([pad, xc_ref[...], pad], axis=0), dtc_ref[...], g, 0, ncc, *local_refs)

    @pl.when(t < nb)
    def _local_latent():
        prev = jnp.where(t == 0, 0.0, xp_ref[...])
        nxt = jnp.where(t == nb - 1, 0.0, xn_ref[...])
        xfull = jnp.concatenate([prev, xm_ref[...], nxt], axis=0)
        _ssd_local(xfull, dt_ref[...], g, ncc + SSD_LAT_CPS * t, SSD_LAT_CPS, *local_refs)

    @pl.when(t == nb)
    def _backward_states_and_context():
        hb_ref[...] = jnp.zeros_like(hb_ref)

        def body(i, carry):
            pos = jnp.where(i < ncc, ncc - 1 - i, ns - 1 + ncc - i)
            r0 = pl.multiple_of(pos * q, q)
            hb = hb_ref[...]
            hbe_ref[pl.ds(r0, q), :] = hb.astype(BF16)
            cd = cd_ref[pl.ds(pl.multiple_of(pos * 8, 8), 8), :][0:1, gw:]
            hb_ref[...] = hb * cd + s_ref[pl.ds(r0, q), gw:].astype(F32)
            return carry

        lax.fori_loop(0, ns, body, 0)
        hf_ref[...] = _ssd_emit(zc_ref, oc_ref, 0, ncc, jnp.zeros(hf_ref.shape, F32), *emit_refs)

    @pl.when(t >= nb)
    def _emit_latent():
        hf_ref[...] = _ssd_emit(z_ref, o_ref, ncc + SSD_LAT_CPS * (t - nb), SSD_LAT_CPS, hf_ref[...], *emit_refs)


def _ssd3(xbc, dt, z, j, conv_w, conv_b, dtb_tab, alog_tab, dsk_x, ng, tri, emat, n_batch, seq, n_ctx):
    rows = xbc.shape[0]
    q = SSD_CHUNK
    blk = SSD_LAT_CPS * q
    assert n_ctx % q == 0 and seq % blk == 0 and (n_batch * seq) % n_ctx == 0
    ncc = n_ctx // q
    nb = seq // blk
    ns = ncc + nb * SSD_LAT_CPS
    per = blk // HALO
    ctx_blk0 = n_batch * seq // n_ctx
    last_halo = rows // HALO - 1
    gw = SSD_GROUP_W

    def rb_in(b, t):
        return b * nb + jnp.minimum(t, nb - 1)

    def rb_out(b, t):
        return b * nb + jnp.maximum(t - nb, 0)

    return pl.pallas_call(
        functools.partial(_ssd3_body, nb, ncc),
        grid=(n_batch, SSD_GROUPS, 2 * nb),
        in_specs=[pl.BlockSpec((blk, SSD_XBC_W), lambda b, g, t: (rb_in(b, t), g)),
                  pl.BlockSpec((HALO, SSD_XBC_W),
                               lambda b, g, t: (jnp.maximum(rb_in(b, t) * per - 1, 0), g)),
                  pl.BlockSpec((HALO, SSD_XBC_W),
                               lambda b, g, t: (jnp.minimum(rb_in(b, t) * per + per, last_halo), g)),
                  pl.BlockSpec((blk, LANES), lambda b, g, t: (rb_in(b, t), 0)),
                  pl.BlockSpec((blk, gw), lambda b, g, t: (rb_out(b, t), g)),
                  pl.BlockSpec((n_ctx, SSD_XBC_W), lambda b, g, t: (ctx_blk0 + b, g)),
                  pl.BlockSpec((n_ctx, LANES), lambda b, g, t: (ctx_blk0 + b, 0)),
                  pl.BlockSpec((n_ctx, gw), lambda b, g, t: (ctx_blk0 + b, g)),
                  pl.BlockSpec((None, SSD_CONV_W, SSD_XBC_W), lambda b, g, t: (j, 0, g)),
                  pl.BlockSpec((None, 1, SSD_XBC_W), lambda b, g, t: (j, 0, g)),
                  pl.BlockSpec((None, 8, LANES), lambda b, g, t: (j, 0, 0)),
                  pl.BlockSpec((None, 8, LANES), lambda b, g, t: (j, 0, 0)),
                  pl.BlockSpec((None, 1, gw), lambda b, g, t: (j, 0, g)),
                  pl.BlockSpec((None, 1, gw), lambda b, g, t: (j, 0, g)),
                  pl.BlockSpec((2 * q, q), lambda b, g, t: (0, 0)),
                  pl.BlockSpec((LANES, 2 * gw), lambda b, g, t: (0, 0))],
        out_specs=[pl.BlockSpec((blk, gw), lambda b, g, t: (rb_out(b, t), g)),
                   pl.BlockSpec((n_ctx, gw), lambda b, g, t: (b, g))],
        out_shape=[jax.ShapeDtypeStruct((n_batch * seq, SSD_INNER), BF16),
                   jax.ShapeDtypeStruct((n_batch * n_ctx, SSD_INNER), BF16)],
        scratch_shapes=[pltpu.VMEM((ns * q, gw), F32),
                        pltpu.VMEM((ns * q, 2 * gw), BF16),
                        pltpu.VMEM((ns * q, SSD_STATE), BF16),
                        pltpu.VMEM((ns * q, LANES), BF16),
                        pltpu.VMEM((ns * 8, 2 * gw), F32),
                        pltpu.VMEM((ns * q, gw), BF16),
                        pltpu.VMEM((SSD_STATE, gw), F32),
                        pltpu.VMEM((SSD_STATE, gw), F32)],
        compiler_params=_cparams("arbitrary", "arbitrary", "arbitrary"),
        name="ssd_bidir",
    )(xbc, xbc, xbc, dt, z, xbc, dt, z, conv_w, conv_b, dtb_tab, alog_tab, dsk_x, ng, tri, emat)


def _stack_heads(qc):
    lane = lax.broadcasted_iota(jnp.int32, qc.shape, 1)
    lo = lane < HEAD_DIM
    zero = jnp.zeros_like(qc)
    return jnp.concatenate([jnp.where(lo, qc, zero), jnp.where(lo, zero, qc)], axis=0)


def _unstack_heads(o, n):
    lane = lax.broadcasted_iota(jnp.int32, (n, LANES), 1)
    return jnp.where(lane < HEAD_DIM, o[:n], o[n:])


def _softmax_pv(scores, values, sink=None):
    m = None
    for sc in scores:
        mx = jnp.max(sc, axis=-1, keepdims=True)
        m = mx if m is None else jnp.maximum(m, mx)
    den = None
    if sink is not None:
        m = jnp.maximum(m, jnp.max(sink, axis=-1, keepdims=True))
        den = jnp.sum(jnp.exp2(sink - m), axis=-1, keepdims=True) * (1.0 / sink.shape[-1])
    acc = None
    for sc, v in zip(scores, values):
        pr = jnp.exp2(sc - m)
        sm = jnp.sum(pr, axis=-1, keepdims=True)
        den = sm if den is None else den + sm
        t = _dot(pr.astype(BF16), v)
        acc = t if acc is None else acc + t
    return acc * (1.0 / den)


WIN_BLOCKS_PER_STEP = 2


def _win_body(seq, q_ref, k_ref, v_ref, kc_ref, vc_ref, sink_ref, o_ref):
    wb = WIN_BLOCK
    r = lax.broadcasted_iota(jnp.int32, (wb, 3 * wb), 0)
    c = lax.broadcasted_iota(jnp.int32, (wb, 3 * wb), 1)
    for i in range(WIN_BLOCKS_PER_STEP):
        n = pl.program_id(1) * WIN_BLOCKS_PER_STEP + i
        qrows = slice(i * wb, (i + 1) * wb)
        start = pl.multiple_of(jnp.clip((n - 1) * wb, 0, seq - 3 * wb), wb)
        kb = k_ref[pl.ds(start, 3 * wb), :]
        vb = v_ref[pl.ds(start, 3 * wb), :]
        dist = (c + start) - (r + n * wb)
        pen = jnp.where(jnp.abs(dist) <= WIN_RADIUS, 0.0, NEG_INF)
        pen = jnp.concatenate([pen, pen], axis=0)
        scores = []
        for m in range(WIN_GQA):
            qs = _stack_heads(q_ref[qrows, LANES * m:LANES * (m + 1)])
            scores.append([_dot_nt(qs, kb) + pen, _dot_nt(qs, kc_ref[...])])
        for m in range(WIN_GQA):
            o = _softmax_pv(scores[m], [vb, vc_ref[...]], sink_ref[2 * m * wb:2 * (m + 1) * wb, :])
            o_ref[qrows, LANES * m:LANES * (m + 1)] = _unstack_heads(o, wb).astype(o_ref.dtype)


def _window_attention(qw, kw, vw, sink_x, j, n_batch, seq, n_ctx):
    wb = WIN_BLOCK
    assert seq // wb >= 3 and seq % (wb * WIN_BLOCKS_PER_STEP) == 0
    qrows = wb * WIN_BLOCKS_PER_STEP
    nb = seq // qrows
    ctx_blk0 = n_batch * seq // n_ctx
    q_spec = pl.BlockSpec((qrows, 512), lambda b, n: (b * nb + n, 0))
    seq_spec = pl.BlockSpec((seq, LANES), lambda b, n: (b, 0))
    cspec = pl.BlockSpec((n_ctx, LANES), lambda b, n: (ctx_blk0 + b, 0))
    return pl.pallas_call(
        functools.partial(_win_body, seq),
        grid=(n_batch, nb),
        in_specs=[q_spec, seq_spec, seq_spec, cspec, cspec, _layer(sink_x, j)],
        out_specs=pl.BlockSpec((qrows, 512), lambda b, n: (b * nb + n, 0)),
        out_shape=jax.ShapeDtypeStruct((n_batch * seq, 512), BF16),
        compiler_params=_cparams("parallel", "arbitrary"),
        name="window_attention",
    )(qw, kw, vw, kw, vw, sink_x)


NA_ROWS_PER_STEP = 4


def _na_body(n_rows, q_ref, k_ref, v_ref, kc_ref, vc_ref, bias_ref, o_ref):
    kr = NA_ROWS
    for j in range(NA_ROWS_PER_STEP):
        r = pl.program_id(1) * NA_ROWS_PER_STEP + j
        first_row = jnp.clip(r - kr // 2, 0, n_rows - kr)
        start = pl.multiple_of(first_row * GRID_W, GRID_W)
        ro0 = first_row - r + NA_ROWS - 1
        rows = slice(j * GRID_W, (j + 1) * GRID_W)
        scores = []
        for m in range(NA_HEADS // 2):
            sl = slice(LANES * m, LANES * (m + 1))
            qs = _stack_heads(q_ref[rows, sl])
            kw = k_ref[pl.ds(start, kr * GRID_W), sl]
            s_loc = _dot_nt(qs, kw) + bias_ref[ro0, 2 * GRID_W * m:2 * GRID_W * (m + 1), :]
            scores.append((s_loc, _dot_nt(qs, kc_ref[:, sl])))
        for m in range(NA_HEADS // 2):
            sl = slice(LANES * m, LANES * (m + 1))
            vw = v_ref[pl.ds(start, kr * GRID_W), sl]
            o = _softmax_pv(list(scores[m]), [vw, vc_ref[:, sl]])
            o_ref[rows, sl] = _unstack_heads(o, GRID_W).astype(o_ref.dtype)


def _na_attention(qn, kn, vn, bias_tab, j, n_batch, seq, n_ctx):
    n_rows = seq // GRID_W
    assert n_rows >= NA_ROWS and n_rows % NA_ROWS_PER_STEP == 0
    n_steps = n_rows // NA_ROWS_PER_STEP
    qrows = NA_ROWS_PER_STEP * GRID_W
    ctx_blk0 = n_batch * seq // n_ctx
    return pl.pallas_call(
        functools.partial(_na_body, n_rows),
        grid=(n_batch, n_steps),
        in_specs=[pl.BlockSpec((qrows, 512), lambda b, r: (b * n_steps + r, 0)),
                  pl.BlockSpec((seq, 512), lambda b, r: (b, 0)),
                  pl.BlockSpec((seq, 512), lambda b, r: (b, 0)),
                  pl.BlockSpec((n_ctx, 512), lambda b, r: (ctx_blk0 + b, 0)),
                  pl.BlockSpec((n_ctx, 512), lambda b, r: (ctx_blk0 + b, 0)),
                  _layer(bias_tab, j)],
        out_specs=pl.BlockSpec((qrows, 512), lambda b, r: (b * n_steps + r, 0)),
        out_shape=jax.ShapeDtypeStruct((n_batch * seq, 512), BF16),
        compiler_params=_cparams("parallel", "arbitrary"),
        name="neighbourhood_attention",
    )(qn, kn, vn, kn, vn, bias_tab)


def _ctx_attn_body(qw_ref, kw_ref, vw_ref, qn_ref, kn_ref, vn_ref, sink_ref, ow_ref, on_ref):
    n = qw_ref.shape[0]
    qw = qw_ref[...]
    lane = lax.broadcasted_iota(jnp.int32, (n, LANES), 1)
    lo = lane < HEAD_DIM
    for m in range(WIN_GQA):
        qs = _stack_heads(qw[:, LANES * m:LANES * (m + 1)])
        r0, r1 = 2 * m * WIN_BLOCK, (2 * m + 1) * WIN_BLOCK
        sink = jnp.concatenate([jnp.broadcast_to(sink_ref[r0:r0 + 1, :], (n, LANES)),
                                jnp.broadcast_to(sink_ref[r1:r1 + 1, :], (n, LANES))], axis=0)
        o = _softmax_pv([_dot_nt(qs, kw_ref[...])], [vw_ref[...]], sink)
        ow_ref[:, LANES * m:LANES * (m + 1)] = jnp.where(lo, o[:n], o[n:]).astype(ow_ref.dtype)
    qn = qn_ref[...]
    for m in range(NA_HEADS // 2):
        sl = slice(LANES * m, LANES * (m + 1))
        qs = _stack_heads(qn[:, sl])
        o = _softmax_pv([_dot_nt(qs, kn_ref[:, sl])], [vn_ref[:, sl]])
        on_ref[:, sl] = _unstack_heads(o, n).astype(on_ref.dtype)


def _ctx_attention(qw, kw, vw, qn, kn, vn, sink_x, j, n_batch, seq, n_ctx):
    blk0 = n_batch * seq // n_ctx
    wide = pl.BlockSpec((n_ctx, 512), lambda b: (blk0 + b, 0))
    narrow = pl.BlockSpec((n_ctx, LANES), lambda b: (blk0 + b, 0))
    out = pl.BlockSpec((n_ctx, 512), lambda b: (b, 0))
    return pl.pallas_call(
        _ctx_attn_body,
        grid=(n_batch,),
        in_specs=[wide, narrow, narrow, wide, wide, wide,
                  _layer(sink_x, j)],
        out_specs=[out, out],
        out_shape=[jax.ShapeDtypeStruct((n_batch * n_ctx, 512), BF16)] * 2,
        compiler_params=_cparams("parallel"),
        name="context_attention",
    )(qw, kw, vw, qn, kn, vn, sink_x)


def _take(w, runs, axis):
    parts = []
    for run in runs:
        if run[0] is None:
            shape = list(w.shape)
            shape[axis] = run[1]
            parts.append(jnp.zeros(shape, w.dtype))
        else:
            parts.append(lax.slice_in_dim(w, run[0], run[1], axis=axis))
    return jnp.concatenate(parts, axis=axis)


def _xbc_runs():
    gn = SSD_GROUPS * SSD_STATE
    runs = []
    for g in range(SSD_GROUPS):
        runs.append((g * SSD_GROUP_W, (g + 1) * SSD_GROUP_W))
        runs.append((SSD_INNER + g * SSD_STATE, SSD_INNER + (g + 1) * SSD_STATE))
        runs.append((SSD_INNER + gn + g * SSD_STATE, SSD_INNER + gn + (g + 1) * SSD_STATE))
    return runs


def _head_rows(t):
    n = t.shape[0]
    flat = jnp.pad(t.astype(F32).reshape(n, 2 * SSD_HEADS), ((0, 0), (0, LANES - 2 * SSD_HEADS)))
    rows = [jnp.roll(flat, -SSD_HPG * g, axis=1) for g in range(SSD_GROUPS)]
    rows += [jnp.zeros_like(flat)] * (8 - SSD_GROUPS)
    return jnp.stack(rows, axis=1)


def _win_head_runs():
    runs = []
    for m in range(WIN_GQA):
        for kvh in range(WIN_KV_HEADS):
            h = kvh * WIN_GQA + m
            runs.append((h * HEAD_DIM, (h + 1) * HEAD_DIM))
    return runs


def _rope_tables(seq):
    pos = np.arange(seq)
    row = (pos // GRID_W).astype(np.float32)
    col = (pos % GRID_W).astype(np.float32)
    n_freq = HEAD_DIM // 4
    inv = (np.float32(ROPE_THETA) ** (-np.arange(n_freq, dtype=np.float32) / n_freq)).astype(np.float32)
    ar = (row[:, None] * inv).astype(np.float32)
    ac = (col[:, None] * inv).astype(np.float32)
    cos_h = np.concatenate([np.cos(ar), np.cos(ar), np.cos(ac), np.cos(ac)], axis=1)
    sin_h = np.concatenate([-np.sin(ar), np.sin(ar), -np.sin(ac), np.sin(ac)], axis=1)
    ident_c = np.ones((ROW_TILE, HEAD_DIM), np.float32)
    ident_s = np.zeros((ROW_TILE, HEAD_DIM), np.float32)
    cos_t = np.concatenate([cos_h, ident_c], axis=0)
    sin_t = np.concatenate([sin_h, ident_s], axis=0)
    return (jnp.asarray(np.tile(cos_t, (1, 2)), F32), jnp.asarray(np.tile(sin_t, (1, 2)), F32))


def _na_bias_table(rpb):
    n, h, nr, _ = rpb.shape
    kr = NA_ROWS
    cols = np.arange(GRID_W)
    col_start = np.clip(cols - NA_COLS // 2, 0, GRID_W - NA_COLS)
    col_ok = (cols[None] >= col_start[:, None]) & (cols[None] < col_start[:, None] + NA_COLS)
    r32 = rpb.astype(F32) * LOG2E
    v = jnp.concatenate([r32[..., NA_COLS - 1:], jnp.zeros((n, h, nr, LANES - (2 * NA_COLS - 1)), F32),
                         r32[..., :NA_COLS - 1]], axis=-1)
    flat = jnp.tile(v, (1, 1, 1, GRID_W))[..., :GRID_W * (LANES - 1)]
    toep = flat.reshape(n, h, nr, GRID_W, LANES - 1)[..., :GRID_W]
    toep = jnp.where(jnp.asarray(col_ok), toep, NEG_INF)
    toep = jnp.transpose(toep, (0, 1, 3, 2, 4))
    tabs = [toep[:, :, :, ro0:ro0 + kr].reshape(n, h * GRID_W, kr * GRID_W) for ro0 in range(NA_ROWS)]
    return jnp.stack(tabs, axis=1)


def kernel(x, c, ctx, c_ctx, w_mod, b_mod, norm_mix_g, norm_ff_g, w_ff1, w_ff2, w_in_even, conv_w, conv_b,
           dt_bias, a_log, d_skip, ssd_norm_g, w_out_even, w_in_odd, q_norm_win, k_norm_win, sink_win,
           q_norm_na, k_norm_na, rpb_na, w_out_odd):
    n_batch, seq, d = x.shape
    n_ctx = ctx.shape[1]
    n_lat = n_batch * seq
    tm = ROW_TILE
    tiles_per_batch = seq // tm
    n_lat_tiles = n_lat // tm
    n_ctx_tiles = (n_batch * n_ctx) // tm
    n2 = seq // DFT_N1

    h = jnp.concatenate([x.reshape(n_lat, d), ctx.reshape(n_batch * n_ctx, d)], axis=0)

    cvec = jnp.concatenate([c, c_ctx[None], jnp.zeros((8 - n_batch - 1, d), F32)], axis=0)
    mod_all = _modulation(cvec, w_mod, b_mod)
    mod_all = mod_all.reshape(-1, 8, 6, 1, d)[:, :n_batch + 1].transpose(0, 2, 1, 3, 4)

    m_a, twc, tws, m_b, m_c, chan = _dft_tables(seq, n_ctx)
    xbc_runs = _xbc_runs()
    s3 = FNET_W + SSD_INNER + SSD_CONV_DIM
    w_four = _weight_product(w_in_even, FNET_W, chan)
    tri = jnp.asarray(np.concatenate([np.tril(np.ones((SSD_CHUNK, SSD_CHUNK))),
                                      np.triu(np.ones((SSD_CHUNK, SSD_CHUNK)))], axis=0), F32)
    emat = np.zeros((LANES, 2 * SSD_GROUP_W), np.float32)
    for dr in range(2):
        for r in range(SSD_HPG):
            c0 = dr * SSD_GROUP_W + r * SSD_HEAD_DIM
            emat[SSD_HEADS * dr + r, c0:c0 + SSD_HEAD_DIM] = 1.0
    emat = jnp.asarray(emat, BF16)

    cos_t, sin_t = _rope_tables(seq)
    gmat_slab = jnp.asarray(np.kron(np.eye(MXU_DIM // HEAD_DIM), np.full((HEAD_DIM, HEAD_DIM), 1.0 / HEAD_DIM)),
                          BF16)
    gmat128 = gmat_slab[:LANES, :LANES]
    win_runs = _win_head_runs()
    wq = WIN_Q_HEADS * HEAD_DIM

    gains_mix = norm_mix_g.astype(F32)[:, None, :]
    gains_ff = norm_ff_g.astype(F32)[:, None, :]
    w_ff1_b, w_ff2_b = w_ff1.astype(BF16), w_ff2.astype(BF16)
    w_even_b = w_in_even.astype(BF16)
    w_dt_b = jnp.pad(w_even_b[:, :, s3:], ((0, 0), (0, 0), (0, LANES - 2 * SSD_HEADS)))
    w_out_even_b = w_out_even.astype(BF16)
    conv_w_p = _take(conv_w.astype(F32), xbc_runs, 2)
    conv_b_p = _take(conv_b.astype(F32)[:, None, :], xbc_runs, 2)
    dtb_rows, alog_rows = _head_rows(dt_bias), _head_rows(a_log)
    dsk_x = jnp.repeat(d_skip.astype(F32), SSD_HEAD_DIM, axis=1)[:, None, :]
    ssd_ng = ssd_norm_g.astype(F32)[:, None, :]
    w_odd_b = w_in_odd.astype(BF16)
    wq_perm = _take(w_odd_b[:, :, :wq], win_runs, 2)
    w_out_odd_b = w_out_odd.astype(BF16)
    wo_win_perm = _take(w_out_odd_b[:, :wq], win_runs, 1)
    qscale = HEAD_DIM ** -0.5 * LOG2E
    rep = lambda g, k: jnp.tile(g.astype(F32), (1, k))
    head_gains = jnp.stack([rep(q_norm_win, WIN_Q_HEADS) * qscale,
                            jnp.pad(rep(k_norm_win, WIN_KV_HEADS), ((0, 0), (0, wq - WIN_KV_HEADS * HEAD_DIM))),
                            rep(q_norm_na, NA_HEADS) * qscale, rep(k_norm_na, NA_HEADS)], axis=1)
    head_gains = jnp.pad(head_gains, ((0, 0), (0, 4), (0, 0)))
    sink_cm = _take(sink_win.astype(F32) * LOG2E, [(r[0] // HEAD_DIM, r[1] // HEAD_DIM) for r in win_runs], 1)
    sink_x = jnp.broadcast_to(jnp.repeat(sink_cm, WIN_BLOCK, axis=1)[:, :, None],
                              (sink_win.shape[0], WIN_Q_HEADS * WIN_BLOCK, LANES))
    bias_tab = _na_bias_table(rpb_na)

    for i in range(DEPTH):
        need_ctx = i < DEPTH - 1
        j = i // 2
        if i % 2 == 0:
            fr, fi, z, xbc, dtr = _inproj_even(h, i, j, gains_mix, mod_all, w_four, w_even_b, w_dt_b,
                                               n_batch, tiles_per_batch)

            br, bi = _dft_stage_a(fr.reshape(-1, n2, FNET_W), fi.reshape(-1, n2, FNET_W),
                                  m_a, twc, tws, n_batch, n2)
            f_lat = _dft_stage_b(br.reshape(n_batch, seq, FNET_W), bi.reshape(n_batch, seq, FNET_W),
                                 m_b, n_batch, n2).reshape(n_lat, FNET_W)
            f_ctx = _ctx_dft(fr, fi, m_c, n_batch, n_ctx, n_lat // n_ctx)

            y_lat, y_ctx = _ssd3(xbc, dtr, z, j, conv_w_p, conv_b_p, dtb_rows, alog_rows, dsk_x, ssd_ng, tri, emat,
                                 n_batch, seq, n_ctx)

            h = _outproj(h, i, mod_all, f_lat, y_lat, w_out_even_b, w_out_even_b, j, FNET_W,
                         n_batch, tiles_per_batch, 0, 0, 0, n_lat_tiles)
            h = _outproj(h, i, mod_all, f_ctx, y_ctx, w_out_even_b, w_out_even_b, j, FNET_W,
                         n_batch, tiles_per_batch, n_lat_tiles, 0, 0, n_ctx_tiles)
        else:
            qw, kw, vw, qn, kn, vn = _inproj_odd(h, i, j, gains_mix, mod_all, wq_perm, w_odd_b, cos_t, sin_t,
                                                 gmat_slab, gmat128, head_gains, n_batch, tiles_per_batch)
            o_win = _window_attention(qw, kw, vw, sink_x, j, n_batch, seq, n_ctx)
            o_na = _na_attention(qn, kn, vn, bias_tab, j, n_batch, seq, n_ctx)
            if need_ctx:
                oc_win, oc_na = _ctx_attention(qw, kw, vw, qn, kn, vn, sink_x, j, n_batch, seq, n_ctx)
            h = _outproj(h, i, mod_all, o_win, o_na, wo_win_perm, w_out_odd_b, j, wq,
                         n_batch, tiles_per_batch, 0, 0, 0, n_lat_tiles)
            if need_ctx:
                h = _outproj(h, i, mod_all, oc_win, oc_na, wo_win_perm, w_out_odd_b, j, wq,
                             n_batch, tiles_per_batch, n_lat_tiles, 0, 0, n_ctx_tiles)
        n_tiles = n_lat_tiles + n_ctx_tiles if need_ctx else n_lat_tiles
        h = _ffn(h, i, gains_ff, mod_all, w_ff1_b, w_ff2_b, n_batch, tiles_per_batch, n_tiles)
    return h.reshape(n_batch, seq, d)
```

```python
import functools
import math

import numpy as np
import jax
import jax.numpy as jnp
from jax import lax
from jax.experimental import pallas as pl
from jax.experimental.pallas import tpu as pltpu

F32 = jnp.float32
BF16 = jnp.bfloat16
HIGHEST = lax.Precision.HIGHEST

D_MODEL = 1024
DEPTH = 4
GRID_W = 64
D_FF = 4 * D_MODEL
NORM_EPS = 1e-6
NEG_INF = -1e30

FNET_GROUPS = 8
FNET_GROUP_W = 64
FNET_W = FNET_GROUPS * FNET_GROUP_W
SSD_HEAD_DIM = 64
SSD_HEADS = 24
SSD_GROUPS = 4
SSD_HPG = SSD_HEADS // SSD_GROUPS
SSD_STATE = 128
SSD_INNER = SSD_HEADS * SSD_HEAD_DIM
SSD_GROUP_W = SSD_HPG * SSD_HEAD_DIM
SSD_XBC_W = SSD_GROUP_W + 2 * SSD_STATE
SSD_CONV_DIM = SSD_INNER + 2 * SSD_GROUPS * SSD_STATE
SSD_CONV_W = 5
SSD_CHUNK = 128
HEAD_DIM = 64
WIN_Q_HEADS = 8
WIN_KV_HEADS = 2
WIN_GQA = WIN_Q_HEADS // WIN_KV_HEADS
WIN_BLOCK = 128
WIN_RADIUS = 128
NA_HEADS = 8
NA_ROWS = 8
NA_COLS = 16
ROPE_THETA = 10000.0
LOG2E = 1.4426950408889634

LANES = 128
MXU_DIM = 256
HALO = 8
ROW_TILE = 512
FF_TILE = 4096
DFT_N1 = 128
VMEM_LIMIT = 56 * 1024 * 1024


def _cparams(*sem):
    return pltpu.CompilerParams(dimension_semantics=sem, vmem_limit_bytes=VMEM_LIMIT)


def _silu(x):
    return x / (1.0 + jnp.exp2(x * -LOG2E))


def _softplus(x):
    return jnp.maximum(x, 0.0) + jnp.log(1.0 + jnp.exp(-jnp.abs(x)))


def _norm_mod(h, g, shift, scale):
    ms = jnp.mean(h * h, axis=-1, keepdims=True)
    y = h * lax.rsqrt(ms + NORM_EPS) * g
    return y * (1.0 + scale) + shift


def _dot(a, b):
    return jnp.dot(a, b, preferred_element_type=F32)


def _dot_nt(a, b):
    return lax.dot_general(a, b, (((1,), (1,)), ((), ())), preferred_element_type=F32)


def _mod_body(c_ref, w_ref, b_ref, o_ref):
    s = _silu(c_ref[...])
    o_ref[...] = jnp.dot(s, w_ref[...], precision=HIGHEST, preferred_element_type=F32) + b_ref[...]


def _modulation(cvec, w_mod, b_mod):
    depth, d, n = w_mod.shape
    tn = 1536
    return pl.pallas_call(
        _mod_body,
        grid=(depth, n // tn),
        in_specs=[pl.BlockSpec((8, d), lambda i, j: (0, 0)),
                  pl.BlockSpec((None, d, tn), lambda i, j: (i, 0, j)),
                  pl.BlockSpec((None, 1, tn), lambda i, j: (i, 0, j))],
        out_specs=pl.BlockSpec((None, 8, tn), lambda i, j: (i, 0, j)),
        out_shape=jax.ShapeDtypeStruct((depth, 8, n), F32),
        compiler_params=_cparams("parallel", "parallel"),
        name="modulation",
    )(cvec, w_mod, b_mod.reshape(depth, 1, n))


def _wprod_body(a_ref, b_ref, o_ref):
    o_ref[...] = jnp.dot(a_ref[...], b_ref[...], precision=HIGHEST,
                         preferred_element_type=F32).astype(o_ref.dtype)


def _weight_product(a, k, b):
    n, m, _ = a.shape
    p = b.shape[1]
    return pl.pallas_call(
        _wprod_body,
        grid=(n,),
        in_specs=[pl.BlockSpec((None, m, k), lambda i: (i, 0, 0)),
                  pl.BlockSpec((k, p), lambda i: (0, 0))],
        out_specs=pl.BlockSpec((None, m, p), lambda i: (i, 0, 0)),
        out_shape=jax.ShapeDtypeStruct((n, m, p), BF16),
        compiler_params=_cparams("parallel"),
        name="fold_channel_dft",
    )(a, b)


def _resident(shape):
    return pl.BlockSpec(shape, lambda *_: (0,) * len(shape), pipeline_mode=pl.Buffered(1))


def _layer(arr, j):
    nd = arr.ndim
    return pl.BlockSpec((None,) + arr.shape[1:], lambda *_: (j,) + (0,) * (nd - 1),
                        pipeline_mode=pl.Buffered(1))


def _mod_spec(layer, which, tiles_per_batch, n_batch, tile_off):
    def imap(m):
        return (layer, which, jnp.minimum((m + tile_off) // tiles_per_batch, n_batch), 0, 0)
    return pl.BlockSpec((None, None, None, 1, D_MODEL), imap)


def _inproj_even_body(n_first, two_inputs, *refs):
    if two_inputs:
        h = jnp.where(pl.program_id(0) < n_first, refs[0][...], refs[1][...])
        refs = refs[2:]
    else:
        h = refs[0][...]
        refs = refs[1:]
    g_ref, sh_ref, sc_ref, wf_ref, w_ref, wdt_ref, fr_ref, fi_ref, z_ref, xbc_ref, dt_ref = refs
    u = _norm_mod(h, g_ref[...], sh_ref[...], sc_ref[...]).astype(BF16)
    fr_ref[...] = _dot(u, wf_ref[:, :FNET_W])
    fi_ref[...] = _dot(u, wf_ref[:, FNET_W:])
    z_ref[...] = _dot(u, w_ref[:, FNET_W:FNET_W + SSD_INNER]).astype(z_ref.dtype)
    c0 = FNET_W + SSD_INNER
    xbc = _dot(u, w_ref[:, c0:c0 + SSD_CONV_DIM])
    gn = SSD_GROUPS * SSD_STATE
    for g in range(SSD_GROUPS):
        o = g * SSD_XBC_W
        b0 = SSD_INNER + g * SSD_STATE
        xbc_ref[:, o:o + SSD_GROUP_W] = xbc[:, g * SSD_GROUP_W:(g + 1) * SSD_GROUP_W]
        xbc_ref[:, o + SSD_GROUP_W:o + SSD_GROUP_W + SSD_STATE] = xbc[:, b0:b0 + SSD_STATE]
        xbc_ref[:, o + SSD_GROUP_W + SSD_STATE:o + SSD_XBC_W] = xbc[:, b0 + gn:b0 + gn + SSD_STATE]
    dt_ref[...] = _dot(u, wdt_ref[...])


def _inproj_even(h, h2, layer, j, gains, mod_all, w_four, w_main, w_dt, n_batch, tiles_per_batch):
    tm = ROW_TILE
    rows = h.shape[0] + (0 if h2 is None else h2.shape[0])
    n_first = h.shape[0] // tm
    widths = (FNET_W, FNET_W, SSD_INNER, SSD_CONV_DIM, LANES)
    dtypes = (F32, F32, BF16, F32, F32)
    if h2 is None:
        h_specs, h_args = [pl.BlockSpec((tm, D_MODEL), lambda m: (m, 0))], (h,)
    else:
        h_specs = [pl.BlockSpec((tm, D_MODEL), lambda m: (jnp.minimum(m, n_first - 1), 0)),
                   pl.BlockSpec((tm, D_MODEL), lambda m: (jnp.maximum(m - n_first, 0), 0))]
        h_args = (h, h2)
    return pl.pallas_call(
        functools.partial(_inproj_even_body, n_first, h2 is not None),
        grid=(rows // tm,),
        in_specs=h_specs + [
                  _layer(gains, layer),
                  _mod_spec(layer, 0, tiles_per_batch, n_batch, 0),
                  _mod_spec(layer, 1, tiles_per_batch, n_batch, 0),
                  _layer(w_four, j), _layer(w_main, j), _layer(w_dt, j)],
        out_specs=[pl.BlockSpec((tm, wd), lambda m: (m, 0)) for wd in widths],
        out_shape=[jax.ShapeDtypeStruct((rows, wd), dt) for wd, dt in zip(widths, dtypes)],
        compiler_params=_cparams("parallel"),
        name="inproj_even",
    )(*h_args, gains, mod_all, mod_all, w_four, w_main, w_dt)


def _group_rms(x, gmat_ref, gain):
    sq = x * x
    hi = sq.astype(BF16)
    lo = (sq - hi.astype(F32)).astype(BF16)
    gw = gmat_ref.shape[0]
    parts = []
    for c0 in range(0, x.shape[1], gw):
        parts.append(_dot(hi[:, c0:c0 + gw], gmat_ref[...]) + _dot(lo[:, c0:c0 + gw], gmat_ref[...]))
    ms = parts[0] if len(parts) == 1 else jnp.concatenate(parts, axis=1)
    return x * lax.rsqrt(ms + NORM_EPS) * gain


def _rope(y, cos, sin_signed):
    w = y.shape[-1]
    reps = w // LANES
    lane = lax.broadcasted_iota(jnp.int32, y.shape, 1)
    first = (lane % 32) < 16
    partner = jnp.where(first, pltpu.roll(y, w - 16, 1), pltpu.roll(y, 16, 1))
    if reps > 1:
        cos = jnp.concatenate([cos] * reps, axis=1)
        sin_signed = jnp.concatenate([sin_signed] * reps, axis=1)
    return y * cos + partner * sin_signed


def _inproj_odd_body(h_ref, g_ref, sh_ref, sc_ref, wq_ref, w_ref, cos_ref, sin_ref, gslab_ref, g128_ref,
                     hg_ref, qw_ref, kw_ref, vw_ref, qn_ref, kn_ref, vn_ref):
    u = _norm_mod(h_ref[...], g_ref[...], sh_ref[...], sc_ref[...]).astype(BF16)
    cos = cos_ref[...]
    sin = sin_ref[...]
    wq = WIN_Q_HEADS * HEAD_DIM
    wk = WIN_KV_HEADS * HEAD_DIM
    nh = NA_HEADS * HEAD_DIM
    qw = _group_rms(_dot(u, wq_ref[...]), gslab_ref, hg_ref[0:1, :])
    qw_ref[...] = _rope(qw, cos, sin).astype(BF16)
    kw = _group_rms(_dot(u, w_ref[:, wq:wq + wk]), g128_ref, hg_ref[1:2, :wk])
    kw_ref[...] = _rope(kw, cos, sin).astype(BF16)
    vw_ref[...] = _dot(u, w_ref[:, wq + wk:wq + 2 * wk]).astype(BF16)
    c0 = wq + 2 * wk
    qn_ref[...] = _group_rms(_dot(u, w_ref[:, c0:c0 + nh]), gslab_ref, hg_ref[2:3, :]).astype(BF16)
    kn_ref[...] = _group_rms(_dot(u, w_ref[:, c0 + nh:c0 + 2 * nh]), gslab_ref, hg_ref[3:4, :]).astype(BF16)
    vn_ref[...] = _dot(u, w_ref[:, c0 + 2 * nh:c0 + 3 * nh]).astype(BF16)


def _inproj_odd(h, layer, j, gains, mod_all, wq_perm, w_main, cos_t, sin_t, gmat_slab, gmat128, head_gains,
                n_batch, tiles_per_batch):
    rows = h.shape[0]
    tm = ROW_TILE
    widths = (512, 128, 128, 512, 512, 512)
    n_lat_tiles = n_batch * tiles_per_batch

    def rope_map(m):
        return (jnp.where(m < n_lat_tiles, m % tiles_per_batch, tiles_per_batch), 0)

    return pl.pallas_call(
        _inproj_odd_body,
        grid=(rows // tm,),
        in_specs=[pl.BlockSpec((tm, D_MODEL), lambda m: (m, 0)),
                  _layer(gains, layer),
                  _mod_spec(layer, 0, tiles_per_batch, n_batch, 0),
                  _mod_spec(layer, 1, tiles_per_batch, n_batch, 0),
                  _layer(wq_perm, j), _layer(w_main, j),
                  pl.BlockSpec((tm, LANES), rope_map),
                  pl.BlockSpec((tm, LANES), rope_map),
                  _resident(gmat_slab.shape), _resident(gmat128.shape), _layer(head_gains, j)],
        out_specs=[pl.BlockSpec((tm, wd), lambda m: (m, 0)) for wd in widths],
        out_shape=[jax.ShapeDtypeStruct((rows, wd), BF16) for wd in widths],
        compiler_params=_cparams("parallel"),
        name="inproj_odd",
    )(h, gains, mod_all, mod_all, wq_perm, w_main, cos_t, sin_t, gmat_slab, gmat128, head_gains)


def _outproj_body(k1, k2, b_row0, h_ref, gate_ref, a1_ref, a2_ref, wa_ref, wb_ref, *rest):
    o_ref = rest[-1]
    acc = _dot(a1_ref[...].astype(BF16), wa_ref[0:k1, :])
    acc += _dot(a2_ref[...].astype(BF16), wb_ref[b_row0:b_row0 + k2, :])
    o_ref[...] = h_ref[...] + gate_ref[...] * acc


def _outproj(h, layer, mod_all, a1, a2, wa, wb, j, b_row0, n_batch, tiles_per_batch, tile_off, a1_off,
             a2_off, n_tiles, res=None, res_off=0, dst_rows=None):
    tm = ROW_TILE
    k1, k2 = a1.shape[1], a2.shape[1]
    in_place = res is None
    src, src_off = (h, tile_off) if in_place else (res, res_off)
    in_specs = [pl.BlockSpec((tm, D_MODEL), lambda m: (m + src_off, 0)),
                _mod_spec(layer, 2, tiles_per_batch, n_batch, tile_off),
                pl.BlockSpec((tm, k1), lambda m: (m + a1_off, 0)),
                pl.BlockSpec((tm, k2), lambda m: (m + a2_off, 0)),
                _layer(wa, j), _layer(wb, j)]
    args = [src, mod_all, a1, a2, wa, wb]
    aliases = {0: 0} if in_place else {}
    if not in_place and h is not None:
        in_specs.append(pl.BlockSpec(memory_space=pl.ANY))
        args.append(h)
        aliases = {len(args) - 1: 0}
    out_rows = dst_rows if h is None else h.shape[0]
    return pl.pallas_call(
        functools.partial(_outproj_body, k1, k2, b_row0),
        grid=(n_tiles,),
        in_specs=in_specs,
        out_specs=pl.BlockSpec((tm, D_MODEL), lambda m: (m + tile_off, 0)),
        out_shape=jax.ShapeDtypeStruct((out_rows, D_MODEL), F32),
        input_output_aliases=aliases,
        compiler_params=_cparams("parallel"),
        name="outproj",
    )(*args)


def _ffn_body(h_ref, g_ref, sh_ref, sc_ref, gate_ref, w1_ref, w2_ref, o_ref):
    h = h_ref[...]
    u = _norm_mod(h, g_ref[...], sh_ref[...], sc_ref[...]).astype(BF16)
    acc = None
    for k in range(D_FF // FF_TILE):
        sl = slice(k * FF_TILE, (k + 1) * FF_TILE)
        a = jnp.maximum(_dot(u, w1_ref[:, sl]), 0.0)
        t = _dot((a * a).astype(BF16), w2_ref[sl, :])
        acc = t if acc is None else acc + t
    o_ref[...] = h + gate_ref[...] * acc


def _ffn(h, layer, gains, mod_all, w1, w2, n_batch, tiles_per_batch, n_tiles):
    tm = ROW_TILE
    return pl.pallas_call(
        _ffn_body,
        grid=(n_tiles,),
        in_specs=[pl.BlockSpec((tm, D_MODEL), lambda m: (m, 0)),
                  _layer(gains, layer),
                  _mod_spec(layer, 3, tiles_per_batch, n_batch, 0),
                  _mod_spec(layer, 4, tiles_per_batch, n_batch, 0),
                  _mod_spec(layer, 5, tiles_per_batch, n_batch, 0),
                  _layer(w1, layer), _layer(w2, layer)],
        out_specs=pl.BlockSpec((tm, D_MODEL), lambda m: (m, 0)),
        out_shape=jax.ShapeDtypeStruct((n_tiles * tm, D_MODEL), F32),
        compiler_params=_cparams("parallel"),
        name="ffn",
    )(h, gains, mod_all, mod_all, mod_all, w1, w2)


def _dft_a_body(xr_ref, xi_ref, m_ref, c_ref, s_ref, br_ref, bi_ref):
    n1 = DFT_N1
    reps = FNET_W // LANES
    for j in range(xr_ref.shape[1]):
        x = jnp.concatenate([xr_ref[:, j, :], xi_ref[:, j, :]], axis=0).astype(BF16)
        a = _dot(m_ref[...], x)
        ar, ai = a[:n1], a[n1:]
        c = jnp.concatenate([c_ref[:, LANES * j:LANES * (j + 1)]] * reps, axis=1)
        s = jnp.concatenate([s_ref[:, LANES * j:LANES * (j + 1)]] * reps, axis=1)
        br_ref[:, j, :] = ar * c + ai * s
        bi_ref[:, j, :] = ai * c - ar * s


def _dft_stage_a(xr, xi, mmat, twc, tws, n_batch, n2):
    n1 = DFT_N1
    lb = 8
    blk = pl.BlockSpec((n1, lb, FNET_W), lambda b, j: (b, j, 0))
    return pl.pallas_call(
        _dft_a_body,
        grid=(n_batch, n2 // lb),
        in_specs=[blk, blk,
                  pl.BlockSpec((2 * n1, 2 * n1), lambda b, j: (0, 0)),
                  pl.BlockSpec((n1, lb * LANES), lambda b, j: (0, j)),
                  pl.BlockSpec((n1, lb * LANES), lambda b, j: (0, j))],
        out_specs=[blk, blk],
        out_shape=[jax.ShapeDtypeStruct((n_batch * n1, n2, FNET_W), F32)] * 2,
        compiler_params=_cparams("parallel", "parallel"),
        name="seq_dft_stage_a",
    )(xr, xi, mmat, twc, tws)


def _dft_b_body(br_ref, bi_ref, m_ref, o_ref):
    x = jnp.concatenate([br_ref[...], bi_ref[...]], axis=0).astype(BF16)
    res = _dot(m_ref[...], x)
    o_ref[...] = res.reshape(o_ref.shape)


def _dft_stage_b(br, bi, mmat, n_batch, n2):
    n1 = DFT_N1
    kb = 8
    return pl.pallas_call(
        _dft_b_body,
        grid=(n_batch, n1 // kb),
        in_specs=[pl.BlockSpec((None, kb * n2, FNET_W), lambda b, j: (b, j, 0)),
                  pl.BlockSpec((None, kb * n2, FNET_W), lambda b, j: (b, j, 0)),
                  pl.BlockSpec(mmat.shape, lambda b, j: (0, 0))],
        out_specs=pl.BlockSpec((None, n2, kb, FNET_W), lambda b, j: (b, 0, j, 0)),
        out_shape=jax.ShapeDtypeStruct((n_batch, n2, n1, FNET_W), F32),
        compiler_params=_cparams("parallel", "parallel"),
        name="seq_dft_stage_b",
    )(br, bi, mmat)


def _ctx_dft_body(xr_ref, xi_ref, m_ref, o_ref):
    x = jnp.concatenate([xr_ref[...], xi_ref[...]], axis=0).astype(BF16)
    o_ref[...] = _dot(m_ref[...], x)


def _ctx_dft(fr, fi, mmat, n_batch, n_ctx, row_block_off):
    return pl.pallas_call(
        _ctx_dft_body,
        grid=(n_batch,),
        in_specs=[pl.BlockSpec((n_ctx, FNET_W), lambda b: (row_block_off + b, 0)),
                  pl.BlockSpec((n_ctx, FNET_W), lambda b: (row_block_off + b, 0)),
                  pl.BlockSpec(mmat.shape, lambda b: (0, 0))],
        out_specs=pl.BlockSpec((n_ctx, FNET_W), lambda b: (b, 0)),
        out_shape=jax.ShapeDtypeStruct((n_batch * n_ctx, FNET_W), F32),
        compiler_params=_cparams("parallel"),
        name="ctx_dft",
    )(fr, fi, mmat)


def _dft_tables(seq, n_ctx):
    n1 = DFT_N1
    n2 = seq // n1
    k1 = np.arange(n1)
    ang1 = 2.0 * np.pi * np.outer(k1, k1) / n1
    c1, s1 = np.cos(ang1), np.sin(ang1)
    m_a = np.block([[c1, s1], [-s1, c1]])
    ang_t = 2.0 * np.pi * np.outer(k1, np.arange(n2)) / seq
    twc = np.repeat(np.cos(ang_t), LANES, axis=1)
    tws = np.repeat(np.sin(ang_t), LANES, axis=1)
    k2 = np.arange(n2)
    ang2 = 2.0 * np.pi * np.outer(k2, k2) / n2
    scale = 1.0 / math.sqrt(seq)
    c2, s2 = np.cos(ang2) * scale, np.sin(ang2) * scale
    eye = np.eye(8)
    m_b = np.concatenate([np.einsum("kl,ab->kabl", c2, eye).reshape(n2 * 8, 8 * n2),
                          np.einsum("kl,ab->kabl", s2, eye).reshape(n2 * 8, 8 * n2)], axis=1)
    kc = np.arange(n_ctx)
    angc = 2.0 * np.pi * np.outer(kc, kc) / n_ctx
    m_c = np.concatenate([np.cos(angc), np.sin(angc)], axis=1) / math.sqrt(n_ctx)
    ch = np.arange(FNET_GROUP_W)
    angg = 2.0 * np.pi * np.outer(ch, ch) / FNET_GROUP_W
    eg = np.eye(FNET_GROUPS)
    chan = np.concatenate([np.kron(eg, np.cos(angg)), -np.kron(eg, np.sin(angg))], axis=1)
    chan = chan / math.sqrt(FNET_GROUP_W)
    return (jnp.asarray(m_a, BF16), jnp.asarray(twc, F32), jnp.asarray(tws, F32),
            jnp.asarray(m_b, BF16), jnp.asarray(m_c, BF16), jnp.asarray(chan, F32))


def _split_dot_rhs(w_bf16, x):
    hi = x.astype(BF16)
    r1 = x - hi.astype(F32)
    mid = r1.astype(BF16)
    lo = (r1 - mid.astype(F32)).astype(BF16)
    return _dot(w_bf16, hi) + _dot(w_bf16, mid) + _dot(w_bf16, lo)


def _split2_dot(x, w_bf16):
    hi = x.astype(BF16)
    lo = (x - hi.astype(F32)).astype(BF16)
    return _dot(hi, w_bf16) + _dot(lo, w_bf16)


SSD_LAT_CPS = 8


def _ssd_local(xfull, dt_raw, g, pos0, n_chunks, cw_ref, cb_ref, dtb_ref, alog_ref, dsk_ref, tri_ref, e_ref,
               yloc_ref, s_ref, c_ref, ea_ref, cd_ref):
    q = SSD_CHUNK
    gw = SSD_GROUP_W
    rows_blk = n_chunks * q
    conv = cb_ref[...]
    for k in range(SSD_CONV_W):
        shift = (SSD_CONV_W // 2 - k) % xfull.shape[0]
        xk = xfull if shift == 0 else pltpu.roll(xfull, shift, 0)
        conv = conv + xk[HALO:HALO + rows_blk, :] * cw_ref[k:k + 1, :]
    act = _silu(conv)
    xs = act[:, :gw]
    bm = act[:, gw:gw + SSD_STATE]
    cm = act[:, gw + SSD_STATE:]

    lane = lax.broadcasted_iota(jnp.int32, (1, LANES), 1)
    is_fwd = lane < SSD_HPG
    dsel = pltpu.roll(dt_raw, (LANES - SSD_HPG * g) % LANES, 1)
    dtv = _softplus(dsel + dtb_ref[pl.ds(g, 1), :])
    head_lane = is_fwd | ((lane >= SSD_HEADS) & (lane < SSD_HEADS + SSD_HPG))
    a_row = jnp.where(head_lane, -jnp.exp(alog_ref[pl.ds(g, 1), :]) * LOG2E, 0.0)
    adt = dtv * a_row
    tri = tri_ref[...]
    tri16 = tri.astype(BF16)
    allowed = (tri[:q] > 0.0, tri[q:] > 0.0)
    emat = e_ref[...]
    lane_q = lax.broadcasted_iota(jnp.int32, (q, LANES), 1)
    lo_half = lane_q < SSD_HEAD_DIM
    for c0 in range(0, n_chunks, 2):
        pair = range(c0, min(c0 + 2, n_chunks))
        prep = {}
        for cc in pair:
            rows = slice(cc * q, (cc + 1) * q)
            cums = _split_dot_rhs(tri16, adt[rows])
            acum = jnp.where(is_fwd, cums[:q], cums[q:])
            end = jnp.where(is_fwd, acum[q - 1:q, :], acum[0:1, :])
            dte = jnp.exp2(end - acum)
            ex = _dot(jnp.concatenate([dtv[rows], dte], axis=0).astype(BF16), emat)
            cdx = _split2_dot(jnp.broadcast_to(jnp.exp2(end), (16, LANES)), emat)
            c16 = cm[rows].astype(BF16)
            b_c = bm[rows]
            cbm = _dot_nt(c16, b_c.astype(BF16))
            prep[cc] = (acum, acum.T, ex, cdx, c16, b_c, cbm)
        for cc in pair:
            rows = slice(cc * q, (cc + 1) * q)
            acum, acum_t, ex, cdx, c16, b_c, cbm = prep[cc]
            xs_c = xs[rows]
            xdt2 = jnp.concatenate([xs_c, xs_c], axis=1) * ex[:q]
            ys = []
            for m in range(SSD_HPG // 2):
                mms, xhs = [], []
                for d in range(2):
                    xpair = xdt2[:, gw * d + LANES * m:gw * d + LANES * (m + 1)]
                    for half in range(2):
                        ln = SSD_HEADS * d + 2 * m + half
                        seg = acum[:, ln:ln + 1] - acum_t[ln:ln + 1, :]
                        dec = jnp.exp2(jnp.where(allowed[d], seg, NEG_INF))
                        mms.append((cbm * dec).astype(BF16))
                        xhs.append(jnp.where(lo_half if half == 0 else ~lo_half, xpair, 0.0).astype(BF16))
                ys.append(_dot(jnp.concatenate(mms, axis=1), jnp.concatenate(xhs, axis=0)))
            y_loc = jnp.concatenate(ys, axis=1) + dsk_ref[...] * xs_c
            sts = _dot(b_c.T.astype(BF16), (xdt2 * ex[q:2 * q]).astype(BF16))
            pos = pos0 + cc
            r0 = pl.multiple_of(pos * q, q)
            yloc_ref[pl.ds(r0, q), :] = y_loc
            s_ref[pl.ds(r0, q), :] = sts.astype(BF16)
            c_ref[pl.ds(r0, q), :] = c16
            ea_ref[pl.ds(r0, q), :] = jnp.exp2(acum).astype(BF16)
            cd_ref[pl.ds(pl.multiple_of(pos * 8, 8), 8), :] = cdx[0:8]


def _ssd_emit(z_ref, o_ref, pos0, n_chunks, hf, ng_ref, e_ref, yloc_ref, s_ref, c_ref, ea_ref, cd_ref, hbe_ref):
    q = SSD_CHUNK
    gw = SSD_GROUP_W
    emat = e_ref[...]
    starts = [pl.multiple_of((pos0 + cc) * q, q) for cc in range(n_chunks)]
    ea_x = [_dot(ea_ref[pl.ds(r0, q), :], emat) for r0 in starts]
    gate = [_silu(z_ref[cc * q:(cc + 1) * q, :].astype(F32)) for cc in range(n_chunks)]
    for cc in range(n_chunks):
        r0 = starts[cc]
        hcat = jnp.concatenate([hf.astype(BF16), hbe_ref[pl.ds(r0, q), :]], axis=1)
        yo = _dot(c_ref[pl.ds(r0, q), :], hcat) * ea_x[cc]
        y = yloc_ref[pl.ds(r0, q), :] + yo[:, :gw] + yo[:, gw:]
        cd = cd_ref[pl.ds(pl.multiple_of((pos0 + cc) * 8, 8), 8), :][0:1, :gw]
        hf = hf * cd + s_ref[pl.ds(r0, q), :gw].astype(F32)
        y = y * gate[cc]
        y = y * lax.rsqrt(jnp.mean(y * y, axis=-1, keepdims=True) + NORM_EPS) * ng_ref[...]
        o_ref[cc * q:(cc + 1) * q, :] = y.astype(o_ref.dtype)
    return hf


def _ssd3_body(nb, ncc, xm_ref, xp_ref, xn_ref, dt_ref, z_ref, xc_ref, dtc_ref, zc_ref, cw_ref, cb_ref, dtb_ref,
               alog_ref, dsk_ref, ng_ref, tri_ref, e_ref, o_ref, oc_ref,
               yloc_ref, s_ref, c_ref, ea_ref, cd_ref, hbe_ref, hf_ref, hb_ref):
    g = pl.program_id(1)
    t = pl.program_id(2)
    q = SSD_CHUNK
    gw = SSD_GROUP_W
    ns = ncc + nb * SSD_LAT_CPS
    local_refs = (cw_ref, cb_ref, dtb_ref, alog_ref, dsk_ref, tri_ref, e_ref, yloc_ref, s_ref, c_ref, ea_ref,
                  cd_ref)
    emit_refs = (ng_ref, e_ref, yloc_ref, s_ref, c_ref, ea_ref, cd_ref, hbe_ref)

    @pl.when(t == 0)
    def _local_context():
        pad = jnp.zeros((HALO, SSD_XBC_W), F32)
        _ssd_local(jnp.concatenate([pad, xc_ref[...], pad], axis=0), dtc_ref[...], g, 0, ncc, *local_refs)

    @pl.when(t < nb)
    def _local_latent():
        prev = jnp.where(t == 0, 0.0, xp_ref[...])
        nxt = jnp.where(t == nb - 1, 0.0, xn_ref[...])
        xfull = jnp.concatenate([prev, xm_ref[...], nxt], axis=0)
        _ssd_local(xfull, dt_ref[...], g, ncc + SSD_LAT_CPS * t, SSD_LAT_CPS, *local_refs)

    @pl.when(t == nb)
    def _backward_states_and_context():
        hb_ref[...] = jnp.zeros_like(hb_ref)

        def body(i, carry):
            pos = jnp.where(i < ncc, ncc - 1 - i, ns - 1 + ncc - i)
            r0 = pl.multiple_of(pos * q, q)
            hb = hb_ref[...]
            hbe_ref[pl.ds(r0, q), :] = hb.astype(BF16)
            cd = cd_ref[pl.ds(pl.multiple_of(pos * 8, 8), 8), :][0:1, gw:]
            hb_ref[...] = hb * cd + s_ref[pl.ds(r0, q), gw:].astype(F32)
            return carry

        lax.fori_loop(0, ns, body, 0)
        hf_ref[...] = _ssd_emit(zc_ref, oc_ref, 0, ncc, jnp.zeros(hf_ref.shape, F32), *emit_refs)

    @pl.when(t >= nb)
    def _emit_latent():
        hf_ref[...] = _ssd_emit(z_ref, o_ref, ncc + SSD_LAT_CPS * (t - nb), SSD_LAT_CPS, hf_ref[...], *emit_refs)


def _ssd3(xbc, dt, z, j, conv_w, conv_b, dtb_tab, alog_tab, dsk_x, ng, tri, emat, n_batch, seq, n_ctx):
    rows = xbc.shape[0]
    q = SSD_CHUNK
    blk = SSD_LAT_CPS * q
    assert n_ctx % q == 0 and seq % blk == 0 and (n_batch * seq) % n_ctx == 0
    ncc = n_ctx // q
    nb = seq // blk
    ns = ncc + nb * SSD_LAT_CPS
    per = blk // HALO
    ctx_blk0 = n_batch * seq // n_ctx
    last_halo = rows // HALO - 1
    gw = SSD_GROUP_W

    def rb_in(b, t):
        return b * nb + jnp.minimum(t, nb - 1)

    def rb_out(b, t):
        return b * nb + jnp.maximum(t - nb, 0)

    return pl.pallas_call(
        functools.partial(_ssd3_body, nb, ncc),
        grid=(n_batch, SSD_GROUPS, 2 * nb),
        in_specs=[pl.BlockSpec((blk, SSD_XBC_W), lambda b, g, t: (rb_in(b, t), g)),
                  pl.BlockSpec((HALO, SSD_XBC_W),
                               lambda b, g, t: (jnp.maximum(rb_in(b, t) * per - 1, 0), g)),
                  pl.BlockSpec((HALO, SSD_XBC_W),
                               lambda b, g, t: (jnp.minimum(rb_in(b, t) * per + per, last_halo), g)),
                  pl.BlockSpec((blk, LANES), lambda b, g, t: (rb_in(b, t), 0)),
                  pl.BlockSpec((blk, gw), lambda b, g, t: (rb_out(b, t), g)),
                  pl.BlockSpec((n_ctx, SSD_XBC_W), lambda b, g, t: (ctx_blk0 + b, g)),
                  pl.BlockSpec((n_ctx, LANES), lambda b, g, t: (ctx_blk0 + b, 0)),
                  pl.BlockSpec((n_ctx, gw), lambda b, g, t: (ctx_blk0 + b, g)),
                  pl.BlockSpec((None, SSD_CONV_W, SSD_XBC_W), lambda b, g, t: (j, 0, g)),
                  pl.BlockSpec((None, 1, SSD_XBC_W), lambda b, g, t: (j, 0, g)),
                  pl.BlockSpec((None, 8, LANES), lambda b, g, t: (j, 0, 0)),
                  pl.BlockSpec((None, 8, LANES), lambda b, g, t: (j, 0, 0)),
                  pl.BlockSpec((None, 1, gw), lambda b, g, t: (j, 0, g)),
                  pl.BlockSpec((None, 1, gw), lambda b, g, t: (j, 0, g)),
                  pl.BlockSpec((2 * q, q), lambda b, g, t: (0, 0)),
                  pl.BlockSpec((LANES, 2 * gw), lambda b, g, t: (0, 0))],
        out_specs=[pl.BlockSpec((blk, gw), lambda b, g, t: (rb_out(b, t), g)),
                   pl.BlockSpec((n_ctx, gw), lambda b, g, t: (b, g))],
        out_shape=[jax.ShapeDtypeStruct((n_batch * seq, SSD_INNER), BF16),
                   jax.ShapeDtypeStruct((n_batch * n_ctx, SSD_INNER), BF16)],
        scratch_shapes=[pltpu.VMEM((ns * q, gw), F32),
                        pltpu.VMEM((ns * q, 2 * gw), BF16),
                        pltpu.VMEM((ns * q, SSD_STATE), BF16),
                        pltpu.VMEM((ns * q, LANES), BF16),
                        pltpu.VMEM((ns * 8, 2 * gw), F32),
                        pltpu.VMEM((ns * q, gw), BF16),
                        pltpu.VMEM((SSD_STATE, gw), F32),
                        pltpu.VMEM((SSD_STATE, gw), F32)],
        compiler_params=_cparams("arbitrary", "arbitrary", "arbitrary"),
        name="ssd_bidir",
    )(xbc, xbc, xbc, dt, z, xbc, dt, z, conv_w, conv_b, dtb_tab, alog_tab, dsk_x, ng, tri, emat)


def _stack_heads(qc):
    lane = lax.broadcasted_iota(jnp.int32, qc.shape, 1)
    lo = lane < HEAD_DIM
    zero = jnp.zeros_like(qc)
    return jnp.concatenate([jnp.where(lo, qc, zero), jnp.where(lo, zero, qc)], axis=0)


def _unstack_heads(o, n):
    lane = lax.broadcasted_iota(jnp.int32, (n, LANES), 1)
    return jnp.where(lane < HEAD_DIM, o[:n], o[n:])


def _softmax_pv(scores, values, sink=None):
    m = None
    for sc in scores:
        mx = jnp.max(sc, axis=-1, keepdims=True)
        m = mx if m is None else jnp.maximum(m, mx)
    den = None
    if sink is not None:
        m = jnp.maximum(m, jnp.max(sink, axis=-1, keepdims=True))
        den = jnp.sum(jnp.exp2(sink - m), axis=-1, keepdims=True) * (1.0 / sink.shape[-1])
    acc = None
    for sc, v in zip(scores, values):
        pr = jnp.exp2(sc - m)
        sm = jnp.sum(pr, axis=-1, keepdims=True)
        den = sm if den is None else den + sm
        t = _dot(pr.astype(BF16), v)
        acc = t if acc is None else acc + t
    return acc * (1.0 / den)


WIN_BLOCKS_PER_STEP = 2


def _win_body(seq, q_ref, k_ref, v_ref, kc_ref, vc_ref, sink_ref, o_ref):
    wb = WIN_BLOCK
    r = lax.broadcasted_iota(jnp.int32, (wb, 3 * wb), 0)
    c = lax.broadcasted_iota(jnp.int32, (wb, 3 * wb), 1)
    for i in range(WIN_BLOCKS_PER_STEP):
        n = pl.program_id(1) * WIN_BLOCKS_PER_STEP + i
        qrows = slice(i * wb, (i + 1) * wb)
        start = pl.multiple_of(jnp.clip((n - 1) * wb, 0, seq - 3 * wb), wb)
        kb = k_ref[pl.ds(start, 3 * wb), :]
        vb = v_ref[pl.ds(start, 3 * wb), :]
        dist = (c + start) - (r + n * wb)
        pen = jnp.where(jnp.abs(dist) <= WIN_RADIUS, 0.0, NEG_INF)
        pen = jnp.concatenate([pen, pen], axis=0)
        scores = []
        for m in range(WIN_GQA):
            qs = _stack_heads(q_ref[qrows, LANES * m:LANES * (m + 1)])
            scores.append([_dot_nt(qs, kb) + pen, _dot_nt(qs, kc_ref[...])])
        for m in range(WIN_GQA):
            o = _softmax_pv(scores[m], [vb, vc_ref[...]], sink_ref[2 * m * wb:2 * (m + 1) * wb, :])
            o_ref[qrows, LANES * m:LANES * (m + 1)] = _unstack_heads(o, wb).astype(o_ref.dtype)


def _window_attention(qw, kw, vw, sink_x, j, n_batch, seq, n_ctx):
    wb = WIN_BLOCK
    assert seq // wb >= 3 and seq % (wb * WIN_BLOCKS_PER_STEP) == 0
    qrows = wb * WIN_BLOCKS_PER_STEP
    nb = seq // qrows
    ctx_blk0 = n_batch * seq // n_ctx
    q_spec = pl.BlockSpec((qrows, 512), lambda b, n: (b * nb + n, 0))
    seq_spec = pl.BlockSpec((seq, LANES), lambda b, n: (b, 0))
    cspec = pl.BlockSpec((n_ctx, LANES), lambda b, n: (ctx_blk0 + b, 0))
    return pl.pallas_call(
        functools.partial(_win_body, seq),
        grid=(n_batch, nb),
        in_specs=[q_spec, seq_spec, seq_spec, cspec, cspec, _layer(sink_x, j)],
        out_specs=pl.BlockSpec((qrows, 512), lambda b, n: (b * nb + n, 0)),
        out_shape=jax.ShapeDtypeStruct((n_batch * seq, 512), BF16),
        compiler_params=_cparams("parallel", "arbitrary"),
        name="window_attention",
    )(qw, kw, vw, kw, vw, sink_x)


NA_ROWS_PER_STEP = 8


def _na_body(n_rows, q_ref, k_ref, v_ref, kc_ref, vc_ref, bias_ref, o_ref):
    kr = NA_ROWS
    for j in range(NA_ROWS_PER_STEP):
        r = pl.program_id(1) * NA_ROWS_PER_STEP + j
        first_row = jnp.clip(r - kr // 2, 0, n_rows - kr)
        start = pl.multiple_of(first_row * GRID_W, GRID_W)
        ro0 = first_row - r + NA_ROWS - 1
        rows = slice(j * GRID_W, (j + 1) * GRID_W)
        scores = []
        for m in range(NA_HEADS // 2):
            sl = slice(LANES * m, LANES * (m + 1))
            qs = _stack_heads(q_ref[rows, sl])
            kw = k_ref[pl.ds(start, kr * GRID_W), sl]
            s_loc = _dot_nt(qs, kw) + bias_ref[ro0, 2 * GRID_W * m:2 * GRID_W * (m + 1), :]
            scores.append((s_loc, _dot_nt(qs, kc_ref[:, sl])))
        for m in range(NA_HEADS // 2):
            sl = slice(LANES * m, LANES * (m + 1))
            vw = v_ref[pl.ds(start, kr * GRID_W), sl]
            o = _softmax_pv(list(scores[m]), [vw, vc_ref[:, sl]])
            o_ref[rows, sl] = _unstack_heads(o, GRID_W).astype(o_ref.dtype)


def _na_attention(qn, kn, vn, bias_tab, j, n_batch, seq, n_ctx):
    n_rows = seq // GRID_W
    assert n_rows >= NA_ROWS and n_rows % NA_ROWS_PER_STEP == 0
    n_steps = n_rows // NA_ROWS_PER_STEP
    qrows = NA_ROWS_PER_STEP * GRID_W
    ctx_blk0 = n_batch * seq // n_ctx
    return pl.pallas_call(
        functools.partial(_na_body, n_rows),
        grid=(n_batch, n_steps),
        in_specs=[pl.BlockSpec((qrows, 512), lambda b, r: (b * n_steps + r, 0)),
                  pl.BlockSpec((seq, 512), lambda b, r: (b, 0)),
                  pl.BlockSpec((seq, 512), lambda b, r: (b, 0)),
                  pl.BlockSpec((n_ctx, 512), lambda b, r: (ctx_blk0 + b, 0)),
                  pl.BlockSpec((n_ctx, 512), lambda b, r: (ctx_blk0 + b, 0)),
                  _layer(bias_tab, j)],
        out_specs=pl.BlockSpec((qrows, 512), lambda b, r: (b * n_steps + r, 0)),
        out_shape=jax.ShapeDtypeStruct((n_batch * seq, 512), BF16),
        compiler_params=_cparams("parallel", "arbitrary"),
        name="neighbourhood_attention",
    )(qn, kn, vn, kn, vn, bias_tab)


def _ctx_attn_body(qw_ref, kw_ref, vw_ref, qn_ref, kn_ref, vn_ref, sink_ref, ow_ref, on_ref):
    n = qw_ref.shape[0]
    qw = qw_ref[...]
    lane = lax.broadcasted_iota(jnp.int32, (n, LANES), 1)
    lo = lane < HEAD_DIM
    for m in range(WIN_GQA):
        qs = _stack_heads(qw[:, LANES * m:LANES * (m + 1)])
        r0, r1 = 2 * m * WIN_BLOCK, (2 * m + 1) * WIN_BLOCK
        sink = jnp.concatenate([jnp.broadcast_to(sink_ref[r0:r0 + 1, :], (n, LANES)),
                                jnp.broadcast_to(sink_ref[r1:r1 + 1, :], (n, LANES))], axis=0)
        o = _softmax_pv([_dot_nt(qs, kw_ref[...])], [vw_ref[...]], sink)
        ow_ref[:, LANES * m:LANES * (m + 1)] = jnp.where(lo, o[:n], o[n:]).astype(ow_ref.dtype)
    qn = qn_ref[...]
    for m in range(NA_HEADS // 2):
        sl = slice(LANES * m, LANES * (m + 1))
        qs = _stack_heads(qn[:, sl])
        o = _softmax_pv([_dot_nt(qs, kn_ref[:, sl])], [vn_ref[:, sl]])
        on_ref[:, sl] = _unstack_heads(o, n).astype(on_ref.dtype)


def _ctx_attention(qw, kw, vw, qn, kn, vn, sink_x, j, n_batch, seq, n_ctx):
    blk0 = n_batch * seq // n_ctx
    wide = pl.BlockSpec((n_ctx, 512), lambda b: (blk0 + b, 0))
    narrow = pl.BlockSpec((n_ctx, LANES), lambda b: (blk0 + b, 0))
    out = pl.BlockSpec((n_ctx, 512), lambda b: (b, 0))
    return pl.pallas_call(
        _ctx_attn_body,
        grid=(n_batch,),
        in_specs=[wide, narrow, narrow, wide, wide, wide,
                  _layer(sink_x, j)],
        out_specs=[out, out],
        out_shape=[jax.ShapeDtypeStruct((n_batch * n_ctx, 512), BF16)] * 2,
        compiler_params=_cparams("parallel"),
        name="context_attention",
    )(qw, kw, vw, qn, kn, vn, sink_x)


def _take(w, runs, axis):
    parts = []
    for run in runs:
        if run[0] is None:
            shape = list(w.shape)
            shape[axis] = run[1]
            parts.append(jnp.zeros(shape, w.dtype))
        else:
            parts.append(lax.slice_in_dim(w, run[0], run[1], axis=axis))
    return jnp.concatenate(parts, axis=axis)


def _xbc_runs():
    gn = SSD_GROUPS * SSD_STATE
    runs = []
    for g in range(SSD_GROUPS):
        runs.append((g * SSD_GROUP_W, (g + 1) * SSD_GROUP_W))
        runs.append((SSD_INNER + g * SSD_STATE, SSD_INNER + (g + 1) * SSD_STATE))
        runs.append((SSD_INNER + gn + g * SSD_STATE, SSD_INNER + gn + (g + 1) * SSD_STATE))
    return runs


def _head_rows(t):
    n = t.shape[0]
    flat = jnp.pad(t.astype(F32).reshape(n, 2 * SSD_HEADS), ((0, 0), (0, LANES - 2 * SSD_HEADS)))
    rows = [jnp.roll(flat, -SSD_HPG * g, axis=1) for g in range(SSD_GROUPS)]
    rows += [jnp.zeros_like(flat)] * (8 - SSD_GROUPS)
    return jnp.stack(rows, axis=1)


def _win_head_runs():
    runs = []
    for m in range(WIN_GQA):
        for kvh in range(WIN_KV_HEADS):
            h = kvh * WIN_GQA + m
            runs.append((h * HEAD_DIM, (h + 1) * HEAD_DIM))
    return runs


def _rope_tables(seq):
    pos = np.arange(seq)
    row = (pos // GRID_W).astype(np.float32)
    col = (pos % GRID_W).astype(np.float32)
    n_freq = HEAD_DIM // 4
    inv = (np.float32(ROPE_THETA) ** (-np.arange(n_freq, dtype=np.float32) / n_freq)).astype(np.float32)
    ar = (row[:, None] * inv).astype(np.float32)
    ac = (col[:, None] * inv).astype(np.float32)
    cos_h = np.concatenate([np.cos(ar), np.cos(ar), np.cos(ac), np.cos(ac)], axis=1)
    sin_h = np.concatenate([-np.sin(ar), np.sin(ar), -np.sin(ac), np.sin(ac)], axis=1)
    ident_c = np.ones((ROW_TILE, HEAD_DIM), np.float32)
    ident_s = np.zeros((ROW_TILE, HEAD_DIM), np.float32)
    cos_t = np.concatenate([cos_h, ident_c], axis=0)
    sin_t = np.concatenate([sin_h, ident_s], axis=0)
    return (jnp.asarray(np.tile(cos_t, (1, 2)), F32), jnp.asarray(np.tile(sin_t, (1, 2)), F32))


def _na_bias_table(rpb):
    n, h, nr, _ = rpb.shape
    kr = NA_ROWS
    cols = np.arange(GRID_W)
    col_start = np.clip(cols - NA_COLS // 2, 0, GRID_W - NA_COLS)
    col_ok = (cols[None] >= col_start[:, None]) & (cols[None] < col_start[:, None] + NA_COLS)
    r32 = rpb.astype(F32) * LOG2E
    ext = jnp.pad(r32, ((0, 0), (0, 0), (0, 0), (GRID_W - NA_COLS, GRID_W - NA_COLS)))
    toep = jnp.stack([ext[..., GRID_W - 1 - w:2 * GRID_W - 1 - w] for w in range(GRID_W)], axis=3)
    toep = jnp.where(jnp.asarray(col_ok), toep, NEG_INF)
    tabs = [jnp.concatenate([toep[:, :, ro0 + k] for k in range(kr)], axis=-1) for ro0 in range(NA_ROWS)]
    return jnp.stack(tabs, axis=1).reshape(n, NA_ROWS, h * GRID_W, kr * GRID_W)


def kernel(x, c, ctx, c_ctx, w_mod, b_mod, norm_mix_g, norm_ff_g, w_ff1, w_ff2, w_in_even, conv_w, conv_b,
           dt_bias, a_log, d_skip, ssd_norm_g, w_out_even, w_in_odd, q_norm_win, k_norm_win, sink_win,
           q_norm_na, k_norm_na, rpb_na, w_out_odd):
    n_batch, seq, d = x.shape
    n_ctx = ctx.shape[1]
    n_lat = n_batch * seq
    tm = ROW_TILE
    tiles_per_batch = seq // tm
    n_lat_tiles = n_lat // tm
    n_ctx_tiles = (n_batch * n_ctx) // tm
    n2 = seq // DFT_N1

    x2d, ctx2d = x.reshape(n_lat, d), ctx.reshape(n_batch * n_ctx, d)
    h = None

    cvec = jnp.concatenate([c, c_ctx[None], jnp.zeros((8 - n_batch - 1, d), F32)], axis=0)
    mod_all = _modulation(cvec, w_mod, b_mod)
    mod_all = mod_all.reshape(-1, 8, 6, 1, d)[:, :n_batch + 1].transpose(0, 2, 1, 3, 4)

    m_a, twc, tws, m_b, m_c, chan = _dft_tables(seq, n_ctx)
    xbc_runs = _xbc_runs()
    s3 = FNET_W + SSD_INNER + SSD_CONV_DIM
    w_four = _weight_product(w_in_even[:, :, :FNET_W], FNET_W, chan)
    tri = jnp.asarray(np.concatenate([np.tril(np.ones((SSD_CHUNK, SSD_CHUNK))),
                                      np.triu(np.ones((SSD_CHUNK, SSD_CHUNK)))], axis=0), F32)
    emat = np.zeros((LANES, 2 * SSD_GROUP_W), np.float32)
    for dr in range(2):
        for r in range(SSD_HPG):
            c0 = dr * SSD_GROUP_W + r * SSD_HEAD_DIM
            emat[SSD_HEADS * dr + r, c0:c0 + SSD_HEAD_DIM] = 1.0
    emat = jnp.asarray(emat, BF16)

    cos_t, sin_t = _rope_tables(seq)
    gmat_slab = jnp.asarray(np.kron(np.eye(MXU_DIM // HEAD_DIM), np.full((HEAD_DIM, HEAD_DIM), 1.0 / HEAD_DIM)),
                          BF16)
    gmat128 = gmat_slab[:LANES, :LANES]
    win_runs = _win_head_runs()
    wq = WIN_Q_HEADS * HEAD_DIM

    gains_mix = norm_mix_g.astype(F32)[:, None, :]
    gains_ff = norm_ff_g.astype(F32)[:, None, :]
    w_ff1_b, w_ff2_b = w_ff1.astype(BF16), w_ff2.astype(BF16)
    w_even_b = w_in_even.astype(BF16)
    w_dt_b = jnp.pad(w_even_b[:, :, s3:], ((0, 0), (0, 0), (0, LANES - 2 * SSD_HEADS)))
    w_out_even_b = w_out_even.astype(BF16)
    conv_w_p = _take(conv_w.astype(F32), xbc_runs, 2)
    conv_b_p = _take(conv_b.astype(F32)[:, None, :], xbc_runs, 2)
    dtb_rows, alog_rows = _head_rows(dt_bias), _head_rows(a_log)
    dsk_x = jnp.repeat(d_skip.astype(F32), SSD_HEAD_DIM, axis=1)[:, None, :]
    ssd_ng = ssd_norm_g.astype(F32)[:, None, :]
    w_odd_b = w_in_odd.astype(BF16)
    wq_perm = _take(w_odd_b[:, :, :wq], win_runs, 2)
    w_out_odd_b = w_out_odd.astype(BF16)
    wo_win_perm = _take(w_out_odd_b[:, :wq], win_runs, 1)
    qscale = HEAD_DIM ** -0.5 * LOG2E
    rep = lambda g, k: jnp.tile(g.astype(F32), (1, k))
    head_gains = jnp.stack([rep(q_norm_win, WIN_Q_HEADS) * qscale,
                            jnp.pad(rep(k_norm_win, WIN_KV_HEADS), ((0, 0), (0, wq - WIN_KV_HEADS * HEAD_DIM))),
                            rep(q_norm_na, NA_HEADS) * qscale, rep(k_norm_na, NA_HEADS)], axis=1)
    head_gains = jnp.pad(head_gains, ((0, 0), (0, 4), (0, 0)))
    sink_cm = _take(sink_win.astype(F32) * LOG2E, [(r[0] // HEAD_DIM, r[1] // HEAD_DIM) for r in win_runs], 1)
    sink_x = jnp.broadcast_to(jnp.repeat(sink_cm, WIN_BLOCK, axis=1)[:, :, None],
                              (sink_win.shape[0], WIN_Q_HEADS * WIN_BLOCK, LANES))
    bias_tab = _na_bias_table(rpb_na)

    for i in range(DEPTH):
        need_ctx = i < DEPTH - 1
        j = i // 2
        if i % 2 == 0:
            src = (x2d, ctx2d) if h is None else (h, None)
            fr, fi, z, xbc, dtr = _inproj_even(*src, i, j, gains_mix, mod_all, w_four, w_even_b, w_dt_b,
                                               n_batch, tiles_per_batch)

            br, bi = _dft_stage_a(fr.reshape(-1, n2, FNET_W), fi.reshape(-1, n2, FNET_W),
                                  m_a, twc, tws, n_batch, n2)
            f_lat = _dft_stage_b(br.reshape(n_batch, seq, FNET_W), bi.reshape(n_batch, seq, FNET_W),
                                 m_b, n_batch, n2).reshape(n_lat, FNET_W)
            f_ctx = _ctx_dft(fr, fi, m_c, n_batch, n_ctx, n_lat // n_ctx)

            y_lat, y_ctx = _ssd3(xbc, dtr, z, j, conv_w_p, conv_b_p, dtb_rows, alog_rows, dsk_x, ssd_ng, tri, emat,
                                 n_batch, seq, n_ctx)

            first = h is None
            h = _outproj(h, i, mod_all, f_lat, y_lat, w_out_even_b, w_out_even_b, j, FNET_W,
                         n_batch, tiles_per_batch, 0, 0, 0, n_lat_tiles,
                         res=x2d if first else None, dst_rows=n_lat + n_batch * n_ctx)
            h = _outproj(h, i, mod_all, f_ctx, y_ctx, w_out_even_b, w_out_even_b, j, FNET_W,
                         n_batch, tiles_per_batch, n_lat_tiles, 0, 0, n_ctx_tiles,
                         res=ctx2d if first else None)
        else:
            qw, kw, vw, qn, kn, vn = _inproj_odd(h, i, j, gains_mix, mod_all, wq_perm, w_odd_b, cos_t, sin_t,
                                                 gmat_slab, gmat128, head_gains, n_batch, tiles_per_batch)
            o_win = _window_attention(qw, kw, vw, sink_x, j, n_batch, seq, n_ctx)
            o_na = _na_attention(qn, kn, vn, bias_tab, j, n_batch, seq, n_ctx)
            if need_ctx:
                oc_win, oc_na = _ctx_attention(qw, kw, vw, qn, kn, vn, sink_x, j, n_batch, seq, n_ctx)
            h = _outproj(h, i, mod_all, o_win, o_na, wo_win_perm, w_out_odd_b, j, wq,
                         n_batch, tiles_per_batch, 0, 0, 0, n_lat_tiles)
            if need_ctx:
                h = _outproj(h, i, mod_all, oc_win, oc_na, wo_win_perm, w_out_odd_b, j, wq,
                             n_batch, tiles_per_batch, n_lat_tiles, 0, 0, n_ctx_tiles)
        n_tiles = n_lat_tiles + n_ctx_tiles if need_ctx else n_lat_tiles
        h = _ffn(h, i, gains_ff, mod_all, w_ff1_b, w_ff2_b, n_batch, tiles_per_batch, n_tiles)
    return h.reshape(n_batch, seq, d)
```

```python
import functools
import math

import numpy as np
import jax
import jax.numpy as jnp
from jax import lax
from jax.experimental import pallas as pl
from jax.experimental.pallas import tpu as pltpu

F32 = jnp.float32
BF16 = jnp.bfloat16
HIGHEST = lax.Precision.HIGHEST

D_MODEL = 1024
DEPTH = 4
GRID_W = 64
D_FF = 4 * D_MODEL
NORM_EPS = 1e-6
NEG_INF = -1e30

FNET_GROUPS = 8
FNET_GROUP_W = 64
FNET_W = FNET_GROUPS * FNET_GROUP_W
SSD_HEAD_DIM = 64
SSD_HEADS = 24
SSD_GROUPS = 4
SSD_HPG = SSD_HEADS // SSD_GROUPS
SSD_STATE = 128
SSD_INNER = SSD_HEADS * SSD_HEAD_DIM
SSD_GROUP_W = SSD_HPG * SSD_HEAD_DIM
SSD_XBC_W = SSD_GROUP_W + 2 * SSD_STATE
SSD_CONV_DIM = SSD_INNER + 2 * SSD_GROUPS * SSD_STATE
SSD_CONV_W = 5
SSD_CHUNK = 128
HEAD_DIM = 64
WIN_Q_HEADS = 8
WIN_KV_HEADS = 2
WIN_GQA = WIN_Q_HEADS // WIN_KV_HEADS
WIN_BLOCK = 128
WIN_RADIUS = 128
NA_HEADS = 8
NA_ROWS = 8
NA_COLS = 16
ROPE_THETA = 10000.0
LOG2E = 1.4426950408889634
WIN_W = WIN_Q_HEADS * HEAD_DIM
WIN_KV_W = WIN_KV_HEADS * HEAD_DIM
NA_W = NA_HEADS * HEAD_DIM

LANES = 128
SUBLANES = 8
MXU_DIM = 256
HALO = SUBLANES
ROW_TILE = 512
FF_TILE = 4096
DFT_N1 = 128
DFT_KB = SUBLANES
VMEM_LIMIT = 56 * 1024 * 1024


def _cparams(*sem):
    return pltpu.CompilerParams(dimension_semantics=sem, vmem_limit_bytes=VMEM_LIMIT)


def _silu(x):
    return x / (1.0 + jnp.exp2(x * -LOG2E))


def _softplus(x):
    return jnp.maximum(x, 0.0) + jnp.log(1.0 + jnp.exp(-jnp.abs(x)))


def _norm_mod(h, g, shift, scale):
    ms = jnp.mean(h * h, axis=-1, keepdims=True)
    y = h * lax.rsqrt(ms + NORM_EPS) * g
    return y * (1.0 + scale) + shift


def _dot(a, b):
    return jnp.dot(a, b, preferred_element_type=F32)


def _dot_nt(a, b):
    return lax.dot_general(a, b, (((1,), (1,)), ((), ())), preferred_element_type=F32)


def _mod_body(c_ref, w_ref, b_ref, o_ref):
    s = _silu(c_ref[...])
    o_ref[...] = jnp.dot(s, w_ref[...], precision=HIGHEST, preferred_element_type=F32) + b_ref[...]


def _modulation(cvec, w_mod, b_mod):
    depth, d, n = w_mod.shape
    tn = 1536
    return pl.pallas_call(
        _mod_body,
        grid=(depth, n // tn),
        in_specs=[pl.BlockSpec((SUBLANES, d), lambda i, j: (0, 0)),
                  pl.BlockSpec((None, d, tn), lambda i, j: (i, 0, j)),
                  pl.BlockSpec((None, 1, tn), lambda i, j: (i, 0, j))],
        out_specs=pl.BlockSpec((None, SUBLANES, tn), lambda i, j: (i, 0, j)),
        out_shape=jax.ShapeDtypeStruct((depth, SUBLANES, n), F32),
        compiler_params=_cparams("parallel", "parallel"),
        name="modulation",
    )(cvec, w_mod, b_mod.reshape(depth, 1, n))


def _wprod_body(a_ref, b_ref, o_ref):
    o_ref[...] = jnp.dot(a_ref[...], b_ref[...], precision=HIGHEST,
                         preferred_element_type=F32).astype(o_ref.dtype)


def _weight_product(a, k, b):
    n, m, _ = a.shape
    p = b.shape[1]
    return pl.pallas_call(
        _wprod_body,
        grid=(n,),
        in_specs=[pl.BlockSpec((None, m, k), lambda i: (i, 0, 0)),
                  pl.BlockSpec((k, p), lambda i: (0, 0))],
        out_specs=pl.BlockSpec((None, m, p), lambda i: (i, 0, 0)),
        out_shape=jax.ShapeDtypeStruct((n, m, p), BF16),
        compiler_params=_cparams("parallel"),
        name="fold_channel_dft",
    )(a, b)


def _resident(shape):
    return pl.BlockSpec(shape, lambda *_: (0,) * len(shape), pipeline_mode=pl.Buffered(1))


def _layer(arr, j):
    nd = arr.ndim
    return pl.BlockSpec((None,) + arr.shape[1:], lambda *_: (j,) + (0,) * (nd - 1),
                        pipeline_mode=pl.Buffered(1))


def _mod_spec(layer, which, tiles_per_batch, n_batch, tile_off):
    def imap(m):
        return (layer, which, jnp.minimum((m + tile_off) // tiles_per_batch, n_batch), 0, 0)
    return pl.BlockSpec((None, None, None, 1, D_MODEL), imap)


def _inproj_even_body(n_first, two_inputs, *refs):
    if two_inputs:
        h = jnp.where(pl.program_id(0) < n_first, refs[0][...], refs[1][...])
        refs = refs[2:]
    else:
        h = refs[0][...]
        refs = refs[1:]
    g_ref, sh_ref, sc_ref, wf_ref, w_ref, wdt_ref, fr_ref, fi_ref, z_ref, xbc_ref, dt_ref = refs
    u = _norm_mod(h, g_ref[...], sh_ref[...], sc_ref[...]).astype(BF16)
    fr_ref[...] = _dot(u, wf_ref[:, :FNET_W])
    fi_ref[...] = _dot(u, wf_ref[:, FNET_W:])
    z_ref[...] = _dot(u, w_ref[:, FNET_W:FNET_W + SSD_INNER]).astype(z_ref.dtype)
    c0 = FNET_W + SSD_INNER
    xbc = _dot(u, w_ref[:, c0:c0 + SSD_CONV_DIM])
    gn = SSD_GROUPS * SSD_STATE
    for g in range(SSD_GROUPS):
        o = g * SSD_XBC_W
        b0 = SSD_INNER + g * SSD_STATE
        xbc_ref[:, o:o + SSD_GROUP_W] = xbc[:, g * SSD_GROUP_W:(g + 1) * SSD_GROUP_W]
        xbc_ref[:, o + SSD_GROUP_W:o + SSD_GROUP_W + SSD_STATE] = xbc[:, b0:b0 + SSD_STATE]
        xbc_ref[:, o + SSD_GROUP_W + SSD_STATE:o + SSD_XBC_W] = xbc[:, b0 + gn:b0 + gn + SSD_STATE]
    dt_ref[...] = _dot(u, wdt_ref[...])


def _inproj_even(h, h2, layer, j, gains, mod_all, w_four, w_main, w_dt, n_batch, tiles_per_batch):
    tm = ROW_TILE
    rows = h.shape[0] + (0 if h2 is None else h2.shape[0])
    n_first = h.shape[0] // tm
    widths = (FNET_W, FNET_W, SSD_INNER, SSD_CONV_DIM, LANES)
    dtypes = (F32, F32, BF16, F32, F32)
    if h2 is None:
        h_specs, h_args = [pl.BlockSpec((tm, D_MODEL), lambda m: (m, 0))], (h,)
    else:
        h_specs = [pl.BlockSpec((tm, D_MODEL), lambda m: (jnp.minimum(m, n_first - 1), 0)),
                   pl.BlockSpec((tm, D_MODEL), lambda m: (jnp.maximum(m - n_first, 0), 0))]
        h_args = (h, h2)
    return pl.pallas_call(
        functools.partial(_inproj_even_body, n_first, h2 is not None),
        grid=(rows // tm,),
        in_specs=h_specs + [
                  _layer(gains, layer),
                  _mod_spec(layer, 0, tiles_per_batch, n_batch, 0),
                  _mod_spec(layer, 1, tiles_per_batch, n_batch, 0),
                  _layer(w_four, j), _layer(w_main, j), _layer(w_dt, j)],
        out_specs=[pl.BlockSpec((tm, wd), lambda m: (m, 0)) for wd in widths],
        out_shape=[jax.ShapeDtypeStruct((rows, wd), dt) for wd, dt in zip(widths, dtypes)],
        compiler_params=_cparams("parallel"),
        name="inproj_even",
    )(*h_args, gains, mod_all, mod_all, w_four, w_main, w_dt)


def _group_rms(x, gmat_ref, gain):
    sq = x * x
    hi = sq.astype(BF16)
    lo = (sq - hi.astype(F32)).astype(BF16)
    gw = gmat_ref.shape[0]
    parts = []
    for c0 in range(0, x.shape[1], gw):
        parts.append(_dot(hi[:, c0:c0 + gw], gmat_ref[...]) + _dot(lo[:, c0:c0 + gw], gmat_ref[...]))
    ms = parts[0] if len(parts) == 1 else jnp.concatenate(parts, axis=1)
    return x * lax.rsqrt(ms + NORM_EPS) * gain


def _rope(y, cos, sin_signed):
    w = y.shape[-1]
    reps = w // LANES
    lane = lax.broadcasted_iota(jnp.int32, y.shape, 1)
    first = (lane % 32) < 16
    partner = jnp.where(first, pltpu.roll(y, w - 16, 1), pltpu.roll(y, 16, 1))
    if reps > 1:
        cos = jnp.concatenate([cos] * reps, axis=1)
        sin_signed = jnp.concatenate([sin_signed] * reps, axis=1)
    return y * cos + partner * sin_signed


def _inproj_odd_body(h_ref, g_ref, sh_ref, sc_ref, wq_ref, w_ref, cos_ref, sin_ref, gslab_ref, g128_ref,
                     hg_ref, qw_ref, kw_ref, vw_ref, qn_ref, kn_ref, vn_ref):
    u = _norm_mod(h_ref[...], g_ref[...], sh_ref[...], sc_ref[...]).astype(BF16)
    cos = cos_ref[...]
    sin = sin_ref[...]
    wq = WIN_Q_HEADS * HEAD_DIM
    wk = WIN_KV_HEADS * HEAD_DIM
    nh = NA_HEADS * HEAD_DIM
    qw = _group_rms(_dot(u, wq_ref[...]), gslab_ref, hg_ref[0:1, :])
    qw_ref[...] = _rope(qw, cos, sin).astype(BF16)
    kw = _group_rms(_dot(u, w_ref[:, wq:wq + wk]), g128_ref, hg_ref[1:2, :wk])
    kw_ref[...] = _rope(kw, cos, sin).astype(BF16)
    vw_ref[...] = _dot(u, w_ref[:, wq + wk:wq + 2 * wk]).astype(BF16)
    c0 = wq + 2 * wk
    qn_ref[...] = _group_rms(_dot(u, w_ref[:, c0:c0 + nh]), gslab_ref, hg_ref[2:3, :]).astype(BF16)
    kn_ref[...] = _group_rms(_dot(u, w_ref[:, c0 + nh:c0 + 2 * nh]), gslab_ref, hg_ref[3:4, :]).astype(BF16)
    vn_ref[...] = _dot(u, w_ref[:, c0 + 2 * nh:c0 + 3 * nh]).astype(BF16)


def _inproj_odd(h, layer, j, gains, mod_all, wq_perm, w_main, cos_t, sin_t, gmat_slab, gmat128, head_gains,
                n_batch, tiles_per_batch):
    rows = h.shape[0]
    tm = ROW_TILE
    widths = (WIN_W, WIN_KV_W, WIN_KV_W, NA_W, NA_W, NA_W)
    n_lat_tiles = n_batch * tiles_per_batch

    def rope_map(m):
        return (jnp.where(m < n_lat_tiles, m % tiles_per_batch, tiles_per_batch), 0)

    return pl.pallas_call(
        _inproj_odd_body,
        grid=(rows // tm,),
        in_specs=[pl.BlockSpec((tm, D_MODEL), lambda m: (m, 0)),
                  _layer(gains, layer),
                  _mod_spec(layer, 0, tiles_per_batch, n_batch, 0),
                  _mod_spec(layer, 1, tiles_per_batch, n_batch, 0),
                  _layer(wq_perm, j), _layer(w_main, j),
                  pl.BlockSpec((tm, LANES), rope_map),
                  pl.BlockSpec((tm, LANES), rope_map),
                  _resident(gmat_slab.shape), _resident(gmat128.shape), _layer(head_gains, j)],
        out_specs=[pl.BlockSpec((tm, wd), lambda m: (m, 0)) for wd in widths],
        out_shape=[jax.ShapeDtypeStruct((rows, wd), BF16) for wd in widths],
        compiler_params=_cparams("parallel"),
        name="inproj_odd",
    )(h, gains, mod_all, mod_all, wq_perm, w_main, cos_t, sin_t, gmat_slab, gmat128, head_gains)


def _outproj_body(k1, k2, b_row0, h_ref, gate_ref, a1_ref, a2_ref, wa_ref, wb_ref, *rest):
    o_ref = rest[-1]
    acc = _dot(a1_ref[...].astype(BF16), wa_ref[0:k1, :])
    acc += _dot(a2_ref[...].astype(BF16), wb_ref[b_row0:b_row0 + k2, :])
    o_ref[...] = h_ref[...] + gate_ref[...] * acc


def _outproj(h, layer, mod_all, a1, a2, wa, wb, j, b_row0, n_batch, tiles_per_batch, tile_off, a1_off,
             a2_off, n_tiles, res=None, res_off=0, dst_rows=None):
    tm = ROW_TILE
    k1, k2 = a1.shape[1], a2.shape[1]
    in_place = res is None
    src, src_off = (h, tile_off) if in_place else (res, res_off)
    in_specs = [pl.BlockSpec((tm, D_MODEL), lambda m: (m + src_off, 0)),
                _mod_spec(layer, 2, tiles_per_batch, n_batch, tile_off),
                pl.BlockSpec((tm, k1), lambda m: (m + a1_off, 0)),
                pl.BlockSpec((tm, k2), lambda m: (m + a2_off, 0)),
                _layer(wa, j), _layer(wb, j)]
    args = [src, mod_all, a1, a2, wa, wb]
    aliases = {0: 0} if in_place else {}
    if not in_place and h is not None:
        in_specs.append(pl.BlockSpec(memory_space=pl.ANY))
        args.append(h)
        aliases = {len(args) - 1: 0}
    out_rows = dst_rows if h is None else h.shape[0]
    return pl.pallas_call(
        functools.partial(_outproj_body, k1, k2, b_row0),
        grid=(n_tiles,),
        in_specs=in_specs,
        out_specs=pl.BlockSpec((tm, D_MODEL), lambda m: (m + tile_off, 0)),
        out_shape=jax.ShapeDtypeStruct((out_rows, D_MODEL), F32),
        input_output_aliases=aliases,
        compiler_params=_cparams("parallel"),
        name="outproj",
    )(*args)


def _ffn_body(h_ref, g_ref, sh_ref, sc_ref, gate_ref, w1_ref, w2_ref, o_ref):
    h = h_ref[...]
    u = _norm_mod(h, g_ref[...], sh_ref[...], sc_ref[...]).astype(BF16)
    acc = None
    for k in range(D_FF // FF_TILE):
        sl = slice(k * FF_TILE, (k + 1) * FF_TILE)
        a = jnp.maximum(_dot(u, w1_ref[:, sl]), 0.0)
        t = _dot((a * a).astype(BF16), w2_ref[sl, :])
        acc = t if acc is None else acc + t
    o_ref[...] = h + gate_ref[...] * acc


def _ffn(h, layer, gains, mod_all, w1, w2, n_batch, tiles_per_batch, n_tiles):
    tm = ROW_TILE
    return pl.pallas_call(
        _ffn_body,
        grid=(n_tiles,),
        in_specs=[pl.BlockSpec((tm, D_MODEL), lambda m: (m, 0)),
                  _layer(gains, layer),
                  _mod_spec(layer, 3, tiles_per_batch, n_batch, 0),
                  _mod_spec(layer, 4, tiles_per_batch, n_batch, 0),
                  _mod_spec(layer, 5, tiles_per_batch, n_batch, 0),
                  _layer(w1, layer), _layer(w2, layer)],
        out_specs=pl.BlockSpec((tm, D_MODEL), lambda m: (m, 0)),
        out_shape=jax.ShapeDtypeStruct((n_tiles * tm, D_MODEL), F32),
        compiler_params=_cparams("parallel"),
        name="ffn",
    )(h, gains, mod_all, mod_all, mod_all, w1, w2)


def _dft_a_body(xr_ref, xi_ref, m_ref, c_ref, s_ref, br_ref, bi_ref):
    n1 = DFT_N1
    reps = FNET_W // LANES
    for j in range(xr_ref.shape[1]):
        x = jnp.concatenate([xr_ref[:, j, :], xi_ref[:, j, :]], axis=0).astype(BF16)
        a = _dot(m_ref[...], x)
        ar, ai = a[:n1], a[n1:]
        c = jnp.concatenate([c_ref[:, LANES * j:LANES * (j + 1)]] * reps, axis=1)
        s = jnp.concatenate([s_ref[:, LANES * j:LANES * (j + 1)]] * reps, axis=1)
        br_ref[:, j, :] = ar * c + ai * s
        bi_ref[:, j, :] = ai * c - ar * s


def _dft_stage_a(xr, xi, mmat, twc, tws, n_batch, n2):
    n1 = DFT_N1
    lb = SUBLANES
    blk = pl.BlockSpec((n1, lb, FNET_W), lambda b, j: (b, j, 0))
    return pl.pallas_call(
        _dft_a_body,
        grid=(n_batch, n2 // lb),
        in_specs=[blk, blk,
                  pl.BlockSpec((2 * n1, 2 * n1), lambda b, j: (0, 0)),
                  pl.BlockSpec((n1, lb * LANES), lambda b, j: (0, j)),
                  pl.BlockSpec((n1, lb * LANES), lambda b, j: (0, j))],
        out_specs=[blk, blk],
        out_shape=[jax.ShapeDtypeStruct((n_batch * n1, n2, FNET_W), F32)] * 2,
        compiler_params=_cparams("parallel", "parallel"),
        name="seq_dft_stage_a",
    )(xr, xi, mmat, twc, tws)


def _dft_b_body(br_ref, bi_ref, m_ref, o_ref):
    x = jnp.concatenate([br_ref[...], bi_ref[...]], axis=0).astype(BF16)
    res = _dot(m_ref[...], x)
    o_ref[...] = res.reshape(o_ref.shape)


def _dft_stage_b(br, bi, mmat, n_batch, n2):
    n1 = DFT_N1
    kb = DFT_KB
    return pl.pallas_call(
        _dft_b_body,
        grid=(n_batch, n1 // kb),
        in_specs=[pl.BlockSpec((None, kb * n2, FNET_W), lambda b, j: (b, j, 0)),
                  pl.BlockSpec((None, kb * n2, FNET_W), lambda b, j: (b, j, 0)),
                  pl.BlockSpec(mmat.shape, lambda b, j: (0, 0))],
        out_specs=pl.BlockSpec((None, n2, kb, FNET_W), lambda b, j: (b, 0, j, 0)),
        out_shape=jax.ShapeDtypeStruct((n_batch, n2, n1, FNET_W), F32),
        compiler_params=_cparams("parallel", "parallel"),
        name="seq_dft_stage_b",
    )(br, bi, mmat)


def _ctx_dft_body(xr_ref, xi_ref, m_ref, o_ref):
    x = jnp.concatenate([xr_ref[...], xi_ref[...]], axis=0).astype(BF16)
    o_ref[...] = _dot(m_ref[...], x)


def _ctx_dft(fr, fi, mmat, n_batch, n_ctx, row_block_off):
    return pl.pallas_call(
        _ctx_dft_body,
        grid=(n_batch,),
        in_specs=[pl.BlockSpec((n_ctx, FNET_W), lambda b: (row_block_off + b, 0)),
                  pl.BlockSpec((n_ctx, FNET_W), lambda b: (row_block_off + b, 0)),
                  pl.BlockSpec(mmat.shape, lambda b: (0, 0))],
        out_specs=pl.BlockSpec((n_ctx, FNET_W), lambda b: (b, 0)),
        out_shape=jax.ShapeDtypeStruct((n_batch * n_ctx, FNET_W), F32),
        compiler_params=_cparams("parallel"),
        name="ctx_dft",
    )(fr, fi, mmat)


def _dft_tables(seq, n_ctx):
    n1 = DFT_N1
    n2 = seq // n1
    k1 = np.arange(n1)
    ang1 = 2.0 * np.pi * np.outer(k1, k1) / n1
    c1, s1 = np.cos(ang1), np.sin(ang1)
    m_a = np.block([[c1, s1], [-s1, c1]])
    ang_t = 2.0 * np.pi * np.outer(k1, np.arange(n2)) / seq
    twc = np.repeat(np.cos(ang_t), LANES, axis=1)
    tws = np.repeat(np.sin(ang_t), LANES, axis=1)
    k2 = np.arange(n2)
    ang2 = 2.0 * np.pi * np.outer(k2, k2) / n2
    scale = 1.0 / math.sqrt(seq)
    c2, s2 = np.cos(ang2) * scale, np.sin(ang2) * scale
    kb = DFT_KB
    eye = np.eye(kb)
    m_b = np.concatenate([np.einsum("kl,ab->kabl", c2, eye).reshape(n2 * kb, kb * n2),
                          np.einsum("kl,ab->kabl", s2, eye).reshape(n2 * kb, kb * n2)], axis=1)
    kc = np.arange(n_ctx)
    angc = 2.0 * np.pi * np.outer(kc, kc) / n_ctx
    m_c = np.concatenate([np.cos(angc), np.sin(angc)], axis=1) / math.sqrt(n_ctx)
    ch = np.arange(FNET_GROUP_W)
    angg = 2.0 * np.pi * np.outer(ch, ch) / FNET_GROUP_W
    eg = np.eye(FNET_GROUPS)
    chan = np.concatenate([np.kron(eg, np.cos(angg)), -np.kron(eg, np.sin(angg))], axis=1)
    chan = chan / math.sqrt(FNET_GROUP_W)
    return (jnp.asarray(m_a, BF16), jnp.asarray(twc, F32), jnp.asarray(tws, F32),
            jnp.asarray(m_b, BF16), jnp.asarray(m_c, BF16), jnp.asarray(chan, F32))


def _split_dot_rhs(w_bf16, x):
    hi = x.astype(BF16)
    r1 = x - hi.astype(F32)
    mid = r1.astype(BF16)
    lo = (r1 - mid.astype(F32)).astype(BF16)
    return _dot(w_bf16, hi) + _dot(w_bf16, mid) + _dot(w_bf16, lo)


def _split2_dot(x, w_bf16):
    hi = x.astype(BF16)
    lo = (x - hi.astype(F32)).astype(BF16)
    return _dot(hi, w_bf16) + _dot(lo, w_bf16)


SSD_LAT_CPS = 8


def _ssd_local(xfull, dt_raw, g, pos0, n_chunks, cw_ref, cb_ref, dtb_ref, alog_ref, dsk_ref, tri_ref, e_ref,
               yloc_ref, s_ref, c_ref, ea_ref, cd_ref):
    q = SSD_CHUNK
    gw = SSD_GROUP_W
    rows_blk = n_chunks * q
    conv = cb_ref[...]
    for k in range(SSD_CONV_W):
        shift = (SSD_CONV_W // 2 - k) % xfull.shape[0]
        xk = xfull if shift == 0 else pltpu.roll(xfull, shift, 0)
        conv = conv + xk[HALO:HALO + rows_blk, :] * cw_ref[k:k + 1, :]
    act = _silu(conv)
    xs = act[:, :gw]
    bm = act[:, gw:gw + SSD_STATE]
    cm = act[:, gw + SSD_STATE:]

    lane = lax.broadcasted_iota(jnp.int32, (1, LANES), 1)
    is_fwd = lane < SSD_HPG
    dsel = pltpu.roll(dt_raw, (LANES - SSD_HPG * g) % LANES, 1)
    dtv = _softplus(dsel + dtb_ref[pl.ds(g, 1), :])
    head_lane = is_fwd | ((lane >= SSD_HEADS) & (lane < SSD_HEADS + SSD_HPG))
    a_row = jnp.where(head_lane, -jnp.exp(alog_ref[pl.ds(g, 1), :]) * LOG2E, 0.0)
    adt = dtv * a_row
    tri = tri_ref[...]
    tri16 = tri.astype(BF16)
    allowed = (tri[:q] > 0.0, tri[q:] > 0.0)
    emat = e_ref[...]
    lane_q = lax.broadcasted_iota(jnp.int32, (q, LANES), 1)
    lo_half = lane_q < SSD_HEAD_DIM
    for c0 in range(0, n_chunks, 2):
        pair = range(c0, min(c0 + 2, n_chunks))
        prep = {}
        for cc in pair:
            rows = slice(cc * q, (cc + 1) * q)
            cums = _split_dot_rhs(tri16, adt[rows])
            acum = jnp.where(is_fwd, cums[:q], cums[q:])
            end = jnp.where(is_fwd, acum[q - 1:q, :], acum[0:1, :])
            dte = jnp.exp2(end - acum)
            ex = _dot(jnp.concatenate([dtv[rows], dte], axis=0).astype(BF16), emat)
            cdx = _split2_dot(jnp.broadcast_to(jnp.exp2(end), (16, LANES)), emat)
            c16 = cm[rows].astype(BF16)
            b_c = bm[rows]
            cbm = _dot_nt(c16, b_c.astype(BF16))
            prep[cc] = (acum, acum.T, ex, cdx, c16, b_c, cbm)
        for cc in pair:
            rows = slice(cc * q, (cc + 1) * q)
            acum, acum_t, ex, cdx, c16, b_c, cbm = prep[cc]
            xs_c = xs[rows]
            xdt2 = jnp.concatenate([xs_c, xs_c], axis=1) * ex[:q]
            ys = []
            for m in range(SSD_HPG // 2):
                mms, xhs = [], []
                for d in range(2):
                    xpair = xdt2[:, gw * d + LANES * m:gw * d + LANES * (m + 1)]
                    for half in range(2):
                        ln = SSD_HEADS * d + 2 * m + half
                        seg = acum[:, ln:ln + 1] - acum_t[ln:ln + 1, :]
                        dec = jnp.exp2(jnp.where(allowed[d], seg, NEG_INF))
                        mms.append((cbm * dec).astype(BF16))
                        xhs.append(jnp.where(lo_half if half == 0 else ~lo_half, xpair, 0.0).astype(BF16))
                ys.append(_dot(jnp.concatenate(mms, axis=1), jnp.concatenate(xhs, axis=0)))
            y_loc = jnp.concatenate(ys, axis=1) + dsk_ref[...] * xs_c
            sts = _dot(b_c.T.astype(BF16), (xdt2 * ex[q:2 * q]).astype(BF16))
            pos = pos0 + cc
            r0 = pl.multiple_of(pos * q, q)
            yloc_ref[pl.ds(r0, q), :] = y_loc
            s_ref[pl.ds(r0, q), :] = sts.astype(BF16)
            c_ref[pl.ds(r0, q), :] = c16
            ea_ref[pl.ds(r0, q), :] = jnp.exp2(acum).astype(BF16)
            cd_ref[pl.ds(pl.multiple_of(pos * SUBLANES, SUBLANES), SUBLANES), :] = cdx[0:SUBLANES]


def _ssd_emit(z_ref, o_ref, pos0, n_chunks, hf, ng_ref, e_ref, yloc_ref, s_ref, c_ref, ea_ref, cd_ref, hbe_ref):
    q = SSD_CHUNK
    gw = SSD_GROUP_W
    emat = e_ref[...]
    starts = [pl.multiple_of((pos0 + cc) * q, q) for cc in range(n_chunks)]
    ea_x = [_dot(ea_ref[pl.ds(r0, q), :], emat) for r0 in starts]
    gate = [_silu(z_ref[cc * q:(cc + 1) * q, :].astype(F32)) for cc in range(n_chunks)]
    for cc in range(n_chunks):
        r0 = starts[cc]
        hcat = jnp.concatenate([hf.astype(BF16), hbe_ref[pl.ds(r0, q), :]], axis=1)
        yo = _dot(c_ref[pl.ds(r0, q), :], hcat) * ea_x[cc]
        y = yloc_ref[pl.ds(r0, q), :] + yo[:, :gw] + yo[:, gw:]
        cd = cd_ref[pl.ds(pl.multiple_of((pos0 + cc) * SUBLANES, SUBLANES), SUBLANES), :][0:1, :gw]
        hf = hf * cd + s_ref[pl.ds(r0, q), :gw].astype(F32)
        y = y * gate[cc]
        y = y * lax.rsqrt(jnp.mean(y * y, axis=-1, keepdims=True) + NORM_EPS) * ng_ref[...]
        o_ref[cc * q:(cc + 1) * q, :] = y.astype(o_ref.dtype)
    return hf


def _ssd3_body(nb, ncc, xm_ref, xp_ref, xn_ref, dt_ref, z_ref, xc_ref, dtc_ref, zc_ref, cw_ref, cb_ref, dtb_ref,
               alog_ref, dsk_ref, ng_ref, tri_ref, e_ref, o_ref, oc_ref,
               yloc_ref, s_ref, c_ref, ea_ref, cd_ref, hbe_ref, hf_ref, hb_ref):
    g = pl.program_id(1)
    t = pl.program_id(2)
    q = SSD_CHUNK
    gw = SSD_GROUP_W
    ns = ncc + nb * SSD_LAT_CPS
    local_refs = (cw_ref, cb_ref, dtb_ref, alog_ref, dsk_ref, tri_ref, e_ref, yloc_ref, s_ref, c_ref, ea_ref,
                  cd_ref)
    emit_refs = (ng_ref, e_ref, yloc_ref, s_ref, c_ref, ea_ref, cd_ref, hbe_ref)

    @pl.when(t == 0)
    def _local_context():
        pad = jnp.zeros((HALO, SSD_XBC_W), F32)
        _ssd_local(jnp.concatenate([pad, xc_ref[...], pad], axis=0), dtc_ref[...], g, 0, ncc, *local_refs)

    @pl.when(t < nb)
    def _local_latent():
        prev = jnp.where(t == 0, 0.0, xp_ref[...])
        nxt = jnp.where(t == nb - 1, 0.0, xn_ref[...])
        xfull = jnp.concatenate([prev, xm_ref[...], nxt], axis=0)
        _ssd_local(xfull, dt_ref[...], g, ncc + SSD_LAT_CPS * t, SSD_LAT_CPS, *local_refs)

    @pl.when(t == nb)
    def _backward_states_and_context():
        hb_ref[...] = jnp.zeros_like(hb_ref)

        def body(i, carry):
            pos = jnp.where(i < ncc, ncc - 1 - i, ns - 1 + ncc - i)
            r0 = pl.multiple_of(pos * q, q)
            hb = hb_ref[...]
            hbe_ref[pl.ds(r0, q), :] = hb.astype(BF16)
            cd = cd_ref[pl.ds(pl.multiple_of(pos * SUBLANES, SUBLANES), SUBLANES), :][0:1, gw:]
            hb_ref[...] = hb * cd + s_ref[pl.ds(r0, q), gw:].astype(F32)
            return carry

        lax.fori_loop(0, ns, body, 0)
        hf_ref[...] = _ssd_emit(zc_ref, oc_ref, 0, ncc, jnp.zeros(hf_ref.shape, F32), *emit_refs)

    @pl.when(t >= nb)
    def _emit_latent():
        hf_ref[...] = _ssd_emit(z_ref, o_ref, ncc + SSD_LAT_CPS * (t - nb), SSD_LAT_CPS, hf_ref[...], *emit_refs)


def _ssd3(xbc, dt, z, j, conv_w, conv_b, dtb_tab, alog_tab, dsk_x, ng, tri, emat, n_batch, seq, n_ctx):
    rows = xbc.shape[0]
    q = SSD_CHUNK
    blk = SSD_LAT_CPS * q
    assert n_ctx % q == 0 and seq % blk == 0 and (n_batch * seq) % n_ctx == 0
    ncc = n_ctx // q
    nb = seq // blk
    ns = ncc + nb * SSD_LAT_CPS
    per = blk // HALO
    ctx_blk0 = n_batch * seq // n_ctx
    last_halo = rows // HALO - 1
    gw = SSD_GROUP_W

    def rb_in(b, t):
        return b * nb + jnp.minimum(t, nb - 1)

    def rb_out(b, t):
        return b * nb + jnp.maximum(t - nb, 0)

    return pl.pallas_call(
        functools.partial(_ssd3_body, nb, ncc),
        grid=(n_batch, SSD_GROUPS, 2 * nb),
        in_specs=[pl.BlockSpec((blk, SSD_XBC_W), lambda b, g, t: (rb_in(b, t), g)),
                  pl.BlockSpec((HALO, SSD_XBC_W),
                               lambda b, g, t: (jnp.maximum(rb_in(b, t) * per - 1, 0), g)),
                  pl.BlockSpec((HALO, SSD_XBC_W),
                               lambda b, g, t: (jnp.minimum(rb_in(b, t) * per + per, last_halo), g)),
                  pl.BlockSpec((blk, LANES), lambda b, g, t: (rb_in(b, t), 0)),
                  pl.BlockSpec((blk, gw), lambda b, g, t: (rb_out(b, t), g)),
                  pl.BlockSpec((n_ctx, SSD_XBC_W), lambda b, g, t: (ctx_blk0 + b, g)),
                  pl.BlockSpec((n_ctx, LANES), lambda b, g, t: (ctx_blk0 + b, 0)),
                  pl.BlockSpec((n_ctx, gw), lambda b, g, t: (ctx_blk0 + b, g)),
                  pl.BlockSpec((None, SSD_CONV_W, SSD_XBC_W), lambda b, g, t: (j, 0, g)),
                  pl.BlockSpec((None, 1, SSD_XBC_W), lambda b, g, t: (j, 0, g)),
                  pl.BlockSpec((None, SUBLANES, LANES), lambda b, g, t: (j, 0, 0)),
                  pl.BlockSpec((None, SUBLANES, LANES), lambda b, g, t: (j, 0, 0)),
                  pl.BlockSpec((None, 1, gw), lambda b, g, t: (j, 0, g)),
                  pl.BlockSpec((None, 1, gw), lambda b, g, t: (j, 0, g)),
                  pl.BlockSpec((2 * q, q), lambda b, g, t: (0, 0)),
                  pl.BlockSpec((LANES, 2 * gw), lambda b, g, t: (0, 0))],
        out_specs=[pl.BlockSpec((blk, gw), lambda b, g, t: (rb_out(b, t), g)),
                   pl.BlockSpec((n_ctx, gw), lambda b, g, t: (b, g))],
        out_shape=[jax.ShapeDtypeStruct((n_batch * seq, SSD_INNER), BF16),
                   jax.ShapeDtypeStruct((n_batch * n_ctx, SSD_INNER), BF16)],
        scratch_shapes=[pltpu.VMEM((ns * q, gw), F32),
                        pltpu.VMEM((ns * q, 2 * gw), BF16),
                        pltpu.VMEM((ns * q, SSD_STATE), BF16),
                        pltpu.VMEM((ns * q, LANES), BF16),
                        pltpu.VMEM((ns * SUBLANES, 2 * gw), F32),
                        pltpu.VMEM((ns * q, gw), BF16),
                        pltpu.VMEM((SSD_STATE, gw), F32),
                        pltpu.VMEM((SSD_STATE, gw), F32)],
        compiler_params=_cparams("arbitrary", "arbitrary", "arbitrary"),
        name="ssd_bidir",
    )(xbc, xbc, xbc, dt, z, xbc, dt, z, conv_w, conv_b, dtb_tab, alog_tab, dsk_x, ng, tri, emat)


def _stack_heads(qc):
    lane = lax.broadcasted_iota(jnp.int32, qc.shape, 1)
    lo = lane < HEAD_DIM
    zero = jnp.zeros_like(qc)
    return jnp.concatenate([jnp.where(lo, qc, zero), jnp.where(lo, zero, qc)], axis=0)


def _unstack_heads(o, n):
    lane = lax.broadcasted_iota(jnp.int32, (n, LANES), 1)
    return jnp.where(lane < HEAD_DIM, o[:n], o[n:])


def _softmax_pv(scores, values, sink=None):
    m = None
    for sc in scores:
        mx = jnp.max(sc, axis=-1, keepdims=True)
        m = mx if m is None else jnp.maximum(m, mx)
    den = None
    if sink is not None:
        m = jnp.maximum(m, jnp.max(sink, axis=-1, keepdims=True))
        den = jnp.sum(jnp.exp2(sink - m), axis=-1, keepdims=True) * (1.0 / sink.shape[-1])
    acc = None
    for sc, v in zip(scores, values):
        pr = jnp.exp2(sc - m)
        sm = jnp.sum(pr, axis=-1, keepdims=True)
        den = sm if den is None else den + sm
        t = _dot(pr.astype(BF16), v)
        acc = t if acc is None else acc + t
    return acc * (1.0 / den)


WIN_BLOCKS_PER_STEP = 2


def _win_body(seq, q_ref, k_ref, v_ref, kc_ref, vc_ref, sink_ref, o_ref):
    wb = WIN_BLOCK
    r = lax.broadcasted_iota(jnp.int32, (wb, 3 * wb), 0)
    c = lax.broadcasted_iota(jnp.int32, (wb, 3 * wb), 1)
    for i in range(WIN_BLOCKS_PER_STEP):
        n = pl.program_id(1) * WIN_BLOCKS_PER_STEP + i
        qrows = slice(i * wb, (i + 1) * wb)
        start = pl.multiple_of(jnp.clip((n - 1) * wb, 0, seq - 3 * wb), wb)
        kb = k_ref[pl.ds(start, 3 * wb), :]
        vb = v_ref[pl.ds(start, 3 * wb), :]
        dist = (c + start) - (r + n * wb)
        pen = jnp.where(jnp.abs(dist) <= WIN_RADIUS, 0.0, NEG_INF)
        pen = jnp.concatenate([pen, pen], axis=0)
        scores = []
        for m in range(WIN_GQA):
            qs = _stack_heads(q_ref[qrows, LANES * m:LANES * (m + 1)])
            scores.append([_dot_nt(qs, kb) + pen, _dot_nt(qs, kc_ref[...])])
        for m in range(WIN_GQA):
            o = _softmax_pv(scores[m], [vb, vc_ref[...]], sink_ref[2 * m * wb:2 * (m + 1) * wb, :])
            o_ref[qrows, LANES * m:LANES * (m + 1)] = _unstack_heads(o, wb).astype(o_ref.dtype)


def _window_attention(qw, kw, vw, sink_x, j, n_batch, seq, n_ctx):
    wb = WIN_BLOCK
    assert seq // wb >= 3 and seq % (wb * WIN_BLOCKS_PER_STEP) == 0
    qrows = wb * WIN_BLOCKS_PER_STEP
    nb = seq // qrows
    ctx_blk0 = n_batch * seq // n_ctx
    q_spec = pl.BlockSpec((qrows, WIN_W), lambda b, n: (b * nb + n, 0))
    seq_spec = pl.BlockSpec((seq, LANES), lambda b, n: (b, 0))
    cspec = pl.BlockSpec((n_ctx, LANES), lambda b, n: (ctx_blk0 + b, 0))
    return pl.pallas_call(
        functools.partial(_win_body, seq),
        grid=(n_batch, nb),
        in_specs=[q_spec, seq_spec, seq_spec, cspec, cspec, _layer(sink_x, j)],
        out_specs=pl.BlockSpec((qrows, WIN_W), lambda b, n: (b * nb + n, 0)),
        out_shape=jax.ShapeDtypeStruct((n_batch * seq, WIN_W), BF16),
        compiler_params=_cparams("parallel", "arbitrary"),
        name="window_attention",
    )(qw, kw, vw, kw, vw, sink_x)


NA_ROWS_PER_STEP = 8


def _na_body(n_rows, q_ref, k_ref, v_ref, kc_ref, vc_ref, bias_ref, o_ref):
    kr = NA_ROWS
    for j in range(NA_ROWS_PER_STEP):
        r = pl.program_id(1) * NA_ROWS_PER_STEP + j
        first_row = jnp.clip(r - kr // 2, 0, n_rows - kr)
        start = pl.multiple_of(first_row * GRID_W, GRID_W)
        ro0 = first_row - r + NA_ROWS - 1
        rows = slice(j * GRID_W, (j + 1) * GRID_W)
        scores = []
        for m in range(NA_HEADS // 2):
            sl = slice(LANES * m, LANES * (m + 1))
            qs = _stack_heads(q_ref[rows, sl])
            kw = k_ref[pl.ds(start, kr * GRID_W), sl]
            s_loc = _dot_nt(qs, kw) + bias_ref[ro0, 2 * GRID_W * m:2 * GRID_W * (m + 1), :]
            scores.append((s_loc, _dot_nt(qs, kc_ref[:, sl])))
        for m in range(NA_HEADS // 2):
            sl = slice(LANES * m, LANES * (m + 1))
            vw = v_ref[pl.ds(start, kr * GRID_W), sl]
            o = _softmax_pv(list(scores[m]), [vw, vc_ref[:, sl]])
            o_ref[rows, sl] = _unstack_heads(o, GRID_W).astype(o_ref.dtype)


def _na_attention(qn, kn, vn, bias_tab, j, n_batch, seq, n_ctx):
    n_rows = seq // GRID_W
    assert n_rows >= NA_ROWS and n_rows % NA_ROWS_PER_STEP == 0
    n_steps = n_rows // NA_ROWS_PER_STEP
    qrows = NA_ROWS_PER_STEP * GRID_W
    ctx_blk0 = n_batch * seq // n_ctx
    return pl.pallas_call(
        functools.partial(_na_body, n_rows),
        grid=(n_batch, n_steps),
        in_specs=[pl.BlockSpec((qrows, NA_W), lambda b, r: (b * n_steps + r, 0)),
                  pl.BlockSpec((seq, NA_W), lambda b, r: (b, 0)),
                  pl.BlockSpec((seq, NA_W), lambda b, r: (b, 0)),
                  pl.BlockSpec((n_ctx, NA_W), lambda b, r: (ctx_blk0 + b, 0)),
                  pl.BlockSpec((n_ctx, NA_W), lambda b, r: (ctx_blk0 + b, 0)),
                  _layer(bias_tab, j)],
        out_specs=pl.BlockSpec((qrows, NA_W), lambda b, r: (b * n_steps + r, 0)),
        out_shape=jax.ShapeDtypeStruct((n_batch * seq, NA_W), BF16),
        compiler_params=_cparams("parallel", "arbitrary"),
        name="neighbourhood_attention",
    )(qn, kn, vn, kn, vn, bias_tab)


def _ctx_attn_body(qw_ref, kw_ref, vw_ref, qn_ref, kn_ref, vn_ref, sink_ref, ow_ref, on_ref):
    n = qw_ref.shape[0]
    qw = qw_ref[...]
    lane = lax.broadcasted_iota(jnp.int32, (n, LANES), 1)
    lo = lane < HEAD_DIM
    for m in range(WIN_GQA):
        qs = _stack_heads(qw[:, LANES * m:LANES * (m + 1)])
        r0, r1 = 2 * m * WIN_BLOCK, (2 * m + 1) * WIN_BLOCK
        sink = jnp.concatenate([jnp.broadcast_to(sink_ref[r0:r0 + 1, :], (n, LANES)),
                                jnp.broadcast_to(sink_ref[r1:r1 + 1, :], (n, LANES))], axis=0)
        o = _softmax_pv([_dot_nt(qs, kw_ref[...])], [vw_ref[...]], sink)
        ow_ref[:, LANES * m:LANES * (m + 1)] = jnp.where(lo, o[:n], o[n:]).astype(ow_ref.dtype)
    qn = qn_ref[...]
    for m in range(NA_HEADS // 2):
        sl = slice(LANES * m, LANES * (m + 1))
        qs = _stack_heads(qn[:, sl])
        o = _softmax_pv([_dot_nt(qs, kn_ref[:, sl])], [vn_ref[:, sl]])
        on_ref[:, sl] = _unstack_heads(o, n).astype(on_ref.dtype)


def _ctx_attention(qw, kw, vw, qn, kn, vn, sink_x, j, n_batch, seq, n_ctx):
    blk0 = n_batch * seq // n_ctx
    assert WIN_W == NA_W
    wide = pl.BlockSpec((n_ctx, NA_W), lambda b: (blk0 + b, 0))
    narrow = pl.BlockSpec((n_ctx, WIN_KV_W), lambda b: (blk0 + b, 0))
    out = pl.BlockSpec((n_ctx, NA_W), lambda b: (b, 0))
    return pl.pallas_call(
        _ctx_attn_body,
        grid=(n_batch,),
        in_specs=[wide, narrow, narrow, wide, wide, wide,
                  _layer(sink_x, j)],
        out_specs=[out, out],
        out_shape=[jax.ShapeDtypeStruct((n_batch * n_ctx, NA_W), BF16)] * 2,
        compiler_params=_cparams("parallel"),
        name="context_attention",
    )(qw, kw, vw, qn, kn, vn, sink_x)


def _take(w, runs, axis):
    return jnp.concatenate([lax.slice_in_dim(w, a, b, axis=axis) for a, b in runs], axis=axis)


def _xbc_runs():
    gn = SSD_GROUPS * SSD_STATE
    runs = []
    for g in range(SSD_GROUPS):
        runs.append((g * SSD_GROUP_W, (g + 1) * SSD_GROUP_W))
        runs.append((SSD_INNER + g * SSD_STATE, SSD_INNER + (g + 1) * SSD_STATE))
        runs.append((SSD_INNER + gn + g * SSD_STATE, SSD_INNER + gn + (g + 1) * SSD_STATE))
    return runs


def _head_rows(t):
    n = t.shape[0]
    flat = jnp.pad(t.astype(F32).reshape(n, 2 * SSD_HEADS), ((0, 0), (0, LANES - 2 * SSD_HEADS)))
    rows = [jnp.roll(flat, -SSD_HPG * g, axis=1) for g in range(SSD_GROUPS)]
    rows += [jnp.zeros_like(flat)] * (SUBLANES - SSD_GROUPS)
    return jnp.stack(rows, axis=1)


def _win_head_runs():
    runs = []
    for m in range(WIN_GQA):
        for kvh in range(WIN_KV_HEADS):
            h = kvh * WIN_GQA + m
            runs.append((h * HEAD_DIM, (h + 1) * HEAD_DIM))
    return runs


def _rope_tables(seq):
    pos = np.arange(seq)
    row = (pos // GRID_W).astype(np.float32)
    col = (pos % GRID_W).astype(np.float32)
    n_freq = HEAD_DIM // 4
    inv = (np.float32(ROPE_THETA) ** (-np.arange(n_freq, dtype=np.float32) / n_freq)).astype(np.float32)
    ar = (row[:, None] * inv).astype(np.float32)
    ac = (col[:, None] * inv).astype(np.float32)
    cos_h = np.concatenate([np.cos(ar), np.cos(ar), np.cos(ac), np.cos(ac)], axis=1)
    sin_h = np.concatenate([-np.sin(ar), np.sin(ar), -np.sin(ac), np.sin(ac)], axis=1)
    ident_c = np.ones((ROW_TILE, HEAD_DIM), np.float32)
    ident_s = np.zeros((ROW_TILE, HEAD_DIM), np.float32)
    cos_t = np.concatenate([cos_h, ident_c], axis=0)
    sin_t = np.concatenate([sin_h, ident_s], axis=0)
    return (jnp.asarray(np.tile(cos_t, (1, 2)), F32), jnp.asarray(np.tile(sin_t, (1, 2)), F32))


def _na_bias_table(rpb):
    n, h, nr, _ = rpb.shape
    kr = NA_ROWS
    cols = np.arange(GRID_W)
    col_start = np.clip(cols - NA_COLS // 2, 0, GRID_W - NA_COLS)
    col_ok = (cols[None] >= col_start[:, None]) & (cols[None] < col_start[:, None] + NA_COLS)
    r32 = rpb.astype(F32) * LOG2E
    ext = jnp.pad(r32, ((0, 0), (0, 0), (0, 0), (GRID_W - NA_COLS, GRID_W - NA_COLS)))
    toep = jnp.stack([ext[..., GRID_W - 1 - w:2 * GRID_W - 1 - w] for w in range(GRID_W)], axis=3)
    toep = jnp.where(jnp.asarray(col_ok), toep, NEG_INF)
    tabs = [jnp.concatenate([toep[:, :, ro0 + k] for k in range(kr)], axis=-1) for ro0 in range(NA_ROWS)]
    return jnp.stack(tabs, axis=1).reshape(n, NA_ROWS, h * GRID_W, kr * GRID_W)


def kernel(x, c, ctx, c_ctx, w_mod, b_mod, norm_mix_g, norm_ff_g, w_ff1, w_ff2, w_in_even, conv_w, conv_b,
           dt_bias, a_log, d_skip, ssd_norm_g, w_out_even, w_in_odd, q_norm_win, k_norm_win, sink_win,
           q_norm_na, k_norm_na, rpb_na, w_out_odd):
    n_batch, seq, d = x.shape
    n_ctx = ctx.shape[1]
    n_lat = n_batch * seq
    tm = ROW_TILE
    tiles_per_batch = seq // tm
    n_lat_tiles = n_lat // tm
    n_ctx_tiles = (n_batch * n_ctx) // tm
    n2 = seq // DFT_N1

    x2d, ctx2d = x.reshape(n_lat, d), ctx.reshape(n_batch * n_ctx, d)
    h = None

    cvec = jnp.concatenate([c, c_ctx[None], jnp.zeros((SUBLANES - n_batch - 1, d), F32)], axis=0)
    mod_all = _modulation(cvec, w_mod, b_mod)
    mod_all = mod_all.reshape(-1, SUBLANES, 6, 1, d)[:, :n_batch + 1].transpose(0, 2, 1, 3, 4)

    m_a, twc, tws, m_b, m_c, chan = _dft_tables(seq, n_ctx)
    xbc_runs = _xbc_runs()
    s3 = FNET_W + SSD_INNER + SSD_CONV_DIM
    w_four = _weight_product(w_in_even[:, :, :FNET_W], FNET_W, chan)
    tri = jnp.asarray(np.concatenate([np.tril(np.ones((SSD_CHUNK, SSD_CHUNK))),
                                      np.triu(np.ones((SSD_CHUNK, SSD_CHUNK)))], axis=0), F32)
    emat = np.zeros((LANES, 2 * SSD_GROUP_W), np.float32)
    for dr in range(2):
        for r in range(SSD_HPG):
            c0 = dr * SSD_GROUP_W + r * SSD_HEAD_DIM
            emat[SSD_HEADS * dr + r, c0:c0 + SSD_HEAD_DIM] = 1.0
    emat = jnp.asarray(emat, BF16)

    cos_t, sin_t = _rope_tables(seq)
    gmat_slab = jnp.asarray(np.kron(np.eye(MXU_DIM // HEAD_DIM), np.full((HEAD_DIM, HEAD_DIM), 1.0 / HEAD_DIM)),
                          BF16)
    gmat128 = gmat_slab[:LANES, :LANES]
    win_runs = _win_head_runs()
    wq = WIN_Q_HEADS * HEAD_DIM

    gains_mix = norm_mix_g.astype(F32)[:, None, :]
    gains_ff = norm_ff_g.astype(F32)[:, None, :]
    w_ff1_b, w_ff2_b = w_ff1.astype(BF16), w_ff2.astype(BF16)
    w_even_b = w_in_even.astype(BF16)
    w_dt_b = jnp.pad(w_even_b[:, :, s3:], ((0, 0), (0, 0), (0, LANES - 2 * SSD_HEADS)))
    w_out_even_b = w_out_even.astype(BF16)
    conv_w_p = _take(conv_w.astype(F32), xbc_runs, 2)
    conv_b_p = _take(conv_b.astype(F32)[:, None, :], xbc_runs, 2)
    dtb_rows, alog_rows = _head_rows(dt_bias), _head_rows(a_log)
    dsk_x = jnp.repeat(d_skip.astype(F32), SSD_HEAD_DIM, axis=1)[:, None, :]
    ssd_ng = ssd_norm_g.astype(F32)[:, None, :]
    w_odd_b = w_in_odd.astype(BF16)
    wq_perm = _take(w_odd_b[:, :, :wq], win_runs, 2)
    w_out_odd_b = w_out_odd.astype(BF16)
    wo_win_perm = _take(w_out_odd_b[:, :wq], win_runs, 1)
    qscale = HEAD_DIM ** -0.5 * LOG2E
    rep = lambda g, k: jnp.tile(g.astype(F32), (1, k))
    head_gains = jnp.stack([rep(q_norm_win, WIN_Q_HEADS) * qscale,
                            jnp.pad(rep(k_norm_win, WIN_KV_HEADS), ((0, 0), (0, wq - WIN_KV_HEADS * HEAD_DIM))),
                            rep(q_norm_na, NA_HEADS) * qscale, rep(k_norm_na, NA_HEADS)], axis=1)
    head_gains = jnp.pad(head_gains, ((0, 0), (0, SUBLANES - head_gains.shape[1]), (0, 0)))
    sink_cm = _take(sink_win.astype(F32) * LOG2E, [(r[0] // HEAD_DIM, r[1] // HEAD_DIM) for r in win_runs], 1)
    sink_x = jnp.broadcast_to(jnp.repeat(sink_cm, WIN_BLOCK, axis=1)[:, :, None],
                              (sink_win.shape[0], WIN_Q_HEADS * WIN_BLOCK, LANES))
    bias_tab = _na_bias_table(rpb_na)

    for i in range(DEPTH):
        need_ctx = i < DEPTH - 1
        j = i // 2
        if i % 2 == 0:
            src = (x2d, ctx2d) if h is None else (h, None)
            fr, fi, z, xbc, dtr = _inproj_even(*src, i, j, gains_mix, mod_all, w_four, w_even_b, w_dt_b,
                                               n_batch, tiles_per_batch)

            br, bi = _dft_stage_a(fr.reshape(-1, n2, FNET_W), fi.reshape(-1, n2, FNET_W),
                                  m_a, twc, tws, n_batch, n2)
            f_lat = _dft_stage_b(br.reshape(n_batch, seq, FNET_W), bi.reshape(n_batch, seq, FNET_W),
                                 m_b, n_batch, n2).reshape(n_lat, FNET_W)
            f_ctx = _ctx_dft(fr, fi, m_c, n_batch, n_ctx, n_lat // n_ctx)

            y_lat, y_ctx = _ssd3(xbc, dtr, z, j, conv_w_p, conv_b_p, dtb_rows, alog_rows, dsk_x, ssd_ng, tri, emat,
                                 n_batch, seq, n_ctx)

            first = h is None
            h = _outproj(h, i, mod_all, f_lat, y_lat, w_out_even_b, w_out_even_b, j, FNET_W,
                         n_batch, tiles_per_batch, 0, 0, 0, n_lat_tiles,
                         res=x2d if first else None, dst_rows=n_lat + n_batch * n_ctx)
            h = _outproj(h, i, mod_all, f_ctx, y_ctx, w_out_even_b, w_out_even_b, j, FNET_W,
                         n_batch, tiles_per_batch, n_lat_tiles, 0, 0, n_ctx_tiles,
                         res=ctx2d if first else None)
        else:
            qw, kw, vw, qn, kn, vn = _inproj_odd(h, i, j, gains_mix, mod_all, wq_perm, w_odd_b, cos_t, sin_t,
                                                 gmat_slab, gmat128, head_gains, n_batch, tiles_per_batch)
            o_win = _window_attention(qw, kw, vw, sink_x, j, n_batch, seq, n_ctx)
            o_na = _na_attention(qn, kn, vn, bias_tab, j, n_batch, seq, n_ctx)
            if need_ctx:
                oc_win, oc_na = _ctx_attention(qw, kw, vw, qn, kn, vn, sink_x, j, n_batch, seq, n_ctx)
            h = _outproj(h, i, mod_all, o_win, o_na, wo_win_perm, w_out_odd_b, j, wq,
                         n_batch, tiles_per_batch, 0, 0, 0, n_lat_tiles)
            if need_ctx:
                h = _outproj(h, i, mod_all, oc_win, oc_na, wo_win_perm, w_out_odd_b, j, wq,
                             n_batch, tiles_per_batch, n_lat_tiles, 0, 0, n_ctx_tiles)
        n_tiles = n_lat_tiles + n_ctx_tiles if need_ctx else n_lat_tiles
        h = _ffn(h, i, gains_ff, mod_all, w_ff1_b, w_ff2_b, n_batch, tiles_per_batch, n_tiles)
    return h.reshape(n_batch, seq, d)
```

```python
import functools
import math

import numpy as np
import jax
import jax.numpy as jnp
from jax import lax
from jax.experimental import pallas as pl
from jax.experimental.pallas import tpu as pltpu

F32 = jnp.float32
BF16 = jnp.bfloat16
HIGHEST = lax.Precision.HIGHEST

D_MODEL = 1024
DEPTH = 4
GRID_W = 64
D_FF = 4 * D_MODEL
NORM_EPS = 1e-6
NEG_INF = -1e30

FNET_GROUPS = 8
FNET_GROUP_W = 64
FNET_W = FNET_GROUPS * FNET_GROUP_W
SSD_HEAD_DIM = 64
SSD_HEADS = 24
SSD_GROUPS = 4
SSD_HPG = SSD_HEADS // SSD_GROUPS
SSD_STATE = 128
SSD_INNER = SSD_HEADS * SSD_HEAD_DIM
SSD_GROUP_W = SSD_HPG * SSD_HEAD_DIM
SSD_XBC_W = SSD_GROUP_W + 2 * SSD_STATE
SSD_CONV_DIM = SSD_INNER + 2 * SSD_GROUPS * SSD_STATE
SSD_CONV_W = 5
SSD_CHUNK = 128
HEAD_DIM = 64
WIN_Q_HEADS = 8
WIN_KV_HEADS = 2
WIN_GQA = WIN_Q_HEADS // WIN_KV_HEADS
WIN_BLOCK = 128
WIN_RADIUS = 128
NA_HEADS = 8
NA_ROWS = 8
NA_COLS = 16
ROPE_THETA = 10000.0
LOG2E = 1.4426950408889634
WIN_W = WIN_Q_HEADS * HEAD_DIM
WIN_KV_W = WIN_KV_HEADS * HEAD_DIM
NA_W = NA_HEADS * HEAD_DIM

LANES = 128
SUBLANES = 8
MXU_DIM = 256
HALO = SUBLANES
ROW_TILE = 512
FF_TILE = 4096
DFT_N1 = 128
DFT_KB = SUBLANES
VMEM_LIMIT = 56 * 1024 * 1024


def _cparams(*sem):
    return pltpu.CompilerParams(dimension_semantics=sem, vmem_limit_bytes=VMEM_LIMIT)


def _silu(x):
    return x / (1.0 + jnp.exp2(x * -LOG2E))


def _softplus(x):
    return jnp.maximum(x, 0.0) + jnp.log(1.0 + jnp.exp(-jnp.abs(x)))


def _norm_mod(h, g, shift, scale):
    ms = jnp.mean(h * h, axis=-1, keepdims=True)
    y = h * lax.rsqrt(ms + NORM_EPS) * g
    return y * (1.0 + scale) + shift


def _dot(a, b):
    return jnp.dot(a, b, preferred_element_type=F32)


def _dot_nt(a, b):
    return lax.dot_general(a, b, (((1,), (1,)), ((), ())), preferred_element_type=F32)


def _mod_body(c_ref, w_ref, b_ref, o_ref):
    s = _silu(c_ref[...])
    o_ref[...] = jnp.dot(s, w_ref[...], precision=HIGHEST, preferred_element_type=F32) + b_ref[...]


def _modulation(cvec, w_mod, b_mod):
    depth, d, n = w_mod.shape
    tn = 1536
    return pl.pallas_call(
        _mod_body,
        grid=(depth, n // tn),
        in_specs=[pl.BlockSpec((SUBLANES, d), lambda i, j: (0, 0)),
                  pl.BlockSpec((None, d, tn), lambda i, j: (i, 0, j)),
                  pl.BlockSpec((None, 1, tn), lambda i, j: (i, 0, j))],
        out_specs=pl.BlockSpec((None, SUBLANES, tn), lambda i, j: (i, 0, j)),
        out_shape=jax.ShapeDtypeStruct((depth, SUBLANES, n), F32),
        compiler_params=_cparams("parallel", "parallel"),
        name="modulation",
    )(cvec, w_mod, b_mod.reshape(depth, 1, n))


def _wprod_body(a_ref, b_ref, o_ref):
    o_ref[...] = jnp.dot(a_ref[...], b_ref[...], precision=HIGHEST,
                         preferred_element_type=F32).astype(o_ref.dtype)


def _weight_product(a, k, b):
    n, m, _ = a.shape
    p = b.shape[1]
    return pl.pallas_call(
        _wprod_body,
        grid=(n,),
        in_specs=[pl.BlockSpec((None, m, k), lambda i: (i, 0, 0)),
                  pl.BlockSpec((k, p), lambda i: (0, 0))],
        out_specs=pl.BlockSpec((None, m, p), lambda i: (i, 0, 0)),
        out_shape=jax.ShapeDtypeStruct((n, m, p), BF16),
        compiler_params=_cparams("parallel"),
        name="fold_channel_dft",
    )(a, b)


def _resident(shape):
    return pl.BlockSpec(shape, lambda *_: (0,) * len(shape), pipeline_mode=pl.Buffered(1))


def _layer(arr, j):
    nd = arr.ndim
    return pl.BlockSpec((None,) + arr.shape[1:], lambda *_: (j,) + (0,) * (nd - 1),
                        pipeline_mode=pl.Buffered(1))


def _mod_spec(layer, which, tiles_per_batch, n_batch, tile_off):
    def imap(m):
        return (layer, which, jnp.minimum((m + tile_off) // tiles_per_batch, n_batch), 0, 0)
    return pl.BlockSpec((None, None, None, 1, D_MODEL), imap)


def _inproj_even_body(n_first, two_inputs, *refs):
    if two_inputs:
        h = jnp.where(pl.program_id(0) < n_first, refs[0][...], refs[1][...])
        refs = refs[2:]
    else:
        h = refs[0][...]
        refs = refs[1:]
    g_ref, sh_ref, sc_ref, wf_ref, w_ref, wdt_ref, fr_ref, fi_ref, z_ref, xbc_ref, dt_ref = refs
    u = _norm_mod(h, g_ref[...], sh_ref[...], sc_ref[...]).astype(BF16)
    fr_ref[...] = _dot(u, wf_ref[:, :FNET_W])
    fi_ref[...] = _dot(u, wf_ref[:, FNET_W:])
    z_ref[...] = _dot(u, w_ref[:, FNET_W:FNET_W + SSD_INNER]).astype(z_ref.dtype)
    c0 = FNET_W + SSD_INNER
    xbc = _dot(u, w_ref[:, c0:c0 + SSD_CONV_DIM])
    gn = SSD_GROUPS * SSD_STATE
    for g in range(SSD_GROUPS):
        o = g * SSD_XBC_W
        b0 = SSD_INNER + g * SSD_STATE
        xbc_ref[:, o:o + SSD_GROUP_W] = xbc[:, g * SSD_GROUP_W:(g + 1) * SSD_GROUP_W]
        xbc_ref[:, o + SSD_GROUP_W:o + SSD_GROUP_W + SSD_STATE] = xbc[:, b0:b0 + SSD_STATE]
        xbc_ref[:, o + SSD_GROUP_W + SSD_STATE:o + SSD_XBC_W] = xbc[:, b0 + gn:b0 + gn + SSD_STATE]
    dt_ref[...] = _dot(u, wdt_ref[...])


def _inproj_even(h, h2, layer, j, gains, mod_all, w_four, w_main, w_dt, n_batch, tiles_per_batch):
    tm = ROW_TILE
    rows = h.shape[0] + (0 if h2 is None else h2.shape[0])
    n_first = h.shape[0] // tm
    widths = (FNET_W, FNET_W, SSD_INNER, SSD_CONV_DIM, LANES)
    dtypes = (F32, F32, BF16, F32, F32)
    if h2 is None:
        h_specs, h_args = [pl.BlockSpec((tm, D_MODEL), lambda m: (m, 0))], (h,)
    else:
        h_specs = [pl.BlockSpec((tm, D_MODEL), lambda m: (jnp.minimum(m, n_first - 1), 0)),
                   pl.BlockSpec((tm, D_MODEL), lambda m: (jnp.maximum(m - n_first, 0), 0))]
        h_args = (h, h2)
    return pl.pallas_call(
        functools.partial(_inproj_even_body, n_first, h2 is not None),
        grid=(rows // tm,),
        in_specs=h_specs + [
                  _layer(gains, layer),
                  _mod_spec(layer, 0, tiles_per_batch, n_batch, 0),
                  _mod_spec(layer, 1, tiles_per_batch, n_batch, 0),
                  _layer(w_four, j), _layer(w_main, j), _layer(w_dt, j)],
        out_specs=[pl.BlockSpec((tm, wd), lambda m: (m, 0)) for wd in widths],
        out_shape=[jax.ShapeDtypeStruct((rows, wd), dt) for wd, dt in zip(widths, dtypes)],
        compiler_params=_cparams("parallel"),
        name="inproj_even",
    )(*h_args, gains, mod_all, mod_all, w_four, w_main, w_dt)


def _group_rms(x, gmat_ref, gain):
    sq = x * x
    hi = sq.astype(BF16)
    lo = (sq - hi.astype(F32)).astype(BF16)
    gw = gmat_ref.shape[0]
    parts = []
    for c0 in range(0, x.shape[1], gw):
        parts.append(_dot(hi[:, c0:c0 + gw], gmat_ref[...]) + _dot(lo[:, c0:c0 + gw], gmat_ref[...]))
    ms = parts[0] if len(parts) == 1 else jnp.concatenate(parts, axis=1)
    return x * lax.rsqrt(ms + NORM_EPS) * gain


def _rope(y, cos, sin_signed):
    w = y.shape[-1]
    reps = w // LANES
    lane = lax.broadcasted_iota(jnp.int32, y.shape, 1)
    first = (lane % 32) < 16
    partner = jnp.where(first, pltpu.roll(y, w - 16, 1), pltpu.roll(y, 16, 1))
    if reps > 1:
        cos = jnp.concatenate([cos] * reps, axis=1)
        sin_signed = jnp.concatenate([sin_signed] * reps, axis=1)
    return y * cos + partner * sin_signed


def _inproj_odd_body(h_ref, g_ref, sh_ref, sc_ref, wq_ref, w_ref, cos_ref, sin_ref, gslab_ref, g128_ref,
                     hg_ref, qw_ref, kw_ref, vw_ref, qn_ref, kn_ref, vn_ref):
    u = _norm_mod(h_ref[...], g_ref[...], sh_ref[...], sc_ref[...]).astype(BF16)
    cos = cos_ref[...]
    sin = sin_ref[...]
    wq = WIN_Q_HEADS * HEAD_DIM
    wk = WIN_KV_HEADS * HEAD_DIM
    nh = NA_HEADS * HEAD_DIM
    c0 = wq + 2 * wk
    qw_raw = _dot(u, wq_ref[...])
    kw_raw = _dot(u, w_ref[:, wq:wq + wk])
    qn_raw = _dot(u, w_ref[:, c0:c0 + nh])
    kn_raw = _dot(u, w_ref[:, c0 + nh:c0 + 2 * nh])
    vw_ref[...] = _dot(u, w_ref[:, wq + wk:wq + 2 * wk]).astype(BF16)
    vn_ref[...] = _dot(u, w_ref[:, c0 + 2 * nh:c0 + 3 * nh]).astype(BF16)
    qw = _group_rms(qw_raw, gslab_ref, hg_ref[0:1, :])
    qw_ref[...] = _rope(qw, cos, sin).astype(BF16)
    kw = _group_rms(kw_raw, g128_ref, hg_ref[1:2, :wk])
    kw_ref[...] = _rope(kw, cos, sin).astype(BF16)
    qn_ref[...] = _group_rms(qn_raw, gslab_ref, hg_ref[2:3, :]).astype(BF16)
    kn_ref[...] = _group_rms(kn_raw, gslab_ref, hg_ref[3:4, :]).astype(BF16)


def _inproj_odd(h, layer, j, gains, mod_all, wq_perm, w_main, cos_t, sin_t, gmat_slab, gmat128, head_gains,
                n_batch, tiles_per_batch):
    rows = h.shape[0]
    tm = ROW_TILE
    widths = (WIN_W, WIN_KV_W, WIN_KV_W, NA_W, NA_W, NA_W)
    n_lat_tiles = n_batch * tiles_per_batch

    def rope_map(m):
        return (jnp.where(m < n_lat_tiles, m % tiles_per_batch, tiles_per_batch), 0)

    return pl.pallas_call(
        _inproj_odd_body,
        grid=(rows // tm,),
        in_specs=[pl.BlockSpec((tm, D_MODEL), lambda m: (m, 0)),
                  _layer(gains, layer),
                  _mod_spec(layer, 0, tiles_per_batch, n_batch, 0),
                  _mod_spec(layer, 1, tiles_per_batch, n_batch, 0),
                  _layer(wq_perm, j), _layer(w_main, j),
                  pl.BlockSpec((tm, LANES), rope_map),
                  pl.BlockSpec((tm, LANES), rope_map),
                  _resident(gmat_slab.shape), _resident(gmat128.shape), _layer(head_gains, j)],
        out_specs=[pl.BlockSpec((tm, wd), lambda m: (m, 0)) for wd in widths],
        out_shape=[jax.ShapeDtypeStruct((rows, wd), BF16) for wd in widths],
        compiler_params=_cparams("parallel"),
        name="inproj_odd",
    )(h, gains, mod_all, mod_all, wq_perm, w_main, cos_t, sin_t, gmat_slab, gmat128, head_gains)


def _outproj_body(k1, k2, b_row0, h_ref, gate_ref, a1_ref, a2_ref, wa_ref, wb_ref, *rest):
    o_ref = rest[-1]
    acc = _dot(a1_ref[...].astype(BF16), wa_ref[0:k1, :])
    acc += _dot(a2_ref[...].astype(BF16), wb_ref[b_row0:b_row0 + k2, :])
    o_ref[...] = h_ref[...] + gate_ref[...] * acc


def _outproj(h, layer, mod_all, a1, a2, wa, wb, j, b_row0, n_batch, tiles_per_batch, tile_off, a1_off,
             a2_off, n_tiles, res=None, res_off=0, dst_rows=None):
    tm = ROW_TILE
    k1, k2 = a1.shape[1], a2.shape[1]
    in_place = res is None
    src, src_off = (h, tile_off) if in_place else (res, res_off)
    in_specs = [pl.BlockSpec((tm, D_MODEL), lambda m: (m + src_off, 0)),
                _mod_spec(layer, 2, tiles_per_batch, n_batch, tile_off),
                pl.BlockSpec((tm, k1), lambda m: (m + a1_off, 0)),
                pl.BlockSpec((tm, k2), lambda m: (m + a2_off, 0)),
                _layer(wa, j), _layer(wb, j)]
    args = [src, mod_all, a1, a2, wa, wb]
    aliases = {0: 0} if in_place else {}
    if not in_place and h is not None:
        in_specs.append(pl.BlockSpec(memory_space=pl.ANY))
        args.append(h)
        aliases = {len(args) - 1: 0}
    out_rows = dst_rows if h is None else h.shape[0]
    return pl.pallas_call(
        functools.partial(_outproj_body, k1, k2, b_row0),
        grid=(n_tiles,),
        in_specs=in_specs,
        out_specs=pl.BlockSpec((tm, D_MODEL), lambda m: (m + tile_off, 0)),
        out_shape=jax.ShapeDtypeStruct((out_rows, D_MODEL), F32),
        input_output_aliases=aliases,
        compiler_params=_cparams("parallel"),
        name="outproj",
    )(*args)


def _ffn_body(h_ref, g_ref, sh_ref, sc_ref, gate_ref, w1_ref, w2_ref, o_ref):
    h = h_ref[...]
    u = _norm_mod(h, g_ref[...], sh_ref[...], sc_ref[...]).astype(BF16)
    acc = None
    for k in range(D_FF // FF_TILE):
        sl = slice(k * FF_TILE, (k + 1) * FF_TILE)
        a = jnp.maximum(_dot(u, w1_ref[:, sl]), 0.0)
        t = _dot((a * a).astype(BF16), w2_ref[sl, :])
        acc = t if acc is None else acc + t
    o_ref[...] = h + gate_ref[...] * acc


def _ffn(h, layer, gains, mod_all, w1, w2, n_batch, tiles_per_batch, n_tiles):
    tm = ROW_TILE
    return pl.pallas_call(
        _ffn_body,
        grid=(n_tiles,),
        in_specs=[pl.BlockSpec((tm, D_MODEL), lambda m: (m, 0)),
                  _layer(gains, layer),
                  _mod_spec(layer, 3, tiles_per_batch, n_batch, 0),
                  _mod_spec(layer, 4, tiles_per_batch, n_batch, 0),
                  _mod_spec(layer, 5, tiles_per_batch, n_batch, 0),
                  _layer(w1, layer), _layer(w2, layer)],
        out_specs=pl.BlockSpec((tm, D_MODEL), lambda m: (m, 0)),
        out_shape=jax.ShapeDtypeStruct((n_tiles * tm, D_MODEL), F32),
        compiler_params=_cparams("parallel"),
        name="ffn",
    )(h, gains, mod_all, mod_all, mod_all, w1, w2)


def _dft_a_body(xr_ref, xi_ref, m_ref, c_ref, s_ref, br_ref, bi_ref):
    n1 = DFT_N1
    reps = FNET_W // LANES
    for j in range(xr_ref.shape[1]):
        x = jnp.concatenate([xr_ref[:, j, :], xi_ref[:, j, :]], axis=0).astype(BF16)
        a = _dot(m_ref[...], x)
        ar, ai = a[:n1], a[n1:]
        c = jnp.concatenate([c_ref[:, LANES * j:LANES * (j + 1)]] * reps, axis=1)
        s = jnp.concatenate([s_ref[:, LANES * j:LANES * (j + 1)]] * reps, axis=1)
        br_ref[:, j, :] = ar * c + ai * s
        bi_ref[:, j, :] = ai * c - ar * s


def _dft_stage_a(xr, xi, mmat, twc, tws, n_batch, n2):
    n1 = DFT_N1
    lb = SUBLANES
    blk = pl.BlockSpec((n1, lb, FNET_W), lambda b, j: (b, j, 0))
    return pl.pallas_call(
        _dft_a_body,
        grid=(n_batch, n2 // lb),
        in_specs=[blk, blk,
                  pl.BlockSpec((2 * n1, 2 * n1), lambda b, j: (0, 0)),
                  pl.BlockSpec((n1, lb * LANES), lambda b, j: (0, j)),
                  pl.BlockSpec((n1, lb * LANES), lambda b, j: (0, j))],
        out_specs=[blk, blk],
        out_shape=[jax.ShapeDtypeStruct((n_batch * n1, n2, FNET_W), F32)] * 2,
        compiler_params=_cparams("parallel", "parallel"),
        name="seq_dft_stage_a",
    )(xr, xi, mmat, twc, tws)


def _dft_b_body(br_ref, bi_ref, m_ref, o_ref):
    x = jnp.concatenate([br_ref[...], bi_ref[...]], axis=0).astype(BF16)
    res = _dot(m_ref[...], x)
    o_ref[...] = res.reshape(o_ref.shape)


def _dft_stage_b(br, bi, mmat, n_batch, n2):
    n1 = DFT_N1
    kb = DFT_KB
    return pl.pallas_call(
        _dft_b_body,
        grid=(n_batch, n1 // kb),
        in_specs=[pl.BlockSpec((None, kb * n2, FNET_W), lambda b, j: (b, j, 0)),
                  pl.BlockSpec((None, kb * n2, FNET_W), lambda b, j: (b, j, 0)),
                  pl.BlockSpec(mmat.shape, lambda b, j: (0, 0))],
        out_specs=pl.BlockSpec((None, n2, kb, FNET_W), lambda b, j: (b, 0, j, 0)),
        out_shape=jax.ShapeDtypeStruct((n_batch, n2, n1, FNET_W), F32),
        compiler_params=_cparams("parallel", "parallel"),
        name="seq_dft_stage_b",
    )(br, bi, mmat)


def _ctx_dft_body(xr_ref, xi_ref, m_ref, o_ref):
    x = jnp.concatenate([xr_ref[...], xi_ref[...]], axis=0).astype(BF16)
    o_ref[...] = _dot(m_ref[...], x)


def _ctx_dft(fr, fi, mmat, n_batch, n_ctx, row_block_off):
    return pl.pallas_call(
        _ctx_dft_body,
        grid=(n_batch,),
        in_specs=[pl.BlockSpec((n_ctx, FNET_W), lambda b: (row_block_off + b, 0)),
                  pl.BlockSpec((n_ctx, FNET_W), lambda b: (row_block_off + b, 0)),
                  pl.BlockSpec(mmat.shape, lambda b: (0, 0))],
        out_specs=pl.BlockSpec((n_ctx, FNET_W), lambda b: (b, 0)),
        out_shape=jax.ShapeDtypeStruct((n_batch * n_ctx, FNET_W), F32),
        compiler_params=_cparams("parallel"),
        name="ctx_dft",
    )(fr, fi, mmat)


def _dft_tables(seq, n_ctx):
    n1 = DFT_N1
    n2 = seq // n1
    k1 = np.arange(n1)
    ang1 = 2.0 * np.pi * np.outer(k1, k1) / n1
    c1, s1 = np.cos(ang1), np.sin(ang1)
    m_a = np.block([[c1, s1], [-s1, c1]])
    ang_t = 2.0 * np.pi * np.outer(k1, np.arange(n2)) / seq
    twc = np.repeat(np.cos(ang_t), LANES, axis=1)
    tws = np.repeat(np.sin(ang_t), LANES, axis=1)
    k2 = np.arange(n2)
    ang2 = 2.0 * np.pi * np.outer(k2, k2) / n2
    scale = 1.0 / math.sqrt(seq)
    c2, s2 = np.cos(ang2) * scale, np.sin(ang2) * scale
    kb = DFT_KB
    eye = np.eye(kb)
    m_b = np.concatenate([np.einsum("kl,ab->kabl", c2, eye).reshape(n2 * kb, kb * n2),
                          np.einsum("kl,ab->kabl", s2, eye).reshape(n2 * kb, kb * n2)], axis=1)
    kc = np.arange(n_ctx)
    angc = 2.0 * np.pi * np.outer(kc, kc) / n_ctx
    m_c = np.concatenate([np.cos(angc), np.sin(angc)], axis=1) / math.sqrt(n_ctx)
    ch = np.arange(FNET_GROUP_W)
    angg = 2.0 * np.pi * np.outer(ch, ch) / FNET_GROUP_W
    eg = np.eye(FNET_GROUPS)
    chan = np.concatenate([np.kron(eg, np.cos(angg)), -np.kron(eg, np.sin(angg))], axis=1)
    chan = chan / math.sqrt(FNET_GROUP_W)
    return (jnp.asarray(m_a, BF16), jnp.asarray(twc, F32), jnp.asarray(tws, F32),
            jnp.asarray(m_b, BF16), jnp.asarray(m_c, BF16), jnp.asarray(chan, F32))


def _split_dot_rhs(w_bf16, x):
    hi = x.astype(BF16)
    r1 = x - hi.astype(F32)
    mid = r1.astype(BF16)
    lo = (r1 - mid.astype(F32)).astype(BF16)
    return _dot(w_bf16, hi) + _dot(w_bf16, mid) + _dot(w_bf16, lo)


def _split2_dot(x, w_bf16):
    hi = x.astype(BF16)
    lo = (x - hi.astype(F32)).astype(BF16)
    return _dot(hi, w_bf16) + _dot(lo, w_bf16)


SSD_LAT_CPS = 8


def _ssd_local(xfull, dt_raw, g, pos0, n_chunks, cw_ref, cb_ref, dtb_ref, alog_ref, dsk_ref, tri_ref, e_ref,
               yloc_ref, s_ref, c_ref, ea_ref, cd_ref):
    q = SSD_CHUNK
    gw = SSD_GROUP_W
    rows_blk = n_chunks * q
    conv = cb_ref[...]
    for k in range(SSD_CONV_W):
        shift = (SSD_CONV_W // 2 - k) % xfull.shape[0]
        xk = xfull if shift == 0 else pltpu.roll(xfull, shift, 0)
        conv = conv + xk[HALO:HALO + rows_blk, :] * cw_ref[k:k + 1, :]
    act = _silu(conv)
    xs = act[:, :gw]
    bm = act[:, gw:gw + SSD_STATE]
    cm = act[:, gw + SSD_STATE:]

    lane = lax.broadcasted_iota(jnp.int32, (1, LANES), 1)
    is_fwd = lane < SSD_HPG
    dsel = pltpu.roll(dt_raw, (LANES - SSD_HPG * g) % LANES, 1)
    dtv = _softplus(dsel + dtb_ref[pl.ds(g, 1), :])
    head_lane = is_fwd | ((lane >= SSD_HEADS) & (lane < SSD_HEADS + SSD_HPG))
    a_row = jnp.where(head_lane, -jnp.exp(alog_ref[pl.ds(g, 1), :]) * LOG2E, 0.0)
    adt = dtv * a_row
    tri = tri_ref[...]
    tri16 = tri.astype(BF16)
    allowed = (tri[:q] > 0.0, tri[q:] > 0.0)
    emat = e_ref[...]
    lane_q = lax.broadcasted_iota(jnp.int32, (q, LANES), 1)
    lo_half = lane_q < SSD_HEAD_DIM
    for c0 in range(0, n_chunks, 2):
        pair = range(c0, min(c0 + 2, n_chunks))
        prep = {}
        for cc in pair:
            rows = slice(cc * q, (cc + 1) * q)
            cums = _split_dot_rhs(tri16, adt[rows])
            acum = jnp.where(is_fwd, cums[:q], cums[q:])
            end = jnp.where(is_fwd, acum[q - 1:q, :], acum[0:1, :])
            dte = jnp.exp2(end - acum)
            ex = _dot(jnp.concatenate([dtv[rows], dte], axis=0).astype(BF16), emat)
            cdx = _split2_dot(jnp.broadcast_to(jnp.exp2(end), (16, LANES)), emat)
            c16 = cm[rows].astype(BF16)
            b_c = bm[rows]
            cbm = _dot_nt(c16, b_c.astype(BF16))
            prep[cc] = (acum, acum.T, ex, cdx, c16, b_c, cbm)
        for cc in pair:
            rows = slice(cc * q, (cc + 1) * q)
            acum, acum_t, ex, cdx, c16, b_c, cbm = prep[cc]
            xs_c = xs[rows]
            xdt2 = jnp.concatenate([xs_c, xs_c], axis=1) * ex[:q]
            ys = []
            for m in range(SSD_HPG // 2):
                mms, xhs = [], []
                for d in range(2):
                    xpair = xdt2[:, gw * d + LANES * m:gw * d + LANES * (m + 1)]
                    for half in range(2):
                        ln = SSD_HEADS * d + 2 * m + half
                        seg = acum[:, ln:ln + 1] - acum_t[ln:ln + 1, :]
                        dec = jnp.exp2(jnp.where(allowed[d], seg, NEG_INF))
                        mms.append((cbm * dec).astype(BF16))
                        xhs.append(jnp.where(lo_half if half == 0 else ~lo_half, xpair, 0.0).astype(BF16))
                ys.append(_dot(jnp.concatenate(mms, axis=1), jnp.concatenate(xhs, axis=0)))
            y_loc = jnp.concatenate(ys, axis=1) + dsk_ref[...] * xs_c
            sts = _dot(b_c.T.astype(BF16), (xdt2 * ex[q:2 * q]).astype(BF16))
            pos = pos0 + cc
            r0 = pl.multiple_of(pos * q, q)
            yloc_ref[pl.ds(r0, q), :] = y_loc
            s_ref[pl.ds(r0, q), :] = sts.astype(BF16)
            c_ref[pl.ds(r0, q), :] = c16
            ea_ref[pl.ds(r0, q), :] = jnp.exp2(acum).astype(BF16)
            cd_ref[pl.ds(pl.multiple_of(pos * SUBLANES, SUBLANES), SUBLANES), :] = cdx[0:SUBLANES]


def _ssd_emit(z_ref, o_ref, pos0, n_chunks, hf, ng_ref, e_ref, yloc_ref, s_ref, c_ref, ea_ref, cd_ref, hbe_ref):
    q = SSD_CHUNK
    gw = SSD_GROUP_W
    emat = e_ref[...]
    starts = [pl.multiple_of((pos0 + cc) * q, q) for cc in range(n_chunks)]
    ea_x = [_dot(ea_ref[pl.ds(r0, q), :], emat) for r0 in starts]
    gate = [_silu(z_ref[cc * q:(cc + 1) * q, :].astype(F32)) for cc in range(n_chunks)]
    for cc in range(n_chunks):
        r0 = starts[cc]
        hcat = jnp.concatenate([hf.astype(BF16), hbe_ref[pl.ds(r0, q), :]], axis=1)
        yo = _dot(c_ref[pl.ds(r0, q), :], hcat) * ea_x[cc]
        y = yloc_ref[pl.ds(r0, q), :] + yo[:, :gw] + yo[:, gw:]
        cd = cd_ref[pl.ds(pl.multiple_of((pos0 + cc) * SUBLANES, SUBLANES), SUBLANES), :][0:1, :gw]
        hf = hf * cd + s_ref[pl.ds(r0, q), :gw].astype(F32)
        y = y * gate[cc]
        y = y * lax.rsqrt(jnp.mean(y * y, axis=-1, keepdims=True) + NORM_EPS) * ng_ref[...]
        o_ref[cc * q:(cc + 1) * q, :] = y.astype(o_ref.dtype)
    return hf


def _ssd3_body(nb, ncc, xm_ref, xp_ref, xn_ref, dt_ref, z_ref, xc_ref, dtc_ref, zc_ref, cw_ref, cb_ref, dtb_ref,
               alog_ref, dsk_ref, ng_ref, tri_ref, e_ref, o_ref, oc_ref,
               yloc_ref, s_ref, c_ref, ea_ref, cd_ref, hbe_ref, hf_ref, hb_ref):
    g = pl.program_id(1)
    t = pl.program_id(2)
    q = SSD_CHUNK
    gw = SSD_GROUP_W
    ns = ncc + nb * SSD_LAT_CPS
    local_refs = (cw_ref, cb_ref, dtb_ref, alog_ref, dsk_ref, tri_ref, e_ref, yloc_ref, s_ref, c_ref, ea_ref,
                  cd_ref)
    emit_refs = (ng_ref, e_ref, yloc_ref, s_ref, c_ref, ea_ref, cd_ref, hbe_ref)

    @pl.when(t == 0)
    def _local_context():
        pad = jnp.zeros((HALO, SSD_XBC_W), F32)
        _ssd_local(jnp.concatenate([pad, xc_ref[...], pad], axis=0), dtc_ref[...], g, 0, ncc, *local_refs)

    @pl.when(t < nb)
    def _local_latent():
        prev = jnp.where(t == 0, 0.0, xp_ref[...])
        nxt = jnp.where(t == nb - 1, 0.0, xn_ref[...])
        xfull = jnp.concatenate([prev, xm_ref[...], nxt], axis=0)
        _ssd_local(xfull, dt_ref[...], g, ncc + SSD_LAT_CPS * t, SSD_LAT_CPS, *local_refs)

    @pl.when(t == nb)
    def _backward_states_and_context():
        hb_ref[...] = jnp.zeros_like(hb_ref)

        def body(i, carry):
            pos = jnp.where(i < ncc, ncc - 1 - i, ns - 1 + ncc - i)
            r0 = pl.multiple_of(pos * q, q)
            hb = hb_ref[...]
            hbe_ref[pl.ds(r0, q), :] = hb.astype(BF16)
            cd = cd_ref[pl.ds(pl.multiple_of(pos * SUBLANES, SUBLANES), SUBLANES), :][0:1, gw:]
            hb_ref[...] = hb * cd + s_ref[pl.ds(r0, q), gw:].astype(F32)
            return carry

        lax.fori_loop(0, ns, body, 0)
        hf_ref[...] = _ssd_emit(zc_ref, oc_ref, 0, ncc, jnp.zeros(hf_ref.shape, F32), *emit_refs)

    @pl.when(t >= nb)
    def _emit_latent():
        hf_ref[...] = _ssd_emit(z_ref, o_ref, ncc + SSD_LAT_CPS * (t - nb), SSD_LAT_CPS, hf_ref[...], *emit_refs)


def _ssd3(xbc, dt, z, j, conv_w, conv_b, dtb_tab, alog_tab, dsk_x, ng, tri, emat, n_batch, seq, n_ctx):
    rows = xbc.shape[0]
    q = SSD_CHUNK
    blk = SSD_LAT_CPS * q
    assert n_ctx % q == 0 and seq % blk == 0 and (n_batch * seq) % n_ctx == 0
    ncc = n_ctx // q
    nb = seq // blk
    ns = ncc + nb * SSD_LAT_CPS
    per = blk // HALO
    ctx_blk0 = n_batch * seq // n_ctx
    last_halo = rows // HALO - 1
    gw = SSD_GROUP_W

    def rb_in(b, t):
        return b * nb + jnp.minimum(t, nb - 1)

    def rb_out(b, t):
        return b * nb + jnp.maximum(t - nb, 0)

    return pl.pallas_call(
        functools.partial(_ssd3_body, nb, ncc),
        grid=(n_batch, SSD_GROUPS, 2 * nb),
        in_specs=[pl.BlockSpec((blk, SSD_XBC_W), lambda b, g, t: (rb_in(b, t), g)),
                  pl.BlockSpec((HALO, SSD_XBC_W),
                               lambda b, g, t: (jnp.maximum(rb_in(b, t) * per - 1, 0), g)),
                  pl.BlockSpec((HALO, SSD_XBC_W),
                               lambda b, g, t: (jnp.minimum(rb_in(b, t) * per + per, last_halo), g)),
                  pl.BlockSpec((blk, LANES), lambda b, g, t: (rb_in(b, t), 0)),
                  pl.BlockSpec((blk, gw), lambda b, g, t: (rb_out(b, t), g)),
                  pl.BlockSpec((n_ctx, SSD_XBC_W), lambda b, g, t: (ctx_blk0 + b, g)),
                  pl.BlockSpec((n_ctx, LANES), lambda b, g, t: (ctx_blk0 + b, 0)),
                  pl.BlockSpec((n_ctx, gw), lambda b, g, t: (ctx_blk0 + b, g)),
                  pl.BlockSpec((None, SSD_CONV_W, SSD_XBC_W), lambda b, g, t: (j, 0, g)),
                  pl.BlockSpec((None, 1, SSD_XBC_W), lambda b, g, t: (j, 0, g)),
                  pl.BlockSpec((None, SUBLANES, LANES), lambda b, g, t: (j, 0, 0)),
                  pl.BlockSpec((None, SUBLANES, LANES), lambda b, g, t: (j, 0, 0)),
                  pl.BlockSpec((None, 1, gw), lambda b, g, t: (j, 0, g)),
                  pl.BlockSpec((None, 1, gw), lambda b, g, t: (j, 0, g)),
                  pl.BlockSpec((2 * q, q), lambda b, g, t: (0, 0)),
                  pl.BlockSpec((LANES, 2 * gw), lambda b, g, t: (0, 0))],
        out_specs=[pl.BlockSpec((blk, gw), lambda b, g, t: (rb_out(b, t), g)),
                   pl.BlockSpec((n_ctx, gw), lambda b, g, t: (b, g))],
        out_shape=[jax.ShapeDtypeStruct((n_batch * seq, SSD_INNER), BF16),
                   jax.ShapeDtypeStruct((n_batch * n_ctx, SSD_INNER), BF16)],
        scratch_shapes=[pltpu.VMEM((ns * q, gw), F32),
                        pltpu.VMEM((ns * q, 2 * gw), BF16),
                        pltpu.VMEM((ns * q, SSD_STATE), BF16),
                        pltpu.VMEM((ns * q, LANES), BF16),
                        pltpu.VMEM((ns * SUBLANES, 2 * gw), F32),
                        pltpu.VMEM((ns * q, gw), BF16),
                        pltpu.VMEM((SSD_STATE, gw), F32),
                        pltpu.VMEM((SSD_STATE, gw), F32)],
        compiler_params=_cparams("arbitrary", "arbitrary", "arbitrary"),
        name="ssd_bidir",
    )(xbc, xbc, xbc, dt, z, xbc, dt, z, conv_w, conv_b, dtb_tab, alog_tab, dsk_x, ng, tri, emat)


def _stack_heads(qc):
    lane = lax.broadcasted_iota(jnp.int32, qc.shape, 1)
    lo = lane < HEAD_DIM
    zero = jnp.zeros_like(qc)
    return jnp.concatenate([jnp.where(lo, qc, zero), jnp.where(lo, zero, qc)], axis=0)


def _unstack_heads(o, n):
    lane = lax.broadcasted_iota(jnp.int32, (n, LANES), 1)
    return jnp.where(lane < HEAD_DIM, o[:n], o[n:])


def _softmax_pv(scores, values, sink=None):
    m = None
    for sc in scores:
        mx = jnp.max(sc, axis=-1, keepdims=True)
        m = mx if m is None else jnp.maximum(m, mx)
    den = None
    if sink is not None:
        m = jnp.maximum(m, jnp.max(sink, axis=-1, keepdims=True))
        den = jnp.sum(jnp.exp2(sink - m), axis=-1, keepdims=True) * (1.0 / sink.shape[-1])
    acc = None
    for sc, v in zip(scores, values):
        pr = jnp.exp2(sc - m)
        sm = jnp.sum(pr, axis=-1, keepdims=True)
        den = sm if den is None else den + sm
        t = _dot(pr.astype(BF16), v)
        acc = t if acc is None else acc + t
    return acc * (1.0 / den)


WIN_BLOCKS_PER_STEP = 2


def _win_body(seq, q_ref, k_ref, v_ref, kc_ref, vc_ref, sink_ref, o_ref):
    wb = WIN_BLOCK
    r = lax.broadcasted_iota(jnp.int32, (wb, 3 * wb), 0)
    c = lax.broadcasted_iota(jnp.int32, (wb, 3 * wb), 1)
    for i in range(WIN_BLOCKS_PER_STEP):
        n = pl.program_id(1) * WIN_BLOCKS_PER_STEP + i
        qrows = slice(i * wb, (i + 1) * wb)
        start = pl.multiple_of(jnp.clip((n - 1) * wb, 0, seq - 3 * wb), wb)
        kb = k_ref[pl.ds(start, 3 * wb), :]
        vb = v_ref[pl.ds(start, 3 * wb), :]
        dist = (c + start) - (r + n * wb)
        pen = jnp.where(jnp.abs(dist) <= WIN_RADIUS, 0.0, NEG_INF)
        pen = jnp.concatenate([pen, pen], axis=0)
        scores = []
        for m in range(WIN_GQA):
            qs = _stack_heads(q_ref[qrows, LANES * m:LANES * (m + 1)])
            scores.append([_dot_nt(qs, kb) + pen, _dot_nt(qs, kc_ref[...])])
        for m in range(WIN_GQA):
            o = _softmax_pv(scores[m], [vb, vc_ref[...]], sink_ref[2 * m * wb:2 * (m + 1) * wb, :])
            o_ref[qrows, LANES * m:LANES * (m + 1)] = _unstack_heads(o, wb).astype(o_ref.dtype)


def _window_attention(qw, kw, vw, sink_x, j, n_batch, seq, n_ctx):
    wb = WIN_BLOCK
    assert seq // wb >= 3 and seq % (wb * WIN_BLOCKS_PER_STEP) == 0
    qrows = wb * WIN_BLOCKS_PER_STEP
    nb = seq // qrows
    ctx_blk0 = n_batch * seq // n_ctx
    q_spec = pl.BlockSpec((qrows, WIN_W), lambda b, n: (b * nb + n, 0))
    seq_spec = pl.BlockSpec((seq, LANES), lambda b, n: (b, 0))
    cspec = pl.BlockSpec((n_ctx, LANES), lambda b, n: (ctx_blk0 + b, 0))
    return pl.pallas_call(
        functools.partial(_win_body, seq),
        grid=(n_batch, nb),
        in_specs=[q_spec, seq_spec, seq_spec, cspec, cspec, _layer(sink_x, j)],
        out_specs=pl.BlockSpec((qrows, WIN_W), lambda b, n: (b * nb + n, 0)),
        out_shape=jax.ShapeDtypeStruct((n_batch * seq, WIN_W), BF16),
        compiler_params=_cparams("parallel", "arbitrary"),
        name="window_attention",
    )(qw, kw, vw, kw, vw, sink_x)


NA_ROWS_PER_STEP = 8


def _na_body(n_rows, q_ref, k_ref, v_ref, kc_ref, vc_ref, bias_ref, o_ref):
    kr = NA_ROWS
    for j in range(NA_ROWS_PER_STEP):
        r = pl.program_id(1) * NA_ROWS_PER_STEP + j
        first_row = jnp.clip(r - kr // 2, 0, n_rows - kr)
        start = pl.multiple_of(first_row * GRID_W, GRID_W)
        ro0 = first_row - r + NA_ROWS - 1
        rows = slice(j * GRID_W, (j + 1) * GRID_W)
        scores = []
        for m in range(NA_HEADS // 2):
            sl = slice(LANES * m, LANES * (m + 1))
            qs = _stack_heads(q_ref[rows, sl])
            kw = k_ref[pl.ds(start, kr * GRID_W), sl]
            s_loc = _dot_nt(qs, kw) + bias_ref[ro0, 2 * GRID_W * m:2 * GRID_W * (m + 1), :]
            scores.append((s_loc, _dot_nt(qs, kc_ref[:, sl])))
        for m in range(NA_HEADS // 2):
            sl = slice(LANES * m, LANES * (m + 1))
            vw = v_ref[pl.ds(start, kr * GRID_W), sl]
            o = _softmax_pv(list(scores[m]), [vw, vc_ref[:, sl]])
            o_ref[rows, sl] = _unstack_heads(o, GRID_W).astype(o_ref.dtype)


def _na_attention(qn, kn, vn, bias_tab, j, n_batch, seq, n_ctx):
    n_rows = seq // GRID_W
    assert n_rows >= NA_ROWS and n_rows % NA_ROWS_PER_STEP == 0
    n_steps = n_rows // NA_ROWS_PER_STEP
    qrows = NA_ROWS_PER_STEP * GRID_W
    ctx_blk0 = n_batch * seq // n_ctx
    return pl.pallas_call(
        functools.partial(_na_body, n_rows),
        grid=(n_batch, n_steps),
        in_specs=[pl.BlockSpec((qrows, NA_W), lambda b, r: (b * n_steps + r, 0)),
                  pl.BlockSpec((seq, NA_W), lambda b, r: (b, 0)),
                  pl.BlockSpec((seq, NA_W), lambda b, r: (b, 0)),
                  pl.BlockSpec((n_ctx, NA_W), lambda b, r: (ctx_blk0 + b, 0)),
                  pl.BlockSpec((n_ctx, NA_W), lambda b, r: (ctx_blk0 + b, 0)),
                  _layer(bias_tab, j)],
        out_specs=pl.BlockSpec((qrows, NA_W), lambda b, r: (b * n_steps + r, 0)),
        out_shape=jax.ShapeDtypeStruct((n_batch * seq, NA_W), BF16),
        compiler_params=_cparams("parallel", "arbitrary"),
        name="neighbourhood_attention",
    )(qn, kn, vn, kn, vn, bias_tab)


def _ctx_attn_body(qw_ref, kw_ref, vw_ref, qn_ref, kn_ref, vn_ref, sink_ref, ow_ref, on_ref):
    n = qw_ref.shape[0]
    qw = qw_ref[...]
    lane = lax.broadcasted_iota(jnp.int32, (n, LANES), 1)
    lo = lane < HEAD_DIM
    for m in range(WIN_GQA):
        qs = _stack_heads(qw[:, LANES * m:LANES * (m + 1)])
        r0, r1 = 2 * m * WIN_BLOCK, (2 * m + 1) * WIN_BLOCK
        sink = jnp.concatenate([jnp.broadcast_to(sink_ref[r0:r0 + 1, :], (n, LANES)),
                                jnp.broadcast_to(sink_ref[r1:r1 + 1, :], (n, LANES))], axis=0)
        o = _softmax_pv([_dot_nt(qs, kw_ref[...])], [vw_ref[...]], sink)
        ow_ref[:, LANES * m:LANES * (m + 1)] = jnp.where(lo, o[:n], o[n:]).astype(ow_ref.dtype)
    qn = qn_ref[...]
    for m in range(NA_HEADS // 2):
        sl = slice(LANES * m, LANES * (m + 1))
        qs = _stack_heads(qn[:, sl])
        o = _softmax_pv([_dot_nt(qs, kn_ref[:, sl])], [vn_ref[:, sl]])
        on_ref[:, sl] = _unstack_heads(o, n).astype(on_ref.dtype)


def _ctx_attention(qw, kw, vw, qn, kn, vn, sink_x, j, n_batch, seq, n_ctx):
    blk0 = n_batch * seq // n_ctx
    assert WIN_W == NA_W
    wide = pl.BlockSpec((n_ctx, NA_W), lambda b: (blk0 + b, 0))
    narrow = pl.BlockSpec((n_ctx, WIN_KV_W), lambda b: (blk0 + b, 0))
    out = pl.BlockSpec((n_ctx, NA_W), lambda b: (b, 0))
    return pl.pallas_call(
        _ctx_attn_body,
        grid=(n_batch,),
        in_specs=[wide, narrow, narrow, wide, wide, wide,
                  _layer(sink_x, j)],
        out_specs=[out, out],
        out_shape=[jax.ShapeDtypeStruct((n_batch * n_ctx, NA_W), BF16)] * 2,
        compiler_params=_cparams("parallel"),
        name="context_attention",
    )(qw, kw, vw, qn, kn, vn, sink_x)


def _take(w, runs, axis):
    return jnp.concatenate([lax.slice_in_dim(w, a, b, axis=axis) for a, b in runs], axis=axis)


def _xbc_runs():
    gn = SSD_GROUPS * SSD_STATE
    runs = []
    for g in range(SSD_GROUPS):
        runs.append((g * SSD_GROUP_W, (g + 1) * SSD_GROUP_W))
        runs.append((SSD_INNER + g * SSD_STATE, SSD_INNER + (g + 1) * SSD_STATE))
        runs.append((SSD_INNER + gn + g * SSD_STATE, SSD_INNER + gn + (g + 1) * SSD_STATE))
    return runs


def _head_rows(t):
    n = t.shape[0]
    flat = jnp.pad(t.astype(F32).reshape(n, 2 * SSD_HEADS), ((0, 0), (0, LANES - 2 * SSD_HEADS)))
    rows = [jnp.roll(flat, -SSD_HPG * g, axis=1) for g in range(SSD_GROUPS)]
    rows += [jnp.zeros_like(flat)] * (SUBLANES - SSD_GROUPS)
    return jnp.stack(rows, axis=1)


def _win_head_runs():
    runs = []
    for m in range(WIN_GQA):
        for kvh in range(WIN_KV_HEADS):
            h = kvh * WIN_GQA + m
            runs.append((h * HEAD_DIM, (h + 1) * HEAD_DIM))
    return runs


def _rope_tables(seq):
    pos = np.arange(seq)
    row = (pos // GRID_W).astype(np.float32)
    col = (pos % GRID_W).astype(np.float32)
    n_freq = HEAD_DIM // 4
    inv = (np.float32(ROPE_THETA) ** (-np.arange(n_freq, dtype=np.float32) / n_freq)).astype(np.float32)
    ar = (row[:, None] * inv).astype(np.float32)
    ac = (col[:, None] * inv).astype(np.float32)
    cos_h = np.concatenate([np.cos(ar), np.cos(ar), np.cos(ac), np.cos(ac)], axis=1)
    sin_h = np.concatenate([-np.sin(ar), np.sin(ar), -np.sin(ac), np.sin(ac)], axis=1)
    ident_c = np.ones((ROW_TILE, HEAD_DIM), np.float32)
    ident_s = np.zeros((ROW_TILE, HEAD_DIM), np.float32)
    cos_t = np.concatenate([cos_h, ident_c], axis=0)
    sin_t = np.concatenate([sin_h, ident_s], axis=0)
    return (jnp.asarray(np.tile(cos_t, (1, 2)), F32), jnp.asarray(np.tile(sin_t, (1, 2)), F32))


def _na_bias_table(rpb):
    n, h, nr, _ = rpb.shape
    kr = NA_ROWS
    cols = np.arange(GRID_W)
    col_start = np.clip(cols - NA_COLS // 2, 0, GRID_W - NA_COLS)
    col_ok = (cols[None] >= col_start[:, None]) & (cols[None] < col_start[:, None] + NA_COLS)
    r32 = rpb.astype(F32) * LOG2E
    ext = jnp.pad(r32, ((0, 0), (0, 0), (0, 0), (GRID_W - NA_COLS, GRID_W - NA_COLS)))
    toep = jnp.stack([ext[..., GRID_W - 1 - w:2 * GRID_W - 1 - w] for w in range(GRID_W)], axis=3)
    toep = jnp.where(jnp.asarray(col_ok), toep, NEG_INF)
    tabs = [jnp.concatenate([toep[:, :, ro0 + k] for k in range(kr)], axis=-1) for ro0 in range(NA_ROWS)]
    return jnp.stack(tabs, axis=1).reshape(n, NA_ROWS, h * GRID_W, kr * GRID_W)


def kernel(x, c, ctx, c_ctx, w_mod, b_mod, norm_mix_g, norm_ff_g, w_ff1, w_ff2, w_in_even, conv_w, conv_b,
           dt_bias, a_log, d_skip, ssd_norm_g, w_out_even, w_in_odd, q_norm_win, k_norm_win, sink_win,
           q_norm_na, k_norm_na, rpb_na, w_out_odd):
    n_batch, seq, d = x.shape
    n_ctx = ctx.shape[1]
    n_lat = n_batch * seq
    tm = ROW_TILE
    tiles_per_batch = seq // tm
    n_lat_tiles = n_lat // tm
    n_ctx_tiles = (n_batch * n_ctx) // tm
    n2 = seq // DFT_N1

    x2d, ctx2d = x.reshape(n_lat, d), ctx.reshape(n_batch * n_ctx, d)
    h = None

    cvec = jnp.concatenate([c, c_ctx[None], jnp.zeros((SUBLANES - n_batch - 1, d), F32)], axis=0)
    mod_all = _modulation(cvec, w_mod, b_mod)
    mod_all = mod_all.reshape(-1, SUBLANES, 6, 1, d)[:, :n_batch + 1].transpose(0, 2, 1, 3, 4)

    m_a, twc, tws, m_b, m_c, chan = _dft_tables(seq, n_ctx)
    xbc_runs = _xbc_runs()
    s3 = FNET_W + SSD_INNER + SSD_CONV_DIM
    w_four = _weight_product(w_in_even[:, :, :FNET_W], FNET_W, chan)
    tri = jnp.asarray(np.concatenate([np.tril(np.ones((SSD_CHUNK, SSD_CHUNK))),
                                      np.triu(np.ones((SSD_CHUNK, SSD_CHUNK)))], axis=0), F32)
    emat = np.zeros((LANES, 2 * SSD_GROUP_W), np.float32)
    for dr in range(2):
        for r in range(SSD_HPG):
            c0 = dr * SSD_GROUP_W + r * SSD_HEAD_DIM
            emat[SSD_HEADS * dr + r, c0:c0 + SSD_HEAD_DIM] = 1.0
    emat = jnp.asarray(emat, BF16)

    cos_t, sin_t = _rope_tables(seq)
    gmat_slab = jnp.asarray(np.kron(np.eye(MXU_DIM // HEAD_DIM), np.full((HEAD_DIM, HEAD_DIM), 1.0 / HEAD_DIM)),
                          BF16)
    gmat128 = gmat_slab[:LANES, :LANES]
    win_runs = _win_head_runs()
    wq = WIN_Q_HEADS * HEAD_DIM

    gains_mix = norm_mix_g.astype(F32)[:, None, :]
    gains_ff = norm_ff_g.astype(F32)[:, None, :]
    w_ff1_b, w_ff2_b = w_ff1.astype(BF16), w_ff2.astype(BF16)
    w_even_b = w_in_even.astype(BF16)
    w_dt_b = jnp.pad(w_even_b[:, :, s3:], ((0, 0), (0, 0), (0, LANES - 2 * SSD_HEADS)))
    w_out_even_b = w_out_even.astype(BF16)
    conv_w_p = _take(conv_w.astype(F32), xbc_runs, 2)
    conv_b_p = _take(conv_b.astype(F32)[:, None, :], xbc_runs, 2)
    dtb_rows, alog_rows = _head_rows(dt_bias), _head_rows(a_log)
    dsk_x = jnp.repeat(d_skip.astype(F32), SSD_HEAD_DIM, axis=1)[:, None, :]
    ssd_ng = ssd_norm_g.astype(F32)[:, None, :]
    w_odd_b = w_in_odd.astype(BF16)
    wq_perm = _take(w_odd_b[:, :, :wq], win_runs, 2)
    w_out_odd_b = w_out_odd.astype(BF16)
    wo_win_perm = _take(w_out_odd_b[:, :wq], win_runs, 1)
    qscale = HEAD_DIM ** -0.5 * LOG2E
    rep = lambda g, k: jnp.tile(g.astype(F32), (1, k))
    head_gains = jnp.stack([rep(q_norm_win, WIN_Q_HEADS) * qscale,
                            jnp.pad(rep(k_norm_win, WIN_KV_HEADS), ((0, 0), (0, wq - WIN_KV_HEADS * HEAD_DIM))),
                            rep(q_norm_na, NA_HEADS) * qscale, rep(k_norm_na, NA_HEADS)], axis=1)
    head_gains = jnp.pad(head_gains, ((0, 0), (0, SUBLANES - head_gains.shape[1]), (0, 0)))
    sink_cm = _take(sink_win.astype(F32) * LOG2E, [(r[0] // HEAD_DIM, r[1] // HEAD_DIM) for r in win_runs], 1)
    sink_x = jnp.broadcast_to(jnp.repeat(sink_cm, WIN_BLOCK, axis=1)[:, :, None],
                              (sink_win.shape[0], WIN_Q_HEADS * WIN_BLOCK, LANES))
    bias_tab = _na_bias_table(rpb_na)

    for i in range(DEPTH):
        need_ctx = i < DEPTH - 1
        j = i // 2
        if i % 2 == 0:
            src = (x2d, ctx2d) if h is None else (h, None)
            fr, fi, z, xbc, dtr = _inproj_even(*src, i, j, gains_mix, mod_all, w_four, w_even_b, w_dt_b,
                                               n_batch, tiles_per_batch)

            br, bi = _dft_stage_a(fr.reshape(-1, n2, FNET_W), fi.reshape(-1, n2, FNET_W),
                                  m_a, twc, tws, n_batch, n2)
            f_lat = _dft_stage_b(br.reshape(n_batch, seq, FNET_W), bi.reshape(n_batch, seq, FNET_W),
                                 m_b, n_batch, n2).reshape(n_lat, FNET_W)
            f_ctx = _ctx_dft(fr, fi, m_c, n_batch, n_ctx, n_lat // n_ctx)

            y_lat, y_ctx = _ssd3(xbc, dtr, z, j, conv_w_p, conv_b_p, dtb_rows, alog_rows, dsk_x, ssd_ng, tri, emat,
                                 n_batch, seq, n_ctx)

            first = h is None
            h = _outproj(h, i, mod_all, f_lat, y_lat, w_out_even_b, w_out_even_b, j, FNET_W,
                         n_batch, tiles_per_batch, 0, 0, 0, n_lat_tiles,
                         res=x2d if first else None, dst_rows=n_lat + n_batch * n_ctx)
            h = _outproj(h, i, mod_all, f_ctx, y_ctx, w_out_even_b, w_out_even_b, j, FNET_W,
                         n_batch, tiles_per_batch, n_lat_tiles, 0, 0, n_ctx_tiles,
                         res=ctx2d if first else None)
        else:
            qw, kw, vw, qn, kn, vn = _inproj_odd(h, i, j, gains_mix, mod_all, wq_perm, w_odd_b, cos_t, sin_t,
                                                 gmat_slab, gmat128, head_gains, n_batch, tiles_per_batch)
            o_win = _window_attention(qw, kw, vw, sink_x, j, n_batch, seq, n_ctx)
            o_na = _na_attention(qn, kn, vn, bias_tab, j, n_batch, seq, n_ctx)
            if need_ctx:
                oc_win, oc_na = _ctx_attention(qw, kw, vw, qn, kn, vn, sink_x, j, n_batch, seq, n_ctx)
            h = _outproj(h, i, mod_all, o_win, o_na, wo_win_perm, w_out_odd_b, j, wq,
                         n_batch, tiles_per_batch, 0, 0, 0, n_lat_tiles)
            if need_ctx:
                h = _outproj(h, i, mod_all, oc_win, oc_na, wo_win_perm, w_out_odd_b, j, wq,
                             n_batch, tiles_per_batch, n_lat_tiles, 0, 0, n_ctx_tiles)
        n_tiles = n_lat_tiles + n_ctx_tiles if need_ctx else n_lat_tiles
        h = _ffn(h, i, gains_ff, mod_all, w_ff1_b, w_ff2_b, n_batch, tiles_per_batch, n_tiles)
    return h.reshape(n_batch, seq, d)
```

```python
import functools
import math

import numpy as np
import jax
import jax.numpy as jnp
from jax import lax
from jax.experimental import pallas as pl
from jax.experimental.pallas import tpu as pltpu

F32 = jnp.float32
BF16 = jnp.bfloat16
HIGHEST = lax.Precision.HIGHEST

D_MODEL = 1024
DEPTH = 4
GRID_W = 64
D_FF = 4 * D_MODEL
NORM_EPS = 1e-6
NEG_INF = -1e30

FNET_GROUPS = 8
FNET_GROUP_W = 64
FNET_W = FNET_GROUPS * FNET_GROUP_W
SSD_HEAD_DIM = 64
SSD_HEADS = 24
SSD_GROUPS = 4
SSD_HPG = SSD_HEADS // SSD_GROUPS
SSD_STATE = 128
SSD_INNER = SSD_HEADS * SSD_HEAD_DIM
SSD_GROUP_W = SSD_HPG * SSD_HEAD_DIM
SSD_XBC_W = SSD_GROUP_W + 2 * SSD_STATE
SSD_CONV_DIM = SSD_INNER + 2 * SSD_GROUPS * SSD_STATE
SSD_CONV_W = 5
SSD_CHUNK = 128
HEAD_DIM = 64
WIN_Q_HEADS = 8
WIN_KV_HEADS = 2
WIN_GQA = WIN_Q_HEADS // WIN_KV_HEADS
WIN_BLOCK = 128
WIN_RADIUS = 128
NA_HEADS = 8
NA_ROWS = 8
NA_COLS = 16
ROPE_THETA = 10000.0
LOG2E = 1.4426950408889634
WIN_W = WIN_Q_HEADS * HEAD_DIM
WIN_KV_W = WIN_KV_HEADS * HEAD_DIM
NA_W = NA_HEADS * HEAD_DIM

LANES = 128
SUBLANES = 8
MXU_DIM = 256
HALO = SUBLANES
ROW_TILE = 512
FF_TILE = 4096
DFT_N1 = 128
DFT_KB = SUBLANES
VMEM_LIMIT = 56 * 1024 * 1024


def _cparams(*sem):
    return pltpu.CompilerParams(dimension_semantics=sem, vmem_limit_bytes=VMEM_LIMIT)


def _silu(x):
    return x / (1.0 + jnp.exp2(x * -LOG2E))


def _softplus(x):
    return jnp.maximum(x, 0.0) + jnp.log(1.0 + jnp.exp(-jnp.abs(x)))


def _norm_mod(h, g, shift, scale):
    ms = jnp.mean(h * h, axis=-1, keepdims=True)
    y = h * lax.rsqrt(ms + NORM_EPS) * g
    return y * (1.0 + scale) + shift


def _dot(a, b):
    return jnp.dot(a, b, preferred_element_type=F32)


def _dot_nt(a, b):
    return lax.dot_general(a, b, (((1,), (1,)), ((), ())), preferred_element_type=F32)


def _mod_body(c_ref, w_ref, b_ref, o_ref):
    s = _silu(c_ref[...])
    o_ref[...] = jnp.dot(s, w_ref[...], precision=HIGHEST, preferred_element_type=F32) + b_ref[...]


def _modulation(cvec, w_mod, b_mod):
    depth, d, n = w_mod.shape
    tn = 1536
    return pl.pallas_call(
        _mod_body,
        grid=(depth, n // tn),
        in_specs=[pl.BlockSpec((SUBLANES, d), lambda i, j: (0, 0)),
                  pl.BlockSpec((None, d, tn), lambda i, j: (i, 0, j)),
                  pl.BlockSpec((None, 1, tn), lambda i, j: (i, 0, j))],
        out_specs=pl.BlockSpec((None, SUBLANES, tn), lambda i, j: (i, 0, j)),
        out_shape=jax.ShapeDtypeStruct((depth, SUBLANES, n), F32),
        compiler_params=_cparams("parallel", "parallel"),
        name="modulation",
    )(cvec, w_mod, b_mod.reshape(depth, 1, n))


def _wprod_body(a_ref, b_ref, o_ref):
    o_ref[...] = jnp.dot(a_ref[...], b_ref[...], precision=HIGHEST,
                         preferred_element_type=F32).astype(o_ref.dtype)


def _weight_product(a, k, b):
    n, m, _ = a.shape
    p = b.shape[1]
    return pl.pallas_call(
        _wprod_body,
        grid=(n,),
        in_specs=[pl.BlockSpec((None, m, k), lambda i: (i, 0, 0)),
                  pl.BlockSpec((k, p), lambda i: (0, 0))],
        out_specs=pl.BlockSpec((None, m, p), lambda i: (i, 0, 0)),
        out_shape=jax.ShapeDtypeStruct((n, m, p), BF16),
        compiler_params=_cparams("parallel"),
        name="fold_channel_dft",
    )(a, b)


def _resident(shape):
    return pl.BlockSpec(shape, lambda *_: (0,) * len(shape), pipeline_mode=pl.Buffered(1))


def _layer(arr, j):
    nd = arr.ndim
    return pl.BlockSpec((None,) + arr.shape[1:], lambda *_: (j,) + (0,) * (nd - 1),
                        pipeline_mode=pl.Buffered(1))


def _mod_spec(layer, which, tiles_per_batch, n_batch, tile_off):
    def imap(m):
        return (layer, which, jnp.minimum((m + tile_off) // tiles_per_batch, n_batch), 0, 0)
    return pl.BlockSpec((None, None, None, 1, D_MODEL), imap)


def _inproj_even_body(n_first, two_inputs, *refs):
    if two_inputs:
        h = jnp.where(pl.program_id(0) < n_first, refs[0][...], refs[1][...])
        refs = refs[2:]
    else:
        h = refs[0][...]
        refs = refs[1:]
    g_ref, sh_ref, sc_ref, wf_ref, w_ref, wdt_ref, fr_ref, fi_ref, z_ref, xbc_ref, dt_ref = refs
    u = _norm_mod(h, g_ref[...], sh_ref[...], sc_ref[...]).astype(BF16)
    fr_ref[...] = _dot(u, wf_ref[:, :FNET_W])
    fi_ref[...] = _dot(u, wf_ref[:, FNET_W:])
    z_ref[...] = _dot(u, w_ref[:, FNET_W:FNET_W + SSD_INNER]).astype(z_ref.dtype)
    c0 = FNET_W + SSD_INNER
    xbc = _dot(u, w_ref[:, c0:c0 + SSD_CONV_DIM])
    gn = SSD_GROUPS * SSD_STATE
    for g in range(SSD_GROUPS):
        o = g * SSD_XBC_W
        b0 = SSD_INNER + g * SSD_STATE
        xbc_ref[:, o:o + SSD_GROUP_W] = xbc[:, g * SSD_GROUP_W:(g + 1) * SSD_GROUP_W]
        xbc_ref[:, o + SSD_GROUP_W:o + SSD_GROUP_W + SSD_STATE] = xbc[:, b0:b0 + SSD_STATE]
        xbc_ref[:, o + SSD_GROUP_W + SSD_STATE:o + SSD_XBC_W] = xbc[:, b0 + gn:b0 + gn + SSD_STATE]
    dt_ref[...] = _dot(u, wdt_ref[...])


def _inproj_even(h, h2, layer, j, gains, mod_all, w_four, w_main, w_dt, n_batch, tiles_per_batch):
    tm = ROW_TILE
    rows = h.shape[0] + (0 if h2 is None else h2.shape[0])
    n_first = h.shape[0] // tm
    widths = (FNET_W, FNET_W, SSD_INNER, SSD_CONV_DIM, LANES)
    dtypes = (F32, F32, BF16, F32, F32)
    if h2 is None:
        h_specs, h_args = [pl.BlockSpec((tm, D_MODEL), lambda m: (m, 0))], (h,)
    else:
        h_specs = [pl.BlockSpec((tm, D_MODEL), lambda m: (jnp.minimum(m, n_first - 1), 0)),
                   pl.BlockSpec((tm, D_MODEL), lambda m: (jnp.maximum(m - n_first, 0), 0))]
        h_args = (h, h2)
    return pl.pallas_call(
        functools.partial(_inproj_even_body, n_first, h2 is not None),
        grid=(rows // tm,),
        in_specs=h_specs + [
                  _layer(gains, layer),
                  _mod_spec(layer, 0, tiles_per_batch, n_batch, 0),
                  _mod_spec(layer, 1, tiles_per_batch, n_batch, 0),
                  _layer(w_four, j), _layer(w_main, j), _layer(w_dt, j)],
        out_specs=[pl.BlockSpec((tm, wd), lambda m: (m, 0)) for wd in widths],
        out_shape=[jax.ShapeDtypeStruct((rows, wd), dt) for wd, dt in zip(widths, dtypes)],
        compiler_params=_cparams("parallel"),
        name="inproj_even",
    )(*h_args, gains, mod_all, mod_all, w_four, w_main, w_dt)


def _group_rms(x, gmat_ref, gain):
    sq = x * x
    hi = sq.astype(BF16)
    lo = (sq - hi.astype(F32)).astype(BF16)
    gw = gmat_ref.shape[0]
    parts = []
    for c0 in range(0, x.shape[1], gw):
        parts.append(_dot(hi[:, c0:c0 + gw], gmat_ref[...]) + _dot(lo[:, c0:c0 + gw], gmat_ref[...]))
    ms = parts[0] if len(parts) == 1 else jnp.concatenate(parts, axis=1)
    return x * lax.rsqrt(ms + NORM_EPS) * gain


def _rope(y, cos, sin_signed):
    w = y.shape[-1]
    reps = w // LANES
    lane = lax.broadcasted_iota(jnp.int32, y.shape, 1)
    first = (lane % 32) < 16
    partner = jnp.where(first, pltpu.roll(y, w - 16, 1), pltpu.roll(y, 16, 1))
    if reps > 1:
        cos = jnp.concatenate([cos] * reps, axis=1)
        sin_signed = jnp.concatenate([sin_signed] * reps, axis=1)
    return y * cos + partner * sin_signed


def _inproj_odd_body(h_ref, g_ref, sh_ref, sc_ref, wq_ref, w_ref, cos_ref, sin_ref, gslab_ref, g128_ref,
                     hg_ref, qw_ref, kw_ref, vw_ref, qn_ref, kn_ref, vn_ref):
    u = _norm_mod(h_ref[...], g_ref[...], sh_ref[...], sc_ref[...]).astype(BF16)
    cos = cos_ref[...]
    sin = sin_ref[...]
    wq = WIN_Q_HEADS * HEAD_DIM
    wk = WIN_KV_HEADS * HEAD_DIM
    nh = NA_HEADS * HEAD_DIM
    c0 = wq + 2 * wk
    qw_raw = _dot(u, wq_ref[...])
    kw_raw = _dot(u, w_ref[:, wq:wq + wk])
    qn_raw = _dot(u, w_ref[:, c0:c0 + nh])
    kn_raw = _dot(u, w_ref[:, c0 + nh:c0 + 2 * nh])
    vw_ref[...] = _dot(u, w_ref[:, wq + wk:wq + 2 * wk]).astype(BF16)
    vn_ref[...] = _dot(u, w_ref[:, c0 + 2 * nh:c0 + 3 * nh]).astype(BF16)
    qw = _group_rms(qw_raw, gslab_ref, hg_ref[0:1, :])
    qw_ref[...] = _rope(qw, cos, sin).astype(BF16)
    kw = _group_rms(kw_raw, g128_ref, hg_ref[1:2, :wk])
    kw_ref[...] = _rope(kw, cos, sin).astype(BF16)
    qn_ref[...] = _group_rms(qn_raw, gslab_ref, hg_ref[2:3, :]).astype(BF16)
    kn_ref[...] = _group_rms(kn_raw, gslab_ref, hg_ref[3:4, :]).astype(BF16)


def _inproj_odd(h, layer, j, gains, mod_all, wq_perm, w_main, cos_t, sin_t, gmat_slab, gmat128, head_gains,
                n_batch, tiles_per_batch):
    rows = h.shape[0]
    tm = ROW_TILE
    widths = (WIN_W, WIN_KV_W, WIN_KV_W, NA_W, NA_W, NA_W)
    n_lat_tiles = n_batch * tiles_per_batch

    def rope_map(m):
        return (jnp.where(m < n_lat_tiles, m % tiles_per_batch, tiles_per_batch), 0)

    return pl.pallas_call(
        _inproj_odd_body,
        grid=(rows // tm,),
        in_specs=[pl.BlockSpec((tm, D_MODEL), lambda m: (m, 0)),
                  _layer(gains, layer),
                  _mod_spec(layer, 0, tiles_per_batch, n_batch, 0),
                  _mod_spec(layer, 1, tiles_per_batch, n_batch, 0),
                  _layer(wq_perm, j), _layer(w_main, j),
                  pl.BlockSpec((tm, LANES), rope_map),
                  pl.BlockSpec((tm, LANES), rope_map),
                  _resident(gmat_slab.shape), _resident(gmat128.shape), _layer(head_gains, j)],
        out_specs=[pl.BlockSpec((tm, wd), lambda m: (m, 0)) for wd in widths],
        out_shape=[jax.ShapeDtypeStruct((rows, wd), BF16) for wd in widths],
        compiler_params=_cparams("parallel"),
        name="inproj_odd",
    )(h, gains, mod_all, mod_all, wq_perm, w_main, cos_t, sin_t, gmat_slab, gmat128, head_gains)


def _outproj_body(k1, k2, b_row0, h_ref, gate_ref, a1_ref, a2_ref, wa_ref, wb_ref, *rest):
    o_ref = rest[-1]
    acc = _dot(a1_ref[...].astype(BF16), wa_ref[0:k1, :])
    acc += _dot(a2_ref[...].astype(BF16), wb_ref[b_row0:b_row0 + k2, :])
    o_ref[...] = h_ref[...] + gate_ref[...] * acc


def _outproj(h, layer, mod_all, a1, a2, wa, wb, j, b_row0, n_batch, tiles_per_batch, tile_off, a1_off,
             a2_off, n_tiles, res=None, res_off=0, dst_rows=None):
    tm = ROW_TILE
    k1, k2 = a1.shape[1], a2.shape[1]
    in_place = res is None
    src, src_off = (h, tile_off) if in_place else (res, res_off)
    in_specs = [pl.BlockSpec((tm, D_MODEL), lambda m: (m + src_off, 0)),
                _mod_spec(layer, 2, tiles_per_batch, n_batch, tile_off),
                pl.BlockSpec((tm, k1), lambda m: (m + a1_off, 0)),
                pl.BlockSpec((tm, k2), lambda m: (m + a2_off, 0)),
                _layer(wa, j), _layer(wb, j)]
    args = [src, mod_all, a1, a2, wa, wb]
    aliases = {0: 0} if in_place else {}
    if not in_place and h is not None:
        in_specs.append(pl.BlockSpec(memory_space=pl.ANY))
        args.append(h)
        aliases = {len(args) - 1: 0}
    out_rows = dst_rows if h is None else h.shape[0]
    return pl.pallas_call(
        functools.partial(_outproj_body, k1, k2, b_row0),
        grid=(n_tiles,),
        in_specs=in_specs,
        out_specs=pl.BlockSpec((tm, D_MODEL), lambda m: (m + tile_off, 0)),
        out_shape=jax.ShapeDtypeStruct((out_rows, D_MODEL), F32),
        input_output_aliases=aliases,
        compiler_params=_cparams("parallel"),
        name="outproj",
    )(*args)


def _ffn_body(h_ref, g_ref, sh_ref, sc_ref, gate_ref, w1_ref, w2_ref, o_ref):
    h = h_ref[...]
    u = _norm_mod(h, g_ref[...], sh_ref[...], sc_ref[...]).astype(BF16)
    acc = None
    for k in range(D_FF // FF_TILE):
        sl = slice(k * FF_TILE, (k + 1) * FF_TILE)
        a = jnp.maximum(_dot(u, w1_ref[:, sl]), 0.0)
        t = _dot((a * a).astype(BF16), w2_ref[sl, :])
        acc = t if acc is None else acc + t
    o_ref[...] = h + gate_ref[...] * acc


def _ffn(h, layer, gains, mod_all, w1, w2, n_batch, tiles_per_batch, n_tiles):
    tm = ROW_TILE
    return pl.pallas_call(
        _ffn_body,
        grid=(n_tiles,),
        in_specs=[pl.BlockSpec((tm, D_MODEL), lambda m: (m, 0)),
                  _layer(gains, layer),
                  _mod_spec(layer, 3, tiles_per_batch, n_batch, 0),
                  _mod_spec(layer, 4, tiles_per_batch, n_batch, 0),
                  _mod_spec(layer, 5, tiles_per_batch, n_batch, 0),
                  _layer(w1, layer), _layer(w2, layer)],
        out_specs=pl.BlockSpec((tm, D_MODEL), lambda m: (m, 0)),
        out_shape=jax.ShapeDtypeStruct((n_tiles * tm, D_MODEL), F32),
        compiler_params=_cparams("parallel"),
        name="ffn",
    )(h, gains, mod_all, mod_all, mod_all, w1, w2)


def _dft_a_body(xr_ref, xi_ref, m_ref, c_ref, s_ref, br_ref, bi_ref):
    n1 = DFT_N1
    reps = FNET_W // LANES
    for j in range(xr_ref.shape[1]):
        x = jnp.concatenate([xr_ref[:, j, :], xi_ref[:, j, :]], axis=0).astype(BF16)
        a = _dot(m_ref[...], x)
        ar, ai = a[:n1], a[n1:]
        c = jnp.concatenate([c_ref[:, LANES * j:LANES * (j + 1)]] * reps, axis=1)
        s = jnp.concatenate([s_ref[:, LANES * j:LANES * (j + 1)]] * reps, axis=1)
        br_ref[:, j, :] = ar * c + ai * s
        bi_ref[:, j, :] = ai * c - ar * s


def _dft_stage_a(xr, xi, mmat, twc, tws, n_batch, n2):
    n1 = DFT_N1
    lb = SUBLANES
    blk = pl.BlockSpec((n1, lb, FNET_W), lambda b, j: (b, j, 0))
    return pl.pallas_call(
        _dft_a_body,
        grid=(n_batch, n2 // lb),
        in_specs=[blk, blk,
                  pl.BlockSpec((2 * n1, 2 * n1), lambda b, j: (0, 0)),
                  pl.BlockSpec((n1, lb * LANES), lambda b, j: (0, j)),
                  pl.BlockSpec((n1, lb * LANES), lambda b, j: (0, j))],
        out_specs=[blk, blk],
        out_shape=[jax.ShapeDtypeStruct((n_batch * n1, n2, FNET_W), F32)] * 2,
        compiler_params=_cparams("parallel", "parallel"),
        name="seq_dft_stage_a",
    )(xr, xi, mmat, twc, tws)


def _dft_b_body(br_ref, bi_ref, m_ref, o_ref):
    x = jnp.concatenate([br_ref[...], bi_ref[...]], axis=0).astype(BF16)
    res = _dot(m_ref[...], x)
    o_ref[...] = res.reshape(o_ref.shape)


def _dft_stage_b(br, bi, mmat, n_batch, n2):
    n1 = DFT_N1
    kb = DFT_KB
    return pl.pallas_call(
        _dft_b_body,
        grid=(n_batch, n1 // kb),
        in_specs=[pl.BlockSpec((None, kb * n2, FNET_W), lambda b, j: (b, j, 0)),
                  pl.BlockSpec((None, kb * n2, FNET_W), lambda b, j: (b, j, 0)),
                  pl.BlockSpec(mmat.shape, lambda b, j: (0, 0))],
        out_specs=pl.BlockSpec((None, n2, kb, FNET_W), lambda b, j: (b, 0, j, 0)),
        out_shape=jax.ShapeDtypeStruct((n_batch, n2, n1, FNET_W), F32),
        compiler_params=_cparams("parallel", "parallel"),
        name="seq_dft_stage_b",
    )(br, bi, mmat)


def _ctx_dft_body(xr_ref, xi_ref, m_ref, o_ref):
    x = jnp.concatenate([xr_ref[...], xi_ref[...]], axis=0).astype(BF16)
    o_ref[...] = _dot(m_ref[...], x)


def _ctx_dft(fr, fi, mmat, n_batch, n_ctx, row_block_off):
    return pl.pallas_call(
        _ctx_dft_body,
        grid=(n_batch,),
        in_specs=[pl.BlockSpec((n_ctx, FNET_W), lambda b: (row_block_off + b, 0)),
                  pl.BlockSpec((n_ctx, FNET_W), lambda b: (row_block_off + b, 0)),
                  pl.BlockSpec(mmat.shape, lambda b: (0, 0))],
        out_specs=pl.BlockSpec((n_ctx, FNET_W), lambda b: (b, 0)),
        out_shape=jax.ShapeDtypeStruct((n_batch * n_ctx, FNET_W), F32),
        compiler_params=_cparams("parallel"),
        name="ctx_dft",
    )(fr, fi, mmat)


def _dft_tables(seq, n_ctx):
    n1 = DFT_N1
    n2 = seq // n1
    k1 = np.arange(n1)
    ang1 = 2.0 * np.pi * np.outer(k1, k1) / n1
    c1, s1 = np.cos(ang1), np.sin(ang1)
    m_a = np.block([[c1, s1], [-s1, c1]])
    ang_t = 2.0 * np.pi * np.outer(k1, np.arange(n2)) / seq
    twc = np.repeat(np.cos(ang_t), LANES, axis=1)
    tws = np.repeat(np.sin(ang_t), LANES, axis=1)
    k2 = np.arange(n2)
    ang2 = 2.0 * np.pi * np.outer(k2, k2) / n2
    scale = 1.0 / math.sqrt(seq)
    c2, s2 = np.cos(ang2) * scale, np.sin(ang2) * scale
    kb = DFT_KB
    eye = np.eye(kb)
    m_b = np.concatenate([np.einsum("kl,ab->kabl", c2, eye).reshape(n2 * kb, kb * n2),
                          np.einsum("kl,ab->kabl", s2, eye).reshape(n2 * kb, kb * n2)], axis=1)
    kc = np.arange(n_ctx)
    angc = 2.0 * np.pi * np.outer(kc, kc) / n_ctx
    m_c = np.concatenate([np.cos(angc), np.sin(angc)], axis=1) / math.sqrt(n_ctx)
    ch = np.arange(FNET_GROUP_W)
    angg = 2.0 * np.pi * np.outer(ch, ch) / FNET_GROUP_W
    eg = np.eye(FNET_GROUPS)
    chan = np.concatenate([np.kron(eg, np.cos(angg)), -np.kron(eg, np.sin(angg))], axis=1)
    chan = chan / math.sqrt(FNET_GROUP_W)
    return (jnp.asarray(m_a, BF16), jnp.asarray(twc, F32), jnp.asarray(tws, F32),
            jnp.asarray(m_b, BF16), jnp.asarray(m_c, BF16), jnp.asarray(chan, F32))


def _split_dot_rhs(w_bf16, x):
    hi = x.astype(BF16)
    r1 = x - hi.astype(F32)
    mid = r1.astype(BF16)
    lo = (r1 - mid.astype(F32)).astype(BF16)
    return _dot(w_bf16, hi) + _dot(w_bf16, mid) + _dot(w_bf16, lo)


def _split2_dot(x, w_bf16):
    hi = x.astype(BF16)
    lo = (x - hi.astype(F32)).astype(BF16)
    return _dot(hi, w_bf16) + _dot(lo, w_bf16)


SSD_LAT_CPS = 8


def _ssd_local(xfull, dt_raw, g, pos0, n_chunks, cw_ref, cb_ref, dtb_ref, alog_ref, dsk_ref, tri_ref, e_ref,
               yloc_ref, s_ref, c_ref, ea_ref, cd_ref):
    q = SSD_CHUNK
    gw = SSD_GROUP_W
    rows_blk = n_chunks * q
    conv = cb_ref[...]
    for k in range(SSD_CONV_W):
        shift = (SSD_CONV_W // 2 - k) % xfull.shape[0]
        xk = xfull if shift == 0 else pltpu.roll(xfull, shift, 0)
        conv = conv + xk[HALO:HALO + rows_blk, :] * cw_ref[k:k + 1, :]
    act = _silu(conv)
    xs = act[:, :gw]
    bm = act[:, gw:gw + SSD_STATE]
    cm = act[:, gw + SSD_STATE:]

    lane = lax.broadcasted_iota(jnp.int32, (1, LANES), 1)
    is_fwd = lane < SSD_HPG
    dsel = pltpu.roll(dt_raw, (LANES - SSD_HPG * g) % LANES, 1)
    dtv = _softplus(dsel + dtb_ref[pl.ds(g, 1), :])
    head_lane = is_fwd | ((lane >= SSD_HEADS) & (lane < SSD_HEADS + SSD_HPG))
    a_row = jnp.where(head_lane, -jnp.exp(alog_ref[pl.ds(g, 1), :]) * LOG2E, 0.0)
    adt = dtv * a_row
    tri = tri_ref[...]
    tri16 = tri.astype(BF16)
    allowed = (tri[:q] > 0.0, tri[q:] > 0.0)
    emat = e_ref[...]
    lane_q = lax.broadcasted_iota(jnp.int32, (q, LANES), 1)
    lo_half = lane_q < SSD_HEAD_DIM
    for c0 in range(0, n_chunks, 2):
        pair = range(c0, min(c0 + 2, n_chunks))
        prep = {}
        for cc in pair:
            rows = slice(cc * q, (cc + 1) * q)
            cums = _split_dot_rhs(tri16, adt[rows])
            acum = jnp.where(is_fwd, cums[:q], cums[q:])
            end = jnp.where(is_fwd, acum[q - 1:q, :], acum[0:1, :])
            dte = jnp.exp2(end - acum)
            ex = _dot(jnp.concatenate([dtv[rows], dte], axis=0).astype(BF16), emat)
            cdx = _split2_dot(jnp.broadcast_to(jnp.exp2(end), (16, LANES)), emat)
            c16 = cm[rows].astype(BF16)
            b_c = bm[rows]
            cbm = _dot_nt(c16, b_c.astype(BF16))
            prep[cc] = (acum, acum.T, ex, cdx, c16, b_c, cbm)
        for cc in pair:
            rows = slice(cc * q, (cc + 1) * q)
            acum, acum_t, ex, cdx, c16, b_c, cbm = prep[cc]
            xs_c = xs[rows]
            xdt2 = jnp.concatenate([xs_c, xs_c], axis=1) * ex[:q]
            ys = []
            for m in range(SSD_HPG // 2):
                mms, xhs = [], []
                for d in range(2):
                    xpair = xdt2[:, gw * d + LANES * m:gw * d + LANES * (m + 1)]
                    for half in range(2):
                        ln = SSD_HEADS * d + 2 * m + half
                        seg = acum[:, ln:ln + 1] - acum_t[ln:ln + 1, :]
                        dec = jnp.exp2(jnp.where(allowed[d], seg, NEG_INF))
                        mms.append((cbm * dec).astype(BF16))
                        xhs.append(jnp.where(lo_half if half == 0 else ~lo_half, xpair, 0.0).astype(BF16))
                ys.append(_dot(jnp.concatenate(mms, axis=1), jnp.concatenate(xhs, axis=0)))
            y_loc = jnp.concatenate(ys, axis=1) + dsk_ref[...] * xs_c
            sts = _dot(b_c.T.astype(BF16), (xdt2 * ex[q:2 * q]).astype(BF16))
            pos = pos0 + cc
            r0 = pl.multiple_of(pos * q, q)
            yloc_ref[pl.ds(r0, q), :] = y_loc
            s_ref[pl.ds(r0, q), :] = sts.astype(BF16)
            c_ref[pl.ds(r0, q), :] = c16
            ea_ref[pl.ds(r0, q), :] = jnp.exp2(acum).astype(BF16)
            cd_ref[pl.ds(pl.multiple_of(pos * SUBLANES, SUBLANES), SUBLANES), :] = cdx[0:SUBLANES]


def _ssd_emit(z_ref, o_ref, pos0, n_chunks, hf, ng_ref, e_ref, yloc_ref, s_ref, c_ref, ea_ref, cd_ref, hbe_ref):
    q = SSD_CHUNK
    gw = SSD_GROUP_W
    emat = e_ref[...]
    starts = [pl.multiple_of((pos0 + cc) * q, q) for cc in range(n_chunks)]
    ea_x = [_dot(ea_ref[pl.ds(r0, q), :], emat) for r0 in starts]
    gate = [_silu(z_ref[cc * q:(cc + 1) * q, :].astype(F32)) for cc in range(n_chunks)]
    for cc in range(n_chunks):
        r0 = starts[cc]
        hcat = jnp.concatenate([hf.astype(BF16), hbe_ref[pl.ds(r0, q), :]], axis=1)
        yo = _dot(c_ref[pl.ds(r0, q), :], hcat) * ea_x[cc]
        y = yloc_ref[pl.ds(r0, q), :] + yo[:, :gw] + yo[:, gw:]
        cd = cd_ref[pl.ds(pl.multiple_of((pos0 + cc) * SUBLANES, SUBLANES), SUBLANES), :][0:1, :gw]
        hf = hf * cd + s_ref[pl.ds(r0, q), :gw].astype(F32)
        y = y * gate[cc]
        y = y * lax.rsqrt(jnp.mean(y * y, axis=-1, keepdims=True) + NORM_EPS) * ng_ref[...]
        o_ref[cc * q:(cc + 1) * q, :] = y.astype(o_ref.dtype)
    return hf


def _ssd3_body(nb, ncc, xm_ref, xp_ref, xn_ref, dt_ref, z_ref, xc_ref, dtc_ref, zc_ref, cw_ref, cb_ref, dtb_ref,
               alog_ref, dsk_ref, ng_ref, tri_ref, e_ref, o_ref, oc_ref,
               yloc_ref, s_ref, c_ref, ea_ref, cd_ref, hbe_ref, hf_ref, hb_ref):
    g = pl.program_id(1)
    t = pl.program_id(2)
    q = SSD_CHUNK
    gw = SSD_GROUP_W
    ns = ncc + nb * SSD_LAT_CPS
    local_refs = (cw_ref, cb_ref, dtb_ref, alog_ref, dsk_ref, tri_ref, e_ref, yloc_ref, s_ref, c_ref, ea_ref,
                  cd_ref)
    emit_refs = (ng_ref, e_ref, yloc_ref, s_ref, c_ref, ea_ref, cd_ref, hbe_ref)

    @pl.when(t == 0)
    def _local_context():
        pad = jnp.zeros((HALO, SSD_XBC_W), F32)
        _ssd_local(jnp.concatenate([pad, xc_ref[...], pad], axis=0), dtc_ref[...], g, 0, ncc, *local_refs)

    @pl.when(t < nb)
    def _local_latent():
        prev = jnp.where(t == 0, 0.0, xp_ref[...])
        nxt = jnp.where(t == nb - 1, 0.0, xn_ref[...])
        xfull = jnp.concatenate([prev, xm_ref[...], nxt], axis=0)
        _ssd_local(xfull, dt_ref[...], g, ncc + SSD_LAT_CPS * t, SSD_LAT_CPS, *local_refs)

    @pl.when(t == nb)
    def _backward_states_and_context():
        hb_ref[...] = jnp.zeros_like(hb_ref)

        def body(i, carry):
            pos = jnp.where(i < ncc, ncc - 1 - i, ns - 1 + ncc - i)
            r0 = pl.multiple_of(pos * q, q)
            hb = hb_ref[...]
            hbe_ref[pl.ds(r0, q), :] = hb.astype(BF16)
            cd = cd_ref[pl.ds(pl.multiple_of(pos * SUBLANES, SUBLANES), SUBLANES), :][0:1, gw:]
            hb_ref[...] = hb * cd + s_ref[pl.ds(r0, q), gw:].astype(F32)
            return carry

        lax.fori_loop(0, ns, body, 0)
        hf_ref[...] = _ssd_emit(zc_ref, oc_ref, 0, ncc, jnp.zeros(hf_ref.shape, F32), *emit_refs)

    @pl.when(t >= nb)
    def _emit_latent():
        hf_ref[...] = _ssd_emit(z_ref, o_ref, ncc + SSD_LAT_CPS * (t - nb), SSD_LAT_CPS, hf_ref[...], *emit_refs)


def _ssd3(xbc, dt, z, j, conv_w, conv_b, dtb_tab, alog_tab, dsk_x, ng, tri, emat, n_batch, seq, n_ctx):
    rows = xbc.shape[0]
    q = SSD_CHUNK
    blk = SSD_LAT_CPS * q
    assert n_ctx % q == 0 and seq % blk == 0 and (n_batch * seq) % n_ctx == 0
    ncc = n_ctx // q
    nb = seq // blk
    ns = ncc + nb * SSD_LAT_CPS
    per = blk // HALO
    ctx_blk0 = n_batch * seq // n_ctx
    last_halo = rows // HALO - 1
    gw = SSD_GROUP_W

    def rb_in(b, t):
        return b * nb + jnp.minimum(t, nb - 1)

    def rb_out(b, t):
        return b * nb + jnp.maximum(t - nb, 0)

    return pl.pallas_call(
        functools.partial(_ssd3_body, nb, ncc),
        grid=(n_batch, SSD_GROUPS, 2 * nb),
        in_specs=[pl.BlockSpec((blk, SSD_XBC_W), lambda b, g, t: (rb_in(b, t), g)),
                  pl.BlockSpec((HALO, SSD_XBC_W),
                               lambda b, g, t: (jnp.maximum(rb_in(b, t) * per - 1, 0), g)),
                  pl.BlockSpec((HALO, SSD_XBC_W),
                               lambda b, g, t: (jnp.minimum(rb_in(b, t) * per + per, last_halo), g)),
                  pl.BlockSpec((blk, LANES), lambda b, g, t: (rb_in(b, t), 0)),
                  pl.BlockSpec((blk, gw), lambda b, g, t: (rb_out(b, t), g)),
                  pl.BlockSpec((n_ctx, SSD_XBC_W), lambda b, g, t: (ctx_blk0 + b, g)),
                  pl.BlockSpec((n_ctx, LANES), lambda b, g, t: (ctx_blk0 + b, 0)),
                  pl.BlockSpec((n_ctx, gw), lambda b, g, t: (ctx_blk0 + b, g)),
                  pl.BlockSpec((None, SSD_CONV_W, SSD_XBC_W), lambda b, g, t: (j, 0, g)),
                  pl.BlockSpec((None, 1, SSD_XBC_W), lambda b, g, t: (j, 0, g)),
                  pl.BlockSpec((None, SUBLANES, LANES), lambda b, g, t: (j, 0, 0)),
                  pl.BlockSpec((None, SUBLANES, LANES), lambda b, g, t: (j, 0, 0)),
                  pl.BlockSpec((None, 1, gw), lambda b, g, t: (j, 0, g)),
                  pl.BlockSpec((None, 1, gw), lambda b, g, t: (j, 0, g)),
                  pl.BlockSpec((2 * q, q), lambda b, g, t: (0, 0)),
                  pl.BlockSpec((LANES, 2 * gw), lambda b, g, t: (0, 0))],
        out_specs=[pl.BlockSpec((blk, gw), lambda b, g, t: (rb_out(b, t), g)),
                   pl.BlockSpec((n_ctx, gw), lambda b, g, t: (b, g))],
        out_shape=[jax.ShapeDtypeStruct((n_batch * seq, SSD_INNER), BF16),
                   jax.ShapeDtypeStruct((n_batch * n_ctx, SSD_INNER), BF16)],
        scratch_shapes=[pltpu.VMEM((ns * q, gw), F32),
                        pltpu.VMEM((ns * q, 2 * gw), BF16),
                        pltpu.VMEM((ns * q, SSD_STATE), BF16),
                        pltpu.VMEM((ns * q, LANES), BF16),
                        pltpu.VMEM((ns * SUBLANES, 2 * gw), F32),
                        pltpu.VMEM((ns * q, gw), BF16),
                        pltpu.VMEM((SSD_STATE, gw), F32),
                        pltpu.VMEM((SSD_STATE, gw), F32)],
        compiler_params=_cparams("arbitrary", "arbitrary", "arbitrary"),
        name="ssd_bidir",
    )(xbc, xbc, xbc, dt, z, xbc, dt, z, conv_w, conv_b, dtb_tab, alog_tab, dsk_x, ng, tri, emat)


def _stack_heads(qc):
    lane = lax.broadcasted_iota(jnp.int32, qc.shape, 1)
    lo = lane < HEAD_DIM
    zero = jnp.zeros_like(qc)
    return jnp.concatenate([jnp.where(lo, qc, zero), jnp.where(lo, zero, qc)], axis=0)


def _unstack_heads(o, n):
    lane = lax.broadcasted_iota(jnp.int32, (n, LANES), 1)
    return jnp.where(lane < HEAD_DIM, o[:n], o[n:])


def _softmax_pv(scores, values, sink=None):
    m = None
    for sc in scores:
        mx = jnp.max(sc, axis=-1, keepdims=True)
        m = mx if m is None else jnp.maximum(m, mx)
    den = None
    if sink is not None:
        m = jnp.maximum(m, jnp.max(sink, axis=-1, keepdims=True))
        den = jnp.sum(jnp.exp2(sink - m), axis=-1, keepdims=True) * (1.0 / sink.shape[-1])
    acc = None
    for sc, v in zip(scores, values):
        pr = jnp.exp2(sc - m)
        sm = jnp.sum(pr, axis=-1, keepdims=True)
        den = sm if den is None else den + sm
        t = _dot(pr.astype(BF16), v)
        acc = t if acc is None else acc + t
    return acc * (1.0 / den)


WIN_BLOCKS_PER_STEP = 2


def _win_body(seq, q_ref, k_ref, v_ref, kc_ref, vc_ref, sink_ref, o_ref):
    wb = WIN_BLOCK
    r = lax.broadcasted_iota(jnp.int32, (wb, 3 * wb), 0)
    c = lax.broadcasted_iota(jnp.int32, (wb, 3 * wb), 1)
    for i in range(WIN_BLOCKS_PER_STEP):
        n = pl.program_id(1) * WIN_BLOCKS_PER_STEP + i
        qrows = slice(i * wb, (i + 1) * wb)
        start = pl.multiple_of(jnp.clip((n - 1) * wb, 0, seq - 3 * wb), wb)
        kb = jnp.concatenate([k_ref[pl.ds(start, 3 * wb), :], kc_ref[...]], axis=0)
        vb = jnp.concatenate([v_ref[pl.ds(start, 3 * wb), :], vc_ref[...]], axis=0)
        dist = (c + start) - (r + n * wb)
        pen = jnp.where(jnp.abs(dist) <= WIN_RADIUS, 0.0, NEG_INF)
        pen = jnp.concatenate([pen, jnp.zeros((wb, kc_ref.shape[0]), F32)], axis=1)
        pen = jnp.concatenate([pen, pen], axis=0)
        scores = []
        for m in range(WIN_GQA):
            qs = _stack_heads(q_ref[qrows, LANES * m:LANES * (m + 1)])
            scores.append([_dot_nt(qs, kb) + pen])
        for m in range(WIN_GQA):
            o = _softmax_pv(scores[m], [vb], sink_ref[2 * m * wb:2 * (m + 1) * wb, :])
            o_ref[qrows, LANES * m:LANES * (m + 1)] = _unstack_heads(o, wb).astype(o_ref.dtype)


def _window_attention(qw, kw, vw, sink_x, j, n_batch, seq, n_ctx):
    wb = WIN_BLOCK
    assert seq // wb >= 3 and seq % (wb * WIN_BLOCKS_PER_STEP) == 0
    qrows = wb * WIN_BLOCKS_PER_STEP
    nb = seq // qrows
    ctx_blk0 = n_batch * seq // n_ctx
    q_spec = pl.BlockSpec((qrows, WIN_W), lambda b, n: (b * nb + n, 0))
    seq_spec = pl.BlockSpec((seq, LANES), lambda b, n: (b, 0))
    cspec = pl.BlockSpec((n_ctx, LANES), lambda b, n: (ctx_blk0 + b, 0))
    return pl.pallas_call(
        functools.partial(_win_body, seq),
        grid=(n_batch, nb),
        in_specs=[q_spec, seq_spec, seq_spec, cspec, cspec, _layer(sink_x, j)],
        out_specs=pl.BlockSpec((qrows, WIN_W), lambda b, n: (b * nb + n, 0)),
        out_shape=jax.ShapeDtypeStruct((n_batch * seq, WIN_W), BF16),
        compiler_params=_cparams("parallel", "arbitrary"),
        name="window_attention",
    )(qw, kw, vw, kw, vw, sink_x)


NA_ROWS_PER_STEP = 8


def _na_body(n_rows, q_ref, k_ref, v_ref, kc_ref, vc_ref, bias_ref, o_ref):
    kr = NA_ROWS
    for j in range(NA_ROWS_PER_STEP):
        r = pl.program_id(1) * NA_ROWS_PER_STEP + j
        first_row = jnp.clip(r - kr // 2, 0, n_rows - kr)
        start = pl.multiple_of(first_row * GRID_W, GRID_W)
        ro0 = first_row - r + NA_ROWS - 1
        rows = slice(j * GRID_W, (j + 1) * GRID_W)
        scores = []
        for m in range(NA_HEADS // 2):
            sl = slice(LANES * m, LANES * (m + 1))
            qs = _stack_heads(q_ref[rows, sl])
            kw = k_ref[pl.ds(start, kr * GRID_W), sl]
            s_loc = _dot_nt(qs, kw) + bias_ref[ro0, 2 * GRID_W * m:2 * GRID_W * (m + 1), :]
            scores.append((s_loc, _dot_nt(qs, kc_ref[:, sl])))
        for m in range(NA_HEADS // 2):
            sl = slice(LANES * m, LANES * (m + 1))
            vw = v_ref[pl.ds(start, kr * GRID_W), sl]
            o = _softmax_pv(list(scores[m]), [vw, vc_ref[:, sl]])
            o_ref[rows, sl] = _unstack_heads(o, GRID_W).astype(o_ref.dtype)


def _na_attention(qn, kn, vn, bias_tab, j, n_batch, seq, n_ctx):
    n_rows = seq // GRID_W
    assert n_rows >= NA_ROWS and n_rows % NA_ROWS_PER_STEP == 0
    n_steps = n_rows // NA_ROWS_PER_STEP
    qrows = NA_ROWS_PER_STEP * GRID_W
    ctx_blk0 = n_batch * seq // n_ctx
    return pl.pallas_call(
        functools.partial(_na_body, n_rows),
        grid=(n_batch, n_steps),
        in_specs=[pl.BlockSpec((qrows, NA_W), lambda b, r: (b * n_steps + r, 0)),
                  pl.BlockSpec((seq, NA_W), lambda b, r: (b, 0)),
                  pl.BlockSpec((seq, NA_W), lambda b, r: (b, 0)),
                  pl.BlockSpec((n_ctx, NA_W), lambda b, r: (ctx_blk0 + b, 0)),
                  pl.BlockSpec((n_ctx, NA_W), lambda b, r: (ctx_blk0 + b, 0)),
                  _layer(bias_tab, j)],
        out_specs=pl.BlockSpec((qrows, NA_W), lambda b, r: (b * n_steps + r, 0)),
        out_shape=jax.ShapeDtypeStruct((n_batch * seq, NA_W), BF16),
        compiler_params=_cparams("parallel", "arbitrary"),
        name="neighbourhood_attention",
    )(qn, kn, vn, kn, vn, bias_tab)


def _ctx_attn_body(qw_ref, kw_ref, vw_ref, qn_ref, kn_ref, vn_ref, sink_ref, ow_ref, on_ref):
    n = qw_ref.shape[0]
    qw = qw_ref[...]
    lane = lax.broadcasted_iota(jnp.int32, (n, LANES), 1)
    lo = lane < HEAD_DIM
    for m in range(WIN_GQA):
        qs = _stack_heads(qw[:, LANES * m:LANES * (m + 1)])
        r0, r1 = 2 * m * WIN_BLOCK, (2 * m + 1) * WIN_BLOCK
        sink = jnp.concatenate([jnp.broadcast_to(sink_ref[r0:r0 + 1, :], (n, LANES)),
                                jnp.broadcast_to(sink_ref[r1:r1 + 1, :], (n, LANES))], axis=0)
        o = _softmax_pv([_dot_nt(qs, kw_ref[...])], [vw_ref[...]], sink)
        ow_ref[:, LANES * m:LANES * (m + 1)] = jnp.where(lo, o[:n], o[n:]).astype(ow_ref.dtype)
    qn = qn_ref[...]
    for m in range(NA_HEADS // 2):
        sl = slice(LANES * m, LANES * (m + 1))
        qs = _stack_heads(qn[:, sl])
        o = _softmax_pv([_dot_nt(qs, kn_ref[:, sl])], [vn_ref[:, sl]])
        on_ref[:, sl] = _unstack_heads(o, n).astype(on_ref.dtype)


def _ctx_attention(qw, kw, vw, qn, kn, vn, sink_x, j, n_batch, seq, n_ctx):
    blk0 = n_batch * seq // n_ctx
    assert WIN_W == NA_W
    wide = pl.BlockSpec((n_ctx, NA_W), lambda b: (blk0 + b, 0))
    narrow = pl.BlockSpec((n_ctx, WIN_KV_W), lambda b: (blk0 + b, 0))
    out = pl.BlockSpec((n_ctx, NA_W), lambda b: (b, 0))
    return pl.pallas_call(
        _ctx_attn_body,
        grid=(n_batch,),
        in_specs=[wide, narrow, narrow, wide, wide, wide,
                  _layer(sink_x, j)],
        out_specs=[out, out],
        out_shape=[jax.ShapeDtypeStruct((n_batch * n_ctx, NA_W), BF16)] * 2,
        compiler_params=_cparams("parallel"),
        name="context_attention",
    )(qw, kw, vw, qn, kn, vn, sink_x)


def _take(w, runs, axis):
    return jnp.concatenate([lax.slice_in_dim(w, a, b, axis=axis) for a, b in runs], axis=axis)


def _xbc_runs():
    gn = SSD_GROUPS * SSD_STATE
    runs = []
    for g in range(SSD_GROUPS):
        runs.append((g * SSD_GROUP_W, (g + 1) * SSD_GROUP_W))
        runs.append((SSD_INNER + g * SSD_STATE, SSD_INNER + (g + 1) * SSD_STATE))
        runs.append((SSD_INNER + gn + g * SSD_STATE, SSD_INNER + gn + (g + 1) * SSD_STATE))
    return runs


def _head_rows(t):
    n = t.shape[0]
    flat = jnp.pad(t.astype(F32).reshape(n, 2 * SSD_HEADS), ((0, 0), (0, LANES - 2 * SSD_HEADS)))
    rows = [jnp.roll(flat, -SSD_HPG * g, axis=1) for g in range(SSD_GROUPS)]
    rows += [jnp.zeros_like(flat)] * (SUBLANES - SSD_GROUPS)
    return jnp.stack(rows, axis=1)


def _win_head_runs():
    runs = []
    for m in range(WIN_GQA):
        for kvh in range(WIN_KV_HEADS):
            h = kvh * WIN_GQA + m
            runs.append((h * HEAD_DIM, (h + 1) * HEAD_DIM))
    return runs


def _rope_tables(seq):
    pos = np.arange(seq)
    row = (pos // GRID_W).astype(np.float32)
    col = (pos % GRID_W).astype(np.float32)
    n_freq = HEAD_DIM // 4
    inv = (np.float32(ROPE_THETA) ** (-np.arange(n_freq, dtype=np.float32) / n_freq)).astype(np.float32)
    ar = (row[:, None] * inv).astype(np.float32)
    ac = (col[:, None] * inv).astype(np.float32)
    cos_h = np.concatenate([np.cos(ar), np.cos(ar), np.cos(ac), np.cos(ac)], axis=1)
    sin_h = np.concatenate([-np.sin(ar), np.sin(ar), -np.sin(ac), np.sin(ac)], axis=1)
    ident_c = np.ones((ROW_TILE, HEAD_DIM), np.float32)
    ident_s = np.zeros((ROW_TILE, HEAD_DIM), np.float32)
    cos_t = np.concatenate([cos_h, ident_c], axis=0)
    sin_t = np.concatenate([sin_h, ident_s], axis=0)
    return (jnp.asarray(np.tile(cos_t, (1, 2)), F32), jnp.asarray(np.tile(sin_t, (1, 2)), F32))


def _na_bias_table(rpb):
    n, h, nr, _ = rpb.shape
    kr = NA_ROWS
    cols = np.arange(GRID_W)
    col_start = np.clip(cols - NA_COLS // 2, 0, GRID_W - NA_COLS)
    col_ok = (cols[None] >= col_start[:, None]) & (cols[None] < col_start[:, None] + NA_COLS)
    r32 = rpb.astype(F32) * LOG2E
    ext = jnp.pad(r32, ((0, 0), (0, 0), (0, 0), (GRID_W - NA_COLS, GRID_W - NA_COLS)))
    toep = jnp.stack([ext[..., GRID_W - 1 - w:2 * GRID_W - 1 - w] for w in range(GRID_W)], axis=3)
    toep = jnp.where(jnp.asarray(col_ok), toep, NEG_INF)
    tabs = [jnp.concatenate([toep[:, :, ro0 + k] for k in range(kr)], axis=-1) for ro0 in range(NA_ROWS)]
    return jnp.stack(tabs, axis=1).reshape(n, NA_ROWS, h * GRID_W, kr * GRID_W)


def kernel(x, c, ctx, c_ctx, w_mod, b_mod, norm_mix_g, norm_ff_g, w_ff1, w_ff2, w_in_even, conv_w, conv_b,
           dt_bias, a_log, d_skip, ssd_norm_g, w_out_even, w_in_odd, q_norm_win, k_norm_win, sink_win,
           q_norm_na, k_norm_na, rpb_na, w_out_odd):
    n_batch, seq, d = x.shape
    n_ctx = ctx.shape[1]
    n_lat = n_batch * seq
    tm = ROW_TILE
    tiles_per_batch = seq // tm
    n_lat_tiles = n_lat // tm
    n_ctx_tiles = (n_batch * n_ctx) // tm
    n2 = seq // DFT_N1

    x2d, ctx2d = x.reshape(n_lat, d), ctx.reshape(n_batch * n_ctx, d)
    h = None

    cvec = jnp.concatenate([c, c_ctx[None], jnp.zeros((SUBLANES - n_batch - 1, d), F32)], axis=0)
    mod_all = _modulation(cvec, w_mod, b_mod)
    mod_all = mod_all.reshape(-1, SUBLANES, 6, 1, d)[:, :n_batch + 1].transpose(0, 2, 1, 3, 4)

    m_a, twc, tws, m_b, m_c, chan = _dft_tables(seq, n_ctx)
    xbc_runs = _xbc_runs()
    s3 = FNET_W + SSD_INNER + SSD_CONV_DIM
    w_four = _weight_product(w_in_even[:, :, :FNET_W], FNET_W, chan)
    tri = jnp.asarray(np.concatenate([np.tril(np.ones((SSD_CHUNK, SSD_CHUNK))),
                                      np.triu(np.ones((SSD_CHUNK, SSD_CHUNK)))], axis=0), F32)
    emat = np.zeros((LANES, 2 * SSD_GROUP_W), np.float32)
    for dr in range(2):
        for r in range(SSD_HPG):
            c0 = dr * SSD_GROUP_W + r * SSD_HEAD_DIM
            emat[SSD_HEADS * dr + r, c0:c0 + SSD_HEAD_DIM] = 1.0
    emat = jnp.asarray(emat, BF16)

    cos_t, sin_t = _rope_tables(seq)
    gmat_slab = jnp.asarray(np.kron(np.eye(MXU_DIM // HEAD_DIM), np.full((HEAD_DIM, HEAD_DIM), 1.0 / HEAD_DIM)),
                          BF16)
    gmat128 = gmat_slab[:LANES, :LANES]
    win_runs = _win_head_runs()
    wq = WIN_Q_HEADS * HEAD_DIM

    gains_mix = norm_mix_g.astype(F32)[:, None, :]
    gains_ff = norm_ff_g.astype(F32)[:, None, :]
    w_ff1_b, w_ff2_b = w_ff1.astype(BF16), w_ff2.astype(BF16)
    w_even_b = w_in_even.astype(BF16)
    w_dt_b = jnp.pad(w_even_b[:, :, s3:], ((0, 0), (0, 0), (0, LANES - 2 * SSD_HEADS)))
    w_out_even_b = w_out_even.astype(BF16)
    conv_w_p = _take(conv_w.astype(F32), xbc_runs, 2)
    conv_b_p = _take(conv_b.astype(F32)[:, None, :], xbc_runs, 2)
    dtb_rows, alog_rows = _head_rows(dt_bias), _head_rows(a_log)
    dsk_x = jnp.repeat(d_skip.astype(F32), SSD_HEAD_DIM, axis=1)[:, None, :]
    ssd_ng = ssd_norm_g.astype(F32)[:, None, :]
    w_odd_b = w_in_odd.astype(BF16)
    wq_perm = _take(w_odd_b[:, :, :wq], win_runs, 2)
    w_out_odd_b = w_out_odd.astype(BF16)
    wo_win_perm = _take(w_out_odd_b[:, :wq], win_runs, 1)
    qscale = HEAD_DIM ** -0.5 * LOG2E
    rep = lambda g, k: jnp.tile(g.astype(F32), (1, k))
    head_gains = jnp.stack([rep(q_norm_win, WIN_Q_HEADS) * qscale,
                            jnp.pad(rep(k_norm_win, WIN_KV_HEADS), ((0, 0), (0, wq - WIN_KV_HEADS * HEAD_DIM))),
                            rep(q_norm_na, NA_HEADS) * qscale, rep(k_norm_na, NA_HEADS)], axis=1)
    head_gains = jnp.pad(head_gains, ((0, 0), (0, SUBLANES - head_gains.shape[1]), (0, 0)))
    sink_cm = _take(sink_win.astype(F32) * LOG2E, [(r[0] // HEAD_DIM, r[1] // HEAD_DIM) for r in win_runs], 1)
    sink_x = jnp.broadcast_to(jnp.repeat(sink_cm, WIN_BLOCK, axis=1)[:, :, None],
                              (sink_win.shape[0], WIN_Q_HEADS * WIN_BLOCK, LANES))
    bias_tab = _na_bias_table(rpb_na)

    for i in range(DEPTH):
        need_ctx = i < DEPTH - 1
        j = i // 2
        if i % 2 == 0:
            src = (x2d, ctx2d) if h is None else (h, None)
            fr, fi, z, xbc, dtr = _inproj_even(*src, i, j, gains_mix, mod_all, w_four, w_even_b, w_dt_b,
                                               n_batch, tiles_per_batch)

            br, bi = _dft_stage_a(fr.reshape(-1, n2, FNET_W), fi.reshape(-1, n2, FNET_W),
                                  m_a, twc, tws, n_batch, n2)
            f_lat = _dft_stage_b(br.reshape(n_batch, seq, FNET_W), bi.reshape(n_batch, seq, FNET_W),
                                 m_b, n_batch, n2).reshape(n_lat, FNET_W)
            f_ctx = _ctx_dft(fr, fi, m_c, n_batch, n_ctx, n_lat // n_ctx)

            y_lat, y_ctx = _ssd3(xbc, dtr, z, j, conv_w_p, conv_b_p, dtb_rows, alog_rows, dsk_x, ssd_ng, tri, emat,
                                 n_batch, seq, n_ctx)

            first = h is None
            h = _outproj(h, i, mod_all, f_lat, y_lat, w_out_even_b, w_out_even_b, j, FNET_W,
                         n_batch, tiles_per_batch, 0, 0, 0, n_lat_tiles,
                         res=x2d if first else None, dst_rows=n_lat + n_batch * n_ctx)
            h = _outproj(h, i, mod_all, f_ctx, y_ctx, w_out_even_b, w_out_even_b, j, FNET_W,
                         n_batch, tiles_per_batch, n_lat_tiles, 0, 0, n_ctx_tiles,
                         res=ctx2d if first else None)
        else:
            qw, kw, vw, qn, kn, vn = _inproj_odd(h, i, j, gains_mix, mod_all, wq_perm, w_odd_b, cos_t, sin_t,
                                                 gmat_slab, gmat128, head_gains, n_batch, tiles_per_batch)
            o_win = _window_attention(qw, kw, vw, sink_x, j, n_batch, seq, n_ctx)
            o_na = _na_attention(qn, kn, vn, bias_tab, j, n_batch, seq, n_ctx)
            if need_ctx:
                oc_win, oc_na = _ctx_attention(qw, kw, vw, qn, kn, vn, sink_x, j, n_batch, seq, n_ctx)
            h = _outproj(h, i, mod_all, o_win, o_na, wo_win_perm, w_out_odd_b, j, wq,
                         n_batch, tiles_per_batch, 0, 0, 0, n_lat_tiles)
            if need_ctx:
                h = _outproj(h, i, mod_all, oc_win, oc_na, wo_win_perm, w_out_odd_b, j, wq,
                             n_batch, tiles_per_batch, n_lat_tiles, 0, 0, n_ctx_tiles)
        n_tiles = n_lat_tiles + n_ctx_tiles if need_ctx else n_lat_tiles
        h = _ffn(h, i, gains_ff, mod_all, w_ff1_b, w_ff2_b, n_batch, tiles_per_batch, n_tiles)
    return h.reshape(n_batch, seq, d)
```
